```python
import jax, jax.numpy as jnp
from jax import lax
import numpy as np

D_MODEL = 1024
BATCH = 4
SEQ = 8192
DEPTH = 2

HEAD_DIM = 64
N_HEADS = 16
DIL_WINDOWS = (128, 512, 2048)
DIL_RATES = (1, 4, 16)
N_DIL = 3
N_KV_HEADS = 2
CMP_STRIDE = 16
CMP_LEN = 32
CMP_HIDDEN = 128
SEL_BLOCK = 64
N_SELECT = 16
SLIDE_WINDOW = 512
D_FF = 2816
Q_BLOCK = 128
ROPE_THETA = 10000.0
EPS = 1e-6

kernel_name = 'yoco_dilated_nsa_macaron_adaln'


def rms_norm(x, g):
    xf = x.astype(jnp.float32)
    y = xf * lax.rsqrt(jnp.mean(xf * xf, axis=-1, keepdims=True) + EPS)
    return (y * g.astype(jnp.float32)).astype(x.dtype)


def modulate(x, shift, scale):
    return x * (1.0 + scale) + shift


def rope(x):
    s = x.shape[1]
    half = HEAD_DIM // 2
    inv = ROPE_THETA ** (-jnp.arange(half, dtype=jnp.float32) / half)
    ang = jnp.arange(s, dtype=jnp.float32)[:, None] * inv[None, :]
    cos = jnp.cos(ang)[None, :, None, :]
    sin = jnp.sin(ang)[None, :, None, :]
    xf = x.astype(jnp.float32)
    x1, x2 = xf[..., :half], xf[..., half:]
    return jnp.concatenate([x1 * cos - x2 * sin, x1 * sin + x2 * cos], axis=-1).astype(x.dtype)


def swiglu(x, w_in, w_out):
    g, u = jnp.split(x @ w_in, 2, axis=-1)
    return (jax.nn.silu(g) * u) @ w_out


def masked_softmax(s, mask):
    s = jnp.where(mask, s, -jnp.inf)
    m = jnp.max(s, axis=-1, keepdims=True)
    m = jnp.where(jnp.isfinite(m), m, 0.0)
    p = jnp.exp(s - m)
    return p / jnp.maximum(jnp.sum(p, axis=-1, keepdims=True), 1e-30)


def _ada(mod, s):
    return mod[:, s, 0][:, None, :], mod[:, s, 1][:, None, :], 1.0 + mod[:, s, 2][:, None, :]


def banded_causal_attention(q, k, v, n_back):
    n, h, l, hd = q.shape
    nb = -(-l // Q_BLOCK)
    lp = nb * Q_BLOCK
    front = (-(-n_back // Q_BLOCK)) * Q_BLOCK
    kw = front + Q_BLOCK
    qp = jnp.pad(q, ((0, 0), (0, 0), (0, lp - l), (0, 0)))
    kp = jnp.pad(k, ((0, 0), (0, 0), (front, lp - l), (0, 0)))
    vp = jnp.pad(v, ((0, 0), (0, 0), (front, lp - l), (0, 0)))
    qi = jnp.arange(Q_BLOCK)[:, None]
    ki = jnp.arange(kw)[None, :]
    dist = front + qi - ki
    band = (dist >= 0) & (dist <= n_back)
    scale = HEAD_DIM ** -0.5

    def step(bi):
        start = bi * Q_BLOCK
        qb = lax.dynamic_slice_in_dim(qp, start, Q_BLOCK, axis=2)
        kb = lax.dynamic_slice_in_dim(kp, start, kw, axis=2)
        vb = lax.dynamic_slice_in_dim(vp, start, kw, axis=2)
        s = jnp.einsum('nhqd,nhkd->nhqk', qb, kb).astype(jnp.float32) * scale
        s = jnp.where(band & (start - front + ki >= 0), s, -jnp.inf)
        m = jnp.max(s, axis=-1, keepdims=True)
        p = jnp.exp(s - m)
        den = jnp.sum(p, axis=-1, keepdims=True)
        o = jnp.einsum('nhqk,nhkd->nhqd', (p / den).astype(vb.dtype), vb)
        return o, (m + jnp.log(den))[..., 0]

    o, lse = lax.map(step, jnp.arange(nb))
    o = jnp.moveaxis(o, 0, 2).reshape(n, h, lp, hd)[:, :, :l]
    lse = jnp.moveaxis(lse, 0, 2).reshape(n, h, lp)[:, :, :l]
    return o, lse


def dilated_attention(q, k, v, window, rate):
    b, s, h, hd = q.shape
    l = s // rate

    def to_res(t):
        return t.reshape(b, l, rate, h, hd).transpose(0, 2, 3, 1, 4).reshape(b * rate, h, l, hd)

    o, lse = banded_causal_attention(to_res(q), to_res(k), to_res(v), window // rate)
    o = o.reshape(b, rate, h, l, hd).transpose(0, 3, 1, 2, 4).reshape(b, s, h, hd)
    lse = lse.reshape(b, rate, h, l).transpose(0, 3, 1, 2).reshape(b, s, h)
    return o, lse


def dilated_mixer(u, w_qkv, q_gain, k_gain, w_o):
    b, s, _ = u.shape
    qkv = (u @ w_qkv).reshape(b, s, N_DIL, 3, N_HEADS, HEAD_DIM)
    outs, lses = [], []
    for g in range(N_DIL):
        q = rope(rms_norm(qkv[:, :, g, 0], q_gain[g]))
        k = rope(rms_norm(qkv[:, :, g, 1], k_gain[g]))
        o, lse = dilated_attention(q, k, qkv[:, :, g, 2], DIL_WINDOWS[g], DIL_RATES[g])
        outs.append(o.astype(jnp.float32))
        lses.append(lse)
    w = jax.nn.softmax(jnp.stack(lses, axis=0), axis=0)
    o = jnp.sum(w[..., None] * jnp.stack(outs, axis=0), axis=0).astype(u.dtype)
    return o.reshape(b, s, N_HEADS * HEAD_DIM) @ w_o


def shared_kv(h, shift, scale, kv_norm_g, w_kv, kv_k_gain, cmp_pos, phi_w1, phi_w2):
    b, s, _ = h.shape
    u = modulate(rms_norm(h, kv_norm_g), shift, scale)
    kv = (u @ w_kv).reshape(b, s, 3, 2, N_KV_HEADS, HEAD_DIM)
    nsb = s // CMP_STRIDE
    nper = CMP_LEN // CMP_STRIDE
    n_c = nsb - nper + 1

    def compress(t, i):
        tb = t.reshape(b, nsb, CMP_STRIDE, N_KV_HEADS, HEAD_DIM)
        blocks = jnp.concatenate([tb[:, j:j + n_c] for j in range(nper)], axis=2)
        blocks = blocks + cmp_pos[i][None, None, :, None, :]
        flat = blocks.transpose(0, 1, 3, 2, 4).reshape(b, n_c, N_KV_HEADS, CMP_LEN * HEAD_DIM)
        return jax.nn.silu(flat @ phi_w1[i]) @ phi_w2[i]

    k_cmp = rms_norm(compress(kv[:, :, 0, 0], 0), kv_k_gain[0]).transpose(0, 2, 1, 3)
    v_cmp = compress(kv[:, :, 0, 1], 1).transpose(0, 2, 1, 3)
    n_s = s // SEL_BLOCK

    def to_blocks(t):
        return t.reshape(b, n_s, SEL_BLOCK, N_KV_HEADS, HEAD_DIM).transpose(0, 3, 1, 2, 4)

    k_slc = to_blocks(rope(rms_norm(kv[:, :, 1, 0], kv_k_gain[1])))
    v_slc = to_blocks(kv[:, :, 1, 1])
    pad = ((0, 0), (0, 0), (SLIDE_WINDOW, 0), (0, 0))
    k_win = jnp.pad(rope(rms_norm(kv[:, :, 2, 0], kv_k_gain[2])).transpose(0, 2, 1, 3), pad)
    v_win = jnp.pad(kv[:, :, 2, 1].transpose(0, 2, 1, 3), pad)
    return k_cmp, v_cmp, k_slc, v_slc, k_win, v_win


def block_importance(p, n_s):
    r = SEL_BLOCK // CMP_STRIDE
    nper = CMP_LEN // CMP_STRIDE
    left = nper - 1
    n_c = p.shape[-1]
    right = r * n_s + r - n_c
    pp = jnp.pad(p, [(0, 0)] * (p.ndim - 1) + [(left, right)])
    imp = None
    for o in range(-left, r):
        w = sum(1 for m in range(r) for n in range(nper) if m - n == o)
        sl = pp[..., o + left:o + left + r * n_s:r]
        imp = w * sl if imp is None else imp + w * sl
    return imp


def nsa_mixer(u, k_cmp, v_cmp, k_slc, v_slc, k_win, v_win, w_qg, q_gain, w_o):
    b, s, _ = u.shape
    grp = N_HEADS // N_KV_HEADS
    hdim = N_HEADS * HEAD_DIM
    qg = u @ w_qg
    q = rms_norm(qg[..., :hdim].reshape(b, s, N_HEADS, HEAD_DIM), q_gain)
    gates = jax.nn.sigmoid(qg[..., hdim:].astype(jnp.float32))
    gates = gates.reshape(b, s, 3, N_KV_HEADS, grp).transpose(2, 0, 3, 4, 1)

    def to_grp(t):
        return t.reshape(b, s, N_KV_HEADS, grp, HEAD_DIM).transpose(0, 2, 3, 1, 4)

    q_nope = to_grp(q)
    q_rope = to_grp(rope(q))
    n_c = k_cmp.shape[2]
    n_s = k_slc.shape[2]
    topk = min(N_SELECT, n_s)
    cmp_end = jnp.arange(n_c) * CMP_STRIDE + CMP_LEN - 1
    blk = jnp.arange(n_s)
    qi = jnp.arange(Q_BLOCK)
    ki = jnp.arange(SLIDE_WINDOW + Q_BLOCK)[None, :]
    wdist = SLIDE_WINDOW + qi[:, None] - ki
    wband = (wdist >= 0) & (wdist < SLIDE_WINDOW)
    bi_idx = jnp.arange(b)[:, None, None, None]
    hi_idx = jnp.arange(N_KV_HEADS)[None, :, None, None]
    scale = HEAD_DIM ** -0.5

    def step(bi):
        start = bi * Q_BLOCK
        t = start + qi
        qn = lax.dynamic_slice_in_dim(q_nope, start, Q_BLOCK, axis=3)
        qr = lax.dynamic_slice_in_dim(q_rope, start, Q_BLOCK, axis=3)
        gb = lax.dynamic_slice_in_dim(gates, start, Q_BLOCK, axis=4)
        sc = jnp.einsum('bkgqd,bkcd->bkgqc', qn, k_cmp).astype(jnp.float32) * scale
        p_cmp = masked_softmax(sc, cmp_end[None, :] <= t[:, None])
        o_cmp = jnp.einsum('bkgqc,bkcd->bkgqd', p_cmp.astype(v_cmp.dtype), v_cmp)
        imp = block_importance(jnp.sum(p_cmp, axis=2), n_s)
        jt = (t // SEL_BLOCK)[:, None]
        forced = (blk == 0) | (blk == jt) | (blk == jt - 1)
        imp = jnp.where(blk > jt, -jnp.inf, jnp.where(forced, jnp.inf, imp))
        _, idx = lax.top_k(imp, topk)
        ks = k_slc[bi_idx, hi_idx, idx].reshape(b, N_KV_HEADS, Q_BLOCK, topk * SEL_BLOCK, HEAD_DIM)
        vs = v_slc[bi_idx, hi_idx, idx].reshape(b, N_KV_HEADS, Q_BLOCK, topk * SEL_BLOCK, HEAD_DIM)
        kpos = (idx[..., None] * SEL_BLOCK + jnp.arange(SEL_BLOCK)).reshape(b, N_KV_HEADS, Q_BLOCK, topk * SEL_BLOCK)
        ss = jnp.einsum('bkgqd,bkqnd->bkgqn', qr, ks).astype(jnp.float32) * scale
        p_s = masked_softmax(ss, (kpos <= t[:, None])[:, :, None])
        o_slc = jnp.einsum('bkgqn,bkqnd->bkgqd', p_s.astype(vs.dtype), vs)
        kw = lax.dynamic_slice_in_dim(k_win, start, SLIDE_WINDOW + Q_BLOCK, axis=2)
        vw = lax.dynamic_slice_in_dim(v_win, start, SLIDE_WINDOW + Q_BLOCK, axis=2)
        sw = jnp.einsum('bkgqd,bknd->bkgqn', qr, kw).astype(jnp.float32) * scale
        p_w = masked_softmax(sw, wband & (start - SLIDE_WINDOW + ki >= 0))
        o_win = jnp.einsum('bkgqn,bknd->bkgqd', p_w.astype(vw.dtype), vw)
        o = gb[0][..., None] * o_cmp + gb[1][..., None] * o_slc + gb[2][..., None] * o_win
        return o.astype(u.dtype)

    o = lax.map(step, jnp.arange(s // Q_BLOCK))
    o = o.transpose(1, 0, 4, 2, 3, 5).reshape(b, s, hdim)
    return o @ w_o


def setup_inputs(seed: int = 0) -> dict:
    key = jax.random.key(seed)
    ks = jax.random.split(key, 22)
    n_a = DEPTH // 2
    n_b = DEPTH - n_a
    d = D_MODEL
    hdim = N_HEADS * HEAD_DIM

    def nrm(k, shape, fan_in, gain=1.0):
        return jax.random.normal(k, shape, jnp.float32) * (gain * fan_in ** -0.5)

    def gain_init(k, shape):
        return 1.0 + 0.02 * jax.random.normal(k, shape, jnp.float32)

    return {
        'x': jax.random.normal(ks[0], (BATCH, SEQ, d), jnp.float32),
        'c': jax.random.normal(ks[1], (BATCH, d), jnp.float32),
        'norm_g': gain_init(ks[2], (DEPTH, 3, d)),
        'w_ada': nrm(ks[3], (DEPTH, d, 9 * d), d, 0.1),
        'b_ada': 0.01 * jax.random.normal(ks[4], (DEPTH, 9 * d), jnp.float32),
        'ffn_w_in': nrm(ks[5], (DEPTH, 2, d, 2 * D_FF), d),
        'ffn_w_out': nrm(ks[6], (DEPTH, 2, D_FF, d), D_FF),
        'a_w_qkv': nrm(ks[7], (n_a, d, N_DIL * 3 * hdim), d),
        'a_q_gain': gain_init(ks[8], (n_a, N_DIL, HEAD_DIM)),
        'a_k_gain': gain_init(ks[9], (n_a, N_DIL, HEAD_DIM)),
        'a_w_o': nrm(ks[10], (n_a, hdim, d), hdim),
        'kv_norm_g': gain_init(ks[11], (d,)),
        'w_ada_kv': nrm(ks[12], (d, 2 * d), d, 0.1),
        'b_ada_kv': 0.01 * jax.random.normal(ks[13], (2 * d,), jnp.float32),
        'w_kv': nrm(ks[14], (d, 3 * 2 * N_KV_HEADS * HEAD_DIM), d),
        'kv_k_gain': gain_init(ks[15], (3, HEAD_DIM)),
        'cmp_pos': 0.1 * jax.random.normal(ks[16], (2, CMP_LEN, HEAD_DIM), jnp.float32),
        'phi_w1': nrm(ks[17], (2, CMP_LEN * HEAD_DIM, CMP_HIDDEN), CMP_LEN * HEAD_DIM),
        'phi_w2': nrm(ks[18], (2, CMP_HIDDEN, HEAD_DIM), CMP_HIDDEN),
        'b_w_qg': nrm(ks[19], (n_b, d, hdim + 3 * N_HEADS), d),
        'b_q_gain': gain_init(ks[20], (n_b, HEAD_DIM)),
        'b_w_o': nrm(ks[21], (n_b, hdim, d), hdim),
    }


def reference(x, c, norm_g, w_ada, b_ada, ffn_w_in, ffn_w_out, a_w_qkv, a_q_gain, a_k_gain, a_w_o,
              kv_norm_g, w_ada_kv, b_ada_kv, w_kv, kv_k_gain, cmp_pos, phi_w1, phi_w2,
              b_w_qg, b_q_gain, b_w_o):
    n_a = DEPTH // 2
    bsz = x.shape[0]
    c_act = jax.nn.silu(c)
    h = x
    shared = None
    for l in range(DEPTH):
        mod = (c_act @ w_ada[l] + b_ada[l]).reshape(bsz, 3, 3, D_MODEL)
        sh, sc, gt = _ada(mod, 0)
        h = h + 0.5 * gt * swiglu(modulate(rms_norm(h, norm_g[l, 0]), sh, sc), ffn_w_in[l, 0], ffn_w_out[l, 0])
        sh, sc, gt = _ada(mod, 1)
        u = modulate(rms_norm(h, norm_g[l, 1]), sh, sc)
        if l < n_a:
            y = dilated_mixer(u, a_w_qkv[l], a_q_gain[l], a_k_gain[l], a_w_o[l])
        else:
            j = l - n_a
            y = nsa_mixer(u, *shared, b_w_qg[j], b_q_gain[j], b_w_o[j])
        h = h + gt * y
        sh, sc, gt = _ada(mod, 2)
        h = h + 0.5 * gt * swiglu(modulate(rms_norm(h, norm_g[l, 2]), sh, sc), ffn_w_in[l, 1], ffn_w_out[l, 1])
        if l == n_a - 1:
            kv_mod = (c_act @ w_ada_kv + b_ada_kv).reshape(bsz, 2, D_MODEL)
            shared = shared_kv(h, kv_mod[:, 0][:, None, :], kv_mod[:, 1][:, None, :], kv_norm_g, w_kv,
                               kv_k_gain, cmp_pos, phi_w1, phi_w2)
    return h
```

```python
import functools

import jax
import jax.numpy as jnp
from jax import lax
from jax.experimental import pallas as pl
from jax.experimental.pallas import tpu as pltpu

HEAD_DIM = 64
N_HEADS = 16
DIL_WINDOWS = (128, 512, 2048)
DIL_RATES = (1, 4, 16)
N_DIL = 3
N_KV_HEADS = 2
GROUP = N_HEADS // N_KV_HEADS
CMP_STRIDE = 16
CMP_LEN = 32
SEL_BLOCK = 64
SEL_SHIFT = SEL_BLOCK.bit_length() - 1
N_SELECT = 16
SLIDE_WINDOW = 512
ROPE_THETA = 10000.0
EPS = 1e-6

LANES = 128
Q_ROWS = 128
NEG = -1e30
LOG2E = 1.4426950408889634
VMEM_LIMIT = 56 * 1024 * 1024

BF16 = jnp.bfloat16
F32 = jnp.float32


def _params(*sem):
    return pltpu.CompilerParams(dimension_semantics=sem, vmem_limit_bytes=VMEM_LIMIT)


def _dot(a, b):
    return jnp.dot(a, b, preferred_element_type=F32)


def _dot_t(a, b):
    return lax.dot_general(a, b, (((1,), (1,)), ((), ())), preferred_element_type=F32)


def _split_dot(a, b):
    a_hi = a.astype(BF16)
    a_lo = (a - a_hi.astype(F32)).astype(BF16)
    b_hi = b.astype(BF16)
    b_lo = (b - b_hi.astype(F32)).astype(BF16)
    return _dot(a_hi, b_hi) + (_dot(a_hi, b_lo) + _dot(a_lo, b_hi))


def _sigmoid(x):
    return 1.0 / (1.0 + jnp.exp(-x))


def _normmod(x, g, shift, scale):
    y = x * lax.rsqrt(jnp.mean(x * x, axis=-1, keepdims=True) + EPS) * g
    return y * (1.0 + scale) + shift


def _head_rms(y, bd):
    ss = _dot((y * y).astype(BF16), bd)
    return y * lax.rsqrt(ss * (1.0 / HEAD_DIM) + EPS)


def _rope(z, cos, sin_signed):
    lane = lax.broadcasted_iota(jnp.int32, z.shape, 1)
    first_half = (lane & (HEAD_DIM - 1)) < HEAD_DIM // 2
    swapped = jnp.where(first_half, pltpu.roll(z, LANES - HEAD_DIM // 2, 1),
                        pltpu.roll(z, HEAD_DIM // 2, 1))
    return z * cos + swapped * sin_signed


def _ada_kernel(c_ref, w_ref, b_ref, o_ref):
    c = c_ref[...]
    a = (c * _sigmoid(c)).astype(BF16)
    o_ref[...] = _dot(a, w_ref[...].astype(BF16)) + b_ref[...]


def _ada(c8, w, b):
    nl, d, n = w.shape
    tn = 1024
    return pl.pallas_call(
        _ada_kernel,
        grid=(nl, n // tn),
        in_specs=[pl.BlockSpec((8, d), lambda l, j: (0, 0)),
                  pl.BlockSpec((None, d, tn), lambda l, j: (l, 0, j)),
                  pl.BlockSpec((None, 1, tn), lambda l, j: (l, 0, j))],
        out_specs=pl.BlockSpec((None, 8, tn), lambda l, j: (l, 0, j)),
        out_shape=jax.ShapeDtypeStruct((nl, 8, n), F32),
        compiler_params=_params("parallel", "parallel"),
        name="ada_table",
    )(c8, w, b)


def _mod_block(d, l, k):
    return pl.BlockSpec((None, None, None, 1, d), lambda b, *_: (l, b, k, 0, 0))


def _ffn_kernel(h_ref, g_ref, sh_ref, sc_ref, gt_ref, wg_ref, wu_ref, wo_ref, o_ref, u_scr, acc_scr):
    j = pl.program_id(2)

    @pl.when(j == 0)
    def _():
        u_scr[...] = _normmod(h_ref[0], g_ref[...], sh_ref[...], sc_ref[...]).astype(BF16)
        acc_scr[...] = jnp.zeros_like(acc_scr)

    u = u_scr[...]
    gate = _dot(u, wg_ref[...])
    up = _dot(u, wu_ref[...])
    act = (gate * _sigmoid(gate) * up).astype(BF16)
    acc_scr[...] += _dot(act, wo_ref[...])

    @pl.when(j == pl.num_programs(2) - 1)
    def _():
        o_ref[0] = h_ref[0] + (0.5 * (1.0 + gt_ref[...])) * acc_scr[...]


def _ffn(h, norm_g3, mod5, w_in, w_out, l, sub, which):
    b, s, d = h.shape
    f = w_out.shape[2]
    tm = min(1024, s)
    tf = 256
    nf = f // tf
    return pl.pallas_call(
        _ffn_kernel,
        grid=(b, s // tm, nf),
        in_specs=[pl.BlockSpec((1, tm, d), lambda bb, i, j: (bb, i, 0)),
                  pl.BlockSpec((None, 1, d), lambda bb, i, j: (l * 3 + sub, 0, 0)),
                  _mod_block(d, l, sub * 3 + 0),
                  _mod_block(d, l, sub * 3 + 1),
                  _mod_block(d, l, sub * 3 + 2),
                  pl.BlockSpec((None, None, d, tf), lambda bb, i, j: (l, which, 0, j)),
                  pl.BlockSpec((None, None, d, tf), lambda bb, i, j: (l, which, 0, j + nf)),
                  pl.BlockSpec((None, None, tf, d), lambda bb, i, j: (l, which, j, 0))],
        out_specs=pl.BlockSpec((1, tm, d), lambda bb, i, j: (bb, i, 0)),
        out_shape=jax.ShapeDtypeStruct((b, s, d), F32),
        scratch_shapes=[pltpu.VMEM((tm, d), BF16), pltpu.VMEM((tm, d), F32)],
        compiler_params=_params("parallel", "parallel", "arbitrary"),
        name="swiglu_half_step",
    )(h, norm_g3, mod5, mod5, mod5, w_in, w_in, w_out)


def _qkv_kernel(h_ref, g_ref, sh_ref, sc_ref, w_ref, gain_ref, cos_ref, sin_ref, bd_ref, o_ref, u_scr):
    n = pl.program_id(2)

    @pl.when(n == 0)
    def _():
        u_scr[...] = _normmod(h_ref[0], g_ref[...], sh_ref[...], sc_ref[...]).astype(BF16)

    y = _dot(u_scr[...], w_ref[...])
    kind = n % 3

    @pl.when(kind == 2)
    def _():
        o_ref[0] = y.astype(BF16)

    @pl.when(kind != 2)
    def _():
        cos = cos_ref[...]
        sin = sin_ref[...]
        bd = bd_ref[...]
        for cb in range(y.shape[1] // LANES):
            sl = slice(cb * LANES, (cb + 1) * LANES)
            z = _head_rms(y[:, sl], bd) * gain_ref[:, sl]
            o_ref[0, :, sl] = _rope(z, cos, sin).astype(BF16)


def _qkv_proj(h, norm_g3, mod5, w, gain, cos, sin, bd, l):
    b, s, d = h.shape
    n = w.shape[1]
    tm = min(1024, s)
    tn = N_HEADS * HEAD_DIM
    return pl.pallas_call(
        _qkv_kernel,
        grid=(b, s // tm, n // tn),
        in_specs=[pl.BlockSpec((1, tm, d), lambda bb, i, j: (bb, i, 0)),
                  pl.BlockSpec((None, 1, d), lambda bb, i, j: (l * 3 + 1, 0, 0)),
                  _mod_block(d, l, 3),
                  _mod_block(d, l, 4),
                  pl.BlockSpec((d, tn), lambda bb, i, j: (0, j)),
                  pl.BlockSpec((None, 1, tn), lambda bb, i, j: (j, 0, 0)),
                  pl.BlockSpec((tm, LANES), lambda bb, i, j: (i, 0)),
                  pl.BlockSpec((tm, LANES), lambda bb, i, j: (i, 0)),
                  pl.BlockSpec((LANES, LANES), lambda bb, i, j: (0, 0))],
        out_specs=pl.BlockSpec((1, tm, tn), lambda bb, i, j: (bb, i, j)),
        out_shape=jax.ShapeDtypeStruct((b, s, n), BF16),
        scratch_shapes=[pltpu.VMEM((tm, d), BF16)],
        compiler_params=_params("parallel", "parallel", "arbitrary"),
        name="qkv_proj",
    )(h, norm_g3, mod5, mod5, w, gain, cos, sin, bd)


def _dil_kernel(q_ref, kp_ref, kc_ref, vp_ref, vc_ref, o_ref, lse_ref, *, tq, n_back):
    i = pl.program_id(3)
    lane = lax.broadcasted_iota(jnp.int32, (Q_ROWS, LANES), 1)
    low = lane < HEAD_DIM
    qi = lax.broadcasted_iota(jnp.int32, (Q_ROWS, 2 * Q_ROWS), 0)
    ki = lax.broadcasted_iota(jnp.int32, (Q_ROWS, 2 * Q_ROWS), 1)
    dist = Q_ROWS + qi - ki
    band = (dist >= 0) & (dist <= n_back)
    for j in range(tq // Q_ROWS):
        q = q_ref[0, j * Q_ROWS:(j + 1) * Q_ROWS, :]
        if j == 0:
            k = jnp.concatenate([kp_ref[0], kc_ref[0, 0:Q_ROWS, :]], axis=0)
            v = jnp.concatenate([vp_ref[0], vc_ref[0, 0:Q_ROWS, :]], axis=0)
        else:
            k = kc_ref[0, (j - 1) * Q_ROWS:(j + 1) * Q_ROWS, :]
            v = vc_ref[0, (j - 1) * Q_ROWS:(j + 1) * Q_ROWS, :]
        valid = band & (i * tq + (j - 1) * Q_ROWS + ki >= 0)
        outs, lses = [], []
        for head_low in (True, False):
            qm = jnp.where(low if head_low else ~low, q, jnp.zeros_like(q))
            s = jnp.where(valid, _dot_t(qm, k), NEG)
            m = jnp.max(s, axis=-1, keepdims=True)
            p = jnp.exp2(s - m)
            den = jnp.sum(p, axis=-1, keepdims=True)
            outs.append(_dot(p.astype(BF16), v) / den)
            lses.append(m + jnp.log(den) * LOG2E)
        o_ref[0, j * Q_ROWS:(j + 1) * Q_ROWS, :] = jnp.where(low, outs[0], outs[1]).astype(BF16)
        lse_ref[0, j * Q_ROWS:(j + 1) * Q_ROWS, :] = jnp.where(low, lses[0], lses[1])


def _dilated_attention(qkv, g):
    b, s, ncol = qkv.shape
    rate = DIL_RATES[g]
    n_back = DIL_WINDOWS[g] // rate
    assert n_back <= Q_ROWS
    l = s // rate
    tq = min(512, l)
    hdim = N_HEADS * HEAD_DIM
    cpr = ncol // LANES
    hpr = hdim // LANES
    qcol = g * 3 * hpr
    sub = tq // Q_ROWS
    qkv_r = qkv.reshape(b, l, rate * ncol)

    def cur(off):
        return pl.BlockSpec((1, tq, LANES), lambda bb, r, hp, i: (bb, i, r * cpr + qcol + off + hp))

    def prev(off):
        return pl.BlockSpec((1, Q_ROWS, LANES),
                            lambda bb, r, hp, i: (bb, jnp.maximum(i * sub - 1, 0), r * cpr + qcol + off + hp))

    out_spec = pl.BlockSpec((1, tq, LANES), lambda bb, r, hp, i: (bb, i, r * hpr + hp))
    o, lse = pl.pallas_call(
        functools.partial(_dil_kernel, tq=tq, n_back=n_back),
        grid=(b, rate, hpr, l // tq),
        in_specs=[cur(0), prev(hpr), cur(hpr), prev(2 * hpr), cur(2 * hpr)],
        out_specs=[out_spec, out_spec],
        out_shape=[jax.ShapeDtypeStruct((b, l, rate * hdim), BF16),
                   jax.ShapeDtypeStruct((b, l, rate * hdim), F32)],
        compiler_params=_params("parallel", "parallel", "parallel", "arbitrary"),
        name=f"dilated_attention_rate{rate}",
    )(qkv_r, qkv_r, qkv_r, qkv_r, qkv_r)
    return o.reshape(b, s, hdim), lse.reshape(b, s, hdim)


def _merge_out_kernel(o0_ref, o1_ref, o2_ref, l0_ref, l1_ref, l2_ref, h_ref, gt_ref, w_ref, out_ref):
    l0, l1, l2 = l0_ref[0], l1_ref[0], l2_ref[0]
    m = jnp.maximum(jnp.maximum(l0, l1), l2)
    e0, e1, e2 = jnp.exp2(l0 - m), jnp.exp2(l1 - m), jnp.exp2(l2 - m)
    o = (e0 * o0_ref[0].astype(F32) + e1 * o1_ref[0].astype(F32) + e2 * o2_ref[0].astype(F32)) / (e0 + e1 + e2)
    out_ref[0] = h_ref[0] + (1.0 + gt_ref[...]) * _dot(o.astype(BF16), w_ref[...])


def _merge_out(os_, lses, h, mod5, w_o, l):
    b, s, d = h.shape
    hdim = w_o.shape[0]
    tm = min(512, s)
    row = lambda width: pl.BlockSpec((1, tm, width), lambda bb, i: (bb, i, 0))
    return pl.pallas_call(
        _merge_out_kernel,
        grid=(b, s // tm),
        in_specs=[row(hdim)] * 6 + [row(d), _mod_block(d, l, 5),
                                    pl.BlockSpec((hdim, d), lambda bb, i: (0, 0))],
        out_specs=row(d),
        out_shape=jax.ShapeDtypeStruct((b, s, d), F32),
        compiler_params=_params("parallel", "parallel"),
        name="merge_out_proj",
    )(*os_, *lses, h, mod5, w_o)


def _out_kernel(o_ref, h_ref, gt_ref, w_ref, out_ref):
    out_ref[0] = h_ref[0] + (1.0 + gt_ref[...]) * _dot(o_ref[0], w_ref[...])


def _out_proj(o, h, mod5, w_o, l):
    b, s, d = h.shape
    hdim = w_o.shape[0]
    tm = min(1024, s)
    row = lambda width: pl.BlockSpec((1, tm, width), lambda bb, i: (bb, i, 0))
    return pl.pallas_call(
        _out_kernel,
        grid=(b, s // tm),
        in_specs=[row(hdim), row(d), _mod_block(d, l, 5), pl.BlockSpec((hdim, d), lambda bb, i: (0, 0))],
        out_specs=row(d),
        out_shape=jax.ShapeDtypeStruct((b, s, d), F32),
        compiler_params=_params("parallel", "parallel"),
        name="out_proj",
    )(o, h, mod5, w_o)


def _kv_kernel(h_ref, g_ref, sh_ref, sc_ref, w_ref, gain_ref, cos_ref, sin_ref, bd_ref,
               ks_ref, kw_ref, vs_ref, vw_ref, kvc_ref, *, tm):
    i = pl.program_id(1)
    u = _normmod(h_ref[0], g_ref[...], sh_ref[...], sc_ref[...]).astype(BF16)
    y = _dot(u, w_ref[...])
    cos = cos_ref[...]
    sin = sin_ref[...]
    bd = bd_ref[...]
    lane = lax.broadcasted_iota(jnp.int32, (tm, LANES), 1)
    pos = i * tm + lax.broadcasted_iota(jnp.int32, (tm, LANES), 0)
    onehot = jnp.where((pos >> SEL_SHIFT) == lane, 1.0, 0.0).astype(BF16)
    for j in range(4):
        z = _head_rms(y[:, j * LANES:(j + 1) * LANES], bd) * gain_ref[j:j + 1, :]
        zr = _rope(z, cos, sin).astype(BF16)
        if j < 2:
            ks_ref[0, :, 2 * j * LANES:(2 * j + 1) * LANES] = zr
            ks_ref[0, :, (2 * j + 1) * LANES:(2 * j + 2) * LANES] = onehot
        else:
            kw_ref[0, :, (j - 2) * LANES:(j - 1) * LANES] = zr
    for j in range(4):
        v = y[:, (4 + j) * LANES:(5 + j) * LANES]
        va = jnp.where(lane < HEAD_DIM, v, 1.0).astype(BF16)
        ref = vs_ref if j < 2 else vw_ref
        ref[0, :, (j % 2) * LANES:(j % 2 + 1) * LANES] = va
    kvc_ref[0] = y[:, 8 * LANES:10 * LANES]


def _kv_proj(h, kv_norm_g, kvmod5, w, gain, cos, sin, bd):
    b, s, d = h.shape
    n = w.shape[1]
    tm = min(512, s)
    row = lambda width, dt: (pl.BlockSpec((1, tm, width), lambda bb, i: (bb, i, 0)),
                             jax.ShapeDtypeStruct((b, s, width), dt))
    outs = [row(4 * LANES, BF16), row(2 * LANES, BF16), row(2 * LANES, BF16), row(2 * LANES, BF16),
            row(2 * LANES, F32)]
    return pl.pallas_call(
        functools.partial(_kv_kernel, tm=tm),
        grid=(b, s // tm),
        in_specs=[pl.BlockSpec((1, tm, d), lambda bb, i: (bb, i, 0)),
                  pl.BlockSpec((1, d), lambda bb, i: (0, 0)),
                  _mod_block(d, 0, 0),
                  _mod_block(d, 0, 1),
                  pl.BlockSpec((d, n), lambda bb, i: (0, 0)),
                  pl.BlockSpec((8, LANES), lambda bb, i: (0, 0)),
                  pl.BlockSpec((tm, LANES), lambda bb, i: (i, 0)),
                  pl.BlockSpec((tm, LANES), lambda bb, i: (i, 0)),
                  pl.BlockSpec((LANES, LANES), lambda bb, i: (0, 0))],
        out_specs=[o[0] for o in outs],
        out_shape=[o[1] for o in outs],
        compiler_params=_params("parallel", "parallel"),
        name="shared_kv_proj",
    )(h, kv_norm_g, kvmod5, kvmod5, w, gain, cos, sin, bd)


def _cmp_kernel(t_ref, w1_ref, w2_ref, pos_ref, gain_ref, o_ref):
    is_key = pl.program_id(0) < N_KV_HEADS
    t = t_ref[...]
    w1 = w1_ref[...]
    half = w1.shape[0] // 2
    first = _split_dot(t, w1[:half])
    second = _split_dot(t, w1[half:])
    second = pltpu.roll(second, second.shape[0] - 1, 0)
    posb = _split_dot(jnp.broadcast_to(pos_ref[...], (8, w1.shape[0])), w1)[0:1]
    hid = first + second + posb
    out = _split_dot(hid * _sigmoid(hid), w2_ref[...])
    key = out * lax.rsqrt(jnp.mean(out * out, axis=-1, keepdims=True) + EPS) * gain_ref[...]
    dup = jnp.concatenate([key, key], axis=-1)
    aug = jnp.concatenate([out, jnp.ones_like(out)], axis=-1)
    o_ref[...] = jnp.where(is_key, dup, aug).astype(BF16)


def _compress(t4, phi_w1, phi_w2, cmp_pos, gain):
    _, b, nsb, width = t4.shape
    hid = phi_w1.shape[2]
    return pl.pallas_call(
        _cmp_kernel,
        grid=(4, b),
        in_specs=[pl.BlockSpec((None, None, nsb, width), lambda t, bb: (t, bb, 0, 0)),
                  pl.BlockSpec((None, CMP_LEN * HEAD_DIM, hid), lambda t, bb: (t // N_KV_HEADS, 0, 0)),
                  pl.BlockSpec((None, hid, HEAD_DIM), lambda t, bb: (t // N_KV_HEADS, 0, 0)),
                  pl.BlockSpec((None, 1, CMP_LEN * HEAD_DIM), lambda t, bb: (t // N_KV_HEADS, 0, 0)),
                  pl.BlockSpec((1, HEAD_DIM), lambda t, bb: (0, 0))],
        out_specs=pl.BlockSpec((None, None, nsb, LANES), lambda t, bb: (t, bb, 0, 0)),
        out_shape=jax.ShapeDtypeStruct((4, b, nsb, LANES), BF16),
        compiler_params=_params("parallel", "parallel"),
        name="compress_blocks",
    )(t4, phi_w1, phi_w2, cmp_pos, gain)


def _qg_kernel(h_ref, g_ref, sh_ref, sc_ref, w_ref, gain_ref, cos_ref, sin_ref, bd_ref,
               qn_ref, qr_ref, gate_ref):
    u = _normmod(h_ref[0], g_ref[...], sh_ref[...], sc_ref[...]).astype(BF16)
    y = _dot(u, w_ref[...])
    cos = cos_ref[...]
    sin = sin_ref[...]
    bd = bd_ref[...]
    hdim = qn_ref.shape[2]
    for cb in range(hdim // LANES):
        sl = slice(cb * LANES, (cb + 1) * LANES)
        z = _head_rms(y[:, sl], bd) * gain_ref[:, sl]
        qn_ref[0, :, sl] = z.astype(BF16)
        qr_ref[0, :, sl] = _rope(z, cos, sin).astype(BF16)
    gate_ref[0] = _sigmoid(y[:, hdim:])


def _qg_proj(h, norm_g3, mod5, w, gain, cos, sin, bd, l):
    b, s, d = h.shape
    n = w.shape[1]
    hdim = N_HEADS * HEAD_DIM
    tm = min(512, s)
    row = lambda width, dt: (pl.BlockSpec((1, tm, width), lambda bb, i: (bb, i, 0)),
                             jax.ShapeDtypeStruct((b, s, width), dt))
    outs = [row(hdim, BF16), row(hdim, BF16), row(n - hdim, F32)]
    return pl.pallas_call(
        _qg_kernel,
        grid=(b, s // tm),
        in_specs=[pl.BlockSpec((1, tm, d), lambda bb, i: (bb, i, 0)),
                  pl.BlockSpec((None, 1, d), lambda bb, i: (l * 3 + 1, 0, 0)),
                  _mod_block(d, l, 3),
                  _mod_block(d, l, 4),
                  pl.BlockSpec((d, n), lambda bb, i: (0, 0)),
                  pl.BlockSpec((1, hdim), lambda bb, i: (0, 0)),
                  pl.BlockSpec((tm, LANES), lambda bb, i: (i, 0)),
                  pl.BlockSpec((tm, LANES), lambda bb, i: (i, 0)),
                  pl.BlockSpec((LANES, LANES), lambda bb, i: (0, 0))],
        out_specs=[o[0] for o in outs],
        out_shape=[o[1] for o in outs],
        compiler_params=_params("parallel", "parallel"),
        name="nsa_query_gate_proj",
    )(h, norm_g3, mod5, mod5, w, gain, cos, sin, bd)


def _nsa_kernel(qn_ref, qr_ref, gate_ref, kc_ref, vc_ref, ks_ref, vs_ref, kw_ref, vw_ref, wimp_ref,
                o_ref, qa_scr, m_scr, acc_scr, *, tq):
    i = pl.program_id(2)
    start = i * tq
    rows = GROUP * tq
    ncp = kc_ref.shape[0]
    lane = lax.broadcasted_iota(jnp.int32, (tq, LANES), 1)
    low = lane < HEAD_DIM

    def t_of(shape):
        return start + (lax.broadcasted_iota(jnp.int32, shape, 0) & (tq - 1))

    def stack(q):
        parts = []
        for g in range(GROUP):
            blk = q[:, (g // 2) * LANES:(g // 2 + 1) * LANES]
            parts.append(jnp.where(low if g % 2 == 0 else ~low, blk, jnp.zeros_like(blk)))
        return parts

    qn = jnp.concatenate(stack(qn_ref[0]), axis=0)
    sc = _dot_t(qn, kc_ref[...])
    c_end = lax.broadcasted_iota(jnp.int32, (rows, ncp), 1) * CMP_STRIDE + (CMP_LEN - 1)
    valid = c_end <= t_of((rows, ncp))
    sc = jnp.where(valid, sc, NEG)
    m = jnp.max(sc, axis=-1, keepdims=True)
    p = jnp.where(valid, jnp.exp2(sc - m), 0.0)
    pn = p / jnp.maximum(jnp.sum(p, axis=-1, keepdims=True), 1e-30)
    o_cmp = _dot(pn.astype(BF16), vc_ref[...])
    psum = pn[0:tq]
    for g in range(1, GROUP):
        psum = psum + pn[g * tq:(g + 1) * tq]

    p_hi = psum.astype(BF16)
    p_lo = (psum - p_hi.astype(F32)).astype(BF16)
    imp = _dot(p_hi, wimp_ref[...]) + _dot(p_lo, wimp_ref[...])
    jt = (start + lax.broadcasted_iota(jnp.int32, (tq, LANES), 0)) >> SEL_SHIFT
    forced = (lane == 0) | (lane == jt) | (lane == jt - 1)
    work = jnp.where(lane > jt, -1.0, jnp.where(forced, jnp.inf, imp))
    sel = lane < 0
    for _ in range(N_SELECT):
        mx = jnp.max(work, axis=-1, keepdims=True)
        idx = jnp.min(jnp.where(work == mx, lane, LANES), axis=-1, keepdims=True)
        pick = lane == idx
        sel = sel | pick
        work = jnp.where(pick, -2.0, work)
    sel_bias = jnp.where(sel, 0.0, NEG).astype(BF16)

    qr_parts = stack(qr_ref[0])
    for g in range(GROUP):
        qa_scr[g * tq:(g + 1) * tq, 0:LANES] = qr_parts[g]
        qa_scr[g * tq:(g + 1) * tq, LANES:2 * LANES] = sel_bias

    t_rows = t_of((rows, tq))
    k_iota = lax.broadcasted_iota(jnp.int32, (rows, tq), 1)

    def online_step(s, v, first):
        if first:
            m_new = jnp.max(s, axis=-1, keepdims=True)
            acc_scr[...] = _dot(jnp.exp2(s - m_new).astype(BF16), v)
        else:
            m_old = m_scr[...]
            m_new = jnp.maximum(m_old, jnp.max(s, axis=-1, keepdims=True))
            acc_scr[...] = jnp.exp2(m_old - m_new) * acc_scr[...] + _dot(jnp.exp2(s - m_new).astype(BF16), v)
        m_scr[...] = m_new

    diag = pl.ds(pl.multiple_of(start, tq), tq)
    s = _dot_t(qa_scr[...], ks_ref[diag, :])
    online_step(jnp.where(start + k_iota <= t_rows, s, NEG), vs_ref[diag, :], True)

    def slc_body(kb, carry):
        blk = pl.ds(pl.multiple_of(kb * tq, tq), tq)
        online_step(_dot_t(qa_scr[...], ks_ref[blk, :]), vs_ref[blk, :], False)
        return carry

    lax.fori_loop(0, i, slc_body, 0)
    a_slc = acc_scr[...]

    qr = qa_scr[:, 0:LANES]
    for dk in range(1 + -(-SLIDE_WINDOW // tq)):
        kb = i - dk
        blk = pl.ds(pl.multiple_of(jnp.maximum(kb, 0) * tq, tq), tq)
        kpos = kb * tq + k_iota
        d = t_rows - kpos
        ok = (d >= 0) & (d < SLIDE_WINDOW) & (kpos >= 0)
        online_step(jnp.where(ok, _dot_t(qr, kw_ref[blk, :]), NEG), vw_ref[blk, :], dk == 0)
    a_win = acc_scr[...]

    gates = gate_ref[0]
    for pr in range(GROUP // 2):
        vals = []
        for hh in range(2):
            g = 2 * pr + hh
            rs = slice(g * tq, (g + 1) * tq)
            g_cmp, g_slc, g_win = (gates[:, br * GROUP + g:br * GROUP + g + 1] for br in range(3))
            oc, a_s, a_w = o_cmp[rs], a_slc[rs], a_win[rs]
            r_s = pltpu.roll(a_s, HEAD_DIM, 1)
            r_w = pltpu.roll(a_w, HEAD_DIM, 1)
            if hh == 0:
                vals.append(g_cmp * oc + g_slc * (a_s / r_s) + g_win * (a_w / r_w))
            else:
                vals.append(g_cmp * pltpu.roll(oc, HEAD_DIM, 1) + g_slc * (r_s / a_s) + g_win * (r_w / a_w))
        o_ref[0, :, pr * LANES:(pr + 1) * LANES] = jnp.where(low, vals[0], vals[1]).astype(BF16)


def _nsa_attention(qn, qr, gates, cmp4, ks, vs, kw, vw, wimp):
    b, s, hdim = qn.shape
    tq = min(256, s)
    assert s // SEL_BLOCK <= LANES and s % tq == 0
    ncp = cmp4.shape[2]
    gw = GROUP * HEAD_DIM
    rows = GROUP * tq
    qspec = pl.BlockSpec((1, tq, gw), lambda bb, kv, i: (bb, i, kv))
    full = lambda width: pl.BlockSpec((None, s, width), lambda bb, kv, i: (bb, 0, kv))
    return pl.pallas_call(
        functools.partial(_nsa_kernel, tq=tq),
        grid=(b, N_KV_HEADS, s // tq),
        in_specs=[qspec, qspec,
                  pl.BlockSpec((1, tq, LANES), lambda bb, kv, i: (bb, i, kv)),
                  pl.BlockSpec((None, None, ncp, LANES), lambda bb, kv, i: (kv, bb, 0, 0)),
                  pl.BlockSpec((None, None, ncp, LANES), lambda bb, kv, i: (N_KV_HEADS + kv, bb, 0, 0)),
                  full(2 * LANES), full(LANES), full(LANES), full(LANES),
                  pl.BlockSpec((ncp, LANES), lambda bb, kv, i: (0, 0))],
        out_specs=qspec,
        out_shape=jax.ShapeDtypeStruct((b, s, hdim), BF16),
        scratch_shapes=[pltpu.VMEM((rows, 2 * LANES), BF16),
                        pltpu.VMEM((rows, 1), F32),
                        pltpu.VMEM((rows, LANES), F32)],
        compiler_params=_params("parallel", "parallel", "arbitrary"),
        name="nsa_attention",
    )(qn, qr, gates, cmp4, cmp4, ks, vs, kw, vw, wimp)


def _rope_tables(s):
    half = HEAD_DIM // 2
    inv = ROPE_THETA ** (-jnp.arange(half, dtype=F32) / half)
    ang = jnp.arange(s, dtype=F32)[:, None] * inv[None, :]
    cos, sin = jnp.cos(ang), jnp.sin(ang)
    reps = LANES // HEAD_DIM
    return jnp.tile(cos, (1, 2 * reps)), jnp.tile(jnp.concatenate([-sin, sin], axis=1), (1, reps))


def _importance_weights(ncp):
    r = SEL_BLOCK // CMP_STRIDE
    nper = CMP_LEN // CMP_STRIDE
    c = jnp.arange(ncp)[:, None]
    j = jnp.arange(LANES)[None, :]
    o = c - r * j
    w = jnp.zeros((ncp, LANES), F32)
    for off in range(-(nper - 1), r):
        cnt = sum(1 for m in range(r) for n in range(nper) if m - n == off)
        w = w + jnp.where(o == off, float(cnt), 0.0)
    return w.astype(BF16)


def kernel(x, c, norm_g, w_ada, b_ada, ffn_w_in, ffn_w_out, a_w_qkv, a_q_gain, a_k_gain, a_w_o, kv_norm_g,
           w_ada_kv, b_ada_kv, w_kv, kv_k_gain, cmp_pos, phi_w1, phi_w2, b_w_qg, b_q_gain, b_w_o):
    b, s, d = x.shape
    depth = norm_g.shape[0]
    n_a = depth // 2
    hdim = N_HEADS * HEAD_DIM
    assert depth == 2 and b <= 8

    c8 = jnp.zeros((8, d), F32).at[:b].set(c)
    mod = _ada(c8, w_ada, b_ada.reshape(depth, 1, 9 * d))
    mod5 = mod[:, :b].reshape(depth, b, 9, 1, d)
    kvmod = _ada(c8, w_ada_kv[None], b_ada_kv.reshape(1, 1, 2 * d))
    kvmod5 = kvmod[:, :b].reshape(1, b, 2, 1, d)

    norm_g3 = norm_g.reshape(depth * 3, 1, d)
    w_in = ffn_w_in.astype(BF16)
    w_out = ffn_w_out.astype(BF16)
    cos, sin = _rope_tables(s)
    blk = jnp.arange(LANES) // HEAD_DIM
    bd = (blk[:, None] == blk[None, :]).astype(BF16)
    q_scale = HEAD_DIM ** -0.5 * LOG2E

    h = x
    for l in range(depth):
        h = _ffn(h, norm_g3, mod5, w_in, w_out, l, 0, 0)
        if l < n_a:
            gain = jnp.stack([jnp.tile(a_q_gain[l] * q_scale, (1, N_HEADS)),
                              jnp.tile(a_k_gain[l], (1, N_HEADS)),
                              jnp.ones((N_DIL, hdim), F32)], axis=1).reshape(3 * N_DIL, 1, hdim)
            qkv = _qkv_proj(h, norm_g3, mod5, a_w_qkv[l].astype(BF16), gain, cos, sin, bd, l)
            parts = [_dilated_attention(qkv, g) for g in range(N_DIL)]
            h = _merge_out([p[0] for p in parts], [p[1] for p in parts], h, mod5, a_w_o[l].astype(BF16), l)
        else:
            j = l - n_a
            w_q = b_w_qg[j][:, :hdim]
            w_g = b_w_qg[j][:, hdim:].reshape(d, 3, N_KV_HEADS, GROUP).transpose(0, 2, 1, 3)
            w_g = jnp.pad(w_g.reshape(d, N_KV_HEADS, 3 * GROUP), ((0, 0), (0, 0), (0, LANES - 3 * GROUP)))
            w_qg = jnp.concatenate([w_q, w_g.reshape(d, N_KV_HEADS * LANES)], axis=1).astype(BF16)
            gain = jnp.tile(b_q_gain[j] * q_scale, N_HEADS)[None, :]
            qn, qr, gates = _qg_proj(h, norm_g3, mod5, w_qg, gain, cos, sin, bd, l)
            o = _nsa_attention(qn, qr, gates, *shared)
            h = _out_proj(o, h, mod5, b_w_o[j].astype(BF16), l)
        h = _ffn(h, norm_g3, mod5, w_in, w_out, l, 2, 1)
        if l == n_a - 1:
            col = lambda br, kv, kvh: br * 4 * HEAD_DIM + kv * 2 * HEAD_DIM + kvh * HEAD_DIM + jnp.arange(HEAD_DIM)
            dup = lambda br, kv, kvh: jnp.concatenate([col(br, kv, kvh)] * 2)
            cols = jnp.concatenate(
                [dup(br, 0, kvh) for br in (1, 2) for kvh in range(N_KV_HEADS)]
                + [dup(br, 1, kvh) for br in (1, 2) for kvh in range(N_KV_HEADS)]
                + [col(0, kv, kvh) for kv in range(2) for kvh in range(N_KV_HEADS)])
            w_kv_ext = w_kv[:, cols].astype(BF16)
            kgain = jnp.concatenate([jnp.tile(kv_k_gain[1], (2, 2)), jnp.tile(kv_k_gain[2], (2, 2)),
                                     jnp.ones((4, LANES), F32)], axis=0)
            ks, kw, vs, vw, kvc = _kv_proj(h, kv_norm_g[None, :], kvmod5, w_kv_ext, kgain, cos, sin, bd)
            nsb = s // CMP_STRIDE
            t4 = kvc.reshape(b, nsb, CMP_STRIDE, 4, HEAD_DIM).transpose(3, 0, 1, 2, 4)
            t4 = t4.reshape(4, b, nsb, CMP_STRIDE * HEAD_DIM)
            cmp4 = _compress(t4, phi_w1, phi_w2, cmp_pos.reshape(2, 1, CMP_LEN * HEAD_DIM), kv_k_gain[0][None, :])
            shared = (cmp4, ks, vs, kw, vw, _importance_weights(nsb))
    return h
```

```python
import functools

import jax
import jax.numpy as jnp
from jax import lax
from jax.experimental import pallas as pl
from jax.experimental.pallas import tpu as pltpu

HEAD_DIM = 64
N_HEADS = 16
DIL_WINDOWS = (128, 512, 2048)
DIL_RATES = (1, 4, 16)
N_DIL = 3
N_KV_HEADS = 2
GROUP = N_HEADS // N_KV_HEADS
CMP_STRIDE = 16
CMP_LEN = 32
SEL_BLOCK = 64
SEL_SHIFT = SEL_BLOCK.bit_length() - 1
N_SELECT = 16
SLIDE_WINDOW = 512
ROPE_THETA = 10000.0
EPS = 1e-6

LANES = 128
Q_ROWS = 128
NEG = -1e30
LOG2E = 1.4426950408889634
SCORE_LIMIT = 60.0
VMEM_LIMIT = 56 * 1024 * 1024

BF16 = jnp.bfloat16
F32 = jnp.float32


def _params(*sem):
    return pltpu.CompilerParams(dimension_semantics=sem, vmem_limit_bytes=VMEM_LIMIT)


def _dot(a, b):
    return jnp.dot(a, b, preferred_element_type=F32)


def _dot_t(a, b):
    return lax.dot_general(a, b, (((1,), (1,)), ((), ())), preferred_element_type=F32)


def _split_dot(a, b):
    a_hi = a.astype(BF16)
    a_lo = (a - a_hi.astype(F32)).astype(BF16)
    b_hi = b.astype(BF16)
    b_lo = (b - b_hi.astype(F32)).astype(BF16)
    return _dot(a_hi, b_hi) + (_dot(a_hi, b_lo) + _dot(a_lo, b_hi))


def _sigmoid(x):
    return 1.0 / (1.0 + jnp.exp(-x))


def _normmod(x, g, shift, scale):
    y = x * lax.rsqrt(jnp.mean(x * x, axis=-1, keepdims=True) + EPS) * g
    return y * (1.0 + scale) + shift


def _head_rms(y, bd):
    ss = _dot((y * y).astype(BF16), bd)
    return y * lax.rsqrt(ss * (1.0 / HEAD_DIM) + EPS)


def _rope(z, cos, sin_signed):
    lane = lax.broadcasted_iota(jnp.int32, z.shape, 1)
    first_half = (lane & (HEAD_DIM - 1)) < HEAD_DIM // 2
    swapped = jnp.where(first_half, pltpu.roll(z, LANES - HEAD_DIM // 2, 1),
                        pltpu.roll(z, HEAD_DIM // 2, 1))
    return z * cos + swapped * sin_signed


def _ada_kernel(c_ref, w_ref, b_ref, o_ref):
    c = c_ref[...]
    a = (c * _sigmoid(c)).astype(BF16)
    o_ref[...] = _dot(a, w_ref[...].astype(BF16)) + b_ref[...]


def _ada(c8, w, b):
    nl, d, n = w.shape
    tn = 1024
    return pl.pallas_call(
        _ada_kernel,
        grid=(nl, n // tn),
        in_specs=[pl.BlockSpec((8, d), lambda l, j: (0, 0)),
                  pl.BlockSpec((None, d, tn), lambda l, j: (l, 0, j)),
                  pl.BlockSpec((None, 1, tn), lambda l, j: (l, 0, j))],
        out_specs=pl.BlockSpec((None, 8, tn), lambda l, j: (l, 0, j)),
        out_shape=jax.ShapeDtypeStruct((nl, 8, n), F32),
        compiler_params=_params("parallel", "parallel"),
        name="ada_table",
    )(c8, w, b)


def _mod_block(d, l, k):
    return pl.BlockSpec((None, None, None, 1, d), lambda b, *_: (l, b, k, 0, 0))


def _ffn_kernel(h_ref, g_ref, sh_ref, sc_ref, gt_ref, wg_ref, wu_ref, wo_ref, o_ref, u_scr, acc_scr):
    j = pl.program_id(2)

    @pl.when(j == 0)
    def _():
        u_scr[...] = _normmod(h_ref[0], g_ref[...], sh_ref[...], sc_ref[...]).astype(BF16)
        acc_scr[...] = jnp.zeros_like(acc_scr)

    u = u_scr[...]
    gate = _dot(u, wg_ref[...])
    up = _dot(u, wu_ref[...])
    act = (gate * _sigmoid(gate) * up).astype(BF16)
    acc_scr[...] += _dot(act, wo_ref[...])

    @pl.when(j == pl.num_programs(2) - 1)
    def _():
        o_ref[0] = h_ref[0] + (0.5 * (1.0 + gt_ref[...])) * acc_scr[...]


def _ffn(h, norm_g3, mod5, w_in, w_out, l, sub, which):
    b, s, d = h.shape
    f = w_out.shape[2]
    tm = min(1024, s)
    tf = 256
    nf = f // tf
    return pl.pallas_call(
        _ffn_kernel,
        grid=(b, s // tm, nf),
        in_specs=[pl.BlockSpec((1, tm, d), lambda bb, i, j: (bb, i, 0)),
                  pl.BlockSpec((None, 1, d), lambda bb, i, j: (l * 3 + sub, 0, 0)),
                  _mod_block(d, l, sub * 3 + 0),
                  _mod_block(d, l, sub * 3 + 1),
                  _mod_block(d, l, sub * 3 + 2),
                  pl.BlockSpec((None, None, d, tf), lambda bb, i, j: (l, which, 0, j)),
                  pl.BlockSpec((None, None, d, tf), lambda bb, i, j: (l, which, 0, j + nf)),
                  pl.BlockSpec((None, None, tf, d), lambda bb, i, j: (l, which, j, 0))],
        out_specs=pl.BlockSpec((1, tm, d), lambda bb, i, j: (bb, i, 0)),
        out_shape=jax.ShapeDtypeStruct((b, s, d), F32),
        scratch_shapes=[pltpu.VMEM((tm, d), BF16), pltpu.VMEM((tm, d), F32)],
        compiler_params=_params("parallel", "parallel", "arbitrary"),
        name="swiglu_half_step",
    )(h, norm_g3, mod5, mod5, mod5, w_in, w_in, w_out)


def _qkv_kernel(h_ref, g_ref, sh_ref, sc_ref, w_ref, gain_ref, cos_ref, sin_ref, bd_ref, o_ref, u_scr, uf_scr,
                *, rate):
    n = pl.program_id(2)
    tm = u_scr.shape[0]
    seg = tm // rate

    @pl.when(n == 0)
    def _():
        u = _normmod(h_ref[0], g_ref[...], sh_ref[...], sc_ref[...])
        if rate == 1:
            u_scr[...] = u.astype(BF16)
        else:
            for cb in range(u.shape[1] // LANES):
                sl = slice(cb * LANES, (cb + 1) * LANES)
                uf_scr[cb] = u[:, sl]
                for res in range(rate):
                    u_scr[res * seg:(res + 1) * seg, sl] = uf_scr[cb, pl.ds(res, seg, stride=rate), :].astype(BF16)

    y = _dot(u_scr[...], w_ref[...])

    @pl.when(n == 2)
    def _():
        for res in range(rate):
            o_ref[res] = y[res * seg:(res + 1) * seg].astype(BF16)

    @pl.when(n != 2)
    def _():
        cos = cos_ref[...]
        sin = sin_ref[...]
        bd = bd_ref[...]
        for cb in range(y.shape[1] // LANES):
            sl = slice(cb * LANES, (cb + 1) * LANES)
            z = _rope(_head_rms(y[:, sl], bd) * gain_ref[:, sl], cos, sin).astype(BF16)
            for res in range(rate):
                o_ref[res, :, sl] = z[res * seg:(res + 1) * seg]


def _qkv_proj(h, norm_g3, mod5, w, gain, cos, sin, bd, l, g):
    b, s, d = h.shape
    rate = DIL_RATES[g]
    tm = min(1024, s)
    seg = tm // rate
    tn = N_HEADS * HEAD_DIM

    def residue_major(t):
        return t.reshape(s // tm, seg, rate, LANES).transpose(0, 2, 1, 3).reshape(s, LANES)

    return pl.pallas_call(
        functools.partial(_qkv_kernel, rate=rate),
        grid=(b, s // tm, 3),
        in_specs=[pl.BlockSpec((1, tm, d), lambda bb, i, j: (bb, i, 0)),
                  pl.BlockSpec((None, 1, d), lambda bb, i, j: (l * 3 + 1, 0, 0)),
                  _mod_block(d, l, 3),
                  _mod_block(d, l, 4),
                  pl.BlockSpec((d, tn), lambda bb, i, j: (0, 3 * g + j)),
                  pl.BlockSpec((None, 1, tn), lambda bb, i, j: (3 * g + j, 0, 0)),
                  pl.BlockSpec((tm, LANES), lambda bb, i, j: (i, 0)),
                  pl.BlockSpec((tm, LANES), lambda bb, i, j: (i, 0)),
                  pl.BlockSpec((LANES, LANES), lambda bb, i, j: (0, 0))],
        out_specs=pl.BlockSpec((None, rate, seg, tn), lambda bb, i, j: (bb, 0, i, j)),
        out_shape=jax.ShapeDtypeStruct((b, rate, s // rate, 3 * tn), BF16),
        scratch_shapes=[pltpu.VMEM((tm, d), BF16), pltpu.VMEM((d // LANES, tm, LANES), F32)],
        compiler_params=_params("parallel", "parallel", "arbitrary"),
        name=f"qkv_proj_rate{rate}",
    )(h, norm_g3, mod5, mod5, w, gain, residue_major(cos), residue_major(sin), bd)


def _dil_kernel(q_ref, kp_ref, kc_ref, vp_ref, vc_ref, o_ref, lse_ref, *, tq, n_back):
    i = pl.program_id(3)
    lane = lax.broadcasted_iota(jnp.int32, (Q_ROWS, LANES), 1)
    low = lane < HEAD_DIM
    qi = lax.broadcasted_iota(jnp.int32, (Q_ROWS, 2 * Q_ROWS), 0)
    ki = lax.broadcasted_iota(jnp.int32, (Q_ROWS, 2 * Q_ROWS), 1)
    dist = Q_ROWS + qi - ki
    band = (dist >= 0) & (dist <= n_back)
    for j in range(tq // Q_ROWS):
        q = q_ref[j * Q_ROWS:(j + 1) * Q_ROWS, :]
        if j == 0:
            k = jnp.concatenate([kp_ref[...], kc_ref[0:Q_ROWS, :]], axis=0)
            v = jnp.concatenate([vp_ref[...], vc_ref[0:Q_ROWS, :]], axis=0)
        else:
            k = kc_ref[(j - 1) * Q_ROWS:(j + 1) * Q_ROWS, :]
            v = vc_ref[(j - 1) * Q_ROWS:(j + 1) * Q_ROWS, :]
        valid = band & (i * tq + (j - 1) * Q_ROWS + ki >= 0)
        outs, lses = [], []
        for head_low in (True, False):
            qm = jnp.where(low if head_low else ~low, q, jnp.zeros_like(q))
            s = jnp.where(valid, _dot_t(qm, k), NEG)
            m = jnp.max(s, axis=-1, keepdims=True)
            p = jnp.exp2(s - m)
            den = jnp.sum(p, axis=-1, keepdims=True)
            outs.append(_dot(p.astype(BF16), v) / den)
            lses.append(m + jnp.log(den) * LOG2E)
        o_ref[0, j * Q_ROWS:(j + 1) * Q_ROWS, :] = jnp.where(low, outs[0], outs[1]).astype(BF16)
        lse_ref[0, j * Q_ROWS:(j + 1) * Q_ROWS, :] = jnp.where(low, lses[0], lses[1])


def _dilated_attention(qkv_r, g):
    b, rate, l, _ = qkv_r.shape
    s = l * rate
    n_back = DIL_WINDOWS[g] // rate
    assert n_back <= Q_ROWS and rate == DIL_RATES[g]
    tq = min(512, l)
    hdim = N_HEADS * HEAD_DIM
    hpr = hdim // LANES
    sub = tq // Q_ROWS

    def cur(off):
        return pl.BlockSpec((None, None, tq, LANES), lambda bb, r, hp, i: (bb, r, i, off + hp))

    def prev(off):
        return pl.BlockSpec((None, None, Q_ROWS, LANES),
                            lambda bb, r, hp, i: (bb, r, jnp.maximum(i * sub - 1, 0), off + hp))

    out_spec = pl.BlockSpec((1, tq, LANES), lambda bb, r, hp, i: (bb, i, r * hpr + hp))
    o, lse = pl.pallas_call(
        functools.partial(_dil_kernel, tq=tq, n_back=n_back),
        grid=(b, rate, hpr, l // tq),
        in_specs=[cur(0), prev(hpr), cur(hpr), prev(2 * hpr), cur(2 * hpr)],
        out_specs=[out_spec, out_spec],
        out_shape=[jax.ShapeDtypeStruct((b, l, rate * hdim), BF16),
                   jax.ShapeDtypeStruct((b, l, rate * hdim), F32)],
        compiler_params=_params("parallel", "parallel", "parallel", "arbitrary"),
        name=f"dilated_attention_rate{rate}",
    )(qkv_r, qkv_r, qkv_r, qkv_r, qkv_r)
    return o.reshape(b, s, hdim), lse.reshape(b, s, hdim)


def _merge_out_kernel(o0_ref, o1_ref, o2_ref, l0_ref, l1_ref, l2_ref, h_ref, gt_ref, w_ref, out_ref):
    l0, l1, l2 = l0_ref[0], l1_ref[0], l2_ref[0]
    m = jnp.maximum(jnp.maximum(l0, l1), l2)
    e0, e1, e2 = jnp.exp2(l0 - m), jnp.exp2(l1 - m), jnp.exp2(l2 - m)
    o = (e0 * o0_ref[0].astype(F32) + e1 * o1_ref[0].astype(F32) + e2 * o2_ref[0].astype(F32)) / (e0 + e1 + e2)
    out_ref[0] = h_ref[0] + (1.0 + gt_ref[...]) * _dot(o.astype(BF16), w_ref[...])


def _merge_out(os_, lses, h, mod5, w_o, l):
    b, s, d = h.shape
    hdim = w_o.shape[0]
    tm = min(512, s)
    row = lambda width: pl.BlockSpec((1, tm, width), lambda bb, i: (bb, i, 0))
    return pl.pallas_call(
        _merge_out_kernel,
        grid=(b, s // tm),
        in_specs=[row(hdim)] * 6 + [row(d), _mod_block(d, l, 5),
                                    pl.BlockSpec((hdim, d), lambda bb, i: (0, 0))],
        out_specs=row(d),
        out_shape=jax.ShapeDtypeStruct((b, s, d), F32),
        compiler_params=_params("parallel", "parallel"),
        name="merge_out_proj",
    )(*os_, *lses, h, mod5, w_o)


def _out_kernel(o_ref, h_ref, gt_ref, w_ref, out_ref):
    out_ref[0] = h_ref[0] + (1.0 + gt_ref[...]) * _dot(o_ref[0], w_ref[...])


def _out_proj(o, h, mod5, w_o, l):
    b, s, d = h.shape
    hdim = w_o.shape[0]
    tm = min(1024, s)
    row = lambda width: pl.BlockSpec((1, tm, width), lambda bb, i: (bb, i, 0))
    return pl.pallas_call(
        _out_kernel,
        grid=(b, s // tm),
        in_specs=[row(hdim), row(d), _mod_block(d, l, 5), pl.BlockSpec((hdim, d), lambda bb, i: (0, 0))],
        out_specs=row(d),
        out_shape=jax.ShapeDtypeStruct((b, s, d), F32),
        compiler_params=_params("parallel", "parallel"),
        name="out_proj",
    )(o, h, mod5, w_o)


def _kv_kernel(h_ref, g_ref, sh_ref, sc_ref, w_ref, gain_ref, cos_ref, sin_ref, bd_ref,
               ks_ref, kw_ref, vs_ref, vw_ref, kvc_ref, *, tm):
    i = pl.program_id(1)
    u = _normmod(h_ref[0], g_ref[...], sh_ref[...], sc_ref[...]).astype(BF16)
    y = _dot(u, w_ref[...])
    cos = cos_ref[...]
    sin = sin_ref[...]
    bd = bd_ref[...]
    lane = lax.broadcasted_iota(jnp.int32, (tm, LANES), 1)
    pos = i * tm + lax.broadcasted_iota(jnp.int32, (tm, LANES), 0)
    onehot = jnp.where((pos >> SEL_SHIFT) == lane, 1.0, 0.0).astype(BF16)
    for j in range(4):
        z = _head_rms(y[:, j * LANES:(j + 1) * LANES], bd) * gain_ref[j:j + 1, :]
        zr = _rope(z, cos, sin).astype(BF16)
        if j < 2:
            ks_ref[0, :, 2 * j * LANES:(2 * j + 1) * LANES] = zr
            ks_ref[0, :, (2 * j + 1) * LANES:(2 * j + 2) * LANES] = onehot
        else:
            kw_ref[0, :, (j - 2) * LANES:(j - 1) * LANES] = zr
    for j in range(4):
        v = y[:, (4 + j) * LANES:(5 + j) * LANES]
        va = jnp.where(lane < HEAD_DIM, v, 1.0).astype(BF16)
        ref = vs_ref if j < 2 else vw_ref
        ref[0, :, (j % 2) * LANES:(j % 2 + 1) * LANES] = va
    kvc_ref[0] = y[:, 8 * LANES:10 * LANES]


def _kv_proj(h, kv_norm_g, kvmod5, w, gain, cos, sin, bd):
    b, s, d = h.shape
    n = w.shape[1]
    tm = min(512, s)
    row = lambda width, dt: (pl.BlockSpec((1, tm, width), lambda bb, i: (bb, i, 0)),
                             jax.ShapeDtypeStruct((b, s, width), dt))
    outs = [row(4 * LANES, BF16), row(2 * LANES, BF16), row(2 * LANES, BF16), row(2 * LANES, BF16),
            row(2 * LANES, F32)]
    return pl.pallas_call(
        functools.partial(_kv_kernel, tm=tm),
        grid=(b, s // tm),
        in_specs=[pl.BlockSpec((1, tm, d), lambda bb, i: (bb, i, 0)),
                  pl.BlockSpec((1, d), lambda bb, i: (0, 0)),
                  _mod_block(d, 0, 0),
                  _mod_block(d, 0, 1),
                  pl.BlockSpec((d, n), lambda bb, i: (0, 0)),
                  pl.BlockSpec((8, LANES), lambda bb, i: (0, 0)),
                  pl.BlockSpec((tm, LANES), lambda bb, i: (i, 0)),
                  pl.BlockSpec((tm, LANES), lambda bb, i: (i, 0)),
                  pl.BlockSpec((LANES, LANES), lambda bb, i: (0, 0))],
        out_specs=[o[0] for o in outs],
        out_shape=[o[1] for o in outs],
        compiler_params=_params("parallel", "parallel"),
        name="shared_kv_proj",
    )(h, kv_norm_g, kvmod5, kvmod5, w, gain, cos, sin, bd)


def _cmp_kernel(t_ref, w1_ref, w2_ref, pos_ref, gain_ref, o_ref):
    is_key = pl.program_id(0) < N_KV_HEADS
    t = t_ref[...]
    w1 = w1_ref[...]
    half = w1.shape[0] // 2
    first = _split_dot(t, w1[:half])
    second = _split_dot(t, w1[half:])
    second = pltpu.roll(second, second.shape[0] - 1, 0)
    posb = _split_dot(jnp.broadcast_to(pos_ref[...], (8, w1.shape[0])), w1)[0:1]
    hid = first + second + posb
    out = _split_dot(hid * _sigmoid(hid), w2_ref[...])
    key = out * lax.rsqrt(jnp.mean(out * out, axis=-1, keepdims=True) + EPS) * gain_ref[...]
    dup = jnp.concatenate([key, key], axis=-1)
    aug = jnp.concatenate([out, jnp.ones_like(out)], axis=-1)
    o_ref[...] = jnp.where(is_key, dup, aug).astype(BF16)


def _compress(t4, phi_w1, phi_w2, cmp_pos, gain):
    _, b, nsb, width = t4.shape
    hid = phi_w1.shape[2]
    return pl.pallas_call(
        _cmp_kernel,
        grid=(4, b),
        in_specs=[pl.BlockSpec((None, None, nsb, width), lambda t, bb: (t, bb, 0, 0)),
                  pl.BlockSpec((None, CMP_LEN * HEAD_DIM, hid), lambda t, bb: (t // N_KV_HEADS, 0, 0)),
                  pl.BlockSpec((None, hid, HEAD_DIM), lambda t, bb: (t // N_KV_HEADS, 0, 0)),
                  pl.BlockSpec((None, 1, CMP_LEN * HEAD_DIM), lambda t, bb: (t // N_KV_HEADS, 0, 0)),
                  pl.BlockSpec((1, HEAD_DIM), lambda t, bb: (0, 0))],
        out_specs=pl.BlockSpec((None, None, nsb, LANES), lambda t, bb: (t, bb, 0, 0)),
        out_shape=jax.ShapeDtypeStruct((4, b, nsb, LANES), BF16),
        compiler_params=_params("parallel", "parallel"),
        name="compress_blocks",
    )(t4, phi_w1, phi_w2, cmp_pos, gain)


def _qg_kernel(h_ref, g_ref, sh_ref, sc_ref, w_ref, gain_ref, cos_ref, sin_ref, bd_ref,
               qn_ref, qr_ref, gate_ref):
    u = _normmod(h_ref[0], g_ref[...], sh_ref[...], sc_ref[...]).astype(BF16)
    y = _dot(u, w_ref[...])
    cos = cos_ref[...]
    sin = sin_ref[...]
    bd = bd_ref[...]
    hdim = qn_ref.shape[2]
    for cb in range(hdim // LANES):
        sl = slice(cb * LANES, (cb + 1) * LANES)
        z = _head_rms(y[:, sl], bd) * gain_ref[:, sl]
        qn_ref[0, :, sl] = z.astype(BF16)
        qr_ref[0, :, sl] = _rope(z, cos, sin).astype(BF16)
    gate_ref[0] = _sigmoid(y[:, hdim:])


def _qg_proj(h, norm_g3, mod5, w, gain, cos, sin, bd, l):
    b, s, d = h.shape
    n = w.shape[1]
    hdim = N_HEADS * HEAD_DIM
    tm = min(512, s)
    row = lambda width, dt: (pl.BlockSpec((1, tm, width), lambda bb, i: (bb, i, 0)),
                             jax.ShapeDtypeStruct((b, s, width), dt))
    outs = [row(hdim, BF16), row(hdim, BF16), row(n - hdim, F32)]
    return pl.pallas_call(
        _qg_kernel,
        grid=(b, s // tm),
        in_specs=[pl.BlockSpec((1, tm, d), lambda bb, i: (bb, i, 0)),
                  pl.BlockSpec((None, 1, d), lambda bb, i: (l * 3 + 1, 0, 0)),
                  _mod_block(d, l, 3),
                  _mod_block(d, l, 4),
                  pl.BlockSpec((d, n), lambda bb, i: (0, 0)),
                  pl.BlockSpec((1, hdim), lambda bb, i: (0, 0)),
                  pl.BlockSpec((tm, LANES), lambda bb, i: (i, 0)),
                  pl.BlockSpec((tm, LANES), lambda bb, i: (i, 0)),
                  pl.BlockSpec((LANES, LANES), lambda bb, i: (0, 0))],
        out_specs=[o[0] for o in outs],
        out_shape=[o[1] for o in outs],
        compiler_params=_params("parallel", "parallel"),
        name="nsa_query_gate_proj",
    )(h, norm_g3, mod5, mod5, w, gain, cos, sin, bd)


def _nsa_body(qn_ref, qr_ref, gate_ref, kc_ref, vc_ref, ks_ref, vs_ref, kw_ref, vw_ref, wimpt_ref,
              o_ref, qa_scr, m_scr, acc_scr, *, tq, shifted):
    i = pl.program_id(2)
    start = i * tq
    rows = GROUP * tq
    ncp = kc_ref.shape[0]
    low = lax.broadcasted_iota(jnp.int32, (tq, LANES), 1) < HEAD_DIM
    low_r = lax.broadcasted_iota(jnp.int32, (rows, LANES), 1) < HEAD_DIM

    def accumulate(s, v, mask, first):
        if not shifted:
            p = jnp.exp2(s)
            if mask is not None:
                p = jnp.where(mask, p, 0.0)
            pv = _dot(p.astype(BF16), v)
            acc_scr[...] = pv if first else acc_scr[...] + pv
            return
        if mask is not None:
            s = jnp.where(mask, s, NEG)
        if first:
            m_new = jnp.max(s, axis=-1, keepdims=True)
            acc_scr[...] = _dot(jnp.exp2(s - m_new).astype(BF16), v)
        else:
            m_old = m_scr[...]
            m_new = jnp.maximum(m_old, jnp.max(s, axis=-1, keepdims=True))
            acc_scr[...] = jnp.exp2(m_old - m_new) * acc_scr[...] + _dot(jnp.exp2(s - m_new).astype(BF16), v)
        m_scr[...] = m_new

    def stack(q):
        parts = []
        for g in range(GROUP):
            blk = q[:, (g // 2) * LANES:(g // 2 + 1) * LANES]
            parts.append(jnp.where(low if g % 2 == 0 else ~low, blk, jnp.zeros_like(blk)))
        return parts

    qn = jnp.concatenate(stack(qn_ref[0]), axis=0)
    sc = _dot_t(qn, kc_ref[...])
    c_end = lax.broadcasted_iota(jnp.int32, (rows, ncp), 1) * CMP_STRIDE + (CMP_LEN - 1)
    valid = c_end <= start + (lax.broadcasted_iota(jnp.int32, (rows, ncp), 0) & (tq - 1))
    if shifted:
        sc = jnp.where(valid, sc, NEG)
        sc = sc - jnp.max(sc, axis=-1, keepdims=True)
    p = jnp.where(valid, jnp.exp2(sc), 0.0)
    oc = _dot(p.astype(BF16), vc_ref[...])
    inv = 1.0 / jnp.maximum(jnp.where(low_r, pltpu.roll(oc, HEAD_DIM, 1), oc), 1e-30)
    o_cmp = oc * inv
    psum_blocks = []
    for cb in range(ncp // LANES):
        cs = slice(cb * LANES, (cb + 1) * LANES)
        tot = p[0:tq, cs] * inv[0:tq]
        for g in range(1, GROUP):
            tot = tot + p[g * tq:(g + 1) * tq, cs] * inv[g * tq:(g + 1) * tq]
        psum_blocks.append(tot)
    psum = jnp.concatenate(psum_blocks, axis=1)

    p_hi = psum.astype(BF16)
    p_lo = (psum - p_hi.astype(F32)).astype(BF16)
    imp_t = _dot_t(wimpt_ref[...], p_hi) + _dot_t(wimpt_ref[...], p_lo)
    blk_id = lax.broadcasted_iota(jnp.int32, (LANES, tq), 0)
    jt = (start + lax.broadcasted_iota(jnp.int32, (LANES, tq), 1)) >> SEL_SHIFT
    forced = (blk_id == 0) | (blk_id == jt) | (blk_id == jt - 1)
    work = jnp.where(blk_id > jt, -1.0, jnp.where(forced, jnp.inf, imp_t))
    sel = blk_id < 0
    for _ in range(N_SELECT):
        mx = jnp.max(work, axis=0, keepdims=True)
        idx = jnp.min(jnp.where(work == mx, blk_id, LANES), axis=0, keepdims=True)
        pick = blk_id == idx
        sel = sel | pick
        work = jnp.where(pick, -2.0, work)
    sel_bias = jnp.transpose(jnp.where(sel, 0.0, NEG)).astype(BF16)

    qr_parts = stack(qr_ref[0])
    for g in range(GROUP):
        qa_scr[g * tq:(g + 1) * tq, 0:LANES] = qr_parts[g]
        qa_scr[g * tq:(g + 1) * tq, LANES:2 * LANES] = sel_bias

    tri = (lax.broadcasted_iota(jnp.int32, (rows, tq), 1)
           <= (lax.broadcasted_iota(jnp.int32, (rows, tq), 0) & (tq - 1)))

    def key_block(kb):
        return pl.ds(pl.multiple_of(kb * tq, tq), tq)

    diag = key_block(i)
    accumulate(_dot_t(qa_scr[...], ks_ref[diag, :]), vs_ref[diag, :], tri, True)

    def slc_body(kb, carry):
        blk = key_block(kb)
        accumulate(_dot_t(qa_scr[...], ks_ref[blk, :]), vs_ref[blk, :], None, False)
        return carry

    lax.fori_loop(0, i, slc_body, 0)
    a_slc = acc_scr[...]

    qr = qa_scr[:, 0:LANES]
    accumulate(_dot_t(qr, kw_ref[diag, :]), vw_ref[diag, :], tri, True)

    @pl.when(i >= 1)
    def _():
        blk = key_block(i - 1)
        accumulate(_dot_t(qr, kw_ref[blk, :]), vw_ref[blk, :], None, False)

    @pl.when(i >= 2)
    def _():
        blk = key_block(i - 2)
        accumulate(_dot_t(qr, kw_ref[blk, :]), vw_ref[blk, :], ~tri, False)

    a_win = acc_scr[...]

    gates = gate_ref[0]
    for pr in range(GROUP // 2):
        vals = []
        for hh in range(2):
            g = 2 * pr + hh
            rs = slice(g * tq, (g + 1) * tq)
            g_cmp, g_slc, g_win = (gates[:, br * GROUP + g:br * GROUP + g + 1] for br in range(3))
            o_c, a_s, a_w = o_cmp[rs], a_slc[rs], a_win[rs]
            r_s = pltpu.roll(a_s, HEAD_DIM, 1)
            r_w = pltpu.roll(a_w, HEAD_DIM, 1)
            if hh == 0:
                vals.append(g_cmp * o_c + g_slc * (a_s / r_s) + g_win * (a_w / r_w))
            else:
                vals.append(g_cmp * pltpu.roll(o_c, HEAD_DIM, 1) + g_slc * (r_s / a_s) + g_win * (r_w / a_w))
        o_ref[0, :, pr * LANES:(pr + 1) * LANES] = jnp.where(low, vals[0], vals[1]).astype(BF16)


def _nsa_kernel(small_ref, *refs, tq):
    @pl.when(small_ref[0] != 0)
    def _():
        _nsa_body(*refs, tq=tq, shifted=False)

    @pl.when(small_ref[0] == 0)
    def _():
        _nsa_body(*refs, tq=tq, shifted=True)


def _nsa_attention(small, qn, qr, gates, cmp4, ks, vs, kw, vw, wimpt):
    b, s, hdim = qn.shape
    tq = min(256, s)
    assert s // SEL_BLOCK <= LANES and s % tq == 0 and SLIDE_WINDOW == 2 * tq
    ncp = cmp4.shape[2]
    gw = GROUP * HEAD_DIM
    rows = GROUP * tq
    qspec = pl.BlockSpec((1, tq, gw), lambda bb, kv, i: (bb, i, kv))
    full = lambda width: pl.BlockSpec((None, s, width), lambda bb, kv, i: (bb, 0, kv))
    return pl.pallas_call(
        functools.partial(_nsa_kernel, tq=tq),
        grid=(b, N_KV_HEADS, s // tq),
        in_specs=[pl.BlockSpec(memory_space=pltpu.SMEM), qspec, qspec,
                  pl.BlockSpec((1, tq, LANES), lambda bb, kv, i: (bb, i, kv)),
                  pl.BlockSpec((None, None, ncp, LANES), lambda bb, kv, i: (kv, bb, 0, 0)),
                  pl.BlockSpec((None, None, ncp, LANES), lambda bb, kv, i: (N_KV_HEADS + kv, bb, 0, 0)),
                  full(2 * LANES), full(LANES), full(LANES), full(LANES),
                  pl.BlockSpec((LANES, ncp), lambda bb, kv, i: (0, 0))],
        out_specs=qspec,
        out_shape=jax.ShapeDtypeStruct((b, s, hdim), BF16),
        scratch_shapes=[pltpu.VMEM((rows, 2 * LANES), BF16),
                        pltpu.VMEM((rows, 1), F32),
                        pltpu.VMEM((rows, LANES), F32)],
        compiler_params=_params("parallel", "parallel", "arbitrary"),
        name="nsa_attention",
    )(small, qn, qr, gates, cmp4, cmp4, ks, vs, kw, vw, wimpt)


def _rope_tables(s):
    half = HEAD_DIM // 2
    inv = ROPE_THETA ** (-jnp.arange(half, dtype=F32) / half)
    ang = jnp.arange(s, dtype=F32)[:, None] * inv[None, :]
    cos, sin = jnp.cos(ang), jnp.sin(ang)
    reps = LANES // HEAD_DIM
    return jnp.tile(cos, (1, 2 * reps)), jnp.tile(jnp.concatenate([-sin, sin], axis=1), (1, reps))


def _importance_weights(ncp):
    r = SEL_BLOCK // CMP_STRIDE
    nper = CMP_LEN // CMP_STRIDE
    c = jnp.arange(ncp)[:, None]
    j = jnp.arange(LANES)[None, :]
    o = c - r * j
    w = jnp.zeros((ncp, LANES), F32)
    for off in range(-(nper - 1), r):
        cnt = sum(1 for m in range(r) for n in range(nper) if m - n == off)
        w = w + jnp.where(o == off, float(cnt), 0.0)
    return w.astype(BF16)


def kernel(x, c, norm_g, w_ada, b_ada, ffn_w_in, ffn_w_out, a_w_qkv, a_q_gain, a_k_gain, a_w_o, kv_norm_g,
           w_ada_kv, b_ada_kv, w_kv, kv_k_gain, cmp_pos, phi_w1, phi_w2, b_w_qg, b_q_gain, b_w_o):
    b, s, d = x.shape
    depth = norm_g.shape[0]
    n_a = depth // 2
    hdim = N_HEADS * HEAD_DIM
    assert depth == 2 and b <= 8

    c8 = jnp.zeros((8, d), F32).at[:b].set(c)
    mod = _ada(c8, w_ada, b_ada.reshape(depth, 1, 9 * d))
    mod5 = mod[:, :b].reshape(depth, b, 9, 1, d)
    kvmod = _ada(c8, w_ada_kv[None], b_ada_kv.reshape(1, 1, 2 * d))
    kvmod5 = kvmod[:, :b].reshape(1, b, 2, 1, d)

    norm_g3 = norm_g.reshape(depth * 3, 1, d)
    w_in = ffn_w_in.astype(BF16)
    w_out = ffn_w_out.astype(BF16)
    cos, sin = _rope_tables(s)
    blk = jnp.arange(LANES) // HEAD_DIM
    bd = (blk[:, None] == blk[None, :]).astype(BF16)
    q_scale = HEAD_DIM ** -0.5 * LOG2E

    h = x
    for l in range(depth):
        h = _ffn(h, norm_g3, mod5, w_in, w_out, l, 0, 0)
        if l < n_a:
            gain = jnp.stack([jnp.tile(a_q_gain[l] * q_scale, (1, N_HEADS)),
                              jnp.tile(a_k_gain[l], (1, N_HEADS)),
                              jnp.ones((N_DIL, hdim), F32)], axis=1).reshape(3 * N_DIL, 1, hdim)
            w_qkv = a_w_qkv[l].astype(BF16)
            parts = [_dilated_attention(_qkv_proj(h, norm_g3, mod5, w_qkv, gain, cos, sin, bd, l, g), g)
                     for g in range(N_DIL)]
            h = _merge_out([p[0] for p in parts], [p[1] for p in parts], h, mod5, a_w_o[l].astype(BF16), l)
        else:
            j = l - n_a
            w_q = b_w_qg[j][:, :hdim]
            w_g = b_w_qg[j][:, hdim:].reshape(d, 3, N_KV_HEADS, GROUP).transpose(0, 2, 1, 3)
            w_g = jnp.pad(w_g.reshape(d, N_KV_HEADS, 3 * GROUP), ((0, 0), (0, 0), (0, LANES - 3 * GROUP)))
            w_qg = jnp.concatenate([w_q, w_g.reshape(d, N_KV_HEADS * LANES)], axis=1).astype(BF16)
            gain = jnp.tile(b_q_gain[j] * q_scale, N_HEADS)[None, :]
            qn, qr, gates = _qg_proj(h, norm_g3, mod5, w_qg, gain, cos, sin, bd, l)
            bound = HEAD_DIM * q_scale * jnp.max(jnp.abs(b_q_gain[j])) * jnp.max(jnp.abs(kv_k_gain))
            small = (bound <= SCORE_LIMIT).astype(jnp.int32).reshape(1)
            o = _nsa_attention(small, qn, qr, gates, *shared)
            h = _out_proj(o, h, mod5, b_w_o[j].astype(BF16), l)
        h = _ffn(h, norm_g3, mod5, w_in, w_out, l, 2, 1)
        if l == n_a - 1:
            col = lambda br, kv, kvh: br * 4 * HEAD_DIM + kv * 2 * HEAD_DIM + kvh * HEAD_DIM + jnp.arange(HEAD_DIM)
            dup = lambda br, kv, kvh: jnp.concatenate([col(br, kv, kvh)] * 2)
            cols = jnp.concatenate(
                [dup(br, 0, kvh) for br in (1, 2) for kvh in range(N_KV_HEADS)]
                + [dup(br, 1, kvh) for br in (1, 2) for kvh in range(N_KV_HEADS)]
                + [col(0, kv, kvh) for kv in range(2) for kvh in range(N_KV_HEADS)])
            w_kv_ext = w_kv[:, cols].astype(BF16)
            kgain = jnp.concatenate([jnp.tile(kv_k_gain[1], (2, 2)), jnp.tile(kv_k_gain[2], (2, 2)),
                                     jnp.ones((4, LANES), F32)], axis=0)
            ks, kw, vs, vw, kvc = _kv_proj(h, kv_norm_g[None, :], kvmod5, w_kv_ext, kgain, cos, sin, bd)
            nsb = s // CMP_STRIDE
            t4 = kvc.reshape(b, nsb, CMP_STRIDE, 4, HEAD_DIM).transpose(3, 0, 1, 2, 4)
            t4 = t4.reshape(4, b, nsb, CMP_STRIDE * HEAD_DIM)
            cmp4 = _compress(t4, phi_w1, phi_w2, cmp_pos.reshape(2, 1, CMP_LEN * HEAD_DIM), kv_k_gain[0][None, :])
            shared = (cmp4, ks, vs, kw, vw, _importance_weights(nsb).T)
    return h
```

```python
import functools

import jax
import jax.numpy as jnp
from jax import lax
from jax.experimental import pallas as pl
from jax.experimental.pallas import tpu as pltpu

HEAD_DIM = 64
N_HEADS = 16
DIL_WINDOWS = (128, 512, 2048)
DIL_RATES = (1, 4, 16)
N_DIL = 3
N_KV_HEADS = 2
GROUP = N_HEADS // N_KV_HEADS
CMP_STRIDE = 16
CMP_LEN = 32
SEL_BLOCK = 64
SEL_SHIFT = SEL_BLOCK.bit_length() - 1
N_SELECT = 16
SLIDE_WINDOW = 512
ROPE_THETA = 10000.0
EPS = 1e-6

LANES = 128
MXU_WIDTH = 256
Q_ROWS = 128
NEG = -1e30
LOG2E = 1.4426950408889634
SCORE_LIMIT = 60.0
VMEM_LIMIT = 56 * 1024 * 1024

BF16 = jnp.bfloat16
F32 = jnp.float32


def _params(*sem):
    return pltpu.CompilerParams(dimension_semantics=sem, vmem_limit_bytes=VMEM_LIMIT)


def _dot(a, b):
    return jnp.dot(a, b, preferred_element_type=F32)


def _dot_t(a, b):
    return lax.dot_general(a, b, (((1,), (1,)), ((), ())), preferred_element_type=F32)


def _split_dot(a, b):
    a_hi = a.astype(BF16)
    a_lo = (a - a_hi.astype(F32)).astype(BF16)
    b_hi = b.astype(BF16)
    b_lo = (b - b_hi.astype(F32)).astype(BF16)
    return _dot(a_hi, b_hi) + (_dot(a_hi, b_lo) + _dot(a_lo, b_hi))


def _sigmoid(x):
    return 1.0 / (1.0 + jnp.exp(-x))


def _normmod(x, g, shift, scale):
    y = x * lax.rsqrt(jnp.mean(x * x, axis=-1, keepdims=True) + EPS) * g
    return y * (1.0 + scale) + shift


def _head_rms(y, bd):
    ss = _dot((y * y).astype(BF16), bd)
    return y * lax.rsqrt(ss * (1.0 / HEAD_DIM) + EPS)


def _rope(z, cos, sin_signed):
    lane = lax.broadcasted_iota(jnp.int32, z.shape, 1)
    first_half = (lane & (HEAD_DIM - 1)) < HEAD_DIM // 2
    swapped = jnp.where(first_half, pltpu.roll(z, z.shape[1] - HEAD_DIM // 2, 1),
                        pltpu.roll(z, HEAD_DIM // 2, 1))
    return z * cos + swapped * sin_signed


def _ada_kernel(c_ref, w_ref, b_ref, o_ref):
    c = c_ref[...]
    a = (c * _sigmoid(c)).astype(BF16)
    o_ref[...] = _dot(a, w_ref[...].astype(BF16)) + b_ref[...]


def _ada(c8, w, b):
    nl, d, n = w.shape
    tn = 1024
    return pl.pallas_call(
        _ada_kernel,
        grid=(nl, n // tn),
        in_specs=[pl.BlockSpec((8, d), lambda l, j: (0, 0)),
                  pl.BlockSpec((None, d, tn), lambda l, j: (l, 0, j)),
                  pl.BlockSpec((None, 1, tn), lambda l, j: (l, 0, j))],
        out_specs=pl.BlockSpec((None, 8, tn), lambda l, j: (l, 0, j)),
        out_shape=jax.ShapeDtypeStruct((nl, 8, n), F32),
        compiler_params=_params("parallel", "parallel"),
        name="ada_table",
    )(c8, w, b)


def _mod_block(d, l, k):
    return pl.BlockSpec((None, None, None, 1, d), lambda b, *_: (l, b, k, 0, 0))


def _ffn_kernel(h_ref, g_ref, sh_ref, sc_ref, gt_ref, wg_ref, wu_ref, wo_ref, o_ref, u_scr, acc_scr):
    j = pl.program_id(2)

    @pl.when(j == 0)
    def _():
        u_scr[...] = _normmod(h_ref[0], g_ref[...], sh_ref[...], sc_ref[...]).astype(BF16)
        acc_scr[...] = jnp.zeros_like(acc_scr)

    u = u_scr[...]
    gate = _dot(u, wg_ref[...])
    up = _dot(u, wu_ref[...])
    act = (gate * _sigmoid(gate) * up).astype(BF16)
    acc_scr[...] += _dot(act, wo_ref[...])

    @pl.when(j == pl.num_programs(2) - 1)
    def _():
        o_ref[0] = h_ref[0] + (0.5 * (1.0 + gt_ref[...])) * acc_scr[...]


def _ffn(h, norm_g3, mod5, w_in, w_out, l, sub, which):
    b, s, d = h.shape
    f = w_out.shape[2]
    tm = min(1024, s)
    tf = 256
    nf = f // tf
    return pl.pallas_call(
        _ffn_kernel,
        grid=(b, s // tm, nf),
        in_specs=[pl.BlockSpec((1, tm, d), lambda bb, i, j: (bb, i, 0)),
                  pl.BlockSpec((None, 1, d), lambda bb, i, j: (l * 3 + sub, 0, 0)),
                  _mod_block(d, l, sub * 3 + 0),
                  _mod_block(d, l, sub * 3 + 1),
                  _mod_block(d, l, sub * 3 + 2),
                  pl.BlockSpec((None, None, d, tf), lambda bb, i, j: (l, which, 0, j)),
                  pl.BlockSpec((None, None, d, tf), lambda bb, i, j: (l, which, 0, j + nf)),
                  pl.BlockSpec((None, None, tf, d), lambda bb, i, j: (l, which, j, 0))],
        out_specs=pl.BlockSpec((1, tm, d), lambda bb, i, j: (bb, i, 0)),
        out_shape=jax.ShapeDtypeStruct((b, s, d), F32),
        scratch_shapes=[pltpu.VMEM((tm, d), BF16), pltpu.VMEM((tm, d), F32)],
        compiler_params=_params("parallel", "parallel", "arbitrary"),
        name="swiglu_half_step",
    )(h, norm_g3, mod5, mod5, mod5, w_in, w_in, w_out)


def _qkv_kernel(h_ref, g_ref, sh_ref, sc_ref, w_ref, gain_ref, cos_ref, sin_ref, bd_ref, o_ref, u_scr, uf_scr,
                *, rate):
    n = pl.program_id(2)
    tm = u_scr.shape[0]
    seg = tm // rate

    @pl.when(n == 0)
    def _():
        u = _normmod(h_ref[0], g_ref[...], sh_ref[...], sc_ref[...])
        if rate == 1:
            u_scr[...] = u.astype(BF16)
        else:
            for cb in range(u.shape[1] // LANES):
                sl = slice(cb * LANES, (cb + 1) * LANES)
                uf_scr[cb] = u[:, sl]
                for res in range(rate):
                    u_scr[res * seg:(res + 1) * seg, sl] = uf_scr[cb, pl.ds(res, seg, stride=rate), :].astype(BF16)

    y = _dot(u_scr[...], w_ref[...])

    @pl.when(n == 2)
    def _():
        for res in range(rate):
            o_ref[res] = y[res * seg:(res + 1) * seg].astype(BF16)

    @pl.when(n != 2)
    def _():
        cos = cos_ref[...]
        sin = sin_ref[...]
        bd = bd_ref[...]
        width = bd.shape[0]
        reps = width // LANES
        cos = jnp.concatenate([cos] * reps, axis=1)
        sin = jnp.concatenate([sin] * reps, axis=1)
        for cb in range(y.shape[1] // width):
            sl = slice(cb * width, (cb + 1) * width)
            z = _rope(_head_rms(y[:, sl], bd) * gain_ref[:, sl], cos, sin).astype(BF16)
            for res in range(rate):
                o_ref[res, :, sl] = z[res * seg:(res + 1) * seg]


def _qkv_proj(h, norm_g3, mod5, w, gain, cos, sin, bd, l, g):
    b, s, d = h.shape
    rate = DIL_RATES[g]
    tm = min(1024, s)
    seg = tm // rate
    tn = N_HEADS * HEAD_DIM

    def residue_major(t):
        return t.reshape(s // tm, seg, rate, LANES).transpose(0, 2, 1, 3).reshape(s, LANES)

    return pl.pallas_call(
        functools.partial(_qkv_kernel, rate=rate),
        grid=(b, s // tm, 3),
        in_specs=[pl.BlockSpec((1, tm, d), lambda bb, i, j: (bb, i, 0)),
                  pl.BlockSpec((None, 1, d), lambda bb, i, j: (l * 3 + 1, 0, 0)),
                  _mod_block(d, l, 3),
                  _mod_block(d, l, 4),
                  pl.BlockSpec((d, tn), lambda bb, i, j: (0, 3 * g + j)),
                  pl.BlockSpec((None, 1, tn), lambda bb, i, j: (3 * g + j, 0, 0)),
                  pl.BlockSpec((tm, LANES), lambda bb, i, j: (i, 0)),
                  pl.BlockSpec((tm, LANES), lambda bb, i, j: (i, 0)),
                  pl.BlockSpec(bd.shape, lambda bb, i, j: (0, 0))],
        out_specs=pl.BlockSpec((None, rate, seg, tn), lambda bb, i, j: (bb, 0, i, j)),
        out_shape=jax.ShapeDtypeStruct((b, rate, s // rate, 3 * tn), BF16),
        scratch_shapes=[pltpu.VMEM((tm, d), BF16), pltpu.VMEM((d // LANES, tm, LANES), F32)],
        compiler_params=_params("parallel", "parallel", "arbitrary"),
        name=f"qkv_proj_rate{rate}",
    )(h, norm_g3, mod5, mod5, w, gain, residue_major(cos), residue_major(sin), bd)


def _dil_body(q_ref, kp_ref, kc_ref, vp_ref, vc_ref, o_ref, lse_ref, *, tq, n_back, shifted):
    i = pl.program_id(2)
    hp = pl.program_id(3)
    lane = lax.broadcasted_iota(jnp.int32, (Q_ROWS, LANES), 1)
    low = lane < HEAD_DIM
    qi = lax.broadcasted_iota(jnp.int32, (2 * Q_ROWS, 2 * Q_ROWS), 0) & (Q_ROWS - 1)
    ki = lax.broadcasted_iota(jnp.int32, (2 * Q_ROWS, 2 * Q_ROWS), 1)
    dist = Q_ROWS + qi - ki
    band = (dist >= 0) & (dist <= n_back)
    seg = lane // (LANES // N_HEADS)
    for j in range(tq // Q_ROWS):
        rs = slice(j * Q_ROWS, (j + 1) * Q_ROWS)
        q = q_ref[rs, :]
        zero = jnp.zeros_like(q)
        q2 = jnp.concatenate([jnp.where(low, q, zero), jnp.where(low, zero, q)], axis=0)
        if j == 0:
            k = jnp.concatenate([kp_ref[...], kc_ref[0:Q_ROWS, :]], axis=0)
            v = jnp.concatenate([vp_ref[...], vc_ref[0:Q_ROWS, :]], axis=0)
        else:
            k = kc_ref[(j - 1) * Q_ROWS:(j + 1) * Q_ROWS, :]
            v = vc_ref[(j - 1) * Q_ROWS:(j + 1) * Q_ROWS, :]
        valid = band & (i * tq + (j - 1) * Q_ROWS + ki >= 0)
        s = _dot_t(q2, k)
        if shifted:
            s = jnp.where(valid, s, NEG)
            m = jnp.max(s, axis=-1, keepdims=True)
            p = jnp.exp2(s - m)
        else:
            p = jnp.where(valid, jnp.exp2(s), 0.0)
        den = jnp.sum(p, axis=-1, keepdims=True)
        o2 = _dot(p.astype(BF16), v) / den
        lse2 = jnp.log(den) * LOG2E
        if shifted:
            lse2 = lse2 + m
        o_ref[0, rs, :] = jnp.where(low, o2[0:Q_ROWS], o2[Q_ROWS:]).astype(BF16)
        old = jnp.where(hp == 0, 0.0, lse_ref[0, rs, :])
        lse_ref[0, rs, :] = jnp.where(seg == 2 * hp, lse2[0:Q_ROWS],
                                      jnp.where(seg == 2 * hp + 1, lse2[Q_ROWS:], old))


def _dil_kernel(small_ref, *refs, tq, n_back):
    @pl.when(small_ref[0] != 0)
    def _():
        _dil_body(*refs, tq=tq, n_back=n_back, shifted=False)

    @pl.when(small_ref[0] == 0)
    def _():
        _dil_body(*refs, tq=tq, n_back=n_back, shifted=True)


def _dilated_attention(small, qkv_r, g):
    b, rate, l, _ = qkv_r.shape
    s = l * rate
    n_back = DIL_WINDOWS[g] // rate
    assert n_back <= Q_ROWS and rate == DIL_RATES[g]
    tq = min(512, l)
    hdim = N_HEADS * HEAD_DIM
    hpr = hdim // LANES
    sub = tq // Q_ROWS

    def cur(off):
        return pl.BlockSpec((None, None, tq, LANES), lambda bb, r, i, hp: (bb, r, i, off + hp))

    def prev(off):
        return pl.BlockSpec((None, None, Q_ROWS, LANES),
                            lambda bb, r, i, hp: (bb, r, jnp.maximum(i * sub - 1, 0), off + hp))

    o, lse = pl.pallas_call(
        functools.partial(_dil_kernel, tq=tq, n_back=n_back),
        grid=(b, rate, l // tq, hpr),
        in_specs=[pl.BlockSpec(memory_space=pltpu.SMEM),
                  cur(0), prev(hpr), cur(hpr), prev(2 * hpr), cur(2 * hpr)],
        out_specs=[pl.BlockSpec((1, tq, LANES), lambda bb, r, i, hp: (bb, i, r * hpr + hp)),
                   pl.BlockSpec((1, tq, LANES), lambda bb, r, i, hp: (bb, i, r))],
        out_shape=[jax.ShapeDtypeStruct((b, l, rate * hdim), BF16),
                   jax.ShapeDtypeStruct((b, l, rate * LANES), F32)],
        compiler_params=_params("parallel", "parallel", "parallel", "arbitrary"),
        name=f"dilated_attention_rate{rate}",
    )(small, qkv_r, qkv_r, qkv_r, qkv_r, qkv_r)
    return o.reshape(b, s, hdim), lse.reshape(b, s, LANES)


def _merge_out_kernel(o0_ref, o1_ref, o2_ref, l0_ref, l1_ref, l2_ref, e_ref, h_ref, gt_ref, w_ref, out_ref):
    l0, l1, l2 = l0_ref[0], l1_ref[0], l2_ref[0]
    m = jnp.maximum(jnp.maximum(l0, l1), l2)
    e0, e1, e2 = jnp.exp2(l0 - m), jnp.exp2(l1 - m), jnp.exp2(l2 - m)
    inv = 1.0 / (e0 + e1 + e2)
    spread = e_ref[...]
    o = (_dot((e0 * inv).astype(BF16), spread) * o0_ref[0].astype(F32)
         + _dot((e1 * inv).astype(BF16), spread) * o1_ref[0].astype(F32)
         + _dot((e2 * inv).astype(BF16), spread) * o2_ref[0].astype(F32))
    out_ref[0] = h_ref[0] + (1.0 + gt_ref[...]) * _dot(o.astype(BF16), w_ref[...])


def _merge_out(os_, lses, h, mod5, w_o, l):
    b, s, d = h.shape
    hdim = w_o.shape[0]
    tm = min(512, s)
    row = lambda width: pl.BlockSpec((1, tm, width), lambda bb, i: (bb, i, 0))
    rep = LANES // N_HEADS
    lane = jnp.arange(LANES)[:, None]
    col = jnp.arange(hdim)[None, :]
    spread = ((lane % rep == 0) & (lane // rep == col // HEAD_DIM)).astype(BF16)
    return pl.pallas_call(
        _merge_out_kernel,
        grid=(b, s // tm),
        in_specs=[row(hdim)] * 3 + [row(LANES)] * 3
        + [pl.BlockSpec((LANES, hdim), lambda bb, i: (0, 0)), row(d), _mod_block(d, l, 5),
           pl.BlockSpec((hdim, d), lambda bb, i: (0, 0))],
        out_specs=row(d),
        out_shape=jax.ShapeDtypeStruct((b, s, d), F32),
        compiler_params=_params("parallel", "parallel"),
        name="merge_out_proj",
    )(*os_, *lses, spread, h, mod5, w_o)


def _out_kernel(o_ref, h_ref, gt_ref, w_ref, out_ref):
    out_ref[0] = h_ref[0] + (1.0 + gt_ref[...]) * _dot(o_ref[0], w_ref[...])


def _out_proj(o, h, mod5, w_o, l):
    b, s, d = h.shape
    hdim = w_o.shape[0]
    tm = min(1024, s)
    row = lambda width: pl.BlockSpec((1, tm, width), lambda bb, i: (bb, i, 0))
    return pl.pallas_call(
        _out_kernel,
        grid=(b, s // tm),
        in_specs=[row(hdim), row(d), _mod_block(d, l, 5), pl.BlockSpec((hdim, d), lambda bb, i: (0, 0))],
        out_specs=row(d),
        out_shape=jax.ShapeDtypeStruct((b, s, d), F32),
        compiler_params=_params("parallel", "parallel"),
        name="out_proj",
    )(o, h, mod5, w_o)


def _kv_kernel(h_ref, g_ref, sh_ref, sc_ref, w_ref, gain_ref, cos_ref, sin_ref, bd_ref,
               ks_ref, kw_ref, vs_ref, vw_ref, kvc_ref, *, tm):
    i = pl.program_id(1)
    u = _normmod(h_ref[0], g_ref[...], sh_ref[...], sc_ref[...]).astype(BF16)
    y = _dot(u, w_ref[...])
    cos = cos_ref[...]
    sin = sin_ref[...]
    bd = bd_ref[...]
    lane = lax.broadcasted_iota(jnp.int32, (tm, LANES), 1)
    pos = i * tm + lax.broadcasted_iota(jnp.int32, (tm, LANES), 0)
    onehot = jnp.where((pos >> SEL_SHIFT) == lane, 1.0, 0.0).astype(BF16)
    for j in range(4):
        z = _head_rms(y[:, j * LANES:(j + 1) * LANES], bd) * gain_ref[j:j + 1, :]
        zr = _rope(z, cos, sin).astype(BF16)
        if j < 2:
            ks_ref[0, :, 2 * j * LANES:(2 * j + 1) * LANES] = zr
            ks_ref[0, :, (2 * j + 1) * LANES:(2 * j + 2) * LANES] = onehot
        else:
            kw_ref[0, :, (j - 2) * LANES:(j - 1) * LANES] = zr
    for j in range(4):
        v = y[:, (4 + j) * LANES:(5 + j) * LANES]
        va = jnp.where(lane < HEAD_DIM, v, 1.0).astype(BF16)
        ref = vs_ref if j < 2 else vw_ref
        ref[0, :, (j % 2) * LANES:(j % 2 + 1) * LANES] = va
    kvc_ref[0] = y[:, 8 * LANES:10 * LANES]


def _kv_proj(h, kv_norm_g, kvmod5, w, gain, cos, sin, bd):
    b, s, d = h.shape
    n = w.shape[1]
    tm = min(512, s)
    row = lambda width, dt: (pl.BlockSpec((1, tm, width), lambda bb, i: (bb, i, 0)),
                             jax.ShapeDtypeStruct((b, s, width), dt))
    outs = [row(4 * LANES, BF16), row(2 * LANES, BF16), row(2 * LANES, BF16), row(2 * LANES, BF16),
            row(2 * LANES, F32)]
    return pl.pallas_call(
        functools.partial(_kv_kernel, tm=tm),
        grid=(b, s // tm),
        in_specs=[pl.BlockSpec((1, tm, d), lambda bb, i: (bb, i, 0)),
                  pl.BlockSpec((1, d), lambda bb, i: (0, 0)),
                  _mod_block(d, 0, 0),
                  _mod_block(d, 0, 1),
                  pl.BlockSpec((d, n), lambda bb, i: (0, 0)),
                  pl.BlockSpec((8, LANES), lambda bb, i: (0, 0)),
                  pl.BlockSpec((tm, LANES), lambda bb, i: (i, 0)),
                  pl.BlockSpec((tm, LANES), lambda bb, i: (i, 0)),
                  pl.BlockSpec((LANES, LANES), lambda bb, i: (0, 0))],
        out_specs=[o[0] for o in outs],
        out_shape=[o[1] for o in outs],
        compiler_params=_params("parallel", "parallel"),
        name="shared_kv_proj",
    )(h, kv_norm_g, kvmod5, kvmod5, w, gain, cos, sin, bd)


def _cmp_kernel(t_ref, w1_ref, w2_ref, pos_ref, gain_ref, o_ref):
    is_key = pl.program_id(0) < N_KV_HEADS
    t = t_ref[...]
    w1 = w1_ref[...]
    half = w1.shape[0] // 2
    first = _split_dot(t, w1[:half])
    second = _split_dot(t, w1[half:])
    second = pltpu.roll(second, second.shape[0] - 1, 0)
    posb = _split_dot(jnp.broadcast_to(pos_ref[...], (8, w1.shape[0])), w1)[0:1]
    hid = first + second + posb
    out = _split_dot(hid * _sigmoid(hid), w2_ref[...])
    key = out * lax.rsqrt(jnp.mean(out * out, axis=-1, keepdims=True) + EPS) * gain_ref[...]
    dup = jnp.concatenate([key, key], axis=-1)
    aug = jnp.concatenate([out, jnp.ones_like(out)], axis=-1)
    o_ref[...] = jnp.where(is_key, dup, aug).astype(BF16)


def _compress(t4, phi_w1, phi_w2, cmp_pos, gain):
    _, b, nsb, width = t4.shape
    hid = phi_w1.shape[2]
    return pl.pallas_call(
        _cmp_kernel,
        grid=(4, b),
        in_specs=[pl.BlockSpec((None, None, nsb, width), lambda t, bb: (t, bb, 0, 0)),
                  pl.BlockSpec((None, CMP_LEN * HEAD_DIM, hid), lambda t, bb: (t // N_KV_HEADS, 0, 0)),
                  pl.BlockSpec((None, hid, HEAD_DIM), lambda t, bb: (t // N_KV_HEADS, 0, 0)),
                  pl.BlockSpec((None, 1, CMP_LEN * HEAD_DIM), lambda t, bb: (t // N_KV_HEADS, 0, 0)),
                  pl.BlockSpec((1, HEAD_DIM), lambda t, bb: (0, 0))],
        out_specs=pl.BlockSpec((None, None, nsb, LANES), lambda t, bb: (t, bb, 0, 0)),
        out_shape=jax.ShapeDtypeStruct((4, b, nsb, LANES), BF16),
        compiler_params=_params("parallel", "parallel"),
        name="compress_blocks",
    )(t4, phi_w1, phi_w2, cmp_pos, gain)


def _qg_kernel(h_ref, g_ref, sh_ref, sc_ref, w_ref, gain_ref, cos_ref, sin_ref, bd_ref,
               qn_ref, qr_ref, gate_ref):
    u = _normmod(h_ref[0], g_ref[...], sh_ref[...], sc_ref[...]).astype(BF16)
    y = _dot(u, w_ref[...])
    cos = cos_ref[...]
    sin = sin_ref[...]
    bd = bd_ref[...]
    hdim = qn_ref.shape[2]
    for cb in range(hdim // LANES):
        sl = slice(cb * LANES, (cb + 1) * LANES)
        z = _head_rms(y[:, sl], bd) * gain_ref[:, sl]
        qn_ref[0, :, sl] = z.astype(BF16)
        qr_ref[0, :, sl] = _rope(z, cos, sin).astype(BF16)
    gate_ref[0] = _sigmoid(y[:, hdim:])


def _qg_proj(h, norm_g3, mod5, w, gain, cos, sin, bd, l):
    b, s, d = h.shape
    n = w.shape[1]
    hdim = N_HEADS * HEAD_DIM
    tm = min(512, s)
    row = lambda width, dt: (pl.BlockSpec((1, tm, width), lambda bb, i: (bb, i, 0)),
                             jax.ShapeDtypeStruct((b, s, width), dt))
    outs = [row(hdim, BF16), row(hdim, BF16), row(n - hdim, F32)]
    return pl.pallas_call(
        _qg_kernel,
        grid=(b, s // tm),
        in_specs=[pl.BlockSpec((1, tm, d), lambda bb, i: (bb, i, 0)),
                  pl.BlockSpec((None, 1, d), lambda bb, i: (l * 3 + 1, 0, 0)),
                  _mod_block(d, l, 3),
                  _mod_block(d, l, 4),
                  pl.BlockSpec((d, n), lambda bb, i: (0, 0)),
                  pl.BlockSpec((1, hdim), lambda bb, i: (0, 0)),
                  pl.BlockSpec((tm, LANES), lambda bb, i: (i, 0)),
                  pl.BlockSpec((tm, LANES), lambda bb, i: (i, 0)),
                  pl.BlockSpec((LANES, LANES), lambda bb, i: (0, 0))],
        out_specs=[o[0] for o in outs],
        out_shape=[o[1] for o in outs],
        compiler_params=_params("parallel", "parallel"),
        name="nsa_query_gate_proj",
    )(h, norm_g3, mod5, mod5, w, gain, cos, sin, bd)


def _nsa_body(qn_ref, qr_ref, gate_ref, kc_ref, vc_ref, ks_ref, vs_ref, kw_ref, vw_ref, wimpt_ref,
              o_ref, qa_scr, m_scr, acc_scr, *, tq, shifted):
    i = pl.program_id(2)
    start = i * tq
    rows = GROUP * tq
    ncp = kc_ref.shape[0]
    low = lax.broadcasted_iota(jnp.int32, (tq, LANES), 1) < HEAD_DIM
    low_r = lax.broadcasted_iota(jnp.int32, (rows, LANES), 1) < HEAD_DIM

    def accumulate(s, v, mask, first):
        if not shifted:
            p = jnp.exp2(s)
            if mask is not None:
                p = jnp.where(mask, p, 0.0)
            pv = _dot(p.astype(BF16), v)
            acc_scr[...] = pv if first else acc_scr[...] + pv
            return
        if mask is not None:
            s = jnp.where(mask, s, NEG)
        if first:
            m_new = jnp.max(s, axis=-1, keepdims=True)
            acc_scr[...] = _dot(jnp.exp2(s - m_new).astype(BF16), v)
        else:
            m_old = m_scr[...]
            m_new = jnp.maximum(m_old, jnp.max(s, axis=-1, keepdims=True))
            acc_scr[...] = jnp.exp2(m_old - m_new) * acc_scr[...] + _dot(jnp.exp2(s - m_new).astype(BF16), v)
        m_scr[...] = m_new

    def stack(q):
        parts = []
        for g in range(GROUP):
            blk = q[:, (g // 2) * LANES:(g // 2 + 1) * LANES]
            parts.append(jnp.where(low if g % 2 == 0 else ~low, blk, jnp.zeros_like(blk)))
        return parts

    qn = jnp.concatenate(stack(qn_ref[0]), axis=0)
    sc = _dot_t(qn, kc_ref[...])
    c_end = lax.broadcasted_iota(jnp.int32, (rows, ncp), 1) * CMP_STRIDE + (CMP_LEN - 1)
    valid = c_end <= start + (lax.broadcasted_iota(jnp.int32, (rows, ncp), 0) & (tq - 1))
    if shifted:
        sc = jnp.where(valid, sc, NEG)
        sc = sc - jnp.max(sc, axis=-1, keepdims=True)
    p = jnp.where(valid, jnp.exp2(sc), 0.0)
    oc = _dot(p.astype(BF16), vc_ref[...])
    inv = 1.0 / jnp.maximum(jnp.where(low_r, pltpu.roll(oc, HEAD_DIM, 1), oc), 1e-30)
    o_cmp = oc * inv
    psum_blocks = []
    for cb in range(ncp // LANES):
        cs = slice(cb * LANES, (cb + 1) * LANES)
        tot = p[0:tq, cs] * inv[0:tq]
        for g in range(1, GROUP):
            tot = tot + p[g * tq:(g + 1) * tq, cs] * inv[g * tq:(g + 1) * tq]
        psum_blocks.append(tot)
    psum = jnp.concatenate(psum_blocks, axis=1)

    p_hi = psum.astype(BF16)
    p_lo = (psum - p_hi.astype(F32)).astype(BF16)
    imp_t = _dot_t(wimpt_ref[...], p_hi) + _dot_t(wimpt_ref[...], p_lo)
    blk_id = lax.broadcasted_iota(jnp.int32, (LANES, tq), 0)
    jt = (start + lax.broadcasted_iota(jnp.int32, (LANES, tq), 1)) >> SEL_SHIFT
    forced = (blk_id == 0) | (blk_id == jt) | (blk_id == jt - 1)
    work = jnp.where(blk_id > jt, -1.0, jnp.where(forced, jnp.inf, imp_t))
    sel = blk_id < 0
    for _ in range(N_SELECT):
        mx = jnp.max(work, axis=0, keepdims=True)
        idx = jnp.min(jnp.where(work == mx, blk_id, LANES), axis=0, keepdims=True)
        pick = blk_id == idx
        sel = sel | pick
        work = jnp.where(pick, -2.0, work)
    sel_bias = jnp.transpose(jnp.where(sel, 0.0, NEG)).astype(BF16)

    qr_parts = stack(qr_ref[0])
    for g in range(GROUP):
        qa_scr[g * tq:(g + 1) * tq, 0:LANES] = qr_parts[g]
        qa_scr[g * tq:(g + 1) * tq, LANES:2 * LANES] = sel_bias

    tri = (lax.broadcasted_iota(jnp.int32, (rows, tq), 1)
           <= (lax.broadcasted_iota(jnp.int32, (rows, tq), 0) & (tq - 1)))

    def key_block(kb):
        return pl.ds(pl.multiple_of(kb * tq, tq), tq)

    diag = key_block(i)
    accumulate(_dot_t(qa_scr[...], ks_ref[diag, :]), vs_ref[diag, :], tri, True)

    def slc_pair(kp, carry):
        blk = pl.ds(pl.multiple_of(kp * 2 * tq, 2 * tq), 2 * tq)
        accumulate(_dot_t(qa_scr[...], ks_ref[blk, :]), vs_ref[blk, :], None, False)
        return carry

    lax.fori_loop(0, i // 2, slc_pair, 0)

    @pl.when(i % 2 == 1)
    def _():
        blk = key_block(i - 1)
        accumulate(_dot_t(qa_scr[...], ks_ref[blk, :]), vs_ref[blk, :], None, False)

    a_slc = acc_scr[...]

    qr = qa_scr[:, 0:LANES]
    accumulate(_dot_t(qr, kw_ref[diag, :]), vw_ref[diag, :], tri, True)

    @pl.when(i >= 1)
    def _():
        blk = key_block(i - 1)
        accumulate(_dot_t(qr, kw_ref[blk, :]), vw_ref[blk, :], None, False)

    @pl.when(i >= 2)
    def _():
        blk = key_block(i - 2)
        accumulate(_dot_t(qr, kw_ref[blk, :]), vw_ref[blk, :], ~tri, False)

    a_win = acc_scr[...]

    gates = gate_ref[0]
    for pr in range(GROUP // 2):
        vals = []
        for hh in range(2):
            g = 2 * pr + hh
            rs = slice(g * tq, (g + 1) * tq)
            g_cmp, g_slc, g_win = (gates[:, br * GROUP + g:br * GROUP + g + 1] for br in range(3))
            o_c, a_s, a_w = o_cmp[rs], a_slc[rs], a_win[rs]
            r_s = pltpu.roll(a_s, HEAD_DIM, 1)
            r_w = pltpu.roll(a_w, HEAD_DIM, 1)
            if hh == 0:
                vals.append(g_cmp * o_c + g_slc * (a_s / r_s) + g_win * (a_w / r_w))
            else:
                vals.append(g_cmp * pltpu.roll(o_c, HEAD_DIM, 1) + g_slc * (r_s / a_s) + g_win * (r_w / a_w))
        o_ref[0, :, pr * LANES:(pr + 1) * LANES] = jnp.where(low, vals[0], vals[1]).astype(BF16)


def _nsa_kernel(small_ref, *refs, tq):
    @pl.when(small_ref[0] != 0)
    def _():
        _nsa_body(*refs, tq=tq, shifted=False)

    @pl.when(small_ref[0] == 0)
    def _():
        _nsa_body(*refs, tq=tq, shifted=True)


def _nsa_attention(small, qn, qr, gates, cmp4, ks, vs, kw, vw, wimpt):
    b, s, hdim = qn.shape
    tq = min(256, s)
    assert s // SEL_BLOCK <= LANES and s % tq == 0 and SLIDE_WINDOW == 2 * tq
    ncp = cmp4.shape[2]
    gw = GROUP * HEAD_DIM
    rows = GROUP * tq
    qspec = pl.BlockSpec((1, tq, gw), lambda bb, kv, i: (bb, i, kv))
    full = lambda width: pl.BlockSpec((None, s, width), lambda bb, kv, i: (bb, 0, kv))
    return pl.pallas_call(
        functools.partial(_nsa_kernel, tq=tq),
        grid=(b, N_KV_HEADS, s // tq),
        in_specs=[pl.BlockSpec(memory_space=pltpu.SMEM), qspec, qspec,
                  pl.BlockSpec((1, tq, LANES), lambda bb, kv, i: (bb, i, kv)),
                  pl.BlockSpec((None, None, ncp, LANES), lambda bb, kv, i: (kv, bb, 0, 0)),
                  pl.BlockSpec((None, None, ncp, LANES), lambda bb, kv, i: (N_KV_HEADS + kv, bb, 0, 0)),
                  full(2 * LANES), full(LANES), full(LANES), full(LANES),
                  pl.BlockSpec((LANES, ncp), lambda bb, kv, i: (0, 0))],
        out_specs=qspec,
        out_shape=jax.ShapeDtypeStruct((b, s, hdim), BF16),
        scratch_shapes=[pltpu.VMEM((rows, 2 * LANES), BF16),
                        pltpu.VMEM((rows, 1), F32),
                        pltpu.VMEM((rows, LANES), F32)],
        compiler_params=_params("parallel", "parallel", "arbitrary"),
        name="nsa_attention",
    )(small, qn, qr, gates, cmp4, cmp4, ks, vs, kw, vw, wimpt)


def _rope_tables(s):
    half = HEAD_DIM // 2
    inv = ROPE_THETA ** (-jnp.arange(half, dtype=F32) / half)
    ang = jnp.arange(s, dtype=F32)[:, None] * inv[None, :]
    cos, sin = jnp.cos(ang), jnp.sin(ang)
    reps = LANES // HEAD_DIM
    return jnp.tile(cos, (1, 2 * reps)), jnp.tile(jnp.concatenate([-sin, sin], axis=1), (1, reps))


def _importance_weights(ncp):
    r = SEL_BLOCK // CMP_STRIDE
    nper = CMP_LEN // CMP_STRIDE
    c = jnp.arange(ncp)[:, None]
    j = jnp.arange(LANES)[None, :]
    o = c - r * j
    w = jnp.zeros((ncp, LANES), F32)
    for off in range(-(nper - 1), r):
        cnt = sum(1 for m in range(r) for n in range(nper) if m - n == off)
        w = w + jnp.where(o == off, float(cnt), 0.0)
    return w.astype(BF16)


def kernel(x, c, norm_g, w_ada, b_ada, ffn_w_in, ffn_w_out, a_w_qkv, a_q_gain, a_k_gain, a_w_o, kv_norm_g,
           w_ada_kv, b_ada_kv, w_kv, kv_k_gain, cmp_pos, phi_w1, phi_w2, b_w_qg, b_q_gain, b_w_o):
    b, s, d = x.shape
    depth = norm_g.shape[0]
    n_a = depth // 2
    hdim = N_HEADS * HEAD_DIM
    assert depth == 2 and b <= 8

    c8 = jnp.zeros((8, d), F32).at[:b].set(c)
    mod = _ada(c8, w_ada, b_ada.reshape(depth, 1, 9 * d))
    mod5 = mod[:, :b].reshape(depth, b, 9, 1, d)
    kvmod = _ada(c8, w_ada_kv[None], b_ada_kv.reshape(1, 1, 2 * d))
    kvmod5 = kvmod[:, :b].reshape(1, b, 2, 1, d)

    norm_g3 = norm_g.reshape(depth * 3, 1, d)
    w_in = ffn_w_in.astype(BF16)
    w_out = ffn_w_out.astype(BF16)
    cos, sin = _rope_tables(s)
    blk = jnp.arange(MXU_WIDTH) // HEAD_DIM
    bd_wide = (blk[:, None] == blk[None, :]).astype(BF16)
    bd = bd_wide[:LANES, :LANES]
    q_scale = HEAD_DIM ** -0.5 * LOG2E

    h = x
    for l in range(depth):
        h = _ffn(h, norm_g3, mod5, w_in, w_out, l, 0, 0)
        if l < n_a:
            gain = jnp.stack([jnp.tile(a_q_gain[l] * q_scale, (1, N_HEADS)),
                              jnp.tile(a_k_gain[l], (1, N_HEADS)),
                              jnp.ones((N_DIL, hdim), F32)], axis=1).reshape(3 * N_DIL, 1, hdim)
            w_qkv = a_w_qkv[l].astype(BF16)
            bound = HEAD_DIM * q_scale * jnp.max(jnp.abs(a_q_gain[l])) * jnp.max(jnp.abs(a_k_gain[l]))
            small = (bound <= SCORE_LIMIT).astype(jnp.int32).reshape(1)
            parts = [_dilated_attention(small, _qkv_proj(h, norm_g3, mod5, w_qkv, gain, cos, sin, bd_wide, l, g), g)
                     for g in range(N_DIL)]
            h = _merge_out([p[0] for p in parts], [p[1] for p in parts], h, mod5, a_w_o[l].astype(BF16), l)
        else:
            j = l - n_a
            w_q = b_w_qg[j][:, :hdim]
            w_g = b_w_qg[j][:, hdim:].reshape(d, 3, N_KV_HEADS, GROUP).transpose(0, 2, 1, 3)
            w_g = jnp.pad(w_g.reshape(d, N_KV_HEADS, 3 * GROUP), ((0, 0), (0, 0), (0, LANES - 3 * GROUP)))
            w_qg = jnp.concatenate([w_q, w_g.reshape(d, N_KV_HEADS * LANES)], axis=1).astype(BF16)
            gain = jnp.tile(b_q_gain[j] * q_scale, N_HEADS)[None, :]
            qn, qr, gates = _qg_proj(h, norm_g3, mod5, w_qg, gain, cos, sin, bd, l)
            bound = HEAD_DIM * q_scale * jnp.max(jnp.abs(b_q_gain[j])) * jnp.max(jnp.abs(kv_k_gain))
            small = (bound <= SCORE_LIMIT).astype(jnp.int32).reshape(1)
            o = _nsa_attention(small, qn, qr, gates, *shared)
            h = _out_proj(o, h, mod5, b_w_o[j].astype(BF16), l)
        h = _ffn(h, norm_g3, mod5, w_in, w_out, l, 2, 1)
        if l == n_a - 1:
            col = lambda br, kv, kvh: br * 4 * HEAD_DIM + kv * 2 * HEAD_DIM + kvh * HEAD_DIM + jnp.arange(HEAD_DIM)
            dup = lambda br, kv, kvh: jnp.concatenate([col(br, kv, kvh)] * 2)
            cols = jnp.concatenate(
                [dup(br, 0, kvh) for br in (1, 2) for kvh in range(N_KV_HEADS)]
                + [dup(br, 1, kvh) for br in (1, 2) for kvh in range(N_KV_HEADS)]
                + [col(0, kv, kvh) for kv in range(2) for kvh in range(N_KV_HEADS)])
            w_kv_ext = w_kv[:, cols].astype(BF16)
            kgain = jnp.concatenate([jnp.tile(kv_k_gain[1], (2, 2)), jnp.tile(kv_k_gain[2], (2, 2)),
                                     jnp.ones((4, LANES), F32)], axis=0)
            ks, kw, vs, vw, kvc = _kv_proj(h, kv_norm_g[None, :], kvmod5, w_kv_ext, kgain, cos, sin, bd)
            nsb = s // CMP_STRIDE
            t4 = kvc.reshape(b, nsb, CMP_STRIDE, 4, HEAD_DIM).transpose(3, 0, 1, 2, 4)
            t4 = t4.reshape(4, b, nsb, CMP_STRIDE * HEAD_DIM)
            cmp4 = _compress(t4, phi_w1, phi_w2, cmp_pos.reshape(2, 1, CMP_LEN * HEAD_DIM), kv_k_gain[0][None, :])
            shared = (cmp4, ks, vs, kw, vw, _importance_weights(nsb).T)
    return h
```

```python
import functools

import jax
import jax.numpy as jnp
from jax import lax
from jax.experimental import pallas as pl
from jax.experimental.pallas import tpu as pltpu

HEAD_DIM = 64
N_HEADS = 16
DIL_WINDOWS = (128, 512, 2048)
DIL_RATES = (1, 4, 16)
N_DIL = 3
N_KV_HEADS = 2
GROUP = N_HEADS // N_KV_HEADS
CMP_STRIDE = 16
CMP_LEN = 32
SEL_BLOCK = 64
SEL_SHIFT = SEL_BLOCK.bit_length() - 1
N_SELECT = 16
SLIDE_WINDOW = 512
ROPE_THETA = 10000.0
EPS = 1e-6

LANES = 128
MXU_WIDTH = 256
Q_ROWS = 128
NEG = -1e30
LOG2E = 1.4426950408889634
SCORE_LIMIT = 60.0
VMEM_LIMIT = 56 * 1024 * 1024

BF16 = jnp.bfloat16
F32 = jnp.float32


def _params(*sem):
    return pltpu.CompilerParams(dimension_semantics=sem, vmem_limit_bytes=VMEM_LIMIT)


def _dot(a, b):
    return jnp.dot(a, b, preferred_element_type=F32)


def _dot_t(a, b):
    return lax.dot_general(a, b, (((1,), (1,)), ((), ())), preferred_element_type=F32)


def _split_dot(a, b):
    a_hi = a.astype(BF16)
    a_lo = (a - a_hi.astype(F32)).astype(BF16)
    b_hi = b.astype(BF16)
    b_lo = (b - b_hi.astype(F32)).astype(BF16)
    return _dot(a_hi, b_hi) + (_dot(a_hi, b_lo) + _dot(a_lo, b_hi))


def _sigmoid(x):
    return 1.0 / (1.0 + jnp.exp(-x))


def _normmod(x, g, shift, scale):
    y = x * lax.rsqrt(jnp.mean(x * x, axis=-1, keepdims=True) + EPS) * g
    return y * (1.0 + scale) + shift


def _head_rms(y, bd):
    ss = _dot((y * y).astype(BF16), bd)
    return y * lax.rsqrt(ss * (1.0 / HEAD_DIM) + EPS)


def _rope(z, cos, sin_signed):
    lane = lax.broadcasted_iota(jnp.int32, z.shape, 1)
    first_half = (lane & (HEAD_DIM - 1)) < HEAD_DIM // 2
    swapped = jnp.where(first_half, pltpu.roll(z, z.shape[1] - HEAD_DIM // 2, 1),
                        pltpu.roll(z, HEAD_DIM // 2, 1))
    return z * cos + swapped * sin_signed


def _ada_kernel(c_ref, w_ref, b_ref, o_ref):
    c = c_ref[...]
    a = (c * _sigmoid(c)).astype(BF16)
    o_ref[...] = _dot(a, w_ref[...].astype(BF16)) + b_ref[...]


def _ada(c8, w, b):
    nl, d, n = w.shape
    tn = 1024
    return pl.pallas_call(
        _ada_kernel,
        grid=(nl, n // tn),
        in_specs=[pl.BlockSpec((8, d), lambda l, j: (0, 0)),
                  pl.BlockSpec((None, d, tn), lambda l, j: (l, 0, j)),
                  pl.BlockSpec((None, 1, tn), lambda l, j: (l, 0, j))],
        out_specs=pl.BlockSpec((None, 8, tn), lambda l, j: (l, 0, j)),
        out_shape=jax.ShapeDtypeStruct((nl, 8, n), F32),
        compiler_params=_params("parallel", "parallel"),
        name="ada_table",
    )(c8, w, b)


def _mod_block(d, l, k):
    return pl.BlockSpec((None, None, None, 1, d), lambda b, *_: (l, b, k, 0, 0))


def _ffn_kernel(h_ref, g_ref, sh_ref, sc_ref, gt_ref, wg_ref, wu_ref, wo_ref, o_ref, u_scr, acc_scr):
    j = pl.program_id(2)

    @pl.when(j == 0)
    def _():
        u_scr[...] = _normmod(h_ref[0], g_ref[...], sh_ref[...], sc_ref[...]).astype(BF16)
        acc_scr[...] = jnp.zeros_like(acc_scr)

    u = u_scr[...]
    gate = _dot(u, wg_ref[...])
    up = _dot(u, wu_ref[...])
    act = (gate * _sigmoid(gate) * up).astype(BF16)
    acc_scr[...] += _dot(act, wo_ref[...])

    @pl.when(j == pl.num_programs(2) - 1)
    def _():
        o_ref[0] = h_ref[0] + (0.5 * (1.0 + gt_ref[...])) * acc_scr[...]


def _ffn(h, norm_g3, mod5, w_in, w_out, l, sub, which):
    b, s, d = h.shape
    f = w_out.shape[2]
    tm = min(1024, s)
    tf = 256
    nf = f // tf
    return pl.pallas_call(
        _ffn_kernel,
        grid=(b, s // tm, nf),
        in_specs=[pl.BlockSpec((1, tm, d), lambda bb, i, j: (bb, i, 0)),
                  pl.BlockSpec((None, 1, d), lambda bb, i, j: (l * 3 + sub, 0, 0)),
                  _mod_block(d, l, sub * 3 + 0),
                  _mod_block(d, l, sub * 3 + 1),
                  _mod_block(d, l, sub * 3 + 2),
                  pl.BlockSpec((None, None, d, tf), lambda bb, i, j: (l, which, 0, j)),
                  pl.BlockSpec((None, None, d, tf), lambda bb, i, j: (l, which, 0, j + nf)),
                  pl.BlockSpec((None, None, tf, d), lambda bb, i, j: (l, which, j, 0))],
        out_specs=pl.BlockSpec((1, tm, d), lambda bb, i, j: (bb, i, 0)),
        out_shape=jax.ShapeDtypeStruct((b, s, d), F32),
        scratch_shapes=[pltpu.VMEM((tm, d), BF16), pltpu.VMEM((tm, d), F32)],
        compiler_params=_params("parallel", "parallel", "arbitrary"),
        name="swiglu_half_step",
    )(h, norm_g3, mod5, mod5, mod5, w_in, w_in, w_out)


def _qkv_kernel(h_ref, g_ref, sh_ref, sc_ref, w_ref, gain_ref, cos_ref, sin_ref, bd_ref, o_ref, u_scr, uf_scr,
                *, rate):
    n = pl.program_id(2)
    tm = u_scr.shape[0]
    seg = tm // rate

    @pl.when(n == 0)
    def _():
        u = _normmod(h_ref[0], g_ref[...], sh_ref[...], sc_ref[...])
        if rate == 1:
            u_scr[...] = u.astype(BF16)
        else:
            for cb in range(u.shape[1] // LANES):
                sl = slice(cb * LANES, (cb + 1) * LANES)
                uf_scr[cb] = u[:, sl]
                for res in range(rate):
                    u_scr[res * seg:(res + 1) * seg, sl] = uf_scr[cb, pl.ds(res, seg, stride=rate), :].astype(BF16)

    y = _dot(u_scr[...], w_ref[...])

    @pl.when(n == 2)
    def _():
        for res in range(rate):
            o_ref[res] = y[res * seg:(res + 1) * seg].astype(BF16)

    @pl.when(n != 2)
    def _():
        cos = cos_ref[...]
        sin = sin_ref[...]
        bd = bd_ref[...]
        width = bd.shape[0]
        reps = width // LANES
        cos = jnp.concatenate([cos] * reps, axis=1)
        sin = jnp.concatenate([sin] * reps, axis=1)
        for cb in range(y.shape[1] // width):
            sl = slice(cb * width, (cb + 1) * width)
            z = _rope(_head_rms(y[:, sl], bd) * gain_ref[:, sl], cos, sin).astype(BF16)
            for res in range(rate):
                o_ref[res, :, sl] = z[res * seg:(res + 1) * seg]


def _qkv_proj(h, norm_g3, mod5, w, gain, cos, sin, bd, l, g):
    b, s, d = h.shape
    rate = DIL_RATES[g]
    tm = min(1024, s)
    seg = tm // rate
    tn = N_HEADS * HEAD_DIM

    def residue_major(t):
        return t.reshape(s // tm, seg, rate, LANES).transpose(0, 2, 1, 3).reshape(s, LANES)

    return pl.pallas_call(
        functools.partial(_qkv_kernel, rate=rate),
        grid=(b, s // tm, 3),
        in_specs=[pl.BlockSpec((1, tm, d), lambda bb, i, j: (bb, i, 0)),
                  pl.BlockSpec((None, 1, d), lambda bb, i, j: (l * 3 + 1, 0, 0)),
                  _mod_block(d, l, 3),
                  _mod_block(d, l, 4),
                  pl.BlockSpec((d, tn), lambda bb, i, j: (0, 3 * g + j)),
                  pl.BlockSpec((None, 1, tn), lambda bb, i, j: (3 * g + j, 0, 0)),
                  pl.BlockSpec((tm, LANES), lambda bb, i, j: (i, 0)),
                  pl.BlockSpec((tm, LANES), lambda bb, i, j: (i, 0)),
                  pl.BlockSpec(bd.shape, lambda bb, i, j: (0, 0))],
        out_specs=pl.BlockSpec((None, rate, seg, tn), lambda bb, i, j: (bb, 0, i, j)),
        out_shape=jax.ShapeDtypeStruct((b, rate, s // rate, 3 * tn), BF16),
        scratch_shapes=[pltpu.VMEM((tm, d), BF16), pltpu.VMEM((d // LANES, tm, LANES), F32)],
        compiler_params=_params("parallel", "parallel", "arbitrary"),
        name=f"qkv_proj_rate{rate}",
    )(h, norm_g3, mod5, mod5, w, gain, residue_major(cos), residue_major(sin), bd)


def _dil_body(q_ref, kp_ref, kc_ref, vp_ref, vc_ref, o_ref, lse_ref, *, tq, n_back, shifted):
    i = pl.program_id(2)
    hp = pl.program_id(3)
    lane = lax.broadcasted_iota(jnp.int32, (Q_ROWS, LANES), 1)
    low = lane < HEAD_DIM
    qi = lax.broadcasted_iota(jnp.int32, (2 * Q_ROWS, 2 * Q_ROWS), 0) & (Q_ROWS - 1)
    ki = lax.broadcasted_iota(jnp.int32, (2 * Q_ROWS, 2 * Q_ROWS), 1)
    dist = Q_ROWS + qi - ki
    band = (dist >= 0) & (dist <= n_back)
    seg = lane // (LANES // N_HEADS)
    for j in range(tq // Q_ROWS):
        rs = slice(j * Q_ROWS, (j + 1) * Q_ROWS)
        q = q_ref[rs, :]
        zero = jnp.zeros_like(q)
        q2 = jnp.concatenate([jnp.where(low, q, zero), jnp.where(low, zero, q)], axis=0)
        if j == 0:
            k = jnp.concatenate([kp_ref[...], kc_ref[0:Q_ROWS, :]], axis=0)
            v = jnp.concatenate([vp_ref[...], vc_ref[0:Q_ROWS, :]], axis=0)
        else:
            k = kc_ref[(j - 1) * Q_ROWS:(j + 1) * Q_ROWS, :]
            v = vc_ref[(j - 1) * Q_ROWS:(j + 1) * Q_ROWS, :]
        valid = band & (i * tq + (j - 1) * Q_ROWS + ki >= 0)
        s = _dot_t(q2, k)
        if shifted:
            s = jnp.where(valid, s, NEG)
            m = jnp.max(s, axis=-1, keepdims=True)
            p = jnp.exp2(s - m)
        else:
            p = jnp.where(valid, jnp.exp2(s), 0.0)
        den = jnp.sum(p, axis=-1, keepdims=True)
        o2 = _dot(p.astype(BF16), v) / den
        lse2 = jnp.log(den) * LOG2E
        if shifted:
            lse2 = lse2 + m
        o_ref[0, rs, :] = jnp.where(low, o2[0:Q_ROWS], o2[Q_ROWS:]).astype(BF16)
        old = jnp.where(hp == 0, 0.0, lse_ref[0, rs, :])
        lse_ref[0, rs, :] = jnp.where(seg == 2 * hp, lse2[0:Q_ROWS],
                                      jnp.where(seg == 2 * hp + 1, lse2[Q_ROWS:], old))


def _dil_kernel(small_ref, *refs, tq, n_back):
    @pl.when(small_ref[0] != 0)
    def _():
        _dil_body(*refs, tq=tq, n_back=n_back, shifted=False)

    @pl.when(small_ref[0] == 0)
    def _():
        _dil_body(*refs, tq=tq, n_back=n_back, shifted=True)


def _dilated_attention(small, qkv_r, g):
    b, rate, l, _ = qkv_r.shape
    s = l * rate
    n_back = DIL_WINDOWS[g] // rate
    assert n_back <= Q_ROWS and rate == DIL_RATES[g]
    tq = min(512, l)
    hdim = N_HEADS * HEAD_DIM
    hpr = hdim // LANES
    sub = tq // Q_ROWS

    def cur(off):
        return pl.BlockSpec((None, None, tq, LANES), lambda bb, r, i, hp: (bb, r, i, off + hp))

    def prev(off):
        return pl.BlockSpec((None, None, Q_ROWS, LANES),
                            lambda bb, r, i, hp: (bb, r, jnp.maximum(i * sub - 1, 0), off + hp))

    o, lse = pl.pallas_call(
        functools.partial(_dil_kernel, tq=tq, n_back=n_back),
        grid=(b, rate, l // tq, hpr),
        in_specs=[pl.BlockSpec(memory_space=pltpu.SMEM),
                  cur(0), prev(hpr), cur(hpr), prev(2 * hpr), cur(2 * hpr)],
        out_specs=[pl.BlockSpec((1, tq, LANES), lambda bb, r, i, hp: (bb, i, r * hpr + hp)),
                   pl.BlockSpec((1, tq, LANES), lambda bb, r, i, hp: (bb, i, r))],
        out_shape=[jax.ShapeDtypeStruct((b, l, rate * hdim), BF16),
                   jax.ShapeDtypeStruct((b, l, rate * LANES), F32)],
        compiler_params=_params("parallel", "parallel", "parallel", "arbitrary"),
        name=f"dilated_attention_rate{rate}",
    )(small, qkv_r, qkv_r, qkv_r, qkv_r, qkv_r)
    return o.reshape(b, s, hdim), lse.reshape(b, s, LANES)


def _merge_out_kernel(o0_ref, o1_ref, o2_ref, l0_ref, l1_ref, l2_ref, e_ref, h_ref, gt_ref, w_ref, out_ref):
    l0, l1, l2 = l0_ref[0], l1_ref[0], l2_ref[0]
    m = jnp.maximum(jnp.maximum(l0, l1), l2)
    e0, e1, e2 = jnp.exp2(l0 - m), jnp.exp2(l1 - m), jnp.exp2(l2 - m)
    inv = 1.0 / (e0 + e1 + e2)
    spread = e_ref[...]
    o = (_dot((e0 * inv).astype(BF16), spread) * o0_ref[0].astype(F32)
         + _dot((e1 * inv).astype(BF16), spread) * o1_ref[0].astype(F32)
         + _dot((e2 * inv).astype(BF16), spread) * o2_ref[0].astype(F32))
    out_ref[0] = h_ref[0] + (1.0 + gt_ref[...]) * _dot(o.astype(BF16), w_ref[...])


def _merge_out(os_, lses, h, mod5, w_o, l):
    b, s, d = h.shape
    hdim = w_o.shape[0]
    tm = min(512, s)
    row = lambda width: pl.BlockSpec((1, tm, width), lambda bb, i: (bb, i, 0))
    rep = LANES // N_HEADS
    lane = jnp.arange(LANES)[:, None]
    col = jnp.arange(hdim)[None, :]
    spread = ((lane % rep == 0) & (lane // rep == col // HEAD_DIM)).astype(BF16)
    return pl.pallas_call(
        _merge_out_kernel,
        grid=(b, s // tm),
        in_specs=[row(hdim)] * 3 + [row(LANES)] * 3
        + [pl.BlockSpec((LANES, hdim), lambda bb, i: (0, 0)), row(d), _mod_block(d, l, 5),
           pl.BlockSpec((hdim, d), lambda bb, i: (0, 0))],
        out_specs=row(d),
        out_shape=jax.ShapeDtypeStruct((b, s, d), F32),
        compiler_params=_params("parallel", "parallel"),
        name="merge_out_proj",
    )(*os_, *lses, spread, h, mod5, w_o)


def _out_kernel(o_ref, h_ref, gt_ref, w_ref, out_ref):
    out_ref[0] = h_ref[0] + (1.0 + gt_ref[...]) * _dot(o_ref[0], w_ref[...])


def _out_proj(o, h, mod5, w_o, l):
    b, s, d = h.shape
    hdim = w_o.shape[0]
    tm = min(1024, s)
    row = lambda width: pl.BlockSpec((1, tm, width), lambda bb, i: (bb, i, 0))
    return pl.pallas_call(
        _out_kernel,
        grid=(b, s // tm),
        in_specs=[row(hdim), row(d), _mod_block(d, l, 5), pl.BlockSpec((hdim, d), lambda bb, i: (0, 0))],
        out_specs=row(d),
        out_shape=jax.ShapeDtypeStruct((b, s, d), F32),
        compiler_params=_params("parallel", "parallel"),
        name="out_proj",
    )(o, h, mod5, w_o)


def _kv_kernel(h_ref, g_ref, sh_ref, sc_ref, w_ref, gain_ref, cos_ref, sin_ref, bd_ref,
               ks_ref, kw_ref, vs_ref, vw_ref, kvc_ref, *, tm):
    i = pl.program_id(1)
    u = _normmod(h_ref[0], g_ref[...], sh_ref[...], sc_ref[...]).astype(BF16)
    y = _dot(u, w_ref[...])
    cos = cos_ref[...]
    sin = sin_ref[...]
    bd = bd_ref[...]
    lane = lax.broadcasted_iota(jnp.int32, (tm, LANES), 1)
    pos = i * tm + lax.broadcasted_iota(jnp.int32, (tm, LANES), 0)
    onehot = jnp.where((pos >> SEL_SHIFT) == lane, 1.0, 0.0).astype(BF16)
    for j in range(4):
        z = _head_rms(y[:, j * LANES:(j + 1) * LANES], bd) * gain_ref[j:j + 1, :]
        zr = _rope(z, cos, sin).astype(BF16)
        if j < 2:
            ks_ref[0, :, 2 * j * LANES:(2 * j + 1) * LANES] = zr
            ks_ref[0, :, (2 * j + 1) * LANES:(2 * j + 2) * LANES] = onehot
        else:
            kw_ref[0, :, (j - 2) * LANES:(j - 1) * LANES] = zr
    for j in range(4):
        v = y[:, (4 + j) * LANES:(5 + j) * LANES]
        va = jnp.where(lane < HEAD_DIM, v, 1.0).astype(BF16)
        ref = vs_ref if j < 2 else vw_ref
        ref[0, :, (j % 2) * LANES:(j % 2 + 1) * LANES] = va
    kvc_ref[0] = y[:, 8 * LANES:10 * LANES]


def _kv_proj(h, kv_norm_g, kvmod5, w, gain, cos, sin, bd):
    b, s, d = h.shape
    n = w.shape[1]
    tm = min(512, s)
    row = lambda width, dt: (pl.BlockSpec((1, tm, width), lambda bb, i: (bb, i, 0)),
                             jax.ShapeDtypeStruct((b, s, width), dt))
    outs = [row(4 * LANES, BF16), row(2 * LANES, BF16), row(2 * LANES, BF16), row(2 * LANES, BF16),
            row(2 * LANES, F32)]
    return pl.pallas_call(
        functools.partial(_kv_kernel, tm=tm),
        grid=(b, s // tm),
        in_specs=[pl.BlockSpec((1, tm, d), lambda bb, i: (bb, i, 0)),
                  pl.BlockSpec((1, d), lambda bb, i: (0, 0)),
                  _mod_block(d, 0, 0),
                  _mod_block(d, 0, 1),
                  pl.BlockSpec((d, n), lambda bb, i: (0, 0)),
                  pl.BlockSpec((8, LANES), lambda bb, i: (0, 0)),
                  pl.BlockSpec((tm, LANES), lambda bb, i: (i, 0)),
                  pl.BlockSpec((tm, LANES), lambda bb, i: (i, 0)),
                  pl.BlockSpec((LANES, LANES), lambda bb, i: (0, 0))],
        out_specs=[o[0] for o in outs],
        out_shape=[o[1] for o in outs],
        compiler_params=_params("parallel", "parallel"),
        name="shared_kv_proj",
    )(h, kv_norm_g, kvmod5, kvmod5, w, gain, cos, sin, bd)


def _cmp_kernel(t_ref, w1_ref, w2_ref, pos_ref, gain_ref, o_ref):
    is_key = pl.program_id(0) < N_KV_HEADS
    t = t_ref[...]
    w1 = w1_ref[...]
    half = w1.shape[0] // 2
    first = _split_dot(t, w1[:half])
    second = _split_dot(t, w1[half:])
    second = pltpu.roll(second, second.shape[0] - 1, 0)
    posb = _split_dot(jnp.broadcast_to(pos_ref[...], (8, w1.shape[0])), w1)[0:1]
    hid = first + second + posb
    out = _split_dot(hid * _sigmoid(hid), w2_ref[...])
    key = out * lax.rsqrt(jnp.mean(out * out, axis=-1, keepdims=True) + EPS) * gain_ref[...]
    dup = jnp.concatenate([key, key], axis=-1)
    aug = jnp.concatenate([out, jnp.ones_like(out)], axis=-1)
    o_ref[...] = jnp.where(is_key, dup, aug).astype(BF16)


def _compress(t4, phi_w1, phi_w2, cmp_pos, gain):
    _, b, nsb, width = t4.shape
    hid = phi_w1.shape[2]
    return pl.pallas_call(
        _cmp_kernel,
        grid=(4, b),
        in_specs=[pl.BlockSpec((None, None, nsb, width), lambda t, bb: (t, bb, 0, 0)),
                  pl.BlockSpec((None, CMP_LEN * HEAD_DIM, hid), lambda t, bb: (t // N_KV_HEADS, 0, 0)),
                  pl.BlockSpec((None, hid, HEAD_DIM), lambda t, bb: (t // N_KV_HEADS, 0, 0)),
                  pl.BlockSpec((None, 1, CMP_LEN * HEAD_DIM), lambda t, bb: (t // N_KV_HEADS, 0, 0)),
                  pl.BlockSpec((1, HEAD_DIM), lambda t, bb: (0, 0))],
        out_specs=pl.BlockSpec((None, None, nsb, LANES), lambda t, bb: (t, bb, 0, 0)),
        out_shape=jax.ShapeDtypeStruct((4, b, nsb, LANES), BF16),
        compiler_params=_params("parallel", "parallel"),
        name="compress_blocks",
    )(t4, phi_w1, phi_w2, cmp_pos, gain)


def _qg_kernel(h_ref, g_ref, sh_ref, sc_ref, w_ref, gain_ref, cos_ref, sin_ref, bd_ref,
               qn_ref, qr_ref, gate_ref):
    u = _normmod(h_ref[0], g_ref[...], sh_ref[...], sc_ref[...]).astype(BF16)
    y = _dot(u, w_ref[...])
    cos = cos_ref[...]
    sin = sin_ref[...]
    bd = bd_ref[...]
    hdim = qn_ref.shape[2]
    for cb in range(hdim // LANES):
        sl = slice(cb * LANES, (cb + 1) * LANES)
        z = _head_rms(y[:, sl], bd) * gain_ref[:, sl]
        qn_ref[0, :, sl] = z.astype(BF16)
        qr_ref[0, :, sl] = _rope(z, cos, sin).astype(BF16)
    gate_ref[0] = _sigmoid(y[:, hdim:])


def _qg_proj(h, norm_g3, mod5, w, gain, cos, sin, bd, l):
    b, s, d = h.shape
    n = w.shape[1]
    hdim = N_HEADS * HEAD_DIM
    tm = min(512, s)
    row = lambda width, dt: (pl.BlockSpec((1, tm, width), lambda bb, i: (bb, i, 0)),
                             jax.ShapeDtypeStruct((b, s, width), dt))
    outs = [row(hdim, BF16), row(hdim, BF16), row(n - hdim, F32)]
    return pl.pallas_call(
        _qg_kernel,
        grid=(b, s // tm),
        in_specs=[pl.BlockSpec((1, tm, d), lambda bb, i: (bb, i, 0)),
                  pl.BlockSpec((None, 1, d), lambda bb, i: (l * 3 + 1, 0, 0)),
                  _mod_block(d, l, 3),
                  _mod_block(d, l, 4),
                  pl.BlockSpec((d, n), lambda bb, i: (0, 0)),
                  pl.BlockSpec((1, hdim), lambda bb, i: (0, 0)),
                  pl.BlockSpec((tm, LANES), lambda bb, i: (i, 0)),
                  pl.BlockSpec((tm, LANES), lambda bb, i: (i, 0)),
                  pl.BlockSpec((LANES, LANES), lambda bb, i: (0, 0))],
        out_specs=[o[0] for o in outs],
        out_shape=[o[1] for o in outs],
        compiler_params=_params("parallel", "parallel"),
        name="nsa_query_gate_proj",
    )(h, norm_g3, mod5, mod5, w, gain, cos, sin, bd)


def _nsa_body(qn_ref, qr_ref, gate_ref, kc_ref, vc_ref, ks_ref, vs_ref, kw_ref, vw_ref, wimpt_ref,
              o_ref, qa_scr, m_scr, acc_scr, mw_scr, accw_scr, *, tq, shifted):
    i = pl.program_id(2)
    start = i * tq
    rows = GROUP * tq
    ncp = kc_ref.shape[0]
    low = lax.broadcasted_iota(jnp.int32, (tq, LANES), 1) < HEAD_DIM
    low_r = lax.broadcasted_iota(jnp.int32, (rows, LANES), 1) < HEAD_DIM

    def accumulate(s, v, mask, first, acc_ref=acc_scr, m_ref=m_scr):
        if not shifted:
            p = jnp.exp2(s)
            if mask is not None:
                p = jnp.where(mask, p, 0.0)
            pv = _dot(p.astype(BF16), v)
            acc_ref[...] = pv if first else acc_ref[...] + pv
            return
        if mask is not None:
            s = jnp.where(mask, s, NEG)
        if first:
            m_new = jnp.max(s, axis=-1, keepdims=True)
            acc_ref[...] = _dot(jnp.exp2(s - m_new).astype(BF16), v)
        else:
            m_old = m_ref[...]
            m_new = jnp.maximum(m_old, jnp.max(s, axis=-1, keepdims=True))
            acc_ref[...] = jnp.exp2(m_old - m_new) * acc_ref[...] + _dot(jnp.exp2(s - m_new).astype(BF16), v)
        m_ref[...] = m_new

    def stack(q):
        parts = []
        for g in range(GROUP):
            blk = q[:, (g // 2) * LANES:(g // 2 + 1) * LANES]
            parts.append(jnp.where(low if g % 2 == 0 else ~low, blk, jnp.zeros_like(blk)))
        return parts

    qn = jnp.concatenate(stack(qn_ref[0]), axis=0)
    sc = _dot_t(qn, kc_ref[...])
    c_end = lax.broadcasted_iota(jnp.int32, (rows, ncp), 1) * CMP_STRIDE + (CMP_LEN - 1)
    valid = c_end <= start + (lax.broadcasted_iota(jnp.int32, (rows, ncp), 0) & (tq - 1))
    if shifted:
        sc = jnp.where(valid, sc, NEG)
        sc = sc - jnp.max(sc, axis=-1, keepdims=True)
    p = jnp.where(valid, jnp.exp2(sc), 0.0)
    oc = _dot(p.astype(BF16), vc_ref[...])
    inv = 1.0 / jnp.maximum(jnp.where(low_r, pltpu.roll(oc, HEAD_DIM, 1), oc), 1e-30)
    o_cmp = oc * inv
    psum_blocks = []
    for cb in range(ncp // LANES):
        cs = slice(cb * LANES, (cb + 1) * LANES)
        tot = p[0:tq, cs] * inv[0:tq]
        for g in range(1, GROUP):
            tot = tot + p[g * tq:(g + 1) * tq, cs] * inv[g * tq:(g + 1) * tq]
        psum_blocks.append(tot)
    psum = jnp.concatenate(psum_blocks, axis=1)

    qr_parts = stack(qr_ref[0])
    for g in range(GROUP):
        qa_scr[g * tq:(g + 1) * tq, 0:LANES] = qr_parts[g]
    qr = jnp.concatenate(qr_parts, axis=0)

    tri = (lax.broadcasted_iota(jnp.int32, (rows, tq), 1)
           <= (lax.broadcasted_iota(jnp.int32, (rows, tq), 0) & (tq - 1)))

    def key_block(kb):
        return pl.ds(pl.multiple_of(kb * tq, tq), tq)

    diag = key_block(i)
    accumulate(_dot_t(qr, kw_ref[diag, :]), vw_ref[diag, :], tri, True, accw_scr, mw_scr)
    blk = key_block(jnp.maximum(i - 1, 0))
    accumulate(_dot_t(qr, kw_ref[blk, :]), vw_ref[blk, :], jnp.broadcast_to(i >= 1, tri.shape), False,
               accw_scr, mw_scr)
    blk = key_block(jnp.maximum(i - 2, 0))
    accumulate(_dot_t(qr, kw_ref[blk, :]), vw_ref[blk, :], (~tri) & (i >= 2), False, accw_scr, mw_scr)

    p_hi = psum.astype(BF16)
    p_lo = (psum - p_hi.astype(F32)).astype(BF16)
    imp_t = _dot_t(wimpt_ref[...], p_hi) + _dot_t(wimpt_ref[...], p_lo)
    blk_id = lax.broadcasted_iota(jnp.int32, (LANES, tq), 0)
    jt = (start + lax.broadcasted_iota(jnp.int32, (LANES, tq), 1)) >> SEL_SHIFT
    forced = (blk_id == 0) | (blk_id == jt) | (blk_id == jt - 1)
    work = jnp.where(blk_id > jt, -1.0, jnp.where(forced, jnp.inf, imp_t))
    sel = blk_id < 0
    for _ in range(N_SELECT):
        mx = jnp.max(work, axis=0, keepdims=True)
        idx = jnp.min(jnp.where(work == mx, blk_id, LANES), axis=0, keepdims=True)
        pick = blk_id == idx
        sel = sel | pick
        work = jnp.where(pick, -2.0, work)
    sel_bias = jnp.transpose(jnp.where(sel, 0.0, NEG)).astype(BF16)

    for g in range(GROUP):
        qa_scr[g * tq:(g + 1) * tq, LANES:2 * LANES] = sel_bias

    accumulate(_dot_t(qa_scr[...], ks_ref[diag, :]), vs_ref[diag, :], tri, True)

    def slc_pair(kp, carry):
        blk = pl.ds(pl.multiple_of(kp * 2 * tq, 2 * tq), 2 * tq)
        accumulate(_dot_t(qa_scr[...], ks_ref[blk, :]), vs_ref[blk, :], None, False)
        return carry

    lax.fori_loop(0, i // 2, slc_pair, 0)

    @pl.when(i % 2 == 1)
    def _():
        blk = key_block(i - 1)
        accumulate(_dot_t(qa_scr[...], ks_ref[blk, :]), vs_ref[blk, :], None, False)

    gates = gate_ref[0]
    for pr in range(GROUP // 2):
        ev = slice(2 * pr * tq, (2 * pr + 1) * tq)
        od = slice((2 * pr + 1) * tq, (2 * pr + 2) * tq)

        def pair(acc):
            a_e, a_o = acc[ev], acc[od]
            r_e = pltpu.roll(a_e, HEAD_DIM, 1)
            r_o = pltpu.roll(a_o, HEAD_DIM, 1)
            return jnp.where(low, a_e, r_o) / jnp.where(low, r_e, a_o)

        def gate(br):
            c = br * GROUP + 2 * pr
            return jnp.where(low, gates[:, c:c + 1], gates[:, c + 1:c + 2])

        o_c = jnp.where(low, o_cmp[ev], pltpu.roll(o_cmp[od], HEAD_DIM, 1))
        out = gate(0) * o_c + gate(1) * pair(acc_scr) + gate(2) * pair(accw_scr)
        o_ref[0, :, pr * LANES:(pr + 1) * LANES] = out.astype(BF16)


def _nsa_kernel(small_ref, *refs, tq):
    @pl.when(small_ref[0] != 0)
    def _():
        _nsa_body(*refs, tq=tq, shifted=False)

    @pl.when(small_ref[0] == 0)
    def _():
        _nsa_body(*refs, tq=tq, shifted=True)


def _nsa_attention(small, qn, qr, gates, cmp4, ks, vs, kw, vw, wimpt):
    b, s, hdim = qn.shape
    tq = min(256, s)
    assert s // SEL_BLOCK <= LANES and s % tq == 0 and SLIDE_WINDOW == 2 * tq
    ncp = cmp4.shape[2]
    gw = GROUP * HEAD_DIM
    rows = GROUP * tq
    qspec = pl.BlockSpec((1, tq, gw), lambda bb, kv, i: (bb, i, kv))
    full = lambda width: pl.BlockSpec((None, s, width), lambda bb, kv, i: (bb, 0, kv))
    return pl.pallas_call(
        functools.partial(_nsa_kernel, tq=tq),
        grid=(b, N_KV_HEADS, s // tq),
        in_specs=[pl.BlockSpec(memory_space=pltpu.SMEM), qspec, qspec,
                  pl.BlockSpec((1, tq, LANES), lambda bb, kv, i: (bb, i, kv)),
                  pl.BlockSpec((None, None, ncp, LANES), lambda bb, kv, i: (kv, bb, 0, 0)),
                  pl.BlockSpec((None, None, ncp, LANES), lambda bb, kv, i: (N_KV_HEADS + kv, bb, 0, 0)),
                  full(2 * LANES), full(LANES), full(LANES), full(LANES),
                  pl.BlockSpec((LANES, ncp), lambda bb, kv, i: (0, 0))],
        out_specs=qspec,
        out_shape=jax.ShapeDtypeStruct((b, s, hdim), BF16),
        scratch_shapes=[pltpu.VMEM((rows, 2 * LANES), BF16),
                        pltpu.VMEM((rows, 1), F32), pltpu.VMEM((rows, LANES), F32),
                        pltpu.VMEM((rows, 1), F32), pltpu.VMEM((rows, LANES), F32)],
        compiler_params=_params("parallel", "parallel", "arbitrary"),
        name="nsa_attention",
    )(small, qn, qr, gates, cmp4, cmp4, ks, vs, kw, vw, wimpt)


def _rope_tables(s):
    half = HEAD_DIM // 2
    inv = ROPE_THETA ** (-jnp.arange(half, dtype=F32) / half)
    ang = jnp.arange(s, dtype=F32)[:, None] * inv[None, :]
    cos, sin = jnp.cos(ang), jnp.sin(ang)
    reps = LANES // HEAD_DIM
    return jnp.tile(cos, (1, 2 * reps)), jnp.tile(jnp.concatenate([-sin, sin], axis=1), (1, reps))


def _importance_weights(ncp):
    r = SEL_BLOCK // CMP_STRIDE
    nper = CMP_LEN // CMP_STRIDE
    c = jnp.arange(ncp)[:, None]
    j = jnp.arange(LANES)[None, :]
    o = c - r * j
    w = jnp.zeros((ncp, LANES), F32)
    for off in range(-(nper - 1), r):
        cnt = sum(1 for m in range(r) for n in range(nper) if m - n == off)
        w = w + jnp.where(o == off, float(cnt), 0.0)
    return w.astype(BF16)


def kernel(x, c, norm_g, w_ada, b_ada, ffn_w_in, ffn_w_out, a_w_qkv, a_q_gain, a_k_gain, a_w_o, kv_norm_g,
           w_ada_kv, b_ada_kv, w_kv, kv_k_gain, cmp_pos, phi_w1, phi_w2, b_w_qg, b_q_gain, b_w_o):
    b, s, d = x.shape
    depth = norm_g.shape[0]
    n_a = depth // 2
    hdim = N_HEADS * HEAD_DIM
    assert depth == 2 and b <= 8

    c8 = jnp.zeros((8, d), F32).at[:b].set(c)
    mod = _ada(c8, w_ada, b_ada.reshape(depth, 1, 9 * d))
    mod5 = mod[:, :b].reshape(depth, b, 9, 1, d)
    kvmod = _ada(c8, w_ada_kv[None], b_ada_kv.reshape(1, 1, 2 * d))
    kvmod5 = kvmod[:, :b].reshape(1, b, 2, 1, d)

    norm_g3 = norm_g.reshape(depth * 3, 1, d)
    w_in = ffn_w_in.astype(BF16)
    w_out = ffn_w_out.astype(BF16)
    cos, sin = _rope_tables(s)
    blk = jnp.arange(MXU_WIDTH) // HEAD_DIM
    bd_wide = (blk[:, None] == blk[None, :]).astype(BF16)
    bd = bd_wide[:LANES, :LANES]
    q_scale = HEAD_DIM ** -0.5 * LOG2E

    h = x
    for l in range(depth):
        h = _ffn(h, norm_g3, mod5, w_in, w_out, l, 0, 0)
        if l < n_a:
            gain = jnp.stack([jnp.tile(a_q_gain[l] * q_scale, (1, N_HEADS)),
                              jnp.tile(a_k_gain[l], (1, N_HEADS)),
                              jnp.ones((N_DIL, hdim), F32)], axis=1).reshape(3 * N_DIL, 1, hdim)
            w_qkv = a_w_qkv[l].astype(BF16)
            bound = HEAD_DIM * q_scale * jnp.max(jnp.abs(a_q_gain[l])) * jnp.max(jnp.abs(a_k_gain[l]))
            small = (bound <= SCORE_LIMIT).astype(jnp.int32).reshape(1)
            parts = [_dilated_attention(small, _qkv_proj(h, norm_g3, mod5, w_qkv, gain, cos, sin, bd_wide, l, g), g)
                     for g in range(N_DIL)]
            h = _merge_out([p[0] for p in parts], [p[1] for p in parts], h, mod5, a_w_o[l].astype(BF16), l)
        else:
            j = l - n_a
            w_q = b_w_qg[j][:, :hdim]
            w_g = b_w_qg[j][:, hdim:].reshape(d, 3, N_KV_HEADS, GROUP).transpose(0, 2, 1, 3)
            w_g = jnp.pad(w_g.reshape(d, N_KV_HEADS, 3 * GROUP), ((0, 0), (0, 0), (0, LANES - 3 * GROUP)))
            w_qg = jnp.concatenate([w_q, w_g.reshape(d, N_KV_HEADS * LANES)], axis=1).astype(BF16)
            gain = jnp.tile(b_q_gain[j] * q_scale, N_HEADS)[None, :]
            qn, qr, gates = _qg_proj(h, norm_g3, mod5, w_qg, gain, cos, sin, bd, l)
            bound = HEAD_DIM * q_scale * jnp.max(jnp.abs(b_q_gain[j])) * jnp.max(jnp.abs(kv_k_gain))
            small = (bound <= SCORE_LIMIT).astype(jnp.int32).reshape(1)
            o = _nsa_attention(small, qn, qr, gates, *shared)
            h = _out_proj(o, h, mod5, b_w_o[j].astype(BF16), l)
        h = _ffn(h, norm_g3, mod5, w_in, w_out, l, 2, 1)
        if l == n_a - 1:
            col = lambda br, kv, kvh: br * 4 * HEAD_DIM + kv * 2 * HEAD_DIM + kvh * HEAD_DIM + jnp.arange(HEAD_DIM)
            dup = lambda br, kv, kvh: jnp.concatenate([col(br, kv, kvh)] * 2)
            cols = jnp.concatenate(
                [dup(br, 0, kvh) for br in (1, 2) for kvh in range(N_KV_HEADS)]
                + [dup(br, 1, kvh) for br in (1, 2) for kvh in range(N_KV_HEADS)]
                + [col(0, kv, kvh) for kv in range(2) for kvh in range(N_KV_HEADS)])
            w_kv_ext = w_kv[:, cols].astype(BF16)
            kgain = jnp.concatenate([jnp.tile(kv_k_gain[1], (2, 2)), jnp.tile(kv_k_gain[2], (2, 2)),
                                     jnp.ones((4, LANES), F32)], axis=0)
            ks, kw, vs, vw, kvc = _kv_proj(h, kv_norm_g[None, :], kvmod5, w_kv_ext, kgain, cos, sin, bd)
            nsb = s // CMP_STRIDE
            t4 = kvc.reshape(b, nsb, CMP_STRIDE, 4, HEAD_DIM).transpose(3, 0, 1, 2, 4)
            t4 = t4.reshape(4, b, nsb, CMP_STRIDE * HEAD_DIM)
            cmp4 = _compress(t4, phi_w1, phi_w2, cmp_pos.reshape(2, 1, CMP_LEN * HEAD_DIM), kv_k_gain[0][None, :])
            shared = (cmp4, ks, vs, kw, vw, _importance_weights(nsb).T)
    return h
```

```python
import functools

import jax
import jax.numpy as jnp
from jax import lax
from jax.experimental import pallas as pl
from jax.experimental.pallas import tpu as pltpu

HEAD_DIM = 64
N_HEADS = 16
DIL_WINDOWS = (128, 512, 2048)
DIL_RATES = (1, 4, 16)
N_DIL = 3
N_KV_HEADS = 2
GROUP = N_HEADS // N_KV_HEADS
CMP_STRIDE = 16
CMP_LEN = 32
SEL_BLOCK = 64
SEL_SHIFT = SEL_BLOCK.bit_length() - 1
N_SELECT = 16
SLIDE_WINDOW = 512
ROPE_THETA = 10000.0
EPS = 1e-6

LANES = 128
MXU_WIDTH = 256
Q_ROWS = 128
NEG = -1e30
LOG2E = 1.4426950408889634
SCORE_LIMIT = 60.0
VMEM_LIMIT = 56 * 1024 * 1024

BF16 = jnp.bfloat16
F32 = jnp.float32


def _params(*sem):
    return pltpu.CompilerParams(dimension_semantics=sem, vmem_limit_bytes=VMEM_LIMIT)


def _dot(a, b):
    return jnp.dot(a, b, preferred_element_type=F32)


def _dot_t(a, b):
    return lax.dot_general(a, b, (((1,), (1,)), ((), ())), preferred_element_type=F32)


def _split_dot(a, b):
    a_hi = a.astype(BF16)
    a_lo = (a - a_hi.astype(F32)).astype(BF16)
    b_hi = b.astype(BF16)
    b_lo = (b - b_hi.astype(F32)).astype(BF16)
    return _dot(a_hi, b_hi) + (_dot(a_hi, b_lo) + _dot(a_lo, b_hi))


def _sigmoid(x):
    return 1.0 / (1.0 + jnp.exp(-x))


def _normmod(x, g, shift, scale):
    y = x * lax.rsqrt(jnp.mean(x * x, axis=-1, keepdims=True) + EPS) * g
    return y * (1.0 + scale) + shift


def _head_rms(y, bd):
    ss = _dot((y * y).astype(BF16), bd)
    return y * lax.rsqrt(ss * (1.0 / HEAD_DIM) + EPS)


def _rope(z, cos, sin_signed):
    lane = lax.broadcasted_iota(jnp.int32, z.shape, 1)
    first_half = (lane & (HEAD_DIM - 1)) < HEAD_DIM // 2
    swapped = jnp.where(first_half, pltpu.roll(z, z.shape[1] - HEAD_DIM // 2, 1),
                        pltpu.roll(z, HEAD_DIM // 2, 1))
    return z * cos + swapped * sin_signed


def _ada_kernel(c_ref, w_ref, b_ref, o_ref):
    c = c_ref[...]
    a = (c * _sigmoid(c)).astype(BF16)
    o_ref[...] = _dot(a, w_ref[...].astype(BF16)) + b_ref[...]


def _ada(c8, w, b):
    nl, d, n = w.shape
    tn = 1024
    return pl.pallas_call(
        _ada_kernel,
        grid=(nl, n // tn),
        in_specs=[pl.BlockSpec((8, d), lambda l, j: (0, 0)),
                  pl.BlockSpec((None, d, tn), lambda l, j: (l, 0, j)),
                  pl.BlockSpec((None, 1, tn), lambda l, j: (l, 0, j))],
        out_specs=pl.BlockSpec((None, 8, tn), lambda l, j: (l, 0, j)),
        out_shape=jax.ShapeDtypeStruct((nl, 8, n), F32),
        compiler_params=_params("parallel", "parallel"),
        name="ada_table",
    )(c8, w, b)


def _mod_block(d, l, k):
    return pl.BlockSpec((None, None, None, 1, d), lambda b, *_: (l, b, k, 0, 0))


def _ffn_kernel(h_ref, g_ref, sh_ref, sc_ref, gt_ref, wg_ref, wu_ref, wo_ref, o_ref, u_scr, acc_scr):
    j = pl.program_id(2)

    @pl.when(j == 0)
    def _():
        u_scr[...] = _normmod(h_ref[0], g_ref[...], sh_ref[...], sc_ref[...]).astype(BF16)
        acc_scr[...] = jnp.zeros_like(acc_scr)

    u = u_scr[...]
    gate = _dot(u, wg_ref[...])
    up = _dot(u, wu_ref[...])
    act = (gate * _sigmoid(gate) * up).astype(BF16)
    acc_scr[...] += _dot(act, wo_ref[...])

    @pl.when(j == pl.num_programs(2) - 1)
    def _():
        o_ref[0] = h_ref[0] + (0.5 * (1.0 + gt_ref[...])) * acc_scr[...]


def _ffn(h, norm_g3, mod5, w_in, w_out, l, sub, which):
    b, s, d = h.shape
    f = w_out.shape[2]
    tm = min(1024, s)
    tf = 256
    nf = f // tf
    return pl.pallas_call(
        _ffn_kernel,
        grid=(b, s // tm, nf),
        in_specs=[pl.BlockSpec((1, tm, d), lambda bb, i, j: (bb, i, 0)),
                  pl.BlockSpec((None, 1, d), lambda bb, i, j: (l * 3 + sub, 0, 0)),
                  _mod_block(d, l, sub * 3 + 0),
                  _mod_block(d, l, sub * 3 + 1),
                  _mod_block(d, l, sub * 3 + 2),
                  pl.BlockSpec((None, None, d, tf), lambda bb, i, j: (l, which, 0, j)),
                  pl.BlockSpec((None, None, d, tf), lambda bb, i, j: (l, which, 0, j + nf)),
                  pl.BlockSpec((None, None, tf, d), lambda bb, i, j: (l, which, j, 0))],
        out_specs=pl.BlockSpec((1, tm, d), lambda bb, i, j: (bb, i, 0)),
        out_shape=jax.ShapeDtypeStruct((b, s, d), F32),
        scratch_shapes=[pltpu.VMEM((tm, d), BF16), pltpu.VMEM((tm, d), F32)],
        compiler_params=_params("parallel", "parallel", "arbitrary"),
        name="swiglu_half_step",
    )(h, norm_g3, mod5, mod5, mod5, w_in, w_in, w_out)


def _qkv_kernel(h_ref, g_ref, sh_ref, sc_ref, w_ref, gain_ref, cos_ref, sin_ref, bd_ref, o_ref, u_scr, uf_scr,
                *, rate):
    n = pl.program_id(2)
    tm = u_scr.shape[0]
    seg = tm // rate

    @pl.when(n == 0)
    def _():
        u = _normmod(h_ref[0], g_ref[...], sh_ref[...], sc_ref[...])
        if rate == 1:
            u_scr[...] = u.astype(BF16)
        else:
            for cb in range(u.shape[1] // LANES):
                sl = slice(cb * LANES, (cb + 1) * LANES)
                uf_scr[cb] = u[:, sl]
                for res in range(rate):
                    u_scr[res * seg:(res + 1) * seg, sl] = uf_scr[cb, pl.ds(res, seg, stride=rate), :].astype(BF16)

    n_grp = o_ref.shape[1]

    def store(grp, z):
        for res in range(rate):
            o_ref[res, grp] = z[res * seg:(res + 1) * seg]

    @pl.when(n == 2)
    def _():
        for grp in range(n_grp):
            store(grp, _dot(u_scr[...], w_ref[:, grp * MXU_WIDTH:(grp + 1) * MXU_WIDTH]).astype(BF16))

    @pl.when(n != 2)
    def _():
        cos = cos_ref[...]
        sin = sin_ref[...]
        bd = bd_ref[...]
        def project(grp):
            return _dot(u_scr[...], w_ref[:, grp * MXU_WIDTH:(grp + 1) * MXU_WIDTH])

        y = project(0)
        for grp in range(n_grp):
            y_next = project(grp + 1) if grp + 1 < n_grp else None
            z = _head_rms(y, bd) * gain_ref[:, grp * MXU_WIDTH:(grp + 1) * MXU_WIDTH]
            x1, x2 = z[:, :LANES], z[:, LANES:]
            store(grp, jnp.concatenate([x1 * cos - x2 * sin, x1 * sin + x2 * cos], axis=1).astype(BF16))
            y = y_next


def _qkv_proj(h, norm_g3, mod5, w, gain, cos, sin, bd, l, g):
    b, s, d = h.shape
    rate = DIL_RATES[g]
    tm = min(1024, s)
    seg = tm // rate
    tn = N_HEADS * HEAD_DIM
    n_grp = tn // MXU_WIDTH

    def residue_major(t):
        return t.reshape(s // tm, seg, rate, LANES).transpose(0, 2, 1, 3).reshape(s, LANES)

    return pl.pallas_call(
        functools.partial(_qkv_kernel, rate=rate),
        grid=(b, s // tm, 3),
        in_specs=[pl.BlockSpec((1, tm, d), lambda bb, i, j: (bb, i, 0)),
                  pl.BlockSpec((None, 1, d), lambda bb, i, j: (l * 3 + 1, 0, 0)),
                  _mod_block(d, l, 3),
                  _mod_block(d, l, 4),
                  pl.BlockSpec((d, tn), lambda bb, i, j: (0, 3 * g + j)),
                  pl.BlockSpec((None, 1, tn), lambda bb, i, j: (3 * g + j, 0, 0)),
                  pl.BlockSpec((tm, LANES), lambda bb, i, j: (i, 0)),
                  pl.BlockSpec((tm, LANES), lambda bb, i, j: (i, 0)),
                  pl.BlockSpec(bd.shape, lambda bb, i, j: (0, 0))],
        out_specs=pl.BlockSpec((None, rate, n_grp, seg, MXU_WIDTH), lambda bb, i, j: (bb, 0, j, i, 0)),
        out_shape=jax.ShapeDtypeStruct((b, rate, 3 * n_grp, s // rate, MXU_WIDTH), BF16),
        scratch_shapes=[pltpu.VMEM((tm, d), BF16), pltpu.VMEM((d // LANES, tm, LANES), F32)],
        compiler_params=_params("parallel", "parallel", "arbitrary"),
        name=f"qkv_proj_rate{rate}",
    )(h, norm_g3, mod5, mod5, w, gain, residue_major(cos), residue_major(sin), bd)


def _dil_body(bias_ref, q_ref, kp_ref, kc_ref, vp_ref, vc_ref, o_ref, lse_ref, *, tq, shifted):
    i = pl.program_id(2)
    grp = pl.program_id(3)
    hpg = MXU_WIDTH // HEAD_DIM
    qlane = lax.broadcasted_iota(jnp.int32, (Q_ROWS, MXU_WIDTH), 1)
    q_head = (qlane & (LANES - 1)) // (HEAD_DIM // 2)
    v_head = qlane // HEAD_DIM
    seg = lax.broadcasted_iota(jnp.int32, (Q_ROWS, LANES), 1) // (LANES // N_HEADS)

    def scores(j):
        q = q_ref[j * Q_ROWS:(j + 1) * Q_ROWS, :]
        zero = jnp.zeros_like(q)
        q2 = jnp.concatenate([jnp.where(q_head == t, q, zero) for t in range(hpg)], axis=0)
        if j == 0:
            k = jnp.concatenate([kp_ref[...], kc_ref[0:Q_ROWS, :]], axis=0)
        else:
            k = kc_ref[(j - 1) * Q_ROWS:(j + 1) * Q_ROWS, :]
        return _dot_t(q2, k)

    for j in range(tq // Q_ROWS):
        rs = slice(j * Q_ROWS, (j + 1) * Q_ROWS)
        s = scores(j) + bias_ref[...]
        if j == 0:
            v = jnp.concatenate([vp_ref[...], vc_ref[0:Q_ROWS, :]], axis=0)
            prev_key = lax.broadcasted_iota(jnp.int32, s.shape, 1) < Q_ROWS
            s = jnp.where(prev_key & (i == 0), NEG, s)
        else:
            v = vc_ref[(j - 1) * Q_ROWS:(j + 1) * Q_ROWS, :]
        if shifted:
            m = jnp.max(s, axis=-1, keepdims=True)
            p = jnp.exp2(s - m)
        else:
            p = jnp.exp2(s)
        den = jnp.sum(p, axis=-1, keepdims=True)
        o2 = _dot(p.astype(BF16), v) / den
        lse2 = jnp.log(den) * LOG2E
        if shifted:
            lse2 = lse2 + m
        out = o2[0:Q_ROWS]
        new = jnp.where(grp == 0, 0.0, lse_ref[0, rs, :])
        for t in range(hpg):
            ts = slice(t * Q_ROWS, (t + 1) * Q_ROWS)
            if t:
                out = jnp.where(v_head == t, o2[ts], out)
            new = jnp.where(seg == hpg * grp + t, lse2[ts], new)
        o_ref[0, rs, :] = out.astype(BF16)
        lse_ref[0, rs, :] = new


def _dil_kernel(small_ref, *refs, tq):
    @pl.when(small_ref[0] != 0)
    def _():
        _dil_body(*refs, tq=tq, shifted=False)

    @pl.when(small_ref[0] == 0)
    def _():
        _dil_body(*refs, tq=tq, shifted=True)


def _dilated_attention(small, qkv_r, g):
    b, rate, n3, l, width = qkv_r.shape
    s = l * rate
    n_back = DIL_WINDOWS[g] // rate
    assert n_back <= Q_ROWS and rate == DIL_RATES[g] and width == MXU_WIDTH
    tq = min(512, l)
    hdim = N_HEADS * HEAD_DIM
    n_grp = n3 // 3
    sub = tq // Q_ROWS

    def cur(off):
        return pl.BlockSpec((None, None, None, tq, width), lambda bb, r, i, gp: (bb, r, off + gp, i, 0))

    def prev(off):
        return pl.BlockSpec((None, None, None, Q_ROWS, width),
                            lambda bb, r, i, gp: (bb, r, off + gp, jnp.maximum(i * sub - 1, 0), 0))

    rows = (width // HEAD_DIM) * Q_ROWS
    dist = Q_ROWS + (jnp.arange(rows)[:, None] % Q_ROWS) - jnp.arange(2 * Q_ROWS)[None, :]
    bias = jnp.where((dist >= 0) & (dist <= n_back), 0.0, NEG).astype(F32)

    o, lse = pl.pallas_call(
        functools.partial(_dil_kernel, tq=tq),
        grid=(b, rate, l // tq, n_grp),
        in_specs=[pl.BlockSpec(memory_space=pltpu.SMEM),
                  pl.BlockSpec(bias.shape, lambda bb, r, i, gp: (0, 0)),
                  cur(0), prev(n_grp), cur(n_grp), prev(2 * n_grp), cur(2 * n_grp)],
        out_specs=[pl.BlockSpec((1, tq, width), lambda bb, r, i, gp: (bb, i, r * n_grp + gp)),
                   pl.BlockSpec((1, tq, LANES), lambda bb, r, i, gp: (bb, i, r))],
        out_shape=[jax.ShapeDtypeStruct((b, l, rate * hdim), BF16),
                   jax.ShapeDtypeStruct((b, l, rate * LANES), F32)],
        compiler_params=_params("parallel", "parallel", "parallel", "arbitrary"),
        name=f"dilated_attention_rate{rate}",
    )(small, bias, qkv_r, qkv_r, qkv_r, qkv_r, qkv_r)
    return o.reshape(b, s, hdim), lse.reshape(b, s, LANES)


def _merge_out_kernel(o0_ref, o1_ref, o2_ref, l0_ref, l1_ref, l2_ref, e_ref, h_ref, gt_ref, w_ref, out_ref):
    l0, l1, l2 = l0_ref[0], l1_ref[0], l2_ref[0]
    m = jnp.maximum(jnp.maximum(l0, l1), l2)
    e0, e1, e2 = jnp.exp2(l0 - m), jnp.exp2(l1 - m), jnp.exp2(l2 - m)
    inv = 1.0 / (e0 + e1 + e2)
    spread = e_ref[...]
    o = (_dot((e0 * inv).astype(BF16), spread) * o0_ref[0].astype(F32)
         + _dot((e1 * inv).astype(BF16), spread) * o1_ref[0].astype(F32)
         + _dot((e2 * inv).astype(BF16), spread) * o2_ref[0].astype(F32))
    out_ref[0] = h_ref[0] + (1.0 + gt_ref[...]) * _dot(o.astype(BF16), w_ref[...])


def _merge_out(os_, lses, h, mod5, w_o, l):
    b, s, d = h.shape
    hdim = w_o.shape[0]
    tm = min(512, s)
    row = lambda width: pl.BlockSpec((1, tm, width), lambda bb, i: (bb, i, 0))
    rep = LANES // N_HEADS
    lane = jnp.arange(LANES)[:, None]
    col = jnp.arange(hdim)[None, :]
    spread = ((lane % rep == 0) & (lane // rep == col // HEAD_DIM)).astype(BF16)
    return pl.pallas_call(
        _merge_out_kernel,
        grid=(b, s // tm),
        in_specs=[row(hdim)] * 3 + [row(LANES)] * 3
        + [pl.BlockSpec((LANES, hdim), lambda bb, i: (0, 0)), row(d), _mod_block(d, l, 5),
           pl.BlockSpec((hdim, d), lambda bb, i: (0, 0))],
        out_specs=row(d),
        out_shape=jax.ShapeDtypeStruct((b, s, d), F32),
        compiler_params=_params("parallel", "parallel"),
        name="merge_out_proj",
    )(*os_, *lses, spread, h, mod5, w_o)


def _out_kernel(o_ref, h_ref, gt_ref, w_ref, out_ref):
    out_ref[0] = h_ref[0] + (1.0 + gt_ref[...]) * _dot(o_ref[0], w_ref[...])


def _out_proj(o, h, mod5, w_o, l):
    b, s, d = h.shape
    hdim = w_o.shape[0]
    tm = min(1024, s)
    row = lambda width: pl.BlockSpec((1, tm, width), lambda bb, i: (bb, i, 0))
    return pl.pallas_call(
        _out_kernel,
        grid=(b, s // tm),
        in_specs=[row(hdim), row(d), _mod_block(d, l, 5), pl.BlockSpec((hdim, d), lambda bb, i: (0, 0))],
        out_specs=row(d),
        out_shape=jax.ShapeDtypeStruct((b, s, d), F32),
        compiler_params=_params("parallel", "parallel"),
        name="out_proj",
    )(o, h, mod5, w_o)


def _kv_kernel(h_ref, g_ref, sh_ref, sc_ref, w_ref, gain_ref, cos_ref, sin_ref, bd_ref,
               ks_ref, kw_ref, vs_ref, vw_ref, kvc_ref, *, tm):
    i = pl.program_id(1)
    u = _normmod(h_ref[0], g_ref[...], sh_ref[...], sc_ref[...]).astype(BF16)
    y = _dot(u, w_ref[...])
    cos = cos_ref[...]
    sin = sin_ref[...]
    bd = bd_ref[...]
    lane = lax.broadcasted_iota(jnp.int32, (tm, LANES), 1)
    pos = i * tm + lax.broadcasted_iota(jnp.int32, (tm, LANES), 0)
    onehot = jnp.where((pos >> SEL_SHIFT) == lane, 1.0, 0.0).astype(BF16)
    for j in range(4):
        z = _head_rms(y[:, j * LANES:(j + 1) * LANES], bd) * gain_ref[j:j + 1, :]
        zr = _rope(z, cos, sin).astype(BF16)
        if j < 2:
            ks_ref[0, :, 2 * j * LANES:(2 * j + 1) * LANES] = zr
            ks_ref[0, :, (2 * j + 1) * LANES:(2 * j + 2) * LANES] = onehot
        else:
            kw_ref[0, :, (j - 2) * LANES:(j - 1) * LANES] = zr
    for j in range(4):
        v = y[:, (4 + j) * LANES:(5 + j) * LANES]
        va = jnp.where(lane < HEAD_DIM, v, 1.0).astype(BF16)
        ref = vs_ref if j < 2 else vw_ref
        ref[0, :, (j % 2) * LANES:(j % 2 + 1) * LANES] = va
    kvc_ref[0] = y[:, 8 * LANES:10 * LANES]


def _kv_proj(h, kv_norm_g, kvmod5, w, gain, cos, sin, bd):
    b, s, d = h.shape
    n = w.shape[1]
    tm = min(512, s)
    row = lambda width, dt: (pl.BlockSpec((1, tm, width), lambda bb, i: (bb, i, 0)),
                             jax.ShapeDtypeStruct((b, s, width), dt))
    outs = [row(4 * LANES, BF16), row(2 * LANES, BF16), row(2 * LANES, BF16), row(2 * LANES, BF16),
            row(2 * LANES, F32)]
    return pl.pallas_call(
        functools.partial(_kv_kernel, tm=tm),
        grid=(b, s // tm),
        in_specs=[pl.BlockSpec((1, tm, d), lambda bb, i: (bb, i, 0)),
                  pl.BlockSpec((1, d), lambda bb, i: (0, 0)),
                  _mod_block(d, 0, 0),
                  _mod_block(d, 0, 1),
                  pl.BlockSpec((d, n), lambda bb, i: (0, 0)),
                  pl.BlockSpec((8, LANES), lambda bb, i: (0, 0)),
                  pl.BlockSpec((tm, LANES), lambda bb, i: (i, 0)),
                  pl.BlockSpec((tm, LANES), lambda bb, i: (i, 0)),
                  pl.BlockSpec((LANES, LANES), lambda bb, i: (0, 0))],
        out_specs=[o[0] for o in outs],
        out_shape=[o[1] for o in outs],
        compiler_params=_params("parallel", "parallel"),
        name="shared_kv_proj",
    )(h, kv_norm_g, kvmod5, kvmod5, w, gain, cos, sin, bd)


def _cmp_kernel(t_ref, w1_ref, w2_ref, pos_ref, gain_ref, o_ref):
    is_key = pl.program_id(0) < N_KV_HEADS
    t = t_ref[...]
    w1 = w1_ref[...]
    half = w1.shape[0] // 2
    first = _split_dot(t, w1[:half])
    second = _split_dot(t, w1[half:])
    second = pltpu.roll(second, second.shape[0] - 1, 0)
    posb = _split_dot(jnp.broadcast_to(pos_ref[...], (8, w1.shape[0])), w1)[0:1]
    hid = first + second + posb
    out = _split_dot(hid * _sigmoid(hid), w2_ref[...])
    key = out * lax.rsqrt(jnp.mean(out * out, axis=-1, keepdims=True) + EPS) * gain_ref[...]
    dup = jnp.concatenate([key, key], axis=-1)
    aug = jnp.concatenate([out, jnp.ones_like(out)], axis=-1)
    o_ref[...] = jnp.where(is_key, dup, aug).astype(BF16)


def _compress(t4, phi_w1, phi_w2, cmp_pos, gain):
    _, b, nsb, width = t4.shape
    hid = phi_w1.shape[2]
    return pl.pallas_call(
        _cmp_kernel,
        grid=(4, b),
        in_specs=[pl.BlockSpec((None, None, nsb, width), lambda t, bb: (t, bb, 0, 0)),
                  pl.BlockSpec((None, CMP_LEN * HEAD_DIM, hid), lambda t, bb: (t // N_KV_HEADS, 0, 0)),
                  pl.BlockSpec((None, hid, HEAD_DIM), lambda t, bb: (t // N_KV_HEADS, 0, 0)),
                  pl.BlockSpec((None, 1, CMP_LEN * HEAD_DIM), lambda t, bb: (t // N_KV_HEADS, 0, 0)),
                  pl.BlockSpec((1, HEAD_DIM), lambda t, bb: (0, 0))],
        out_specs=pl.BlockSpec((None, None, nsb, LANES), lambda t, bb: (t, bb, 0, 0)),
        out_shape=jax.ShapeDtypeStruct((4, b, nsb, LANES), BF16),
        compiler_params=_params("parallel", "parallel"),
        name="compress_blocks",
    )(t4, phi_w1, phi_w2, cmp_pos, gain)


def _qg_kernel(h_ref, g_ref, sh_ref, sc_ref, w_ref, gain_ref, cos_ref, sin_ref, bd_ref,
               qn_ref, qr_ref, gate_ref):
    u = _normmod(h_ref[0], g_ref[...], sh_ref[...], sc_ref[...]).astype(BF16)
    y = _dot(u, w_ref[...])
    cos = cos_ref[...]
    sin = sin_ref[...]
    bd = bd_ref[...]
    hdim = qn_ref.shape[2]
    for cb in range(hdim // LANES):
        sl = slice(cb * LANES, (cb + 1) * LANES)
        z = _head_rms(y[:, sl], bd) * gain_ref[:, sl]
        qn_ref[0, :, sl] = z.astype(BF16)
        qr_ref[0, :, sl] = _rope(z, cos, sin).astype(BF16)
    gate_ref[0] = _sigmoid(y[:, hdim:])


def _qg_proj(h, norm_g3, mod5, w, gain, cos, sin, bd, l):
    b, s, d = h.shape
    n = w.shape[1]
    hdim = N_HEADS * HEAD_DIM
    tm = min(512, s)
    row = lambda width, dt: (pl.BlockSpec((1, tm, width), lambda bb, i: (bb, i, 0)),
                             jax.ShapeDtypeStruct((b, s, width), dt))
    outs = [row(hdim, BF16), row(hdim, BF16), row(n - hdim, F32)]
    return pl.pallas_call(
        _qg_kernel,
        grid=(b, s // tm),
        in_specs=[pl.BlockSpec((1, tm, d), lambda bb, i: (bb, i, 0)),
                  pl.BlockSpec((None, 1, d), lambda bb, i: (l * 3 + 1, 0, 0)),
                  _mod_block(d, l, 3),
                  _mod_block(d, l, 4),
                  pl.BlockSpec((d, n), lambda bb, i: (0, 0)),
                  pl.BlockSpec((1, hdim), lambda bb, i: (0, 0)),
                  pl.BlockSpec((tm, LANES), lambda bb, i: (i, 0)),
                  pl.BlockSpec((tm, LANES), lambda bb, i: (i, 0)),
                  pl.BlockSpec((LANES, LANES), lambda bb, i: (0, 0))],
        out_specs=[o[0] for o in outs],
        out_shape=[o[1] for o in outs],
        compiler_params=_params("parallel", "parallel"),
        name="nsa_query_gate_proj",
    )(h, norm_g3, mod5, mod5, w, gain, cos, sin, bd)


def _nsa_body(qn_ref, qr_ref, gate_ref, kc_ref, vc_ref, ks_ref, vs_ref, kw_ref, vw_ref, wimpt_ref,
              o_ref, qa_scr, m_scr, acc_scr, mw_scr, accw_scr, *, tq, shifted):
    i = pl.program_id(2)
    start = i * tq
    rows = GROUP * tq
    ncp = kc_ref.shape[0]
    low = lax.broadcasted_iota(jnp.int32, (tq, LANES), 1) < HEAD_DIM
    low_r = lax.broadcasted_iota(jnp.int32, (rows, LANES), 1) < HEAD_DIM

    def accumulate(s, v, mask, first, acc_ref=acc_scr, m_ref=m_scr):
        if not shifted:
            p = jnp.exp2(s)
            if mask is not None:
                p = jnp.where(mask, p, 0.0)
            pv = _dot(p.astype(BF16), v)
            acc_ref[...] = pv if first else acc_ref[...] + pv
            return
        if mask is not None:
            s = jnp.where(mask, s, NEG)
        if first:
            m_new = jnp.max(s, axis=-1, keepdims=True)
            acc_ref[...] = _dot(jnp.exp2(s - m_new).astype(BF16), v)
        else:
            m_old = m_ref[...]
            m_new = jnp.maximum(m_old, jnp.max(s, axis=-1, keepdims=True))
            acc_ref[...] = jnp.exp2(m_old - m_new) * acc_ref[...] + _dot(jnp.exp2(s - m_new).astype(BF16), v)
        m_ref[...] = m_new

    def stack(q):
        parts = []
        for g in range(GROUP):
            blk = q[:, (g // 2) * LANES:(g // 2 + 1) * LANES]
            parts.append(jnp.where(low if g % 2 == 0 else ~low, blk, jnp.zeros_like(blk)))
        return parts

    qn = jnp.concatenate(stack(qn_ref[0]), axis=0)
    sc = _dot_t(qn, kc_ref[...])
    c_end = lax.broadcasted_iota(jnp.int32, (rows, ncp), 1) * CMP_STRIDE + (CMP_LEN - 1)
    valid = c_end <= start + (lax.broadcasted_iota(jnp.int32, (rows, ncp), 0) & (tq - 1))
    if shifted:
        sc = jnp.where(valid, sc, NEG)
        sc = sc - jnp.max(sc, axis=-1, keepdims=True)
    p = jnp.where(valid, jnp.exp2(sc), 0.0)
    oc = _dot(p.astype(BF16), vc_ref[...])
    inv = 1.0 / jnp.maximum(jnp.where(low_r, pltpu.roll(oc, HEAD_DIM, 1), oc), 1e-30)
    o_cmp = oc * inv
    psum_blocks = []
    for cb in range(ncp // LANES):
        cs = slice(cb * LANES, (cb + 1) * LANES)
        tot = p[0:tq, cs] * inv[0:tq]
        for g in range(1, GROUP):
            tot = tot + p[g * tq:(g + 1) * tq, cs] * inv[g * tq:(g + 1) * tq]
        psum_blocks.append(tot)
    psum = jnp.concatenate(psum_blocks, axis=1)

    qr_parts = stack(qr_ref[0])
    for g in range(GROUP):
        qa_scr[g * tq:(g + 1) * tq, 0:LANES] = qr_parts[g]
    qr = jnp.concatenate(qr_parts, axis=0)

    tri = (lax.broadcasted_iota(jnp.int32, (rows, tq), 1)
           <= (lax.broadcasted_iota(jnp.int32, (rows, tq), 0) & (tq - 1)))

    def key_block(kb):
        return pl.ds(pl.multiple_of(kb * tq, tq), tq)

    diag = key_block(i)
    accumulate(_dot_t(qr, kw_ref[diag, :]), vw_ref[diag, :], tri, True, accw_scr, mw_scr)
    blk = key_block(jnp.maximum(i - 1, 0))
    accumulate(_dot_t(qr, kw_ref[blk, :]), vw_ref[blk, :], jnp.broadcast_to(i >= 1, tri.shape), False,
               accw_scr, mw_scr)
    blk = key_block(jnp.maximum(i - 2, 0))
    accumulate(_dot_t(qr, kw_ref[blk, :]), vw_ref[blk, :], (~tri) & (i >= 2), False, accw_scr, mw_scr)

    p_hi = psum.astype(BF16)
    p_lo = (psum - p_hi.astype(F32)).astype(BF16)
    imp_t = _dot_t(wimpt_ref[...], p_hi) + _dot_t(wimpt_ref[...], p_lo)
    blk_id = lax.broadcasted_iota(jnp.int32, (LANES, tq), 0)
    jt = (start + lax.broadcasted_iota(jnp.int32, (LANES, tq), 1)) >> SEL_SHIFT
    forced = (blk_id == 0) | (blk_id == jt) | (blk_id == jt - 1)
    work = jnp.where(blk_id > jt, -1.0, jnp.where(forced, jnp.inf, imp_t))
    sel = blk_id < 0
    for _ in range(N_SELECT):
        mx = jnp.max(work, axis=0, keepdims=True)
        idx = jnp.min(jnp.where(work == mx, blk_id, LANES), axis=0, keepdims=True)
        pick = blk_id == idx
        sel = sel | pick
        work = jnp.where(pick, -2.0, work)
    sel_bias = jnp.transpose(jnp.where(sel, 0.0, NEG)).astype(BF16)

    for g in range(GROUP):
        qa_scr[g * tq:(g + 1) * tq, LANES:2 * LANES] = sel_bias

    accumulate(_dot_t(qa_scr[...], ks_ref[diag, :]), vs_ref[diag, :], tri, True)

    def slc_pair(kp, carry):
        blk = pl.ds(pl.multiple_of(kp * 2 * tq, 2 * tq), 2 * tq)
        accumulate(_dot_t(qa_scr[...], ks_ref[blk, :]), vs_ref[blk, :], None, False)
        return carry

    lax.fori_loop(0, i // 2, slc_pair, 0)

    @pl.when(i % 2 == 1)
    def _():
        blk = key_block(i - 1)
        accumulate(_dot_t(qa_scr[...], ks_ref[blk, :]), vs_ref[blk, :], None, False)

    gates = gate_ref[0]
    for pr in range(GROUP // 2):
        ev = slice(2 * pr * tq, (2 * pr + 1) * tq)
        od = slice((2 * pr + 1) * tq, (2 * pr + 2) * tq)

        def pair(acc):
            a_e, a_o = acc[ev], acc[od]
            r_e = pltpu.roll(a_e, HEAD_DIM, 1)
            r_o = pltpu.roll(a_o, HEAD_DIM, 1)
            return jnp.where(low, a_e, r_o) / jnp.where(low, r_e, a_o)

        def gate(br):
            c = br * GROUP + 2 * pr
            return jnp.where(low, gates[:, c:c + 1], gates[:, c + 1:c + 2])

        o_c = jnp.where(low, o_cmp[ev], pltpu.roll(o_cmp[od], HEAD_DIM, 1))
        out = gate(0) * o_c + gate(1) * pair(acc_scr) + gate(2) * pair(accw_scr)
        o_ref[0, :, pr * LANES:(pr + 1) * LANES] = out.astype(BF16)


def _nsa_kernel(small_ref, *refs, tq):
    @pl.when(small_ref[0] != 0)
    def _():
        _nsa_body(*refs, tq=tq, shifted=False)

    @pl.when(small_ref[0] == 0)
    def _():
        _nsa_body(*refs, tq=tq, shifted=True)


def _nsa_attention(small, qn, qr, gates, cmp4, ks, vs, kw, vw, wimpt):
    b, s, hdim = qn.shape
    tq = min(256, s)
    assert s // SEL_BLOCK <= LANES and s % tq == 0 and SLIDE_WINDOW == 2 * tq
    ncp = cmp4.shape[2]
    gw = GROUP * HEAD_DIM
    rows = GROUP * tq
    qspec = pl.BlockSpec((1, tq, gw), lambda bb, kv, i: (bb, i, kv))
    full = lambda width: pl.BlockSpec((None, s, width), lambda bb, kv, i: (bb, 0, kv))
    return pl.pallas_call(
        functools.partial(_nsa_kernel, tq=tq),
        grid=(b, N_KV_HEADS, s // tq),
        in_specs=[pl.BlockSpec(memory_space=pltpu.SMEM), qspec, qspec,
                  pl.BlockSpec((1, tq, LANES), lambda bb, kv, i: (bb, i, kv)),
                  pl.BlockSpec((None, None, ncp, LANES), lambda bb, kv, i: (kv, bb, 0, 0)),
                  pl.BlockSpec((None, None, ncp, LANES), lambda bb, kv, i: (N_KV_HEADS + kv, bb, 0, 0)),
                  full(2 * LANES), full(LANES), full(LANES), full(LANES),
                  pl.BlockSpec((LANES, ncp), lambda bb, kv, i: (0, 0))],
        out_specs=qspec,
        out_shape=jax.ShapeDtypeStruct((b, s, hdim), BF16),
        scratch_shapes=[pltpu.VMEM((rows, 2 * LANES), BF16),
                        pltpu.VMEM((rows, 1), F32), pltpu.VMEM((rows, LANES), F32),
                        pltpu.VMEM((rows, 1), F32), pltpu.VMEM((rows, LANES), F32)],
        compiler_params=_params("parallel", "parallel", "arbitrary"),
        name="nsa_attention",
    )(small, qn, qr, gates, cmp4, cmp4, ks, vs, kw, vw, wimpt)


def _rope_tables(s):
    half = HEAD_DIM // 2
    inv = ROPE_THETA ** (-jnp.arange(half, dtype=F32) / half)
    ang = jnp.arange(s, dtype=F32)[:, None] * inv[None, :]
    cos, sin = jnp.cos(ang), jnp.sin(ang)
    reps = LANES // HEAD_DIM
    return (jnp.tile(cos, (1, 2 * reps)), jnp.tile(jnp.concatenate([-sin, sin], axis=1), (1, reps)),
            jnp.tile(sin, (1, 2 * reps)))


def _split_half_perm(hdim):
    hpg = MXU_WIDTH // HEAD_DIM
    half = HEAD_DIM // 2
    c = jnp.arange(hdim)
    grp, within = c // MXU_WIDTH, c % MXU_WIDTH
    part, head, dim = within // LANES, (within % LANES) // half, within % half
    return (grp * hpg + head) * HEAD_DIM + part * half + dim


def _importance_weights(ncp):
    r = SEL_BLOCK // CMP_STRIDE
    nper = CMP_LEN // CMP_STRIDE
    c = jnp.arange(ncp)[:, None]
    j = jnp.arange(LANES)[None, :]
    o = c - r * j
    w = jnp.zeros((ncp, LANES), F32)
    for off in range(-(nper - 1), r):
        cnt = sum(1 for m in range(r) for n in range(nper) if m - n == off)
        w = w + jnp.where(o == off, float(cnt), 0.0)
    return w.astype(BF16)


def kernel(x, c, norm_g, w_ada, b_ada, ffn_w_in, ffn_w_out, a_w_qkv, a_q_gain, a_k_gain, a_w_o, kv_norm_g,
           w_ada_kv, b_ada_kv, w_kv, kv_k_gain, cmp_pos, phi_w1, phi_w2, b_w_qg, b_q_gain, b_w_o):
    b, s, d = x.shape
    depth = norm_g.shape[0]
    n_a = depth // 2
    hdim = N_HEADS * HEAD_DIM
    assert depth == 2 and b <= 8

    c8 = jnp.zeros((8, d), F32).at[:b].set(c)
    mod = _ada(c8, w_ada, b_ada.reshape(depth, 1, 9 * d))
    mod5 = mod[:, :b].reshape(depth, b, 9, 1, d)
    kvmod = _ada(c8, w_ada_kv[None], b_ada_kv.reshape(1, 1, 2 * d))
    kvmod5 = kvmod[:, :b].reshape(1, b, 2, 1, d)

    norm_g3 = norm_g.reshape(depth * 3, 1, d)
    w_in = ffn_w_in.astype(BF16)
    w_out = ffn_w_out.astype(BF16)
    cos, sin, sin_plain = _rope_tables(s)
    blk = jnp.arange(LANES) // HEAD_DIM
    bd = (blk[:, None] == blk[None, :]).astype(BF16)
    blk = (jnp.arange(MXU_WIDTH) % LANES) // (HEAD_DIM // 2)
    bd_split = (blk[:, None] == blk[None, :]).astype(BF16)
    perm = _split_half_perm(hdim)
    q_scale = HEAD_DIM ** -0.5 * LOG2E

    h = x
    for l in range(depth):
        h = _ffn(h, norm_g3, mod5, w_in, w_out, l, 0, 0)
        if l < n_a:
            gain = jnp.stack([jnp.tile(a_q_gain[l] * q_scale, (1, N_HEADS))[:, perm],
                              jnp.tile(a_k_gain[l], (1, N_HEADS))[:, perm],
                              jnp.ones((N_DIL, hdim), F32)], axis=1).reshape(3 * N_DIL, 1, hdim)
            w_qkv = a_w_qkv[l].reshape(d, N_DIL, 3, hdim)
            w_qkv = jnp.concatenate([w_qkv[:, :, :2, perm], w_qkv[:, :, 2:]], axis=2)
            w_qkv = w_qkv.reshape(d, N_DIL * 3 * hdim).astype(BF16)
            bound = HEAD_DIM * q_scale * jnp.max(jnp.abs(a_q_gain[l])) * jnp.max(jnp.abs(a_k_gain[l]))
            small = (bound <= SCORE_LIMIT).astype(jnp.int32).reshape(1)
            parts = [_dilated_attention(small, _qkv_proj(h, norm_g3, mod5, w_qkv, gain, cos, sin_plain, bd_split, l, g), g)
                     for g in range(N_DIL)]
            h = _merge_out([p[0] for p in parts], [p[1] for p in parts], h, mod5, a_w_o[l].astype(BF16), l)
        else:
            j = l - n_a
            w_q = b_w_qg[j][:, :hdim]
            w_g = b_w_qg[j][:, hdim:].reshape(d, 3, N_KV_HEADS, GROUP).transpose(0, 2, 1, 3)
            w_g = jnp.pad(w_g.reshape(d, N_KV_HEADS, 3 * GROUP), ((0, 0), (0, 0), (0, LANES - 3 * GROUP)))
            w_qg = jnp.concatenate([w_q, w_g.reshape(d, N_KV_HEADS * LANES)], axis=1).astype(BF16)
            gain = jnp.tile(b_q_gain[j] * q_scale, N_HEADS)[None, :]
            qn, qr, gates = _qg_proj(h, norm_g3, mod5, w_qg, gain, cos, sin, bd, l)
            bound = HEAD_DIM * q_scale * jnp.max(jnp.abs(b_q_gain[j])) * jnp.max(jnp.abs(kv_k_gain))
            small = (bound <= SCORE_LIMIT).astype(jnp.int32).reshape(1)
            o = _nsa_attention(small, qn, qr, gates, *shared)
            h = _out_proj(o, h, mod5, b_w_o[j].astype(BF16), l)
        h = _ffn(h, norm_g3, mod5, w_in, w_out, l, 2, 1)
        if l == n_a - 1:
            col = lambda br, kv, kvh: br * 4 * HEAD_DIM + kv * 2 * HEAD_DIM + kvh * HEAD_DIM + jnp.arange(HEAD_DIM)
            dup = lambda br, kv, kvh: jnp.concatenate([col(br, kv, kvh)] * 2)
            cols = jnp.concatenate(
                [dup(br, 0, kvh) for br in (1, 2) for kvh in range(N_KV_HEADS)]
                + [dup(br, 1, kvh) for br in (1, 2) for kvh in range(N_KV_HEADS)]
                + [col(0, kv, kvh) for kv in range(2) for kvh in range(N_KV_HEADS)])
            w_kv_ext = w_kv[:, cols].astype(BF16)
            kgain = jnp.concatenate([jnp.tile(kv_k_gain[1], (2, 2)), jnp.tile(kv_k_gain[2], (2, 2)),
                                     jnp.ones((4, LANES), F32)], axis=0)
            ks, kw, vs, vw, kvc = _kv_proj(h, kv_norm_g[None, :], kvmod5, w_kv_ext, kgain, cos, sin, bd)
            nsb = s // CMP_STRIDE
            t4 = kvc.reshape(b, nsb, CMP_STRIDE, 4, HEAD_DIM).transpose(3, 0, 1, 2, 4)
            t4 = t4.reshape(4, b, nsb, CMP_STRIDE * HEAD_DIM)
            cmp4 = _compress(t4, phi_w1, phi_w2, cmp_pos.reshape(2, 1, CMP_LEN * HEAD_DIM), kv_k_gain[0][None, :])
            shared = (cmp4, ks, vs, kw, vw, _importance_weights(nsb).T)
    return h
```

```python
import functools

import jax
import jax.numpy as jnp
from jax import lax
from jax.experimental import pallas as pl
from jax.experimental.pallas import tpu as pltpu

HEAD_DIM = 64
N_HEADS = 16
DIL_WINDOWS = (128, 512, 2048)
DIL_RATES = (1, 4, 16)
N_DIL = 3
N_KV_HEADS = 2
GROUP = N_HEADS // N_KV_HEADS
CMP_STRIDE = 16
CMP_LEN = 32
SEL_BLOCK = 64
SEL_SHIFT = SEL_BLOCK.bit_length() - 1
N_SELECT = 16
SLIDE_WINDOW = 512
ROPE_THETA = 10000.0
EPS = 1e-6

LANES = 128
MXU_WIDTH = 256
Q_ROWS = 128
NEG = -1e30
LOG2E = 1.4426950408889634
SCORE_LIMIT = 60.0
VMEM_LIMIT = 56 * 1024 * 1024

BF16 = jnp.bfloat16
F32 = jnp.float32


def _params(*sem):
    return pltpu.CompilerParams(dimension_semantics=sem, vmem_limit_bytes=VMEM_LIMIT)


def _dot(a, b):
    return jnp.dot(a, b, preferred_element_type=F32)


def _dot_t(a, b):
    return lax.dot_general(a, b, (((1,), (1,)), ((), ())), preferred_element_type=F32)


def _split_dot(a, b):
    a_hi = a.astype(BF16)
    a_lo = (a - a_hi.astype(F32)).astype(BF16)
    b_hi = b.astype(BF16)
    b_lo = (b - b_hi.astype(F32)).astype(BF16)
    return _dot(a_hi, b_hi) + (_dot(a_hi, b_lo) + _dot(a_lo, b_hi))


def _sigmoid(x):
    return 1.0 / (1.0 + jnp.exp(-x))


def _normmod(x, g, shift, scale):
    y = x * lax.rsqrt(jnp.mean(x * x, axis=-1, keepdims=True) + EPS) * g
    return y * (1.0 + scale) + shift


def _head_rms(y, bd):
    ss = _dot((y * y).astype(BF16), bd)
    return y * lax.rsqrt(ss * (1.0 / HEAD_DIM) + EPS)


def _rope(z, cos, sin_signed):
    lane = lax.broadcasted_iota(jnp.int32, z.shape, 1)
    first_half = (lane & (HEAD_DIM - 1)) < HEAD_DIM // 2
    swapped = jnp.where(first_half, pltpu.roll(z, z.shape[1] - HEAD_DIM // 2, 1),
                        pltpu.roll(z, HEAD_DIM // 2, 1))
    return z * cos + swapped * sin_signed


def _ada_kernel(c_ref, w_ref, b_ref, o_ref):
    c = c_ref[...]
    a = (c * _sigmoid(c)).astype(BF16)
    o_ref[...] = _dot(a, w_ref[...].astype(BF16)) + b_ref[...]


def _ada(c8, w, b):
    nl, d, n = w.shape
    tn = 1024
    return pl.pallas_call(
        _ada_kernel,
        grid=(nl, n // tn),
        in_specs=[pl.BlockSpec((8, d), lambda l, j: (0, 0)),
                  pl.BlockSpec((None, d, tn), lambda l, j: (l, 0, j)),
                  pl.BlockSpec((None, 1, tn), lambda l, j: (l, 0, j))],
        out_specs=pl.BlockSpec((None, 8, tn), lambda l, j: (l, 0, j)),
        out_shape=jax.ShapeDtypeStruct((nl, 8, n), F32),
        compiler_params=_params("parallel", "parallel"),
        name="ada_table",
    )(c8, w, b)


def _mod_block(d, l, k):
    return pl.BlockSpec((None, None, None, 1, d), lambda b, *_: (l, b, k, 0, 0))


def _ffn_kernel(h_ref, g_ref, sh_ref, sc_ref, gt_ref, wg_ref, wu_ref, wo_ref, o_ref, u_scr, acc_scr):
    j = pl.program_id(2)

    @pl.when(j == 0)
    def _():
        u_scr[...] = _normmod(h_ref[0], g_ref[...], sh_ref[...], sc_ref[...]).astype(BF16)
        acc_scr[...] = jnp.zeros_like(acc_scr)

    u = u_scr[...]
    gate = _dot(u, wg_ref[...])
    up = _dot(u, wu_ref[...])
    act = (gate * _sigmoid(gate) * up).astype(BF16)
    acc_scr[...] += _dot(act, wo_ref[...])

    @pl.when(j == pl.num_programs(2) - 1)
    def _():
        o_ref[0] = h_ref[0] + (0.5 * (1.0 + gt_ref[...])) * acc_scr[...]


def _ffn(h, norm_g3, mod5, w_in, w_out, l, sub, which):
    b, s, d = h.shape
    f = w_out.shape[2]
    tm = min(1024, s)
    tf = 256
    nf = f // tf
    return pl.pallas_call(
        _ffn_kernel,
        grid=(b, s // tm, nf),
        in_specs=[pl.BlockSpec((1, tm, d), lambda bb, i, j: (bb, i, 0)),
                  pl.BlockSpec((None, 1, d), lambda bb, i, j: (l * 3 + sub, 0, 0)),
                  _mod_block(d, l, sub * 3 + 0),
                  _mod_block(d, l, sub * 3 + 1),
                  _mod_block(d, l, sub * 3 + 2),
                  pl.BlockSpec((None, None, d, tf), lambda bb, i, j: (l, which, 0, j)),
                  pl.BlockSpec((None, None, d, tf), lambda bb, i, j: (l, which, 0, j + nf)),
                  pl.BlockSpec((None, None, tf, d), lambda bb, i, j: (l, which, j, 0))],
        out_specs=pl.BlockSpec((1, tm, d), lambda bb, i, j: (bb, i, 0)),
        out_shape=jax.ShapeDtypeStruct((b, s, d), F32),
        scratch_shapes=[pltpu.VMEM((tm, d), BF16), pltpu.VMEM((tm, d), F32)],
        compiler_params=_params("parallel", "parallel", "arbitrary"),
        name="swiglu_half_step",
    )(h, norm_g3, mod5, mod5, mod5, w_in, w_in, w_out)


def _qkv_kernel(h_ref, g_ref, sh_ref, sc_ref, w_ref, gain_ref, cos_ref, sin_ref, bd_ref, o_ref, u_scr, uf_scr,
                *, rate):
    n = pl.program_id(2)
    tm = u_scr.shape[0]
    seg = tm // rate

    @pl.when(n == 0)
    def _():
        u = _normmod(h_ref[0], g_ref[...], sh_ref[...], sc_ref[...])
        if rate == 1:
            u_scr[...] = u.astype(BF16)
        else:
            for cb in range(u.shape[1] // LANES):
                sl = slice(cb * LANES, (cb + 1) * LANES)
                uf_scr[cb] = u[:, sl]
                for res in range(rate):
                    u_scr[res * seg:(res + 1) * seg, sl] = uf_scr[cb, pl.ds(res, seg, stride=rate), :].astype(BF16)

    n_grp = o_ref.shape[1]

    def store(grp, z):
        for res in range(rate):
            o_ref[res, grp] = z[res * seg:(res + 1) * seg]

    @pl.when(n == 2)
    def _():
        for grp in range(n_grp):
            store(grp, _dot(u_scr[...], w_ref[:, grp * MXU_WIDTH:(grp + 1) * MXU_WIDTH]).astype(BF16))

    @pl.when(n != 2)
    def _():
        cos = cos_ref[...]
        sin = sin_ref[...]
        bd = bd_ref[...]
        def project(grp):
            return _dot(u_scr[...], w_ref[:, grp * MXU_WIDTH:(grp + 1) * MXU_WIDTH])

        y = project(0)
        for grp in range(n_grp):
            y_next = project(grp + 1) if grp + 1 < n_grp else None
            z = _head_rms(y, bd) * gain_ref[:, grp * MXU_WIDTH:(grp + 1) * MXU_WIDTH]
            x1, x2 = z[:, :LANES], z[:, LANES:]
            store(grp, jnp.concatenate([x1 * cos - x2 * sin, x1 * sin + x2 * cos], axis=1).astype(BF16))
            y = y_next


def _qkv_proj(h, norm_g3, mod5, w, gain, cos, sin, bd, l, g):
    b, s, d = h.shape
    rate = DIL_RATES[g]
    tm = min(1024, s)
    seg = tm // rate
    tn = N_HEADS * HEAD_DIM
    n_grp = tn // MXU_WIDTH

    def residue_major(t):
        return t.reshape(s // tm, seg, rate, LANES).transpose(0, 2, 1, 3).reshape(s, LANES)

    return pl.pallas_call(
        functools.partial(_qkv_kernel, rate=rate),
        grid=(b, s // tm, 3),
        in_specs=[pl.BlockSpec((1, tm, d), lambda bb, i, j: (bb, i, 0)),
                  pl.BlockSpec((None, 1, d), lambda bb, i, j: (l * 3 + 1, 0, 0)),
                  _mod_block(d, l, 3),
                  _mod_block(d, l, 4),
                  pl.BlockSpec((d, tn), lambda bb, i, j: (0, 3 * g + j)),
                  pl.BlockSpec((None, 1, tn), lambda bb, i, j: (3 * g + j, 0, 0)),
                  pl.BlockSpec((tm, LANES), lambda bb, i, j: (i, 0)),
                  pl.BlockSpec((tm, LANES), lambda bb, i, j: (i, 0)),
                  pl.BlockSpec(bd.shape, lambda bb, i, j: (0, 0))],
        out_specs=pl.BlockSpec((None, rate, n_grp, seg, MXU_WIDTH), lambda bb, i, j: (bb, 0, j, i, 0)),
        out_shape=jax.ShapeDtypeStruct((b, rate, 3 * n_grp, s // rate, MXU_WIDTH), BF16),
        scratch_shapes=[pltpu.VMEM((tm, d), BF16), pltpu.VMEM((d // LANES, tm, LANES), F32)],
        compiler_params=_params("parallel", "parallel", "arbitrary"),
        name=f"qkv_proj_rate{rate}",
    )(h, norm_g3, mod5, mod5, w, gain, residue_major(cos), residue_major(sin), bd)


def _dil_body(bias_ref, q_ref, kp_ref, kc_ref, vp_ref, vc_ref, o_ref, lse_ref, *, tq, shifted):
    i = pl.program_id(2)
    grp = pl.program_id(3)
    hpg = MXU_WIDTH // HEAD_DIM
    qlane = lax.broadcasted_iota(jnp.int32, (Q_ROWS, MXU_WIDTH), 1)
    q_head = (qlane & (LANES - 1)) // (HEAD_DIM // 2)
    v_head = qlane // HEAD_DIM
    seg = lax.broadcasted_iota(jnp.int32, (Q_ROWS, LANES), 1) // (LANES // N_HEADS)

    def scores(j):
        q = q_ref[j * Q_ROWS:(j + 1) * Q_ROWS, :]
        zero = jnp.zeros_like(q)
        q2 = jnp.concatenate([jnp.where(q_head == t, q, zero) for t in range(hpg)], axis=0)
        if j == 0:
            k = jnp.concatenate([kp_ref[...], kc_ref[0:Q_ROWS, :]], axis=0)
        else:
            k = kc_ref[(j - 1) * Q_ROWS:(j + 1) * Q_ROWS, :]
        return _dot_t(q2, k)

    for j in range(tq // Q_ROWS):
        rs = slice(j * Q_ROWS, (j + 1) * Q_ROWS)
        s = scores(j) + bias_ref[...]
        if j == 0:
            v = jnp.concatenate([vp_ref[...], vc_ref[0:Q_ROWS, :]], axis=0)
            prev_key = lax.broadcasted_iota(jnp.int32, s.shape, 1) < Q_ROWS
            s = jnp.where(prev_key & (i == 0), NEG, s)
        else:
            v = vc_ref[(j - 1) * Q_ROWS:(j + 1) * Q_ROWS, :]
        if shifted:
            m = jnp.max(s, axis=-1, keepdims=True)
            p = jnp.exp2(s - m)
        else:
            p = jnp.exp2(s)
        den = jnp.sum(p, axis=-1, keepdims=True)
        o2 = _dot(p.astype(BF16), v) / den
        lse2 = jnp.log(den) * LOG2E
        if shifted:
            lse2 = lse2 + m
        out = o2[0:Q_ROWS]
        new = jnp.where(grp == 0, 0.0, lse_ref[0, rs, :])
        for t in range(hpg):
            ts = slice(t * Q_ROWS, (t + 1) * Q_ROWS)
            if t:
                out = jnp.where(v_head == t, o2[ts], out)
            new = jnp.where(seg == hpg * grp + t, lse2[ts], new)
        o_ref[0, rs, :] = out.astype(BF16)
        lse_ref[0, rs, :] = new


def _dil_kernel(small_ref, *refs, tq):
    @pl.when(small_ref[0] != 0)
    def _():
        _dil_body(*refs, tq=tq, shifted=False)

    @pl.when(small_ref[0] == 0)
    def _():
        _dil_body(*refs, tq=tq, shifted=True)


def _dilated_attention(small, qkv_r, g):
    b, rate, n3, l, width = qkv_r.shape
    s = l * rate
    n_back = DIL_WINDOWS[g] // rate
    assert n_back <= Q_ROWS and rate == DIL_RATES[g] and width == MXU_WIDTH
    tq = min(512, l)
    hdim = N_HEADS * HEAD_DIM
    n_grp = n3 // 3
    sub = tq // Q_ROWS

    def cur(off):
        return pl.BlockSpec((None, None, None, tq, width), lambda bb, r, i, gp: (bb, r, off + gp, i, 0))

    def prev(off):
        return pl.BlockSpec((None, None, None, Q_ROWS, width),
                            lambda bb, r, i, gp: (bb, r, off + gp, jnp.maximum(i * sub - 1, 0), 0))

    rows = (width // HEAD_DIM) * Q_ROWS
    dist = Q_ROWS + (jnp.arange(rows)[:, None] % Q_ROWS) - jnp.arange(2 * Q_ROWS)[None, :]
    bias = jnp.where((dist >= 0) & (dist <= n_back), 0.0, NEG).astype(F32)

    o, lse = pl.pallas_call(
        functools.partial(_dil_kernel, tq=tq),
        grid=(b, rate, l // tq, n_grp),
        in_specs=[pl.BlockSpec(memory_space=pltpu.SMEM),
                  pl.BlockSpec(bias.shape, lambda bb, r, i, gp: (0, 0)),
                  cur(0), prev(n_grp), cur(n_grp), prev(2 * n_grp), cur(2 * n_grp)],
        out_specs=[pl.BlockSpec((1, None, tq, width), lambda bb, r, i, gp: (bb, r, i, gp)),
                   pl.BlockSpec((1, tq, LANES), lambda bb, r, i, gp: (bb, i, r))],
        out_shape=[jax.ShapeDtypeStruct((b, rate, l, hdim), BF16),
                   jax.ShapeDtypeStruct((b, l, rate * LANES), F32)],
        compiler_params=_params("parallel", "parallel", "parallel", "arbitrary"),
        name=f"dilated_attention_rate{rate}",
    )(small, bias, qkv_r, qkv_r, qkv_r, qkv_r, qkv_r)
    return o, lse.reshape(b, s, LANES)


def _merge_out_kernel(o0_ref, o1_ref, o2_ref, l0_ref, l1_ref, l2_ref, p1_ref, p2_ref, e_ref, h_ref, gt_ref, w_ref,
                      out_ref):
    l0, l1, l2 = l0_ref[0], l1_ref[0], l2_ref[0]
    m = jnp.maximum(jnp.maximum(l0, l1), l2)
    e0, e1, e2 = jnp.exp2(l0 - m), jnp.exp2(l1 - m), jnp.exp2(l2 - m)
    inv = 1.0 / (e0 + e1 + e2)
    spread = e_ref[...]

    def positions(o_ref, p_ref):
        x = jnp.concatenate([o_ref[res] for res in range(o_ref.shape[0])], axis=0)
        return x.astype(F32) if p_ref is None else _dot(p_ref[...], x)

    o = (_dot((e0 * inv).astype(BF16), spread) * positions(o0_ref, None)
         + _dot((e1 * inv).astype(BF16), spread) * positions(o1_ref, p1_ref)
         + _dot((e2 * inv).astype(BF16), spread) * positions(o2_ref, p2_ref))
    out_ref[0] = h_ref[0] + (1.0 + gt_ref[...]) * _dot(o.astype(BF16), w_ref[...])


def _merge_out(os_, lses, h, mod5, w_o, l):
    b, s, d = h.shape
    hdim = w_o.shape[0]
    tm = min(512, s)
    row = lambda width: pl.BlockSpec((1, tm, width), lambda bb, i: (bb, i, 0))
    rep = LANES // N_HEADS
    lane = jnp.arange(LANES)[:, None]
    col = jnp.arange(hdim)[None, :]
    spread = ((lane % rep == 0) & (lane // rep == col // HEAD_DIM)).astype(BF16)

    def unpermute(rate):
        pos = jnp.arange(tm)
        return (((pos % rate) * (tm // rate) + pos // rate)[:, None] == jnp.arange(tm)[None, :]).astype(BF16)

    o_specs = [pl.BlockSpec((None, o.shape[1], tm // o.shape[1], hdim), lambda bb, i: (bb, 0, i, 0)) for o in os_]
    const = lambda shape: pl.BlockSpec(shape, lambda bb, i: (0, 0))
    return pl.pallas_call(
        _merge_out_kernel,
        grid=(b, s // tm),
        in_specs=o_specs + [row(LANES)] * 3
        + [const((tm, tm)), const((tm, tm)), const((LANES, hdim)), row(d), _mod_block(d, l, 5), const((hdim, d))],
        out_specs=row(d),
        out_shape=jax.ShapeDtypeStruct((b, s, d), F32),
        compiler_params=_params("parallel", "parallel"),
        name="merge_out_proj",
    )(*os_, *lses, unpermute(os_[1].shape[1]), unpermute(os_[2].shape[1]), spread, h, mod5, w_o)


def _out_kernel(o_ref, h_ref, gt_ref, w_ref, out_ref):
    out_ref[0] = h_ref[0] + (1.0 + gt_ref[...]) * _dot(o_ref[0], w_ref[...])


def _out_proj(o, h, mod5, w_o, l):
    b, s, d = h.shape
    hdim = w_o.shape[0]
    tm = min(1024, s)
    row = lambda width: pl.BlockSpec((1, tm, width), lambda bb, i: (bb, i, 0))
    return pl.pallas_call(
        _out_kernel,
        grid=(b, s // tm),
        in_specs=[row(hdim), row(d), _mod_block(d, l, 5), pl.BlockSpec((hdim, d), lambda bb, i: (0, 0))],
        out_specs=row(d),
        out_shape=jax.ShapeDtypeStruct((b, s, d), F32),
        compiler_params=_params("parallel", "parallel"),
        name="out_proj",
    )(o, h, mod5, w_o)


def _kv_kernel(h_ref, g_ref, sh_ref, sc_ref, w_ref, gain_ref, cos_ref, sin_ref, bd_ref,
               ks_ref, kw_ref, vs_ref, vw_ref, kvc_ref, *, tm):
    i = pl.program_id(1)
    u = _normmod(h_ref[0], g_ref[...], sh_ref[...], sc_ref[...]).astype(BF16)
    y = _dot(u, w_ref[...])
    cos = cos_ref[...]
    sin = sin_ref[...]
    bd = bd_ref[...]
    lane = lax.broadcasted_iota(jnp.int32, (tm, LANES), 1)
    pos = i * tm + lax.broadcasted_iota(jnp.int32, (tm, LANES), 0)
    onehot = jnp.where((pos >> SEL_SHIFT) == lane, 1.0, 0.0).astype(BF16)
    for j in range(4):
        z = _head_rms(y[:, j * LANES:(j + 1) * LANES], bd) * gain_ref[j:j + 1, :]
        zr = _rope(z, cos, sin).astype(BF16)
        if j < 2:
            ks_ref[0, :, 2 * j * LANES:(2 * j + 1) * LANES] = zr
            ks_ref[0, :, (2 * j + 1) * LANES:(2 * j + 2) * LANES] = onehot
        else:
            kw_ref[0, :, (j - 2) * LANES:(j - 1) * LANES] = zr
    for j in range(4):
        v = y[:, (4 + j) * LANES:(5 + j) * LANES]
        va = jnp.where(lane < HEAD_DIM, v, 1.0).astype(BF16)
        ref = vs_ref if j < 2 else vw_ref
        ref[0, :, (j % 2) * LANES:(j % 2 + 1) * LANES] = va
    kvc_ref[0] = y[:, 8 * LANES:10 * LANES]


def _kv_proj(h, kv_norm_g, kvmod5, w, gain, cos, sin, bd):
    b, s, d = h.shape
    n = w.shape[1]
    tm = min(512, s)
    row = lambda width, dt: (pl.BlockSpec((1, tm, width), lambda bb, i: (bb, i, 0)),
                             jax.ShapeDtypeStruct((b, s, width), dt))
    outs = [row(4 * LANES, BF16), row(2 * LANES, BF16), row(2 * LANES, BF16), row(2 * LANES, BF16),
            row(2 * LANES, F32)]
    return pl.pallas_call(
        functools.partial(_kv_kernel, tm=tm),
        grid=(b, s // tm),
        in_specs=[pl.BlockSpec((1, tm, d), lambda bb, i: (bb, i, 0)),
                  pl.BlockSpec((1, d), lambda bb, i: (0, 0)),
                  _mod_block(d, 0, 0),
                  _mod_block(d, 0, 1),
                  pl.BlockSpec((d, n), lambda bb, i: (0, 0)),
                  pl.BlockSpec((8, LANES), lambda bb, i: (0, 0)),
                  pl.BlockSpec((tm, LANES), lambda bb, i: (i, 0)),
                  pl.BlockSpec((tm, LANES), lambda bb, i: (i, 0)),
                  pl.BlockSpec((LANES, LANES), lambda bb, i: (0, 0))],
        out_specs=[o[0] for o in outs],
        out_shape=[o[1] for o in outs],
        compiler_params=_params("parallel", "parallel"),
        name="shared_kv_proj",
    )(h, kv_norm_g, kvmod5, kvmod5, w, gain, cos, sin, bd)


def _cmp_kernel(t_ref, w1_ref, w2_ref, pos_ref, gain_ref, o_ref):
    is_key = pl.program_id(0) < N_KV_HEADS
    t = t_ref[...]
    w1 = w1_ref[...]
    half = w1.shape[0] // 2
    first = _split_dot(t, w1[:half])
    second = _split_dot(t, w1[half:])
    second = pltpu.roll(second, second.shape[0] - 1, 0)
    posb = _split_dot(jnp.broadcast_to(pos_ref[...], (8, w1.shape[0])), w1)[0:1]
    hid = first + second + posb
    out = _split_dot(hid * _sigmoid(hid), w2_ref[...])
    key = out * lax.rsqrt(jnp.mean(out * out, axis=-1, keepdims=True) + EPS) * gain_ref[...]
    dup = jnp.concatenate([key, key], axis=-1)
    aug = jnp.concatenate([out, jnp.ones_like(out)], axis=-1)
    o_ref[...] = jnp.where(is_key, dup, aug).astype(BF16)


def _compress(t4, phi_w1, phi_w2, cmp_pos, gain):
    _, b, nsb, width = t4.shape
    hid = phi_w1.shape[2]
    return pl.pallas_call(
        _cmp_kernel,
        grid=(4, b),
        in_specs=[pl.BlockSpec((None, None, nsb, width), lambda t, bb: (t, bb, 0, 0)),
                  pl.BlockSpec((None, CMP_LEN * HEAD_DIM, hid), lambda t, bb: (t // N_KV_HEADS, 0, 0)),
                  pl.BlockSpec((None, hid, HEAD_DIM), lambda t, bb: (t // N_KV_HEADS, 0, 0)),
                  pl.BlockSpec((None, 1, CMP_LEN * HEAD_DIM), lambda t, bb: (t // N_KV_HEADS, 0, 0)),
                  pl.BlockSpec((1, HEAD_DIM), lambda t, bb: (0, 0))],
        out_specs=pl.BlockSpec((None, None, nsb, LANES), lambda t, bb: (t, bb, 0, 0)),
        out_shape=jax.ShapeDtypeStruct((4, b, nsb, LANES), BF16),
        compiler_params=_params("parallel", "parallel"),
        name="compress_blocks",
    )(t4, phi_w1, phi_w2, cmp_pos, gain)


def _qg_kernel(h_ref, g_ref, sh_ref, sc_ref, w_ref, gain_ref, cos_ref, sin_ref, bd_ref,
               qn_ref, qr_ref, gate_ref):
    u = _normmod(h_ref[0], g_ref[...], sh_ref[...], sc_ref[...]).astype(BF16)
    y = _dot(u, w_ref[...])
    cos = cos_ref[...]
    sin = sin_ref[...]
    bd = bd_ref[...]
    hdim = qn_ref.shape[2]
    for cb in range(hdim // LANES):
        sl = slice(cb * LANES, (cb + 1) * LANES)
        z = _head_rms(y[:, sl], bd) * gain_ref[:, sl]
        qn_ref[0, :, sl] = z.astype(BF16)
        qr_ref[0, :, sl] = _rope(z, cos, sin).astype(BF16)
    gate_ref[0] = _sigmoid(y[:, hdim:])


def _qg_proj(h, norm_g3, mod5, w, gain, cos, sin, bd, l):
    b, s, d = h.shape
    n = w.shape[1]
    hdim = N_HEADS * HEAD_DIM
    tm = min(512, s)
    row = lambda width, dt: (pl.BlockSpec((1, tm, width), lambda bb, i: (bb, i, 0)),
                             jax.ShapeDtypeStruct((b, s, width), dt))
    outs = [row(hdim, BF16), row(hdim, BF16), row(n - hdim, F32)]
    return pl.pallas_call(
        _qg_kernel,
        grid=(b, s // tm),
        in_specs=[pl.BlockSpec((1, tm, d), lambda bb, i: (bb, i, 0)),
                  pl.BlockSpec((None, 1, d), lambda bb, i: (l * 3 + 1, 0, 0)),
                  _mod_block(d, l, 3),
                  _mod_block(d, l, 4),
                  pl.BlockSpec((d, n), lambda bb, i: (0, 0)),
                  pl.BlockSpec((1, hdim), lambda bb, i: (0, 0)),
                  pl.BlockSpec((tm, LANES), lambda bb, i: (i, 0)),
                  pl.BlockSpec((tm, LANES), lambda bb, i: (i, 0)),
                  pl.BlockSpec((LANES, LANES), lambda bb, i: (0, 0))],
        out_specs=[o[0] for o in outs],
        out_shape=[o[1] for o in outs],
        compiler_params=_params("parallel", "parallel"),
        name="nsa_query_gate_proj",
    )(h, norm_g3, mod5, mod5, w, gain, cos, sin, bd)


def _nsa_body(qn_ref, qr_ref, gate_ref, kc_ref, vc_ref, ks_ref, vs_ref, kw_ref, vw_ref, wimpt_ref, gsel_ref,
              o_ref, qa_scr, m_scr, acc_scr, mw_scr, accw_scr, *, tq, shifted):
    i = pl.program_id(2)
    start = i * tq
    rows = GROUP * tq
    ncp = kc_ref.shape[0]
    low = lax.broadcasted_iota(jnp.int32, (tq, LANES), 1) < HEAD_DIM
    low_r = lax.broadcasted_iota(jnp.int32, (rows, LANES), 1) < HEAD_DIM

    def accumulate(s, v, mask, first, acc_ref=acc_scr, m_ref=m_scr):
        if not shifted:
            p = jnp.exp2(s)
            if mask is not None:
                p = jnp.where(mask, p, 0.0)
            pv = _dot(p.astype(BF16), v)
            acc_ref[...] = pv if first else acc_ref[...] + pv
            return
        if mask is not None:
            s = jnp.where(mask, s, NEG)
        if first:
            m_new = jnp.max(s, axis=-1, keepdims=True)
            acc_ref[...] = _dot(jnp.exp2(s - m_new).astype(BF16), v)
        else:
            m_old = m_ref[...]
            m_new = jnp.maximum(m_old, jnp.max(s, axis=-1, keepdims=True))
            acc_ref[...] = jnp.exp2(m_old - m_new) * acc_ref[...] + _dot(jnp.exp2(s - m_new).astype(BF16), v)
        m_ref[...] = m_new

    def stack(q):
        parts = []
        for g in range(GROUP):
            blk = q[:, (g // 2) * LANES:(g // 2 + 1) * LANES]
            parts.append(jnp.where(low if g % 2 == 0 else ~low, blk, jnp.zeros_like(blk)))
        return parts

    qn = jnp.concatenate(stack(qn_ref[0]), axis=0)
    sc = _dot_t(qn, kc_ref[...])
    c_end = lax.broadcasted_iota(jnp.int32, (tq, ncp), 1) * CMP_STRIDE + (CMP_LEN - 1)
    valid_t = c_end <= start + lax.broadcasted_iota(jnp.int32, (tq, ncp), 0)
    if shifted:
        valid = jnp.concatenate([valid_t] * GROUP, axis=0)
        sc = jnp.where(valid, sc, NEG)
        sc = sc - jnp.max(sc, axis=-1, keepdims=True)
        p = jnp.where(valid, jnp.exp2(sc), 0.0)
    else:
        p = jnp.exp2(sc + jnp.concatenate([jnp.where(valid_t, 0.0, NEG)] * GROUP, axis=0))
    oc = _dot(p.astype(BF16), vc_ref[...])
    inv = 1.0 / jnp.maximum(jnp.where(low_r, pltpu.roll(oc, HEAD_DIM, 1), oc), 1e-30)
    o_cmp = oc * inv
    psum_blocks = []
    for cb in range(ncp // LANES):
        cs = slice(cb * LANES, (cb + 1) * LANES)
        tot = p[0:tq, cs] * inv[0:tq]
        for g in range(1, GROUP):
            tot = tot + p[g * tq:(g + 1) * tq, cs] * inv[g * tq:(g + 1) * tq]
        psum_blocks.append(tot)
    psum = jnp.concatenate(psum_blocks, axis=1)

    qr_parts = stack(qr_ref[0])
    for g in range(GROUP):
        qa_scr[g * tq:(g + 1) * tq, 0:LANES] = qr_parts[g]
    qr = jnp.concatenate(qr_parts, axis=0)

    tri = (lax.broadcasted_iota(jnp.int32, (rows, tq), 1)
           <= (lax.broadcasted_iota(jnp.int32, (rows, tq), 0) & (tq - 1)))

    def key_block(kb):
        return pl.ds(pl.multiple_of(kb * tq, tq), tq)

    diag = key_block(i)
    accumulate(_dot_t(qr, kw_ref[diag, :]), vw_ref[diag, :], tri, True, accw_scr, mw_scr)
    blk = key_block(jnp.maximum(i - 1, 0))
    accumulate(_dot_t(qr, kw_ref[blk, :]), vw_ref[blk, :], jnp.broadcast_to(i >= 1, tri.shape), False,
               accw_scr, mw_scr)
    blk = key_block(jnp.maximum(i - 2, 0))
    accumulate(_dot_t(qr, kw_ref[blk, :]), vw_ref[blk, :], (~tri) & (i >= 2), False, accw_scr, mw_scr)

    p_hi = psum.astype(BF16)
    p_lo = (psum - p_hi.astype(F32)).astype(BF16)
    imp_t = _dot_t(wimpt_ref[...], p_hi) + _dot_t(wimpt_ref[...], p_lo)
    blk_id = lax.broadcasted_iota(jnp.int32, (LANES, tq), 0)
    jt = (start + lax.broadcasted_iota(jnp.int32, (LANES, tq), 1)) >> SEL_SHIFT
    forced = (blk_id == 0) | (blk_id == jt) | (blk_id == jt - 1)
    work = jnp.where(blk_id > jt, -1.0, jnp.where(forced, -2.0, imp_t))
    sel = forced
    for _ in range(N_SELECT - 3):
        mx = jnp.max(work, axis=0, keepdims=True)
        idx = jnp.min(jnp.where(work == mx, blk_id, LANES), axis=0, keepdims=True)
        pick = blk_id == idx
        sel = sel | pick
        work = jnp.where(pick, -2.0, work)
    sel_bias = jnp.transpose(jnp.where(sel, 0.0, NEG)).astype(BF16)

    for g in range(GROUP):
        qa_scr[g * tq:(g + 1) * tq, LANES:2 * LANES] = sel_bias

    accumulate(_dot_t(qa_scr[...], ks_ref[diag, :]), vs_ref[diag, :], tri, True)

    def slc_pair(kp, carry):
        blk = pl.ds(pl.multiple_of(kp * 2 * tq, 2 * tq), 2 * tq)
        accumulate(_dot_t(qa_scr[...], ks_ref[blk, :]), vs_ref[blk, :], None, False)
        return carry

    lax.fori_loop(0, i // 2, slc_pair, 0)

    @pl.when(i % 2 == 1)
    def _():
        blk = key_block(i - 1)
        accumulate(_dot_t(qa_scr[...], ks_ref[blk, :]), vs_ref[blk, :], None, False)

    gates = gate_ref[0]
    g_hi = gates.astype(BF16)
    g_lo = (gates - g_hi.astype(F32)).astype(BF16)
    gate_wide = _dot(g_hi, gsel_ref[...]) + _dot(g_lo, gsel_ref[...])
    for pr in range(GROUP // 2):
        ev = slice(2 * pr * tq, (2 * pr + 1) * tq)
        od = slice((2 * pr + 1) * tq, (2 * pr + 2) * tq)

        def pair(acc):
            a_e, a_o = acc[ev], acc[od]
            r_e = pltpu.roll(a_e, HEAD_DIM, 1)
            r_o = pltpu.roll(a_o, HEAD_DIM, 1)
            return jnp.where(low, a_e, r_o) / jnp.where(low, r_e, a_o)

        def gate(br):
            c = (br * (GROUP // 2) + pr) * LANES
            return gate_wide[:, c:c + LANES]

        o_c = jnp.where(low, o_cmp[ev], pltpu.roll(o_cmp[od], HEAD_DIM, 1))
        out = gate(0) * o_c + gate(1) * pair(acc_scr) + gate(2) * pair(accw_scr)
        o_ref[0, :, pr * LANES:(pr + 1) * LANES] = out.astype(BF16)


def _nsa_kernel(small_ref, *refs, tq):
    @pl.when(small_ref[0] != 0)
    def _():
        _nsa_body(*refs, tq=tq, shifted=False)

    @pl.when(small_ref[0] == 0)
    def _():
        _nsa_body(*refs, tq=tq, shifted=True)


def _nsa_attention(small, qn, qr, gates, cmp4, ks, vs, kw, vw, wimpt):
    b, s, hdim = qn.shape
    tq = min(256, s)
    assert s // SEL_BLOCK <= LANES and s % tq == 0 and SLIDE_WINDOW == 2 * tq
    ncp = cmp4.shape[2]
    gw = GROUP * HEAD_DIM
    rows = GROUP * tq
    qspec = pl.BlockSpec((1, tq, gw), lambda bb, kv, i: (bb, i, kv))
    full = lambda width: pl.BlockSpec((None, s, width), lambda bb, kv, i: (bb, 0, kv))
    col = jnp.arange(3 * (GROUP // 2) * LANES)[None, :]
    src = (col // LANES // (GROUP // 2)) * GROUP + 2 * ((col // LANES) % (GROUP // 2)) + (col % LANES) // HEAD_DIM
    gsel = (jnp.arange(LANES)[:, None] == src).astype(BF16)
    return pl.pallas_call(
        functools.partial(_nsa_kernel, tq=tq),
        grid=(b, N_KV_HEADS, s // tq),
        in_specs=[pl.BlockSpec(memory_space=pltpu.SMEM), qspec, qspec,
                  pl.BlockSpec((1, tq, LANES), lambda bb, kv, i: (bb, i, kv)),
                  pl.BlockSpec((None, None, ncp, LANES), lambda bb, kv, i: (kv, bb, 0, 0)),
                  pl.BlockSpec((None, None, ncp, LANES), lambda bb, kv, i: (N_KV_HEADS + kv, bb, 0, 0)),
                  full(2 * LANES), full(LANES), full(LANES), full(LANES),
                  pl.BlockSpec((LANES, ncp), lambda bb, kv, i: (0, 0)),
                  pl.BlockSpec(gsel.shape, lambda bb, kv, i: (0, 0))],
        out_specs=qspec,
        out_shape=jax.ShapeDtypeStruct((b, s, hdim), BF16),
        scratch_shapes=[pltpu.VMEM((rows, 2 * LANES), BF16),
                        pltpu.VMEM((rows, 1), F32), pltpu.VMEM((rows, LANES), F32),
                        pltpu.VMEM((rows, 1), F32), pltpu.VMEM((rows, LANES), F32)],
        compiler_params=_params("parallel", "parallel", "arbitrary"),
        name="nsa_attention",
    )(small, qn, qr, gates, cmp4, cmp4, ks, vs, kw, vw, wimpt, gsel)


def _rope_tables(s):
    half = HEAD_DIM // 2
    inv = ROPE_THETA ** (-jnp.arange(half, dtype=F32) / half)
    ang = jnp.arange(s, dtype=F32)[:, None] * inv[None, :]
    cos, sin = jnp.cos(ang), jnp.sin(ang)
    reps = LANES // HEAD_DIM
    return (jnp.tile(cos, (1, 2 * reps)), jnp.tile(jnp.concatenate([-sin, sin], axis=1), (1, reps)),
            jnp.tile(sin, (1, 2 * reps)))


def _split_half_perm(hdim):
    hpg = MXU_WIDTH // HEAD_DIM
    half = HEAD_DIM // 2
    c = jnp.arange(hdim)
    grp, within = c // MXU_WIDTH, c % MXU_WIDTH
    part, head, dim = within // LANES, (within % LANES) // half, within % half
    return (grp * hpg + head) * HEAD_DIM + part * half + dim


def _importance_weights(ncp):
    r = SEL_BLOCK // CMP_STRIDE
    nper = CMP_LEN // CMP_STRIDE
    c = jnp.arange(ncp)[:, None]
    j = jnp.arange(LANES)[None, :]
    o = c - r * j
    w = jnp.zeros((ncp, LANES), F32)
    for off in range(-(nper - 1), r):
        cnt = sum(1 for m in range(r) for n in range(nper) if m - n == off)
        w = w + jnp.where(o == off, float(cnt), 0.0)
    return w.astype(BF16)


def kernel(x, c, norm_g, w_ada, b_ada, ffn_w_in, ffn_w_out, a_w_qkv, a_q_gain, a_k_gain, a_w_o, kv_norm_g,
           w_ada_kv, b_ada_kv, w_kv, kv_k_gain, cmp_pos, phi_w1, phi_w2, b_w_qg, b_q_gain, b_w_o):
    b, s, d = x.shape
    depth = norm_g.shape[0]
    n_a = depth // 2
    hdim = N_HEADS * HEAD_DIM
    assert depth == 2 and b <= 8

    c8 = jnp.zeros((8, d), F32).at[:b].set(c)
    mod = _ada(c8, w_ada, b_ada.reshape(depth, 1, 9 * d))
    mod5 = mod[:, :b].reshape(depth, b, 9, 1, d)
    kvmod = _ada(c8, w_ada_kv[None], b_ada_kv.reshape(1, 1, 2 * d))
    kvmod5 = kvmod[:, :b].reshape(1, b, 2, 1, d)

    norm_g3 = norm_g.reshape(depth * 3, 1, d)
    w_in = ffn_w_in.astype(BF16)
    w_out = ffn_w_out.astype(BF16)
    cos, sin, sin_plain = _rope_tables(s)
    blk = jnp.arange(LANES) // HEAD_DIM
    bd = (blk[:, None] == blk[None, :]).astype(BF16)
    blk = (jnp.arange(MXU_WIDTH) % LANES) // (HEAD_DIM // 2)
    bd_split = (blk[:, None] == blk[None, :]).astype(BF16)
    perm = _split_half_perm(hdim)
    q_scale = HEAD_DIM ** -0.5 * LOG2E

    h = x
    for l in range(depth):
        h = _ffn(h, norm_g3, mod5, w_in, w_out, l, 0, 0)
        if l < n_a:
            gain = jnp.stack([jnp.tile(a_q_gain[l] * q_scale, (1, N_HEADS))[:, perm],
                              jnp.tile(a_k_gain[l], (1, N_HEADS))[:, perm],
                              jnp.ones((N_DIL, hdim), F32)], axis=1).reshape(3 * N_DIL, 1, hdim)
            w_qkv = a_w_qkv[l].reshape(d, N_DIL, 3, hdim)
            w_qkv = jnp.concatenate([w_qkv[:, :, :2, perm], w_qkv[:, :, 2:]], axis=2)
            w_qkv = w_qkv.reshape(d, N_DIL * 3 * hdim).astype(BF16)
            bound = HEAD_DIM * q_scale * jnp.max(jnp.abs(a_q_gain[l])) * jnp.max(jnp.abs(a_k_gain[l]))
            small = (bound <= SCORE_LIMIT).astype(jnp.int32).reshape(1)
            parts = [_dilated_attention(small, _qkv_proj(h, norm_g3, mod5, w_qkv, gain, cos, sin_plain, bd_split, l, g), g)
                     for g in range(N_DIL)]
            h = _merge_out([p[0] for p in parts], [p[1] for p in parts], h, mod5, a_w_o[l].astype(BF16), l)
        else:
            j = l - n_a
            w_q = b_w_qg[j][:, :hdim]
            w_g = b_w_qg[j][:, hdim:].reshape(d, 3, N_KV_HEADS, GROUP).transpose(0, 2, 1, 3)
            w_g = jnp.pad(w_g.reshape(d, N_KV_HEADS, 3 * GROUP), ((0, 0), (0, 0), (0, LANES - 3 * GROUP)))
            w_qg = jnp.concatenate([w_q, w_g.reshape(d, N_KV_HEADS * LANES)], axis=1).astype(BF16)
            gain = jnp.tile(b_q_gain[j] * q_scale, N_HEADS)[None, :]
            qn, qr, gates = _qg_proj(h, norm_g3, mod5, w_qg, gain, cos, sin, bd, l)
            bound = HEAD_DIM * q_scale * jnp.max(jnp.abs(b_q_gain[j])) * jnp.max(jnp.abs(kv_k_gain))
            small = (bound <= SCORE_LIMIT).astype(jnp.int32).reshape(1)
            o = _nsa_attention(small, qn, qr, gates, *shared)
            h = _out_proj(o, h, mod5, b_w_o[j].astype(BF16), l)
        h = _ffn(h, norm_g3, mod5, w_in, w_out, l, 2, 1)
        if l == n_a - 1:
            col = lambda br, kv, kvh: br * 4 * HEAD_DIM + kv * 2 * HEAD_DIM + kvh * HEAD_DIM + jnp.arange(HEAD_DIM)
            dup = lambda br, kv, kvh: jnp.concatenate([col(br, kv, kvh)] * 2)
            cols = jnp.concatenate(
                [dup(br, 0, kvh) for br in (1, 2) for kvh in range(N_KV_HEADS)]
                + [dup(br, 1, kvh) for br in (1, 2) for kvh in range(N_KV_HEADS)]
                + [col(0, kv, kvh) for kv in range(2) for kvh in range(N_KV_HEADS)])
            w_kv_ext = w_kv[:, cols].astype(BF16)
            kgain = jnp.concatenate([jnp.tile(kv_k_gain[1], (2, 2)), jnp.tile(kv_k_gain[2], (2, 2)),
                                     jnp.ones((4, LANES), F32)], axis=0)
            ks, kw, vs, vw, kvc = _kv_proj(h, kv_norm_g[None, :], kvmod5, w_kv_ext, kgain, cos, sin, bd)
            nsb = s // CMP_STRIDE
            t4 = kvc.reshape(b, nsb, CMP_STRIDE, 4, HEAD_DIM).transpose(3, 0, 1, 2, 4)
            t4 = t4.reshape(4, b, nsb, CMP_STRIDE * HEAD_DIM)
            cmp4 = _compress(t4, phi_w1, phi_w2, cmp_pos.reshape(2, 1, CMP_LEN * HEAD_DIM), kv_k_gain[0][None, :])
            shared = (cmp4, ks, vs, kw, vw, _importance_weights(nsb).T)
    return h
```

```python
import functools

import jax
import jax.numpy as jnp
import numpy as np
from jax import lax
from jax.experimental import pallas as pl
from jax.experimental.pallas import tpu as pltpu

HEAD_DIM = 64
N_HEADS = 16
DIL_WINDOWS = (128, 512, 2048)
DIL_RATES = (1, 4, 16)
N_DIL = 3
N_KV_HEADS = 2
GROUP = N_HEADS // N_KV_HEADS
CMP_STRIDE = 16
CMP_LEN = 32
SEL_BLOCK = 64
SEL_SHIFT = SEL_BLOCK.bit_length() - 1
N_SELECT = 16
SLIDE_WINDOW = 512
ROPE_THETA = 10000.0
EPS = 1e-6

LANES = 128
MXU_WIDTH = 256
Q_ROWS = 128
NEG = -1e30
LOG2E = 1.4426950408889634
SCORE_LIMIT = 60.0
VMEM_LIMIT = 56 * 1024 * 1024

BF16 = jnp.bfloat16
F32 = jnp.float32


def _params(*sem):
    return pltpu.CompilerParams(dimension_semantics=sem, vmem_limit_bytes=VMEM_LIMIT)


def _dot(a, b):
    return jnp.dot(a, b, preferred_element_type=F32)


def _dot_t(a, b):
    return lax.dot_general(a, b, (((1,), (1,)), ((), ())), preferred_element_type=F32)


def _split_dot(a, b):
    a_hi = a.astype(BF16)
    a_lo = (a - a_hi.astype(F32)).astype(BF16)
    b_hi = b.astype(BF16)
    b_lo = (b - b_hi.astype(F32)).astype(BF16)
    return _dot(a_hi, b_hi) + (_dot(a_hi, b_lo) + _dot(a_lo, b_hi))


def _sigmoid(x):
    return 1.0 / (1.0 + jnp.exp(-x))


def _normmod(x, g, shift, scale):
    y = x * lax.rsqrt(jnp.mean(x * x, axis=-1, keepdims=True) + EPS) * g
    return y * (1.0 + scale) + shift


def _head_rms(y, bd):
    ss = _dot((y * y).astype(BF16), bd)
    return y * lax.rsqrt(ss * (1.0 / HEAD_DIM) + EPS)


def _rope(z, cos, sin_signed):
    lane = lax.broadcasted_iota(jnp.int32, z.shape, 1)
    first_half = (lane & (HEAD_DIM - 1)) < HEAD_DIM // 2
    swapped = jnp.where(first_half, pltpu.roll(z, z.shape[1] - HEAD_DIM // 2, 1),
                        pltpu.roll(z, HEAD_DIM // 2, 1))
    return z * cos + swapped * sin_signed


def _ada_kernel(c_ref, w_ref, b_ref, o_ref):
    c = c_ref[...]
    a = (c * _sigmoid(c)).astype(BF16)
    o_ref[...] = _dot(a, w_ref[...].astype(BF16)) + b_ref[...]


def _ada(c8, w, b):
    nl, d, n = w.shape
    tn = 1024
    return pl.pallas_call(
        _ada_kernel,
        grid=(nl, n // tn),
        in_specs=[pl.BlockSpec((8, d), lambda l, j: (0, 0)),
                  pl.BlockSpec((None, d, tn), lambda l, j: (l, 0, j)),
                  pl.BlockSpec((None, 1, tn), lambda l, j: (l, 0, j))],
        out_specs=pl.BlockSpec((None, 8, tn), lambda l, j: (l, 0, j)),
        out_shape=jax.ShapeDtypeStruct((nl, 8, n), F32),
        compiler_params=_params("parallel", "parallel"),
        name="ada_table",
    )(c8, w, b)


def _mod_block(d, l, k):
    return pl.BlockSpec((None, None, None, 1, d), lambda b, *_: (l, b, k, 0, 0))


def _ffn_kernel(h_ref, g_ref, sh_ref, sc_ref, gt_ref, wg_ref, wu_ref, wo_ref, o_ref, u_scr, acc_scr):
    j = pl.program_id(2)

    @pl.when(j == 0)
    def _():
        u_scr[...] = _normmod(h_ref[0], g_ref[...], sh_ref[...], sc_ref[...]).astype(BF16)
        acc_scr[...] = jnp.zeros_like(acc_scr)

    u = u_scr[...]
    gate = _dot(u, wg_ref[...])
    up = _dot(u, wu_ref[...])
    act = (gate * _sigmoid(gate) * up).astype(BF16)
    acc_scr[...] += _dot(act, wo_ref[...])

    @pl.when(j == pl.num_programs(2) - 1)
    def _():
        o_ref[0] = h_ref[0] + (0.5 * (1.0 + gt_ref[...])) * acc_scr[...]


def _ffn(h, norm_g3, mod5, w_in, w_out, l, sub, which):
    b, s, d = h.shape
    f = w_out.shape[2]
    tm = min(1024, s)
    tf = 256
    nf = f // tf
    return pl.pallas_call(
        _ffn_kernel,
        grid=(b, s // tm, nf),
        in_specs=[pl.BlockSpec((1, tm, d), lambda bb, i, j: (bb, i, 0)),
                  pl.BlockSpec((None, 1, d), lambda bb, i, j: (l * 3 + sub, 0, 0)),
                  _mod_block(d, l, sub * 3 + 0),
                  _mod_block(d, l, sub * 3 + 1),
                  _mod_block(d, l, sub * 3 + 2),
                  pl.BlockSpec((None, None, d, tf), lambda bb, i, j: (l, which, 0, j)),
                  pl.BlockSpec((None, None, d, tf), lambda bb, i, j: (l, which, 0, j + nf)),
                  pl.BlockSpec((None, None, tf, d), lambda bb, i, j: (l, which, j, 0))],
        out_specs=pl.BlockSpec((1, tm, d), lambda bb, i, j: (bb, i, 0)),
        out_shape=jax.ShapeDtypeStruct((b, s, d), F32),
        scratch_shapes=[pltpu.VMEM((tm, d), BF16), pltpu.VMEM((tm, d), F32)],
        compiler_params=_params("parallel", "parallel", "arbitrary"),
        name="swiglu_half_step",
    )(h, norm_g3, mod5, mod5, mod5, w_in, w_in, w_out)


def _qkv_kernel(h_ref, g_ref, sh_ref, sc_ref, w_ref, gain_ref, cos_ref, sin_ref, bd_ref, o_ref, u_scr, uf_scr,
                *, rate):
    n = pl.program_id(2)
    tm = u_scr.shape[0]
    seg = tm // rate

    @pl.when(n == 0)
    def _():
        u = _normmod(h_ref[0], g_ref[...], sh_ref[...], sc_ref[...])
        if rate == 1:
            u_scr[...] = u.astype(BF16)
        else:
            for cb in range(u.shape[1] // LANES):
                sl = slice(cb * LANES, (cb + 1) * LANES)
                uf_scr[cb] = u[:, sl]
                for res in range(rate):
                    u_scr[res * seg:(res + 1) * seg, sl] = uf_scr[cb, pl.ds(res, seg, stride=rate), :].astype(BF16)

    n_grp = o_ref.shape[1]

    def store(grp, z):
        for res in range(rate):
            o_ref[res, grp] = z[res * seg:(res + 1) * seg]

    @pl.when(n == 2)
    def _():
        for grp in range(n_grp):
            store(grp, _dot(u_scr[...], w_ref[:, grp * MXU_WIDTH:(grp + 1) * MXU_WIDTH]).astype(BF16))

    @pl.when(n != 2)
    def _():
        cos = cos_ref[...]
        sin = sin_ref[...]
        bd = bd_ref[...]
        def project(grp):
            return _dot(u_scr[...], w_ref[:, grp * MXU_WIDTH:(grp + 1) * MXU_WIDTH])

        y = project(0)
        for grp in range(n_grp):
            y_next = project(grp + 1) if grp + 1 < n_grp else None
            z = _head_rms(y, bd) * gain_ref[:, grp * MXU_WIDTH:(grp + 1) * MXU_WIDTH]
            x1, x2 = z[:, :LANES], z[:, LANES:]
            store(grp, jnp.concatenate([x1 * cos - x2 * sin, x1 * sin + x2 * cos], axis=1).astype(BF16))
            y = y_next


def _qkv_proj(h, norm_g3, mod5, w, gain, bd, l, g):
    b, s, d = h.shape
    rate = DIL_RATES[g]
    tm = min(1024, s)
    seg = tm // rate
    tn = N_HEADS * HEAD_DIM
    n_grp = tn // MXU_WIDTH
    cos, _, sin = _rope_tables(s, tm, rate)

    return pl.pallas_call(
        functools.partial(_qkv_kernel, rate=rate),
        grid=(b, s // tm, 3),
        in_specs=[pl.BlockSpec((1, tm, d), lambda bb, i, j: (bb, i, 0)),
                  pl.BlockSpec((None, 1, d), lambda bb, i, j: (l * 3 + 1, 0, 0)),
                  _mod_block(d, l, 3),
                  _mod_block(d, l, 4),
                  pl.BlockSpec((d, tn), lambda bb, i, j: (0, 3 * g + j)),
                  pl.BlockSpec((None, 1, tn), lambda bb, i, j: (3 * g + j, 0, 0)),
                  pl.BlockSpec((tm, LANES), lambda bb, i, j: (i, 0)),
                  pl.BlockSpec((tm, LANES), lambda bb, i, j: (i, 0)),
                  pl.BlockSpec(bd.shape, lambda bb, i, j: (0, 0))],
        out_specs=pl.BlockSpec((None, rate, n_grp, seg, MXU_WIDTH), lambda bb, i, j: (bb, 0, j, i, 0)),
        out_shape=jax.ShapeDtypeStruct((b, rate, 3 * n_grp, s // rate, MXU_WIDTH), BF16),
        scratch_shapes=[pltpu.VMEM((tm, d), BF16), pltpu.VMEM((d // LANES, tm, LANES), F32)],
        compiler_params=_params("parallel", "parallel", "arbitrary"),
        name=f"qkv_proj_rate{rate}",
    )(h, norm_g3, mod5, mod5, w, gain, cos, sin, bd)


def _dil_body(bias_ref, q_ref, kp_ref, kc_ref, vp_ref, vc_ref, o_ref, lse_ref, *, tq, shifted):
    i = pl.program_id(2)
    grp = pl.program_id(3)
    hpg = MXU_WIDTH // HEAD_DIM
    qlane = lax.broadcasted_iota(jnp.int32, (Q_ROWS, MXU_WIDTH), 1)
    q_head = (qlane & (LANES - 1)) // (HEAD_DIM // 2)
    v_head = qlane // HEAD_DIM
    seg = lax.broadcasted_iota(jnp.int32, (Q_ROWS, LANES), 1) // (LANES // N_HEADS)

    def scores(j):
        q = q_ref[j * Q_ROWS:(j + 1) * Q_ROWS, :]
        zero = jnp.zeros_like(q)
        q2 = jnp.concatenate([jnp.where(q_head == t, q, zero) for t in range(hpg)], axis=0)
        if j == 0:
            k = jnp.concatenate([kp_ref[...], kc_ref[0:Q_ROWS, :]], axis=0)
        else:
            k = kc_ref[(j - 1) * Q_ROWS:(j + 1) * Q_ROWS, :]
        return _dot_t(q2, k)

    for j in range(tq // Q_ROWS):
        rs = slice(j * Q_ROWS, (j + 1) * Q_ROWS)
        s = scores(j) + bias_ref[...]
        if j == 0:
            v = jnp.concatenate([vp_ref[...], vc_ref[0:Q_ROWS, :]], axis=0)
            prev_key = lax.broadcasted_iota(jnp.int32, s.shape, 1) < Q_ROWS
            s = jnp.where(prev_key & (i == 0), NEG, s)
        else:
            v = vc_ref[(j - 1) * Q_ROWS:(j + 1) * Q_ROWS, :]
        if shifted:
            m = jnp.max(s, axis=-1, keepdims=True)
            p = jnp.exp2(s - m)
        else:
            p = jnp.exp2(s)
        den = jnp.sum(p, axis=-1, keepdims=True)
        o2 = _dot(p.astype(BF16), v) / den
        lse2 = jnp.log(den) * LOG2E
        if shifted:
            lse2 = lse2 + m
        out = o2[0:Q_ROWS]
        new = jnp.where(grp == 0, 0.0, lse_ref[0, rs, :])
        for t in range(hpg):
            ts = slice(t * Q_ROWS, (t + 1) * Q_ROWS)
            if t:
                out = jnp.where(v_head == t, o2[ts], out)
            new = jnp.where(seg == hpg * grp + t, lse2[ts], new)
        o_ref[0, rs, :] = out.astype(BF16)
        lse_ref[0, rs, :] = new


def _dil_kernel(small_ref, *refs, tq):
    @pl.when(small_ref[0] != 0)
    def _():
        _dil_body(*refs, tq=tq, shifted=False)

    @pl.when(small_ref[0] == 0)
    def _():
        _dil_body(*refs, tq=tq, shifted=True)


def _dilated_attention(small, qkv_r, g):
    b, rate, n3, l, width = qkv_r.shape
    s = l * rate
    n_back = DIL_WINDOWS[g] // rate
    assert n_back <= Q_ROWS and rate == DIL_RATES[g] and width == MXU_WIDTH
    tq = min(512, l)
    hdim = N_HEADS * HEAD_DIM
    n_grp = n3 // 3
    sub = tq // Q_ROWS

    def cur(off):
        return pl.BlockSpec((None, None, None, tq, width), lambda bb, r, i, gp: (bb, r, off + gp, i, 0))

    def prev(off):
        return pl.BlockSpec((None, None, None, Q_ROWS, width),
                            lambda bb, r, i, gp: (bb, r, off + gp, jnp.maximum(i * sub - 1, 0), 0))

    rows = (width // HEAD_DIM) * Q_ROWS
    dist = Q_ROWS + (np.arange(rows)[:, None] % Q_ROWS) - np.arange(2 * Q_ROWS)[None, :]
    bias = np.where((dist >= 0) & (dist <= n_back), 0.0, NEG).astype(np.float32)

    o, lse = pl.pallas_call(
        functools.partial(_dil_kernel, tq=tq),
        grid=(b, rate, l // tq, n_grp),
        in_specs=[pl.BlockSpec(memory_space=pltpu.SMEM),
                  pl.BlockSpec(bias.shape, lambda bb, r, i, gp: (0, 0)),
                  cur(0), prev(n_grp), cur(n_grp), prev(2 * n_grp), cur(2 * n_grp)],
        out_specs=[pl.BlockSpec((1, None, tq, width), lambda bb, r, i, gp: (bb, r, i, gp)),
                   pl.BlockSpec((1, None, tq, LANES), lambda bb, r, i, gp: (bb, r, i, 0))],
        out_shape=[jax.ShapeDtypeStruct((b, rate, l, hdim), BF16),
                   jax.ShapeDtypeStruct((b, rate, l, LANES), F32)],
        compiler_params=_params("parallel", "parallel", "parallel", "arbitrary"),
        name=f"dilated_attention_rate{rate}",
    )(small, bias, qkv_r, qkv_r, qkv_r, qkv_r, qkv_r)
    return o, lse


def _merge_out_kernel(o0_ref, o1_ref, o2_ref, l0_ref, l1_ref, l2_ref, p1_ref, p2_ref, e_ref, h_ref, gt_ref, w_ref,
                      out_ref):
    def positions(x_ref, p_ref):
        x = jnp.concatenate([x_ref[res] for res in range(x_ref.shape[0])], axis=0)
        if p_ref is None:
            return x.astype(F32)
        if x.dtype == BF16:
            return _dot(p_ref[...], x)
        hi = x.astype(BF16)
        return _dot(p_ref[...], hi) + _dot(p_ref[...], (x - hi.astype(F32)).astype(BF16))

    l0, l1, l2 = positions(l0_ref, None), positions(l1_ref, p1_ref), positions(l2_ref, p2_ref)
    m = jnp.maximum(jnp.maximum(l0, l1), l2)
    e0, e1, e2 = jnp.exp2(l0 - m), jnp.exp2(l1 - m), jnp.exp2(l2 - m)
    inv = 1.0 / (e0 + e1 + e2)
    spread = e_ref[...]
    o = (_dot((e0 * inv).astype(BF16), spread) * positions(o0_ref, None)
         + _dot((e1 * inv).astype(BF16), spread) * positions(o1_ref, p1_ref)
         + _dot((e2 * inv).astype(BF16), spread) * positions(o2_ref, p2_ref))
    out_ref[0] = h_ref[0] + (1.0 + gt_ref[...]) * _dot(o.astype(BF16), w_ref[...])


def _merge_out(os_, lses, h, mod5, w_o, l):
    b, s, d = h.shape
    hdim = w_o.shape[0]
    tm = min(512, s)
    row = lambda width: pl.BlockSpec((1, tm, width), lambda bb, i: (bb, i, 0))
    rep = LANES // N_HEADS
    lane = np.arange(LANES)[:, None]
    col = np.arange(hdim)[None, :]
    spread = ((lane % rep == 0) & (lane // rep == col // HEAD_DIM)).astype(BF16)

    def unpermute(rate):
        pos = np.arange(tm)
        return (((pos % rate) * (tm // rate) + pos // rate)[:, None] == np.arange(tm)[None, :]).astype(BF16)

    def residue_spec(t):
        rate, width = t.shape[1], t.shape[3]
        return pl.BlockSpec((None, rate, tm // rate, width), lambda bb, i: (bb, 0, i, 0))

    const = lambda shape: pl.BlockSpec(shape, lambda bb, i: (0, 0))
    return pl.pallas_call(
        _merge_out_kernel,
        grid=(b, s // tm),
        in_specs=[residue_spec(t) for t in (*os_, *lses)]
        + [const((tm, tm)), const((tm, tm)), const((LANES, hdim)), row(d), _mod_block(d, l, 5), const((hdim, d))],
        out_specs=row(d),
        out_shape=jax.ShapeDtypeStruct((b, s, d), F32),
        compiler_params=_params("parallel", "parallel"),
        name="merge_out_proj",
    )(*os_, *lses, unpermute(os_[1].shape[1]), unpermute(os_[2].shape[1]), spread, h, mod5, w_o)


def _out_kernel(o_ref, h_ref, gt_ref, w_ref, out_ref):
    out_ref[0] = h_ref[0] + (1.0 + gt_ref[...]) * _dot(o_ref[0], w_ref[...])


def _out_proj(o, h, mod5, w_o, l):
    b, s, d = h.shape
    hdim = w_o.shape[0]
    tm = min(1024, s)
    row = lambda width: pl.BlockSpec((1, tm, width), lambda bb, i: (bb, i, 0))
    return pl.pallas_call(
        _out_kernel,
        grid=(b, s // tm),
        in_specs=[row(hdim), row(d), _mod_block(d, l, 5), pl.BlockSpec((hdim, d), lambda bb, i: (0, 0))],
        out_specs=row(d),
        out_shape=jax.ShapeDtypeStruct((b, s, d), F32),
        compiler_params=_params("parallel", "parallel"),
        name="out_proj",
    )(o, h, mod5, w_o)


def _kv_kernel(h_ref, g_ref, sh_ref, sc_ref, w_ref, gain_ref, cos_ref, sin_ref, bd_ref,
               ks_ref, kw_ref, vs_ref, vw_ref, kvc_ref, *, tm):
    i = pl.program_id(1)
    u = _normmod(h_ref[0], g_ref[...], sh_ref[...], sc_ref[...]).astype(BF16)
    y = _dot(u, w_ref[...])
    cos = cos_ref[...]
    sin = sin_ref[...]
    bd = bd_ref[...]
    lane = lax.broadcasted_iota(jnp.int32, (tm, LANES), 1)
    pos = i * tm + lax.broadcasted_iota(jnp.int32, (tm, LANES), 0)
    onehot = jnp.where((pos >> SEL_SHIFT) == lane, 1.0, 0.0).astype(BF16)
    for j in range(4):
        z = _head_rms(y[:, j * LANES:(j + 1) * LANES], bd) * gain_ref[j:j + 1, :]
        zr = _rope(z, cos, sin).astype(BF16)
        if j < 2:
            ks_ref[0, :, 2 * j * LANES:(2 * j + 1) * LANES] = zr
            ks_ref[0, :, (2 * j + 1) * LANES:(2 * j + 2) * LANES] = onehot
        else:
            kw_ref[0, :, (j - 2) * LANES:(j - 1) * LANES] = zr
    for j in range(4):
        v = y[:, (4 + j) * LANES:(5 + j) * LANES]
        va = jnp.where(lane < HEAD_DIM, v, 1.0).astype(BF16)
        ref = vs_ref if j < 2 else vw_ref
        ref[0, :, (j % 2) * LANES:(j % 2 + 1) * LANES] = va
    kvc_ref[0] = y[:, 8 * LANES:10 * LANES]


def _kv_proj(h, kv_norm_g, kvmod5, w, gain, cos, sin, bd):
    b, s, d = h.shape
    n = w.shape[1]
    tm = min(512, s)
    row = lambda width, dt: (pl.BlockSpec((1, tm, width), lambda bb, i: (bb, i, 0)),
                             jax.ShapeDtypeStruct((b, s, width), dt))
    outs = [row(4 * LANES, BF16), row(2 * LANES, BF16), row(2 * LANES, BF16), row(2 * LANES, BF16),
            row(2 * LANES, F32)]
    return pl.pallas_call(
        functools.partial(_kv_kernel, tm=tm),
        grid=(b, s // tm),
        in_specs=[pl.BlockSpec((1, tm, d), lambda bb, i: (bb, i, 0)),
                  pl.BlockSpec((1, d), lambda bb, i: (0, 0)),
                  _mod_block(d, 0, 0),
                  _mod_block(d, 0, 1),
                  pl.BlockSpec((d, n), lambda bb, i: (0, 0)),
                  pl.BlockSpec((8, LANES), lambda bb, i: (0, 0)),
                  pl.BlockSpec((tm, LANES), lambda bb, i: (i, 0)),
                  pl.BlockSpec((tm, LANES), lambda bb, i: (i, 0)),
                  pl.BlockSpec((LANES, LANES), lambda bb, i: (0, 0))],
        out_specs=[o[0] for o in outs],
        out_shape=[o[1] for o in outs],
        compiler_params=_params("parallel", "parallel"),
        name="shared_kv_proj",
    )(h, kv_norm_g, kvmod5, kvmod5, w, gain, cos, sin, bd)


def _cmp_kernel(t_ref, w1_ref, w2_ref, pos_ref, gain_ref, o_ref):
    is_key = pl.program_id(0) < N_KV_HEADS
    t = t_ref[...]
    w1 = w1_ref[...]
    half = w1.shape[0] // 2
    first = _split_dot(t, w1[:half])
    second = _split_dot(t, w1[half:])
    second = pltpu.roll(second, second.shape[0] - 1, 0)
    posb = _split_dot(jnp.broadcast_to(pos_ref[...], (8, w1.shape[0])), w1)[0:1]
    hid = first + second + posb
    out = _split_dot(hid * _sigmoid(hid), w2_ref[...])
    key = out * lax.rsqrt(jnp.mean(out * out, axis=-1, keepdims=True) + EPS) * gain_ref[...]
    dup = jnp.concatenate([key, key], axis=-1)
    aug = jnp.concatenate([out, jnp.ones_like(out)], axis=-1)
    o_ref[...] = jnp.where(is_key, dup, aug).astype(BF16)


def _compress(t4, phi_w1, phi_w2, cmp_pos, gain):
    _, b, nsb, width = t4.shape
    hid = phi_w1.shape[2]
    return pl.pallas_call(
        _cmp_kernel,
        grid=(4, b),
        in_specs=[pl.BlockSpec((None, None, nsb, width), lambda t, bb: (t, bb, 0, 0)),
                  pl.BlockSpec((None, CMP_LEN * HEAD_DIM, hid), lambda t, bb: (t // N_KV_HEADS, 0, 0)),
                  pl.BlockSpec((None, hid, HEAD_DIM), lambda t, bb: (t // N_KV_HEADS, 0, 0)),
                  pl.BlockSpec((None, 1, CMP_LEN * HEAD_DIM), lambda t, bb: (t // N_KV_HEADS, 0, 0)),
                  pl.BlockSpec((1, HEAD_DIM), lambda t, bb: (0, 0))],
        out_specs=pl.BlockSpec((None, None, nsb, LANES), lambda t, bb: (t, bb, 0, 0)),
        out_shape=jax.ShapeDtypeStruct((4, b, nsb, LANES), BF16),
        compiler_params=_params("parallel", "parallel"),
        name="compress_blocks",
    )(t4, phi_w1, phi_w2, cmp_pos, gain)


def _qg_kernel(h_ref, g_ref, sh_ref, sc_ref, w_ref, gain_ref, cos_ref, sin_ref, bd_ref,
               qn_ref, qr_ref, gate_ref):
    u = _normmod(h_ref[0], g_ref[...], sh_ref[...], sc_ref[...]).astype(BF16)
    y = _dot(u, w_ref[...])
    cos = cos_ref[...]
    sin = sin_ref[...]
    bd = bd_ref[...]
    hdim = qn_ref.shape[2]
    for cb in range(hdim // LANES):
        sl = slice(cb * LANES, (cb + 1) * LANES)
        z = _head_rms(y[:, sl], bd) * gain_ref[:, sl]
        qn_ref[0, :, sl] = z.astype(BF16)
        qr_ref[0, :, sl] = _rope(z, cos, sin).astype(BF16)
    gate_ref[0] = _sigmoid(y[:, hdim:])


def _qg_proj(h, norm_g3, mod5, w, gain, cos, sin, bd, l):
    b, s, d = h.shape
    n = w.shape[1]
    hdim = N_HEADS * HEAD_DIM
    tm = min(512, s)
    row = lambda width, dt: (pl.BlockSpec((1, tm, width), lambda bb, i: (bb, i, 0)),
                             jax.ShapeDtypeStruct((b, s, width), dt))
    outs = [row(hdim, BF16), row(hdim, BF16), row(n - hdim, F32)]
    return pl.pallas_call(
        _qg_kernel,
        grid=(b, s // tm),
        in_specs=[pl.BlockSpec((1, tm, d), lambda bb, i: (bb, i, 0)),
                  pl.BlockSpec((None, 1, d), lambda bb, i: (l * 3 + 1, 0, 0)),
                  _mod_block(d, l, 3),
                  _mod_block(d, l, 4),
                  pl.BlockSpec((d, n), lambda bb, i: (0, 0)),
                  pl.BlockSpec((1, hdim), lambda bb, i: (0, 0)),
                  pl.BlockSpec((tm, LANES), lambda bb, i: (i, 0)),
                  pl.BlockSpec((tm, LANES), lambda bb, i: (i, 0)),
                  pl.BlockSpec((LANES, LANES), lambda bb, i: (0, 0))],
        out_specs=[o[0] for o in outs],
        out_shape=[o[1] for o in outs],
        compiler_params=_params("parallel", "parallel"),
        name="nsa_query_gate_proj",
    )(h, norm_g3, mod5, mod5, w, gain, cos, sin, bd)


def _nsa_body(qn_ref, qr_ref, gate_ref, kc_ref, vc_ref, ks_ref, vs_ref, kw_ref, vw_ref, wimpt_ref, gsel_ref,
              o_ref, qa_scr, m_scr, acc_scr, mw_scr, accw_scr, *, tq, shifted):
    i = pl.program_id(2)
    start = i * tq
    rows = GROUP * tq
    ncp = kc_ref.shape[0]
    low = lax.broadcasted_iota(jnp.int32, (tq, LANES), 1) < HEAD_DIM
    low_r = lax.broadcasted_iota(jnp.int32, (rows, LANES), 1) < HEAD_DIM

    def accumulate(s, v, mask, first, acc_ref=acc_scr, m_ref=m_scr):
        if not shifted:
            p = jnp.exp2(s)
            if mask is not None:
                p = jnp.where(mask, p, 0.0)
            pv = _dot(p.astype(BF16), v)
            acc_ref[...] = pv if first else acc_ref[...] + pv
            return
        if mask is not None:
            s = jnp.where(mask, s, NEG)
        if first:
            m_new = jnp.max(s, axis=-1, keepdims=True)
            acc_ref[...] = _dot(jnp.exp2(s - m_new).astype(BF16), v)
        else:
            m_old = m_ref[...]
            m_new = jnp.maximum(m_old, jnp.max(s, axis=-1, keepdims=True))
            acc_ref[...] = jnp.exp2(m_old - m_new) * acc_ref[...] + _dot(jnp.exp2(s - m_new).astype(BF16), v)
        m_ref[...] = m_new

    def stack(q):
        parts = []
        for g in range(GROUP):
            blk = q[:, (g // 2) * LANES:(g // 2 + 1) * LANES]
            parts.append(jnp.where(low if g % 2 == 0 else ~low, blk, jnp.zeros_like(blk)))
        return parts

    qn = jnp.concatenate(stack(qn_ref[0]), axis=0)
    sc = _dot_t(qn, kc_ref[...])
    c_end = lax.broadcasted_iota(jnp.int32, (tq, ncp), 1) * CMP_STRIDE + (CMP_LEN - 1)
    valid_t = c_end <= start + lax.broadcasted_iota(jnp.int32, (tq, ncp), 0)
    if shifted:
        valid = jnp.concatenate([valid_t] * GROUP, axis=0)
        sc = jnp.where(valid, sc, NEG)
        sc = sc - jnp.max(sc, axis=-1, keepdims=True)
        p = jnp.where(valid, jnp.exp2(sc), 0.0)
    else:
        p = jnp.exp2(sc + jnp.concatenate([jnp.where(valid_t, 0.0, NEG)] * GROUP, axis=0))
    oc = _dot(p.astype(BF16), vc_ref[...])
    inv = 1.0 / jnp.maximum(jnp.where(low_r, pltpu.roll(oc, HEAD_DIM, 1), oc), 1e-30)
    o_cmp = oc * inv
    psum_blocks = []
    for cb in range(ncp // LANES):
        cs = slice(cb * LANES, (cb + 1) * LANES)
        tot = p[0:tq, cs] * inv[0:tq]
        for g in range(1, GROUP):
            tot = tot + p[g * tq:(g + 1) * tq, cs] * inv[g * tq:(g + 1) * tq]
        psum_blocks.append(tot)
    psum = jnp.concatenate(psum_blocks, axis=1)

    qr_parts = stack(qr_ref[0])
    for g in range(GROUP):
        qa_scr[g * tq:(g + 1) * tq, 0:LANES] = qr_parts[g]
    qr = jnp.concatenate(qr_parts, axis=0)

    tri = (lax.broadcasted_iota(jnp.int32, (rows, tq), 1)
           <= (lax.broadcasted_iota(jnp.int32, (rows, tq), 0) & (tq - 1)))

    def key_block(kb):
        return pl.ds(pl.multiple_of(kb * tq, tq), tq)

    diag = key_block(i)
    accumulate(_dot_t(qr, kw_ref[diag, :]), vw_ref[diag, :], tri, True, accw_scr, mw_scr)
    blk = key_block(jnp.maximum(i - 1, 0))
    accumulate(_dot_t(qr, kw_ref[blk, :]), vw_ref[blk, :], jnp.broadcast_to(i >= 1, tri.shape), False,
               accw_scr, mw_scr)
    blk = key_block(jnp.maximum(i - 2, 0))
    accumulate(_dot_t(qr, kw_ref[blk, :]), vw_ref[blk, :], (~tri) & (i >= 2), False, accw_scr, mw_scr)

    p_hi = psum.astype(BF16)
    p_lo = (psum - p_hi.astype(F32)).astype(BF16)
    imp_t = _dot_t(wimpt_ref[...], p_hi) + _dot_t(wimpt_ref[...], p_lo)
    blk_id = lax.broadcasted_iota(jnp.int32, (LANES, tq), 0)
    jt = (start + lax.broadcasted_iota(jnp.int32, (LANES, tq), 1)) >> SEL_SHIFT
    forced = (blk_id == 0) | (blk_id == jt) | (blk_id == jt - 1)
    work = jnp.where(blk_id > jt, -1.0, jnp.where(forced, -2.0, imp_t))
    sel = forced
    for _ in range(N_SELECT - 3):
        mx = jnp.max(work, axis=0, keepdims=True)
        idx = jnp.min(jnp.where(work == mx, blk_id, LANES), axis=0, keepdims=True)
        pick = blk_id == idx
        sel = sel | pick
        work = jnp.where(pick, -2.0, work)
    sel_bias = jnp.transpose(jnp.where(sel, 0.0, NEG)).astype(BF16)

    for g in range(GROUP):
        qa_scr[g * tq:(g + 1) * tq, LANES:2 * LANES] = sel_bias

    accumulate(_dot_t(qa_scr[...], ks_ref[diag, :]), vs_ref[diag, :], tri, True)

    def slc_pair(kp, carry):
        blk = pl.ds(pl.multiple_of(kp * 2 * tq, 2 * tq), 2 * tq)
        accumulate(_dot_t(qa_scr[...], ks_ref[blk, :]), vs_ref[blk, :], None, False)
        return carry

    lax.fori_loop(0, i // 2, slc_pair, 0)

    @pl.when(i % 2 == 1)
    def _():
        blk = key_block(i - 1)
        accumulate(_dot_t(qa_scr[...], ks_ref[blk, :]), vs_ref[blk, :], None, False)

    gates = gate_ref[0]
    g_hi = gates.astype(BF16)
    g_lo = (gates - g_hi.astype(F32)).astype(BF16)
    gate_wide = _dot(g_hi, gsel_ref[...]) + _dot(g_lo, gsel_ref[...])
    for pr in range(GROUP // 2):
        ev = slice(2 * pr * tq, (2 * pr + 1) * tq)
        od = slice((2 * pr + 1) * tq, (2 * pr + 2) * tq)

        def pair(acc):
            a_e, a_o = acc[ev], acc[od]
            r_e = pltpu.roll(a_e, HEAD_DIM, 1)
            r_o = pltpu.roll(a_o, HEAD_DIM, 1)
            return jnp.where(low, a_e, r_o) / jnp.where(low, r_e, a_o)

        def gate(br):
            c = (br * (GROUP // 2) + pr) * LANES
            return gate_wide[:, c:c + LANES]

        o_c = jnp.where(low, o_cmp[ev], pltpu.roll(o_cmp[od], HEAD_DIM, 1))
        out = gate(0) * o_c + gate(1) * pair(acc_scr) + gate(2) * pair(accw_scr)
        o_ref[0, :, pr * LANES:(pr + 1) * LANES] = out.astype(BF16)


def _nsa_kernel(small_ref, *refs, tq):
    @pl.when(small_ref[0] != 0)
    def _():
        _nsa_body(*refs, tq=tq, shifted=False)

    @pl.when(small_ref[0] == 0)
    def _():
        _nsa_body(*refs, tq=tq, shifted=True)


def _nsa_attention(small, qn, qr, gates, cmp4, ks, vs, kw, vw, wimpt):
    b, s, hdim = qn.shape
    tq = min(256, s)
    assert s // SEL_BLOCK <= LANES and s % tq == 0 and SLIDE_WINDOW == 2 * tq
    ncp = cmp4.shape[2]
    gw = GROUP * HEAD_DIM
    rows = GROUP * tq
    qspec = pl.BlockSpec((1, tq, gw), lambda bb, kv, i: (bb, i, kv))
    full = lambda width: pl.BlockSpec((None, s, width), lambda bb, kv, i: (bb, 0, kv))
    col = np.arange(3 * (GROUP // 2) * LANES)[None, :]
    src = (col // LANES // (GROUP // 2)) * GROUP + 2 * ((col // LANES) % (GROUP // 2)) + (col % LANES) // HEAD_DIM
    gsel = (np.arange(LANES)[:, None] == src).astype(BF16)
    return pl.pallas_call(
        functools.partial(_nsa_kernel, tq=tq),
        grid=(b, N_KV_HEADS, s // tq),
        in_specs=[pl.BlockSpec(memory_space=pltpu.SMEM), qspec, qspec,
                  pl.BlockSpec((1, tq, LANES), lambda bb, kv, i: (bb, i, kv)),
                  pl.BlockSpec((None, None, ncp, LANES), lambda bb, kv, i: (kv, bb, 0, 0)),
                  pl.BlockSpec((None, None, ncp, LANES), lambda bb, kv, i: (N_KV_HEADS + kv, bb, 0, 0)),
                  full(2 * LANES), full(LANES), full(LANES), full(LANES),
                  pl.BlockSpec((LANES, ncp), lambda bb, kv, i: (0, 0)),
                  pl.BlockSpec(gsel.shape, lambda bb, kv, i: (0, 0))],
        out_specs=qspec,
        out_shape=jax.ShapeDtypeStruct((b, s, hdim), BF16),
        scratch_shapes=[pltpu.VMEM((rows, 2 * LANES), BF16),
                        pltpu.VMEM((rows, 1), F32), pltpu.VMEM((rows, LANES), F32),
                        pltpu.VMEM((rows, 1), F32), pltpu.VMEM((rows, LANES), F32)],
        compiler_params=_params("parallel", "parallel", "arbitrary"),
        name="nsa_attention",
    )(small, qn, qr, gates, cmp4, cmp4, ks, vs, kw, vw, wimpt, gsel)


def _rope_tables(s, tm=None, rate=1):
    half = HEAD_DIM // 2
    row = lax.broadcasted_iota(jnp.int32, (s, LANES), 0)
    lane = lax.broadcasted_iota(jnp.int32, (s, LANES), 1)
    if rate > 1:
        seg = tm // rate
        local = row % tm
        row = row - local + (local % seg) * rate + local // seg
    inv = jnp.tile(ROPE_THETA ** (-jnp.arange(half, dtype=F32) / half), LANES // half)
    ang = row.astype(F32) * inv[None, :]
    cos, sin = jnp.cos(ang), jnp.sin(ang)
    return cos, jnp.where(lane % HEAD_DIM < half, -sin, sin), sin


def _split_half(t, axis):
    hpg = MXU_WIDTH // HEAD_DIM
    half = HEAD_DIM // 2
    shape = t.shape
    t = t.reshape(shape[:axis] + (shape[axis] // MXU_WIDTH, hpg, 2, half) + shape[axis + 1:])
    return jnp.swapaxes(t, axis + 1, axis + 2).reshape(shape)


def _importance_weights(ncp):
    r = SEL_BLOCK // CMP_STRIDE
    nper = CMP_LEN // CMP_STRIDE
    o = np.arange(ncp)[:, None] - r * np.arange(LANES)[None, :]
    w = np.zeros((ncp, LANES), np.float32)
    for off in range(-(nper - 1), r):
        cnt = sum(1 for m in range(r) for n in range(nper) if m - n == off)
        w += np.where(o == off, float(cnt), 0.0)
    return w.astype(BF16)


def kernel(x, c, norm_g, w_ada, b_ada, ffn_w_in, ffn_w_out, a_w_qkv, a_q_gain, a_k_gain, a_w_o, kv_norm_g,
           w_ada_kv, b_ada_kv, w_kv, kv_k_gain, cmp_pos, phi_w1, phi_w2, b_w_qg, b_q_gain, b_w_o):
    b, s, d = x.shape
    depth = norm_g.shape[0]
    n_a = depth // 2
    hdim = N_HEADS * HEAD_DIM
    assert depth == 2 and b <= 8

    c8 = jnp.zeros((8, d), F32).at[:b].set(c)
    mod = _ada(c8, w_ada, b_ada.reshape(depth, 1, 9 * d))
    mod5 = mod[:, :b].reshape(depth, b, 9, 1, d)
    kvmod = _ada(c8, w_ada_kv[None], b_ada_kv.reshape(1, 1, 2 * d))
    kvmod5 = kvmod[:, :b].reshape(1, b, 2, 1, d)

    norm_g3 = norm_g.reshape(depth * 3, 1, d)
    w_in = ffn_w_in.astype(BF16)
    w_out = ffn_w_out.astype(BF16)
    cos, sin, _ = _rope_tables(s)
    blk = np.arange(LANES) // HEAD_DIM
    bd = (blk[:, None] == blk[None, :]).astype(BF16)
    blk = (np.arange(MXU_WIDTH) % LANES) // (HEAD_DIM // 2)
    bd_split = (blk[:, None] == blk[None, :]).astype(BF16)
    q_scale = HEAD_DIM ** -0.5 * LOG2E

    h = x
    for l in range(depth):
        h = _ffn(h, norm_g3, mod5, w_in, w_out, l, 0, 0)
        if l < n_a:
            gain = jnp.stack([_split_half(jnp.tile(a_q_gain[l] * q_scale, (1, N_HEADS)), 1),
                              _split_half(jnp.tile(a_k_gain[l], (1, N_HEADS)), 1),
                              jnp.ones((N_DIL, hdim), F32)], axis=1).reshape(3 * N_DIL, 1, hdim)
            w_qkv = a_w_qkv[l].reshape(d, N_DIL, 3, hdim)
            w_qkv = jnp.concatenate([_split_half(w_qkv[:, :, :2], 3), w_qkv[:, :, 2:]], axis=2)
            w_qkv = w_qkv.reshape(d, N_DIL * 3 * hdim).astype(BF16)
            bound = HEAD_DIM * q_scale * jnp.max(jnp.abs(a_q_gain[l])) * jnp.max(jnp.abs(a_k_gain[l]))
            small = (bound <= SCORE_LIMIT).astype(jnp.int32).reshape(1)
            parts = [_dilated_attention(small, _qkv_proj(h, norm_g3, mod5, w_qkv, gain, bd_split, l, g), g)
                     for g in range(N_DIL)]
            h = _merge_out([p[0] for p in parts], [p[1] for p in parts], h, mod5, a_w_o[l].astype(BF16), l)
        else:
            j = l - n_a
            w_q = b_w_qg[j][:, :hdim]
            w_g = b_w_qg[j][:, hdim:].reshape(d, 3, N_KV_HEADS, GROUP).transpose(0, 2, 1, 3)
            w_g = jnp.pad(w_g.reshape(d, N_KV_HEADS, 3 * GROUP), ((0, 0), (0, 0), (0, LANES - 3 * GROUP)))
            w_qg = jnp.concatenate([w_q, w_g.reshape(d, N_KV_HEADS * LANES)], axis=1).astype(BF16)
            gain = jnp.tile(b_q_gain[j] * q_scale, N_HEADS)[None, :]
            qn, qr, gates = _qg_proj(h, norm_g3, mod5, w_qg, gain, cos, sin, bd, l)
            bound = HEAD_DIM * q_scale * jnp.max(jnp.abs(b_q_gain[j])) * jnp.max(jnp.abs(kv_k_gain))
            small = (bound <= SCORE_LIMIT).astype(jnp.int32).reshape(1)
            o = _nsa_attention(small, qn, qr, gates, *shared)
            h = _out_proj(o, h, mod5, b_w_o[j].astype(BF16), l)
        h = _ffn(h, norm_g3, mod5, w_in, w_out, l, 2, 1)
        if l == n_a - 1:
            col = lambda br, kv, kvh: br * 4 * HEAD_DIM + kv * 2 * HEAD_DIM + kvh * HEAD_DIM + jnp.arange(HEAD_DIM)
            dup = lambda br, kv, kvh: jnp.concatenate([col(br, kv, kvh)] * 2)
            cols = jnp.concatenate(
                [dup(br, 0, kvh) for br in (1, 2) for kvh in range(N_KV_HEADS)]
                + [dup(br, 1, kvh) for br in (1, 2) for kvh in range(N_KV_HEADS)]
                + [col(0, kv, kvh) for kv in range(2) for kvh in range(N_KV_HEADS)])
            w_kv_ext = w_kv[:, cols].astype(BF16)
            kgain = jnp.concatenate([jnp.tile(kv_k_gain[1], (2, 2)), jnp.tile(kv_k_gain[2], (2, 2)),
                                     jnp.ones((4, LANES), F32)], axis=0)
            ks, kw, vs, vw, kvc = _kv_proj(h, kv_norm_g[None, :], kvmod5, w_kv_ext, kgain, cos, sin, bd)
            nsb = s // CMP_STRIDE
            t4 = kvc.reshape(b, nsb, CMP_STRIDE, 4, HEAD_DIM).transpose(3, 0, 1, 2, 4)
            t4 = t4.reshape(4, b, nsb, CMP_STRIDE * HEAD_DIM)
            cmp4 = _compress(t4, phi_w1, phi_w2, cmp_pos.reshape(2, 1, CMP_LEN * HEAD_DIM), kv_k_gain[0][None, :])
            shared = (cmp4, ks, vs, kw, vw, _importance_weights(nsb).T)
    return h
```

```python
import functools

import jax
import jax.numpy as jnp
import numpy as np
from jax import lax
from jax.experimental import pallas as pl
from jax.experimental.pallas import tpu as pltpu

HEAD_DIM = 64
N_HEADS = 16
DIL_WINDOWS = (128, 512, 2048)
DIL_RATES = (1, 4, 16)
N_DIL = 3
N_KV_HEADS = 2
GROUP = N_HEADS // N_KV_HEADS
CMP_STRIDE = 16
CMP_LEN = 32
SEL_BLOCK = 64
SEL_SHIFT = SEL_BLOCK.bit_length() - 1
N_SELECT = 16
SLIDE_WINDOW = 512
ROPE_THETA = 10000.0
EPS = 1e-6

LANES = 128
MXU_WIDTH = 256
Q_ROWS = 128
NEG = -1e30
LOG2E = 1.4426950408889634
SCORE_LIMIT = 60.0
VMEM_LIMIT = 56 * 1024 * 1024

BF16 = jnp.bfloat16
F32 = jnp.float32


def _params(*sem):
    return pltpu.CompilerParams(dimension_semantics=sem, vmem_limit_bytes=VMEM_LIMIT)


def _dot(a, b):
    return jnp.dot(a, b, preferred_element_type=F32)


def _dot_t(a, b):
    return lax.dot_general(a, b, (((1,), (1,)), ((), ())), preferred_element_type=F32)


def _split_dot(a, b):
    a_hi = a.astype(BF16)
    a_lo = (a - a_hi.astype(F32)).astype(BF16)
    b_hi = b.astype(BF16)
    b_lo = (b - b_hi.astype(F32)).astype(BF16)
    return _dot(a_hi, b_hi) + (_dot(a_hi, b_lo) + _dot(a_lo, b_hi))


def _sigmoid(x):
    return 1.0 / (1.0 + jnp.exp(-x))


def _normmod(x, g, shift, scale):
    return x * lax.rsqrt(jnp.mean(x * x, axis=-1, keepdims=True) + EPS) * (g * (1.0 + scale)) + shift


def _head_rms(y, bd):
    ss = _dot((y * y).astype(BF16), bd)
    return y * lax.rsqrt(ss * (1.0 / HEAD_DIM) + EPS)


def _rope(z, cos, sin_signed):
    lane = lax.broadcasted_iota(jnp.int32, z.shape, 1)
    first_half = (lane & (HEAD_DIM - 1)) < HEAD_DIM // 2
    swapped = jnp.where(first_half, pltpu.roll(z, z.shape[1] - HEAD_DIM // 2, 1),
                        pltpu.roll(z, HEAD_DIM // 2, 1))
    return z * cos + swapped * sin_signed


def _ada_kernel(c_ref, w_ref, b_ref, o_ref):
    c = c_ref[...]
    a = (c * _sigmoid(c)).astype(BF16)
    o_ref[...] = _dot(a, w_ref[...].astype(BF16)) + b_ref[...]


def _ada(c8, w, b):
    nl, d, n = w.shape
    tn = 1024
    return pl.pallas_call(
        _ada_kernel,
        grid=(nl, n // tn),
        in_specs=[pl.BlockSpec((8, d), lambda l, j: (0, 0)),
                  pl.BlockSpec((None, d, tn), lambda l, j: (l, 0, j)),
                  pl.BlockSpec((None, 1, tn), lambda l, j: (l, 0, j))],
        out_specs=pl.BlockSpec((None, 8, tn), lambda l, j: (l, 0, j)),
        out_shape=jax.ShapeDtypeStruct((nl, 8, n), F32),
        compiler_params=_params("parallel", "parallel"),
        name="ada_table",
    )(c8, w, b)


def _mod_block(d, l, k):
    return pl.BlockSpec((None, None, None, 1, d), lambda b, *_: (l, b, k, 0, 0))


def _ffn_kernel(h_ref, g_ref, sh_ref, sc_ref, gt_ref, wg_ref, wu_ref, wo_ref, o_ref, u_scr, acc_scr):
    j = pl.program_id(2)

    @pl.when(j == 0)
    def _():
        u_scr[...] = _normmod(h_ref[0], g_ref[...], sh_ref[...], sc_ref[...]).astype(BF16)
        acc_scr[...] = jnp.zeros_like(acc_scr)

    u = u_scr[...]
    gate = _dot(u, wg_ref[...])
    up = _dot(u, wu_ref[...])
    act = (gate * _sigmoid(gate) * up).astype(BF16)
    acc_scr[...] += _dot(act, wo_ref[...])

    @pl.when(j == pl.num_programs(2) - 1)
    def _():
        o_ref[0] = h_ref[0] + (0.5 * (1.0 + gt_ref[...])) * acc_scr[...]


def _ffn(h, norm_g3, mod5, w_in, w_out, l, sub, which):
    b, s, d = h.shape
    f = w_out.shape[2]
    tm = min(1024, s)
    tf = 256
    nf = f // tf
    return pl.pallas_call(
        _ffn_kernel,
        grid=(b, s // tm, nf),
        in_specs=[pl.BlockSpec((1, tm, d), lambda bb, i, j: (bb, i, 0)),
                  pl.BlockSpec((None, 1, d), lambda bb, i, j: (l * 3 + sub, 0, 0)),
                  _mod_block(d, l, sub * 3 + 0),
                  _mod_block(d, l, sub * 3 + 1),
                  _mod_block(d, l, sub * 3 + 2),
                  pl.BlockSpec((None, None, d, tf), lambda bb, i, j: (l, which, 0, j)),
                  pl.BlockSpec((None, None, d, tf), lambda bb, i, j: (l, which, 0, j + nf)),
                  pl.BlockSpec((None, None, tf, d), lambda bb, i, j: (l, which, j, 0))],
        out_specs=pl.BlockSpec((1, tm, d), lambda bb, i, j: (bb, i, 0)),
        out_shape=jax.ShapeDtypeStruct((b, s, d), F32),
        scratch_shapes=[pltpu.VMEM((tm, d), BF16), pltpu.VMEM((tm, d), F32)],
        compiler_params=_params("parallel", "parallel", "arbitrary"),
        name="swiglu_half_step",
    )(h, norm_g3, mod5, mod5, mod5, w_in, w_in, w_out)


def _qkv_kernel(h_ref, g_ref, sh_ref, sc_ref, w_ref, gain_ref, cos_ref, sin_ref, bd_ref, o_ref, u_scr, uf_scr,
                *, rate):
    n = pl.program_id(2)
    tm = u_scr.shape[0]
    seg = tm // rate

    @pl.when(n == 0)
    def _():
        u = _normmod(h_ref[0], g_ref[...], sh_ref[...], sc_ref[...])
        if rate == 1:
            u_scr[...] = u.astype(BF16)
        else:
            for cb in range(u.shape[1] // LANES):
                sl = slice(cb * LANES, (cb + 1) * LANES)
                uf_scr[cb] = u[:, sl]
                for res in range(rate):
                    u_scr[res * seg:(res + 1) * seg, sl] = uf_scr[cb, pl.ds(res, seg, stride=rate), :].astype(BF16)

    n_grp = o_ref.shape[1]

    def store(grp, z):
        for res in range(rate):
            o_ref[res, grp] = z[res * seg:(res + 1) * seg]

    @pl.when(n == 2)
    def _():
        for grp in range(n_grp):
            store(grp, _dot(u_scr[...], w_ref[:, grp * MXU_WIDTH:(grp + 1) * MXU_WIDTH]).astype(BF16))

    @pl.when(n != 2)
    def _():
        cos = cos_ref[...]
        sin = sin_ref[...]
        bd = bd_ref[...]
        def project(grp):
            return _dot(u_scr[...], w_ref[:, grp * MXU_WIDTH:(grp + 1) * MXU_WIDTH])

        y = project(0)
        for grp in range(n_grp):
            y_next = project(grp + 1) if grp + 1 < n_grp else None
            z = _head_rms(y, bd) * gain_ref[:, grp * MXU_WIDTH:(grp + 1) * MXU_WIDTH]
            x1, x2 = z[:, :LANES], z[:, LANES:]
            store(grp, jnp.concatenate([x1 * cos - x2 * sin, x1 * sin + x2 * cos], axis=1).astype(BF16))
            y = y_next


def _qkv_proj(h, norm_g3, mod5, w, gain, bd, l, g):
    b, s, d = h.shape
    rate = DIL_RATES[g]
    tm = min(1024, s)
    seg = tm // rate
    tn = N_HEADS * HEAD_DIM
    n_grp = tn // MXU_WIDTH
    cos, _, sin = _rope_tables(s, tm, rate)

    return pl.pallas_call(
        functools.partial(_qkv_kernel, rate=rate),
        grid=(b, s // tm, 3),
        in_specs=[pl.BlockSpec((1, tm, d), lambda bb, i, j: (bb, i, 0)),
                  pl.BlockSpec((None, 1, d), lambda bb, i, j: (l * 3 + 1, 0, 0)),
                  _mod_block(d, l, 3),
                  _mod_block(d, l, 4),
                  pl.BlockSpec((d, tn), lambda bb, i, j: (0, 3 * g + j)),
                  pl.BlockSpec((None, 1, tn), lambda bb, i, j: (3 * g + j, 0, 0)),
                  pl.BlockSpec((tm, LANES), lambda bb, i, j: (i, 0)),
                  pl.BlockSpec((tm, LANES), lambda bb, i, j: (i, 0)),
                  pl.BlockSpec(bd.shape, lambda bb, i, j: (0, 0))],
        out_specs=pl.BlockSpec((None, rate, n_grp, seg, MXU_WIDTH), lambda bb, i, j: (bb, 0, j, i, 0)),
        out_shape=jax.ShapeDtypeStruct((b, rate, 3 * n_grp, s // rate, MXU_WIDTH), BF16),
        scratch_shapes=[pltpu.VMEM((tm, d), BF16), pltpu.VMEM((d // LANES, tm, LANES), F32)],
        compiler_params=_params("parallel", "parallel", "arbitrary"),
        name=f"qkv_proj_rate{rate}",
    )(h, norm_g3, mod5, mod5, w, gain, cos, sin, bd)


def _dil_body(bias_ref, q_ref, kp_ref, kc_ref, vp_ref, vc_ref, o_ref, lse_ref, *, tq, shifted):
    i = pl.program_id(2)
    grp = pl.program_id(3)
    hpg = MXU_WIDTH // HEAD_DIM
    qlane = lax.broadcasted_iota(jnp.int32, (Q_ROWS, MXU_WIDTH), 1)
    q_head = (qlane & (LANES - 1)) // (HEAD_DIM // 2)
    v_head = qlane // HEAD_DIM
    seg = lax.broadcasted_iota(jnp.int32, (Q_ROWS, LANES), 1) // (LANES // N_HEADS)

    def scores(j):
        q = q_ref[j * Q_ROWS:(j + 1) * Q_ROWS, :]
        zero = jnp.zeros_like(q)
        q2 = jnp.concatenate([jnp.where(q_head == t, q, zero) for t in range(hpg)], axis=0)
        if j == 0:
            k = jnp.concatenate([kp_ref[...], kc_ref[0:Q_ROWS, :]], axis=0)
        else:
            k = kc_ref[(j - 1) * Q_ROWS:(j + 1) * Q_ROWS, :]
        return _dot_t(q2, k)

    for j in range(tq // Q_ROWS):
        rs = slice(j * Q_ROWS, (j + 1) * Q_ROWS)
        s = scores(j) + bias_ref[...]
        if j == 0:
            v = jnp.concatenate([vp_ref[...], vc_ref[0:Q_ROWS, :]], axis=0)
            prev_key = lax.broadcasted_iota(jnp.int32, s.shape, 1) < Q_ROWS
            s = jnp.where(prev_key & (i == 0), NEG, s)
        else:
            v = vc_ref[(j - 1) * Q_ROWS:(j + 1) * Q_ROWS, :]
        if shifted:
            m = jnp.max(s, axis=-1, keepdims=True)
            p = jnp.exp2(s - m)
        else:
            p = jnp.exp2(s)
        den = jnp.sum(p, axis=-1, keepdims=True)
        o2 = _dot(p.astype(BF16), v) / den
        lse2 = jnp.log(den) * LOG2E
        if shifted:
            lse2 = lse2 + m
        out = o2[0:Q_ROWS]
        new = jnp.where(grp == 0, 0.0, lse_ref[0, rs, :])
        for t in range(hpg):
            ts = slice(t * Q_ROWS, (t + 1) * Q_ROWS)
            if t:
                out = jnp.where(v_head == t, o2[ts], out)
            new = jnp.where(seg == hpg * grp + t, lse2[ts], new)
        o_ref[0, rs, :] = out.astype(BF16)
        lse_ref[0, rs, :] = new


def _dil_kernel(small_ref, *refs, tq):
    @pl.when(small_ref[0] != 0)
    def _():
        _dil_body(*refs, tq=tq, shifted=False)

    @pl.when(small_ref[0] == 0)
    def _():
        _dil_body(*refs, tq=tq, shifted=True)


def _dilated_attention(small, qkv_r, g):
    b, rate, n3, l, width = qkv_r.shape
    s = l * rate
    n_back = DIL_WINDOWS[g] // rate
    assert n_back <= Q_ROWS and rate == DIL_RATES[g] and width == MXU_WIDTH
    tq = min(512, l)
    hdim = N_HEADS * HEAD_DIM
    n_grp = n3 // 3
    sub = tq // Q_ROWS

    def cur(off):
        return pl.BlockSpec((None, None, None, tq, width), lambda bb, r, i, gp: (bb, r, off + gp, i, 0))

    def prev(off):
        return pl.BlockSpec((None, None, None, Q_ROWS, width),
                            lambda bb, r, i, gp: (bb, r, off + gp, jnp.maximum(i * sub - 1, 0), 0))

    rows = (width // HEAD_DIM) * Q_ROWS
    dist = Q_ROWS + (np.arange(rows)[:, None] % Q_ROWS) - np.arange(2 * Q_ROWS)[None, :]
    bias = np.where((dist >= 0) & (dist <= n_back), 0.0, NEG).astype(np.float32)

    o, lse = pl.pallas_call(
        functools.partial(_dil_kernel, tq=tq),
        grid=(b, rate, l // tq, n_grp),
        in_specs=[pl.BlockSpec(memory_space=pltpu.SMEM),
                  pl.BlockSpec(bias.shape, lambda bb, r, i, gp: (0, 0)),
                  cur(0), prev(n_grp), cur(n_grp), prev(2 * n_grp), cur(2 * n_grp)],
        out_specs=[pl.BlockSpec((1, None, tq, width), lambda bb, r, i, gp: (bb, r, i, gp)),
                   pl.BlockSpec((1, None, tq, LANES), lambda bb, r, i, gp: (bb, r, i, 0))],
        out_shape=[jax.ShapeDtypeStruct((b, rate, l, hdim), BF16),
                   jax.ShapeDtypeStruct((b, rate, l, LANES), F32)],
        compiler_params=_params("parallel", "parallel", "parallel", "arbitrary"),
        name=f"dilated_attention_rate{rate}",
    )(small, bias, qkv_r, qkv_r, qkv_r, qkv_r, qkv_r)
    return o, lse


def _merge_out_kernel(o0_ref, o1_ref, o2_ref, l0_ref, l1_ref, l2_ref, p1_ref, p2_ref, e_ref, h_ref, gt_ref, w_ref,
                      out_ref):
    def positions(x_ref, p_ref):
        x = jnp.concatenate([x_ref[res] for res in range(x_ref.shape[0])], axis=0)
        if p_ref is None:
            return x.astype(F32)
        if x.dtype == BF16:
            return _dot(p_ref[...], x)
        hi = x.astype(BF16)
        return _dot(p_ref[...], hi) + _dot(p_ref[...], (x - hi.astype(F32)).astype(BF16))

    l0, l1, l2 = positions(l0_ref, None), positions(l1_ref, p1_ref), positions(l2_ref, p2_ref)
    m = jnp.maximum(jnp.maximum(l0, l1), l2)
    e0, e1, e2 = jnp.exp2(l0 - m), jnp.exp2(l1 - m), jnp.exp2(l2 - m)
    inv = 1.0 / (e0 + e1 + e2)
    spread = e_ref[...]
    o = (_dot((e0 * inv).astype(BF16), spread) * positions(o0_ref, None)
         + _dot((e1 * inv).astype(BF16), spread) * positions(o1_ref, p1_ref)
         + _dot((e2 * inv).astype(BF16), spread) * positions(o2_ref, p2_ref))
    out_ref[0] = h_ref[0] + (1.0 + gt_ref[...]) * _dot(o.astype(BF16), w_ref[...])


def _merge_out(os_, lses, h, mod5, w_o, l):
    b, s, d = h.shape
    hdim = w_o.shape[0]
    tm = min(512, s)
    row = lambda width: pl.BlockSpec((1, tm, width), lambda bb, i: (bb, i, 0))
    rep = LANES // N_HEADS
    lane = np.arange(LANES)[:, None]
    col = np.arange(hdim)[None, :]
    spread = ((lane % rep == 0) & (lane // rep == col // HEAD_DIM)).astype(BF16)

    def unpermute(rate):
        pos = np.arange(tm)
        return (((pos % rate) * (tm // rate) + pos // rate)[:, None] == np.arange(tm)[None, :]).astype(BF16)

    def residue_spec(t):
        rate, width = t.shape[1], t.shape[3]
        return pl.BlockSpec((None, rate, tm // rate, width), lambda bb, i: (bb, 0, i, 0))

    const = lambda shape: pl.BlockSpec(shape, lambda bb, i: (0, 0))
    return pl.pallas_call(
        _merge_out_kernel,
        grid=(b, s // tm),
        in_specs=[residue_spec(t) for t in (*os_, *lses)]
        + [const((tm, tm)), const((tm, tm)), const((LANES, hdim)), row(d), _mod_block(d, l, 5), const((hdim, d))],
        out_specs=row(d),
        out_shape=jax.ShapeDtypeStruct((b, s, d), F32),
        compiler_params=_params("parallel", "parallel"),
        name="merge_out_proj",
    )(*os_, *lses, unpermute(os_[1].shape[1]), unpermute(os_[2].shape[1]), spread, h, mod5, w_o)


def _out_kernel(o_ref, h_ref, gt_ref, w_ref, out_ref):
    out_ref[0] = h_ref[0] + (1.0 + gt_ref[...]) * _dot(o_ref[0], w_ref[...])


def _out_proj(o, h, mod5, w_o, l):
    b, s, d = h.shape
    hdim = w_o.shape[0]
    tm = min(1024, s)
    row = lambda width: pl.BlockSpec((1, tm, width), lambda bb, i: (bb, i, 0))
    return pl.pallas_call(
        _out_kernel,
        grid=(b, s // tm),
        in_specs=[row(hdim), row(d), _mod_block(d, l, 5), pl.BlockSpec((hdim, d), lambda bb, i: (0, 0))],
        out_specs=row(d),
        out_shape=jax.ShapeDtypeStruct((b, s, d), F32),
        compiler_params=_params("parallel", "parallel"),
        name="out_proj",
    )(o, h, mod5, w_o)


def _kv_kernel(h_ref, g_ref, sh_ref, sc_ref, w_ref, gain_ref, cos_ref, sin_ref, bd_ref,
               ks_ref, kw_ref, vs_ref, vw_ref, kvc_ref, *, tm):
    i = pl.program_id(1)
    u = _normmod(h_ref[0], g_ref[...], sh_ref[...], sc_ref[...]).astype(BF16)
    y = _dot(u, w_ref[...])
    cos = cos_ref[...]
    sin = sin_ref[...]
    bd = bd_ref[...]
    lane = lax.broadcasted_iota(jnp.int32, (tm, LANES), 1)
    pos = i * tm + lax.broadcasted_iota(jnp.int32, (tm, LANES), 0)
    onehot = jnp.where((pos >> SEL_SHIFT) == lane, 1.0, 0.0).astype(BF16)
    for j in range(4):
        z = _head_rms(y[:, j * LANES:(j + 1) * LANES], bd) * gain_ref[j:j + 1, :]
        zr = _rope(z, cos, sin).astype(BF16)
        if j < 2:
            ks_ref[0, :, 2 * j * LANES:(2 * j + 1) * LANES] = zr
            ks_ref[0, :, (2 * j + 1) * LANES:(2 * j + 2) * LANES] = onehot
        else:
            kw_ref[0, :, (j - 2) * LANES:(j - 1) * LANES] = zr
    for j in range(4):
        v = y[:, (4 + j) * LANES:(5 + j) * LANES]
        va = jnp.where(lane < HEAD_DIM, v, 1.0).astype(BF16)
        ref = vs_ref if j < 2 else vw_ref
        ref[0, :, (j % 2) * LANES:(j % 2 + 1) * LANES] = va
    kvc_ref[0] = y[:, 8 * LANES:10 * LANES]


def _kv_proj(h, kv_norm_g, kvmod5, w, gain, cos, sin, bd):
    b, s, d = h.shape
    n = w.shape[1]
    tm = min(512, s)
    row = lambda width, dt: (pl.BlockSpec((1, tm, width), lambda bb, i: (bb, i, 0)),
                             jax.ShapeDtypeStruct((b, s, width), dt))
    outs = [row(4 * LANES, BF16), row(2 * LANES, BF16), row(2 * LANES, BF16), row(2 * LANES, BF16),
            row(2 * LANES, F32)]
    return pl.pallas_call(
        functools.partial(_kv_kernel, tm=tm),
        grid=(b, s // tm),
        in_specs=[pl.BlockSpec((1, tm, d), lambda bb, i: (bb, i, 0)),
                  pl.BlockSpec((1, d), lambda bb, i: (0, 0)),
                  _mod_block(d, 0, 0),
                  _mod_block(d, 0, 1),
                  pl.BlockSpec((d, n), lambda bb, i: (0, 0)),
                  pl.BlockSpec((8, LANES), lambda bb, i: (0, 0)),
                  pl.BlockSpec((tm, LANES), lambda bb, i: (i, 0)),
                  pl.BlockSpec((tm, LANES), lambda bb, i: (i, 0)),
                  pl.BlockSpec((LANES, LANES), lambda bb, i: (0, 0))],
        out_specs=[o[0] for o in outs],
        out_shape=[o[1] for o in outs],
        compiler_params=_params("parallel", "parallel"),
        name="shared_kv_proj",
    )(h, kv_norm_g, kvmod5, kvmod5, w, gain, cos, sin, bd)


def _cmp_kernel(t_ref, w1_ref, w2_ref, pos_ref, gain_ref, o_ref):
    is_key = pl.program_id(0) < N_KV_HEADS
    t = t_ref[...]
    w1 = w1_ref[...]
    half = w1.shape[0] // 2
    first = _split_dot(t, w1[:half])
    second = _split_dot(t, w1[half:])
    second = pltpu.roll(second, second.shape[0] - 1, 0)
    posb = _split_dot(jnp.broadcast_to(pos_ref[...], (8, w1.shape[0])), w1)[0:1]
    hid = first + second + posb
    out = _split_dot(hid * _sigmoid(hid), w2_ref[...])
    key = out * lax.rsqrt(jnp.mean(out * out, axis=-1, keepdims=True) + EPS) * gain_ref[...]
    dup = jnp.concatenate([key, key], axis=-1)
    aug = jnp.concatenate([out, jnp.ones_like(out)], axis=-1)
    o_ref[...] = jnp.where(is_key, dup, aug).astype(BF16)


def _compress(t4, phi_w1, phi_w2, cmp_pos, gain):
    _, b, nsb, width = t4.shape
    hid = phi_w1.shape[2]
    return pl.pallas_call(
        _cmp_kernel,
        grid=(4, b),
        in_specs=[pl.BlockSpec((None, None, nsb, width), lambda t, bb: (t, bb, 0, 0)),
                  pl.BlockSpec((None, CMP_LEN * HEAD_DIM, hid), lambda t, bb: (t // N_KV_HEADS, 0, 0)),
                  pl.BlockSpec((None, hid, HEAD_DIM), lambda t, bb: (t // N_KV_HEADS, 0, 0)),
                  pl.BlockSpec((None, 1, CMP_LEN * HEAD_DIM), lambda t, bb: (t // N_KV_HEADS, 0, 0)),
                  pl.BlockSpec((1, HEAD_DIM), lambda t, bb: (0, 0))],
        out_specs=pl.BlockSpec((None, None, nsb, LANES), lambda t, bb: (t, bb, 0, 0)),
        out_shape=jax.ShapeDtypeStruct((4, b, nsb, LANES), BF16),
        compiler_params=_params("parallel", "parallel"),
        name="compress_blocks",
    )(t4, phi_w1, phi_w2, cmp_pos, gain)


def _qg_kernel(h_ref, g_ref, sh_ref, sc_ref, w_ref, gain_ref, cos_ref, sin_ref, bd_ref,
               qn_ref, qr_ref, gate_ref):
    u = _normmod(h_ref[0], g_ref[...], sh_ref[...], sc_ref[...]).astype(BF16)
    y = _dot(u, w_ref[...])
    cos = cos_ref[...]
    sin = sin_ref[...]
    bd = bd_ref[...]
    hdim = qn_ref.shape[2]
    for cb in range(hdim // LANES):
        sl = slice(cb * LANES, (cb + 1) * LANES)
        z = _head_rms(y[:, sl], bd) * gain_ref[:, sl]
        qn_ref[0, :, sl] = z.astype(BF16)
        qr_ref[0, :, sl] = _rope(z, cos, sin).astype(BF16)
    gate_ref[0] = _sigmoid(y[:, hdim:])


def _qg_proj(h, norm_g3, mod5, w, gain, cos, sin, bd, l):
    b, s, d = h.shape
    n = w.shape[1]
    hdim = N_HEADS * HEAD_DIM
    tm = min(512, s)
    row = lambda width, dt: (pl.BlockSpec((1, tm, width), lambda bb, i: (bb, i, 0)),
                             jax.ShapeDtypeStruct((b, s, width), dt))
    outs = [row(hdim, BF16), row(hdim, BF16), row(n - hdim, F32)]
    return pl.pallas_call(
        _qg_kernel,
        grid=(b, s // tm),
        in_specs=[pl.BlockSpec((1, tm, d), lambda bb, i: (bb, i, 0)),
                  pl.BlockSpec((None, 1, d), lambda bb, i: (l * 3 + 1, 0, 0)),
                  _mod_block(d, l, 3),
                  _mod_block(d, l, 4),
                  pl.BlockSpec((d, n), lambda bb, i: (0, 0)),
                  pl.BlockSpec((1, hdim), lambda bb, i: (0, 0)),
                  pl.BlockSpec((tm, LANES), lambda bb, i: (i, 0)),
                  pl.BlockSpec((tm, LANES), lambda bb, i: (i, 0)),
                  pl.BlockSpec((LANES, LANES), lambda bb, i: (0, 0))],
        out_specs=[o[0] for o in outs],
        out_shape=[o[1] for o in outs],
        compiler_params=_params("parallel", "parallel"),
        name="nsa_query_gate_proj",
    )(h, norm_g3, mod5, mod5, w, gain, cos, sin, bd)


def _nsa_body(qn_ref, qr_ref, gate_ref, kc_ref, vc_ref, ks_ref, vs_ref, kw_ref, vw_ref, wimpt_ref, gsel_ref,
              o_ref, qa_scr, m_scr, acc_scr, mw_scr, accw_scr, *, tq, shifted):
    i = pl.program_id(2)
    start = i * tq
    rows = GROUP * tq
    ncp = kc_ref.shape[0]
    low = lax.broadcasted_iota(jnp.int32, (tq, LANES), 1) < HEAD_DIM
    low_r = lax.broadcasted_iota(jnp.int32, (rows, LANES), 1) < HEAD_DIM

    def accumulate(s, v, mask, first, acc_ref=acc_scr, m_ref=m_scr):
        if not shifted:
            p = jnp.exp2(s)
            if mask is not None:
                p = jnp.where(mask, p, 0.0)
            pv = _dot(p.astype(BF16), v)
            acc_ref[...] = pv if first else acc_ref[...] + pv
            return
        if mask is not None:
            s = jnp.where(mask, s, NEG)
        if first:
            m_new = jnp.max(s, axis=-1, keepdims=True)
            acc_ref[...] = _dot(jnp.exp2(s - m_new).astype(BF16), v)
        else:
            m_old = m_ref[...]
            m_new = jnp.maximum(m_old, jnp.max(s, axis=-1, keepdims=True))
            acc_ref[...] = jnp.exp2(m_old - m_new) * acc_ref[...] + _dot(jnp.exp2(s - m_new).astype(BF16), v)
        m_ref[...] = m_new

    def stack(q):
        parts = []
        for g in range(GROUP):
            blk = q[:, (g // 2) * LANES:(g // 2 + 1) * LANES]
            parts.append(jnp.where(low if g % 2 == 0 else ~low, blk, jnp.zeros_like(blk)))
        return parts

    qn = jnp.concatenate(stack(qn_ref[0]), axis=0)
    sc = _dot_t(qn, kc_ref[...])
    c_end = lax.broadcasted_iota(jnp.int32, (tq, ncp), 1) * CMP_STRIDE + (CMP_LEN - 1)
    valid_t = c_end <= start + lax.broadcasted_iota(jnp.int32, (tq, ncp), 0)
    if shifted:
        valid = jnp.concatenate([valid_t] * GROUP, axis=0)
        sc = jnp.where(valid, sc, NEG)
        sc = sc - jnp.max(sc, axis=-1, keepdims=True)
        p = jnp.where(valid, jnp.exp2(sc), 0.0)
    else:
        p = jnp.exp2(sc + jnp.concatenate([jnp.where(valid_t, 0.0, NEG)] * GROUP, axis=0))
    oc = _dot(p.astype(BF16), vc_ref[...])
    inv = 1.0 / jnp.maximum(jnp.where(low_r, pltpu.roll(oc, HEAD_DIM, 1), oc), 1e-30)
    o_cmp = oc * inv
    psum_blocks = []
    for cb in range(ncp // LANES):
        cs = slice(cb * LANES, (cb + 1) * LANES)
        tot = p[0:tq, cs] * inv[0:tq]
        for g in range(1, GROUP):
            tot = tot + p[g * tq:(g + 1) * tq, cs] * inv[g * tq:(g + 1) * tq]
        psum_blocks.append(tot)
    psum = jnp.concatenate(psum_blocks, axis=1)

    qr_parts = stack(qr_ref[0])
    for g in range(GROUP):
        qa_scr[g * tq:(g + 1) * tq, 0:LANES] = qr_parts[g]
    qr = jnp.concatenate(qr_parts, axis=0)

    tri = (lax.broadcasted_iota(jnp.int32, (rows, tq), 1)
           <= (lax.broadcasted_iota(jnp.int32, (rows, tq), 0) & (tq - 1)))

    def key_block(kb):
        return pl.ds(pl.multiple_of(kb * tq, tq), tq)

    diag = key_block(i)
    accumulate(_dot_t(qr, kw_ref[diag, :]), vw_ref[diag, :], tri, True, accw_scr, mw_scr)

    p_hi = psum.astype(BF16)
    p_lo = (psum - p_hi.astype(F32)).astype(BF16)
    imp_t = _dot_t(wimpt_ref[...], p_hi) + _dot_t(wimpt_ref[...], p_lo)

    blk = key_block(jnp.maximum(i - 1, 0))
    accumulate(_dot_t(qr, kw_ref[blk, :]), vw_ref[blk, :], jnp.broadcast_to(i >= 1, tri.shape), False,
               accw_scr, mw_scr)
    blk = key_block(jnp.maximum(i - 2, 0))
    accumulate(_dot_t(qr, kw_ref[blk, :]), vw_ref[blk, :], (~tri) & (i >= 2), False, accw_scr, mw_scr)

    blk_id = lax.broadcasted_iota(jnp.int32, (LANES, tq), 0)
    jt = (start + lax.broadcasted_iota(jnp.int32, (LANES, tq), 1)) >> SEL_SHIFT
    forced = (blk_id == 0) | (blk_id == jt) | (blk_id == jt - 1)
    work = jnp.where(blk_id > jt, -1.0, jnp.where(forced, -2.0, imp_t))
    sel = forced
    for _ in range(N_SELECT - 3):
        mx = jnp.max(work, axis=0, keepdims=True)
        idx = jnp.min(jnp.where(work == mx, blk_id, LANES), axis=0, keepdims=True)
        pick = blk_id == idx
        sel = sel | pick
        work = jnp.where(pick, -2.0, work)
    sel_bias = jnp.transpose(jnp.where(sel, 0.0, NEG)).astype(BF16)

    for g in range(GROUP):
        qa_scr[g * tq:(g + 1) * tq, LANES:2 * LANES] = sel_bias

    accumulate(_dot_t(qa_scr[...], ks_ref[diag, :]), vs_ref[diag, :], tri, True)

    def slc_pair(kp, carry):
        blk = pl.ds(pl.multiple_of(kp * 2 * tq, 2 * tq), 2 * tq)
        accumulate(_dot_t(qa_scr[...], ks_ref[blk, :]), vs_ref[blk, :], None, False)
        return carry

    lax.fori_loop(0, i // 2, slc_pair, 0)

    @pl.when(i % 2 == 1)
    def _():
        blk = key_block(i - 1)
        accumulate(_dot_t(qa_scr[...], ks_ref[blk, :]), vs_ref[blk, :], None, False)

    gates = gate_ref[0]
    g_hi = gates.astype(BF16)
    g_lo = (gates - g_hi.astype(F32)).astype(BF16)
    gate_wide = _dot(g_hi, gsel_ref[...]) + _dot(g_lo, gsel_ref[...])
    for pr in range(GROUP // 2):
        ev = slice(2 * pr * tq, (2 * pr + 1) * tq)
        od = slice((2 * pr + 1) * tq, (2 * pr + 2) * tq)

        def pair(acc):
            a_e, a_o = acc[ev], acc[od]
            r_e = pltpu.roll(a_e, HEAD_DIM, 1)
            r_o = pltpu.roll(a_o, HEAD_DIM, 1)
            return jnp.where(low, a_e, r_o) / jnp.where(low, r_e, a_o)

        def gate(br):
            c = (br * (GROUP // 2) + pr) * LANES
            return gate_wide[:, c:c + LANES]

        o_c = jnp.where(low, o_cmp[ev], pltpu.roll(o_cmp[od], HEAD_DIM, 1))
        out = gate(0) * o_c + gate(1) * pair(acc_scr) + gate(2) * pair(accw_scr)
        o_ref[0, :, pr * LANES:(pr + 1) * LANES] = out.astype(BF16)


def _nsa_kernel(small_ref, *refs, tq):
    @pl.when(small_ref[0] != 0)
    def _():
        _nsa_body(*refs, tq=tq, shifted=False)

    @pl.when(small_ref[0] == 0)
    def _():
        _nsa_body(*refs, tq=tq, shifted=True)


def _nsa_attention(small, qn, qr, gates, cmp4, ks, vs, kw, vw, wimpt):
    b, s, hdim = qn.shape
    tq = min(256, s)
    assert s // SEL_BLOCK <= LANES and s % tq == 0 and SLIDE_WINDOW == 2 * tq
    ncp = cmp4.shape[2]
    gw = GROUP * HEAD_DIM
    rows = GROUP * tq
    qspec = pl.BlockSpec((1, tq, gw), lambda bb, kv, i: (bb, i, kv))
    full = lambda width: pl.BlockSpec((None, s, width), lambda bb, kv, i: (bb, 0, kv))
    col = np.arange(3 * (GROUP // 2) * LANES)[None, :]
    src = (col // LANES // (GROUP // 2)) * GROUP + 2 * ((col // LANES) % (GROUP // 2)) + (col % LANES) // HEAD_DIM
    gsel = (np.arange(LANES)[:, None] == src).astype(BF16)
    return pl.pallas_call(
        functools.partial(_nsa_kernel, tq=tq),
        grid=(b, N_KV_HEADS, s // tq),
        in_specs=[pl.BlockSpec(memory_space=pltpu.SMEM), qspec, qspec,
                  pl.BlockSpec((1, tq, LANES), lambda bb, kv, i: (bb, i, kv)),
                  pl.BlockSpec((None, None, ncp, LANES), lambda bb, kv, i: (kv, bb, 0, 0)),
                  pl.BlockSpec((None, None, ncp, LANES), lambda bb, kv, i: (N_KV_HEADS + kv, bb, 0, 0)),
                  full(2 * LANES), full(LANES), full(LANES), full(LANES),
                  pl.BlockSpec((LANES, ncp), lambda bb, kv, i: (0, 0)),
                  pl.BlockSpec(gsel.shape, lambda bb, kv, i: (0, 0))],
        out_specs=qspec,
        out_shape=jax.ShapeDtypeStruct((b, s, hdim), BF16),
        scratch_shapes=[pltpu.VMEM((rows, 2 * LANES), BF16),
                        pltpu.VMEM((rows, 1), F32), pltpu.VMEM((rows, LANES), F32),
                        pltpu.VMEM((rows, 1), F32), pltpu.VMEM((rows, LANES), F32)],
        compiler_params=_params("parallel", "parallel", "arbitrary"),
        name="nsa_attention",
    )(small, qn, qr, gates, cmp4, cmp4, ks, vs, kw, vw, wimpt, gsel)


def _rope_tables(s, tm=None, rate=1):
    half = HEAD_DIM // 2
    row = lax.broadcasted_iota(jnp.int32, (s, LANES), 0)
    lane = lax.broadcasted_iota(jnp.int32, (s, LANES), 1)
    if rate > 1:
        seg = tm // rate
        local = row % tm
        row = row - local + (local % seg) * rate + local // seg
    inv = jnp.tile(ROPE_THETA ** (-jnp.arange(half, dtype=F32) / half), LANES // half)
    ang = row.astype(F32) * inv[None, :]
    cos, sin = jnp.cos(ang), jnp.sin(ang)
    return cos, jnp.where(lane % HEAD_DIM < half, -sin, sin), sin


def _split_half(t, axis):
    hpg = MXU_WIDTH // HEAD_DIM
    half = HEAD_DIM // 2
    shape = t.shape
    t = t.reshape(shape[:axis] + (shape[axis] // MXU_WIDTH, hpg, 2, half) + shape[axis + 1:])
    return t.swapaxes(axis + 1, axis + 2).reshape(shape)


def _importance_weights(ncp):
    r = SEL_BLOCK // CMP_STRIDE
    nper = CMP_LEN // CMP_STRIDE
    o = np.arange(ncp)[:, None] - r * np.arange(LANES)[None, :]
    w = np.zeros((ncp, LANES), np.float32)
    for off in range(-(nper - 1), r):
        cnt = sum(1 for m in range(r) for n in range(nper) if m - n == off)
        w += np.where(o == off, float(cnt), 0.0)
    return w.astype(BF16)


def kernel(x, c, norm_g, w_ada, b_ada, ffn_w_in, ffn_w_out, a_w_qkv, a_q_gain, a_k_gain, a_w_o, kv_norm_g,
           w_ada_kv, b_ada_kv, w_kv, kv_k_gain, cmp_pos, phi_w1, phi_w2, b_w_qg, b_q_gain, b_w_o):
    b, s, d = x.shape
    depth = norm_g.shape[0]
    n_a = depth // 2
    hdim = N_HEADS * HEAD_DIM
    assert depth == 2 and b <= 8

    c8 = jnp.zeros((8, d), F32).at[:b].set(c)
    mod = _ada(c8, w_ada, b_ada.reshape(depth, 1, 9 * d))
    mod5 = mod[:, :b].reshape(depth, b, 9, 1, d)
    kvmod = _ada(c8, w_ada_kv[None], b_ada_kv.reshape(1, 1, 2 * d))
    kvmod5 = kvmod[:, :b].reshape(1, b, 2, 1, d)

    norm_g3 = norm_g.reshape(depth * 3, 1, d)
    w_in = ffn_w_in.astype(BF16)
    w_out = ffn_w_out.astype(BF16)
    cos, sin, _ = _rope_tables(s)
    blk = np.arange(LANES) // HEAD_DIM
    bd = (blk[:, None] == blk[None, :]).astype(BF16)
    blk = (np.arange(MXU_WIDTH) % LANES) // (HEAD_DIM // 2)
    bd_split = (blk[:, None] == blk[None, :]).astype(BF16)
    q_scale = HEAD_DIM ** -0.5 * LOG2E

    h = x
    for l in range(depth):
        h = _ffn(h, norm_g3, mod5, w_in, w_out, l, 0, 0)
        if l < n_a:
            gain = jnp.stack([_split_half(jnp.tile(a_q_gain[l] * q_scale, (1, N_HEADS)), 1),
                              _split_half(jnp.tile(a_k_gain[l], (1, N_HEADS)), 1),
                              jnp.ones((N_DIL, hdim), F32)], axis=1).reshape(3 * N_DIL, 1, hdim)
            cols = np.arange(N_DIL * 3 * hdim).reshape(N_DIL, 3, hdim)
            cols = np.concatenate([_split_half(cols[:, :2], 2), cols[:, 2:]], axis=1).reshape(-1)
            w_qkv = a_w_qkv[l][:, cols].astype(BF16)
            bound = HEAD_DIM * q_scale * jnp.max(jnp.abs(a_q_gain[l])) * jnp.max(jnp.abs(a_k_gain[l]))
            small = (bound <= SCORE_LIMIT).astype(jnp.int32).reshape(1)
            parts = [_dilated_attention(small, _qkv_proj(h, norm_g3, mod5, w_qkv, gain, bd_split, l, g), g)
                     for g in range(N_DIL)]
            h = _merge_out([p[0] for p in parts], [p[1] for p in parts], h, mod5, a_w_o[l].astype(BF16), l)
        else:
            j = l - n_a
            w_q = b_w_qg[j][:, :hdim]
            w_g = b_w_qg[j][:, hdim:].reshape(d, 3, N_KV_HEADS, GROUP).transpose(0, 2, 1, 3)
            w_g = jnp.pad(w_g.reshape(d, N_KV_HEADS, 3 * GROUP), ((0, 0), (0, 0), (0, LANES - 3 * GROUP)))
            w_qg = jnp.concatenate([w_q, w_g.reshape(d, N_KV_HEADS * LANES)], axis=1).astype(BF16)
            gain = jnp.tile(b_q_gain[j] * q_scale, N_HEADS)[None, :]
            qn, qr, gates = _qg_proj(h, norm_g3, mod5, w_qg, gain, cos, sin, bd, l)
            bound = HEAD_DIM * q_scale * jnp.max(jnp.abs(b_q_gain[j])) * jnp.max(jnp.abs(kv_k_gain))
            small = (bound <= SCORE_LIMIT).astype(jnp.int32).reshape(1)
            o = _nsa_attention(small, qn, qr, gates, *shared)
            h = _out_proj(o, h, mod5, b_w_o[j].astype(BF16), l)
        h = _ffn(h, norm_g3, mod5, w_in, w_out, l, 2, 1)
        if l == n_a - 1:
            col = lambda br, kv, kvh: br * 4 * HEAD_DIM + kv * 2 * HEAD_DIM + kvh * HEAD_DIM + jnp.arange(HEAD_DIM)
            dup = lambda br, kv, kvh: jnp.concatenate([col(br, kv, kvh)] * 2)
            cols = jnp.concatenate(
                [dup(br, 0, kvh) for br in (1, 2) for kvh in range(N_KV_HEADS)]
                + [dup(br, 1, kvh) for br in (1, 2) for kvh in range(N_KV_HEADS)]
                + [col(0, kv, kvh) for kv in range(2) for kvh in range(N_KV_HEADS)])
            w_kv_ext = w_kv[:, cols].astype(BF16)
            kgain = jnp.concatenate([jnp.tile(kv_k_gain[1], (2, 2)), jnp.tile(kv_k_gain[2], (2, 2)),
                                     jnp.ones((4, LANES), F32)], axis=0)
            ks, kw, vs, vw, kvc = _kv_proj(h, kv_norm_g[None, :], kvmod5, w_kv_ext, kgain, cos, sin, bd)
            nsb = s // CMP_STRIDE
            t4 = kvc.reshape(b, nsb, CMP_STRIDE, 4, HEAD_DIM).transpose(3, 0, 1, 2, 4)
            t4 = t4.reshape(4, b, nsb, CMP_STRIDE * HEAD_DIM)
            cmp4 = _compress(t4, phi_w1, phi_w2, cmp_pos.reshape(2, 1, CMP_LEN * HEAD_DIM), kv_k_gain[0][None, :])
            shared = (cmp4, ks, vs, kw, vw, _importance_weights(nsb).T)
    return h
```

```python
import functools

import jax
import jax.numpy as jnp
import numpy as np
from jax import lax
from jax.experimental import pallas as pl
from jax.experimental.pallas import tpu as pltpu

HEAD_DIM = 64
N_HEADS = 16
DIL_WINDOWS = (128, 512, 2048)
DIL_RATES = (1, 4, 16)
N_DIL = 3
N_KV_HEADS = 2
GROUP = N_HEADS // N_KV_HEADS
CMP_STRIDE = 16
CMP_LEN = 32
SEL_BLOCK = 64
SEL_SHIFT = SEL_BLOCK.bit_length() - 1
N_SELECT = 16
SLIDE_WINDOW = 512
ROPE_THETA = 10000.0
EPS = 1e-6

LANES = 128
MXU_WIDTH = 256
Q_ROWS = 128
NEG = -1e30
LOG2E = 1.4426950408889634
SCORE_LIMIT = 60.0
VMEM_LIMIT = 56 * 1024 * 1024

BF16 = jnp.bfloat16
F32 = jnp.float32


def _params(*sem):
    return pltpu.CompilerParams(dimension_semantics=sem, vmem_limit_bytes=VMEM_LIMIT)


def _dot(a, b):
    return jnp.dot(a, b, preferred_element_type=F32)


def _dot_t(a, b):
    return lax.dot_general(a, b, (((1,), (1,)), ((), ())), preferred_element_type=F32)


def _split_dot(a, b):
    a_hi = a.astype(BF16)
    a_lo = (a - a_hi.astype(F32)).astype(BF16)
    b_hi = b.astype(BF16)
    b_lo = (b - b_hi.astype(F32)).astype(BF16)
    return _dot(a_hi, b_hi) + (_dot(a_hi, b_lo) + _dot(a_lo, b_hi))


def _sigmoid(x):
    return 1.0 / (1.0 + jnp.exp(-x))


def _normmod(x, g, shift, scale):
    return x * lax.rsqrt(jnp.mean(x * x, axis=-1, keepdims=True) + EPS) * (g * (1.0 + scale)) + shift


def _head_rms(y, bd):
    ss = _dot((y * y).astype(BF16), bd)
    return y * lax.rsqrt(ss * (1.0 / HEAD_DIM) + EPS)


def _rope(z, cos, sin_signed):
    lane = lax.broadcasted_iota(jnp.int32, z.shape, 1)
    first_half = (lane & (HEAD_DIM - 1)) < HEAD_DIM // 2
    swapped = jnp.where(first_half, pltpu.roll(z, z.shape[1] - HEAD_DIM // 2, 1),
                        pltpu.roll(z, HEAD_DIM // 2, 1))
    return z * cos + swapped * sin_signed


def _ada_kernel(c_ref, w_ref, b_ref, o_ref):
    c = c_ref[...]
    a = (c * _sigmoid(c)).astype(BF16)
    o_ref[...] = _dot(a, w_ref[...].astype(BF16)) + b_ref[...]


def _ada(c8, w, b):
    nl, d, n = w.shape
    tn = 1024
    return pl.pallas_call(
        _ada_kernel,
        grid=(nl, n // tn),
        in_specs=[pl.BlockSpec((8, d), lambda l, j: (0, 0)),
                  pl.BlockSpec((None, d, tn), lambda l, j: (l, 0, j)),
                  pl.BlockSpec((None, 1, tn), lambda l, j: (l, 0, j))],
        out_specs=pl.BlockSpec((None, 8, tn), lambda l, j: (l, 0, j)),
        out_shape=jax.ShapeDtypeStruct((nl, 8, n), F32),
        compiler_params=_params("parallel", "parallel"),
        name="ada_table",
    )(c8, w, b)


def _mod_block(d, l, k):
    return pl.BlockSpec((None, None, None, 1, d), lambda b, *_: (l, b, k, 0, 0))


def _ffn_kernel(h_ref, g_ref, sh_ref, sc_ref, gt_ref, wg_ref, wu_ref, wo_ref, o_ref, u_scr, acc_scr):
    j = pl.program_id(2)

    @pl.when(j == 0)
    def _():
        u_scr[...] = _normmod(h_ref[0], g_ref[...], sh_ref[...], sc_ref[...]).astype(BF16)
        acc_scr[...] = jnp.zeros_like(acc_scr)

    u = u_scr[...]
    tf = wg_ref.shape[1]
    bounds = [(lo, min(lo + MXU_WIDTH, tf)) for lo in range(0, tf, MXU_WIDTH)]

    def gate_up(c):
        lo, hi = bounds[c]
        return _dot(u, wg_ref[:, lo:hi]), _dot(u, wu_ref[:, lo:hi])

    nxt = gate_up(0)
    for c, (lo, hi) in enumerate(bounds):
        gate, up = nxt
        if c + 1 < len(bounds):
            nxt = gate_up(c + 1)
        act = (gate * _sigmoid(gate) * up).astype(BF16)
        acc_scr[...] += _dot(act, wo_ref[lo:hi, :])

    @pl.when(j == pl.num_programs(2) - 1)
    def _():
        o_ref[0] = h_ref[0] + (0.5 * (1.0 + gt_ref[...])) * acc_scr[...]


def _ffn(h, norm_g3, mod5, w_in, w_out, l, sub, which):
    b, s, d = h.shape
    f = w_out.shape[2]
    tm = min(1024, s)
    nf = 2
    tf = f // nf
    assert f % nf == 0 and tf % LANES == 0
    return pl.pallas_call(
        _ffn_kernel,
        grid=(b, s // tm, nf),
        in_specs=[pl.BlockSpec((1, tm, d), lambda bb, i, j: (bb, i, 0)),
                  pl.BlockSpec((None, 1, d), lambda bb, i, j: (l * 3 + sub, 0, 0)),
                  _mod_block(d, l, sub * 3 + 0),
                  _mod_block(d, l, sub * 3 + 1),
                  _mod_block(d, l, sub * 3 + 2),
                  pl.BlockSpec((None, None, d, tf), lambda bb, i, j: (l, which, 0, j)),
                  pl.BlockSpec((None, None, d, tf), lambda bb, i, j: (l, which, 0, j + nf)),
                  pl.BlockSpec((None, None, tf, d), lambda bb, i, j: (l, which, j, 0))],
        out_specs=pl.BlockSpec((1, tm, d), lambda bb, i, j: (bb, i, 0)),
        out_shape=jax.ShapeDtypeStruct((b, s, d), F32),
        scratch_shapes=[pltpu.VMEM((tm, d), BF16), pltpu.VMEM((tm, d), F32)],
        compiler_params=_params("parallel", "parallel", "arbitrary"),
        name="swiglu_half_step",
    )(h, norm_g3, mod5, mod5, mod5, w_in, w_in, w_out)


def _qkv_kernel(h_ref, g_ref, sh_ref, sc_ref, w_ref, gain_ref, cos_ref, sin_ref, bd_ref, o_ref, u_scr, uf_scr,
                *, rate):
    n = pl.program_id(2)
    tm = u_scr.shape[0]
    seg = tm // rate

    @pl.when(n == 0)
    def _():
        u = _normmod(h_ref[0], g_ref[...], sh_ref[...], sc_ref[...])
        if rate == 1:
            u_scr[...] = u.astype(BF16)
        else:
            for cb in range(u.shape[1] // LANES):
                sl = slice(cb * LANES, (cb + 1) * LANES)
                uf_scr[cb] = u[:, sl]
                for res in range(rate):
                    u_scr[res * seg:(res + 1) * seg, sl] = uf_scr[cb, pl.ds(res, seg, stride=rate), :].astype(BF16)

    n_grp = o_ref.shape[1]

    def store(grp, z):
        for res in range(rate):
            o_ref[res, grp] = z[res * seg:(res + 1) * seg]

    @pl.when(n == 2)
    def _():
        for grp in range(n_grp):
            store(grp, _dot(u_scr[...], w_ref[:, grp * MXU_WIDTH:(grp + 1) * MXU_WIDTH]).astype(BF16))

    @pl.when(n != 2)
    def _():
        cos = cos_ref[...]
        sin = sin_ref[...]
        bd = bd_ref[...]
        def project(grp):
            return _dot(u_scr[...], w_ref[:, grp * MXU_WIDTH:(grp + 1) * MXU_WIDTH])

        y = project(0)
        for grp in range(n_grp):
            y_next = project(grp + 1) if grp + 1 < n_grp else None
            z = _head_rms(y, bd) * gain_ref[:, grp * MXU_WIDTH:(grp + 1) * MXU_WIDTH]
            x1, x2 = z[:, :LANES], z[:, LANES:]
            store(grp, jnp.concatenate([x1 * cos - x2 * sin, x1 * sin + x2 * cos], axis=1).astype(BF16))
            y = y_next


def _qkv_proj(h, norm_g3, mod5, w, gain, bd, l, g):
    b, s, d = h.shape
    rate = DIL_RATES[g]
    tm = min(1024, s)
    seg = tm // rate
    tn = N_HEADS * HEAD_DIM
    n_grp = tn // MXU_WIDTH
    cos, _, sin = _rope_tables(s, tm, rate)

    return pl.pallas_call(
        functools.partial(_qkv_kernel, rate=rate),
        grid=(b, s // tm, 3),
        in_specs=[pl.BlockSpec((1, tm, d), lambda bb, i, j: (bb, i, 0)),
                  pl.BlockSpec((None, 1, d), lambda bb, i, j: (l * 3 + 1, 0, 0)),
                  _mod_block(d, l, 3),
                  _mod_block(d, l, 4),
                  pl.BlockSpec((d, tn), lambda bb, i, j: (0, 3 * g + j)),
                  pl.BlockSpec((None, 1, tn), lambda bb, i, j: (3 * g + j, 0, 0)),
                  pl.BlockSpec((tm, LANES), lambda bb, i, j: (i, 0)),
                  pl.BlockSpec((tm, LANES), lambda bb, i, j: (i, 0)),
                  pl.BlockSpec(bd.shape, lambda bb, i, j: (0, 0))],
        out_specs=pl.BlockSpec((None, rate, n_grp, seg, MXU_WIDTH), lambda bb, i, j: (bb, 0, j, i, 0)),
        out_shape=jax.ShapeDtypeStruct((b, rate, 3 * n_grp, s // rate, MXU_WIDTH), BF16),
        scratch_shapes=[pltpu.VMEM((tm, d), BF16), pltpu.VMEM((d // LANES, tm, LANES), F32)],
        compiler_params=_params("parallel", "parallel", "arbitrary"),
        name=f"qkv_proj_rate{rate}",
    )(h, norm_g3, mod5, mod5, w, gain, cos, sin, bd)


def _dil_body(bias_ref, q_ref, kp_ref, kc_ref, vp_ref, vc_ref, o_ref, lse_ref, *, tq, shifted):
    i = pl.program_id(2)
    grp = pl.program_id(3)
    hpg = MXU_WIDTH // HEAD_DIM
    qlane = lax.broadcasted_iota(jnp.int32, (Q_ROWS, MXU_WIDTH), 1)
    q_head = (qlane & (LANES - 1)) // (HEAD_DIM // 2)
    v_head = qlane // HEAD_DIM
    seg = lax.broadcasted_iota(jnp.int32, (Q_ROWS, LANES), 1) // (LANES // N_HEADS)

    def scores(j):
        q = q_ref[j * Q_ROWS:(j + 1) * Q_ROWS, :]
        zero = jnp.zeros_like(q)
        q2 = jnp.concatenate([jnp.where(q_head == t, q, zero) for t in range(hpg)], axis=0)
        if j == 0:
            k = jnp.concatenate([kp_ref[...], kc_ref[0:Q_ROWS, :]], axis=0)
        else:
            k = kc_ref[(j - 1) * Q_ROWS:(j + 1) * Q_ROWS, :]
        return _dot_t(q2, k)

    for j in range(tq // Q_ROWS):
        rs = slice(j * Q_ROWS, (j + 1) * Q_ROWS)
        s = scores(j) + bias_ref[...]
        if j == 0:
            v = jnp.concatenate([vp_ref[...], vc_ref[0:Q_ROWS, :]], axis=0)
            prev_key = lax.broadcasted_iota(jnp.int32, s.shape, 1) < Q_ROWS
            s = jnp.where(prev_key & (i == 0), NEG, s)
        else:
            v = vc_ref[(j - 1) * Q_ROWS:(j + 1) * Q_ROWS, :]
        if shifted:
            m = jnp.max(s, axis=-1, keepdims=True)
            p = jnp.exp2(s - m)
        else:
            p = jnp.exp2(s)
        den = jnp.sum(p, axis=-1, keepdims=True)
        o2 = _dot(p.astype(BF16), v) / den
        lse2 = jnp.log(den) * LOG2E
        if shifted:
            lse2 = lse2 + m
        out = o2[0:Q_ROWS]
        new = jnp.where(grp == 0, 0.0, lse_ref[0, rs, :])
        for t in range(hpg):
            ts = slice(t * Q_ROWS, (t + 1) * Q_ROWS)
            if t:
                out = jnp.where(v_head == t, o2[ts], out)
            new = jnp.where(seg == hpg * grp + t, lse2[ts], new)
        o_ref[0, rs, :] = out.astype(BF16)
        lse_ref[0, rs, :] = new


def _dil_kernel(small_ref, *refs, tq):
    @pl.when(small_ref[0] != 0)
    def _():
        _dil_body(*refs, tq=tq, shifted=False)

    @pl.when(small_ref[0] == 0)
    def _():
        _dil_body(*refs, tq=tq, shifted=True)


def _dilated_attention(small, qkv_r, g):
    b, rate, n3, l, width = qkv_r.shape
    s = l * rate
    n_back = DIL_WINDOWS[g] // rate
    assert n_back <= Q_ROWS and rate == DIL_RATES[g] and width == MXU_WIDTH
    tq = min(512, l)
    hdim = N_HEADS * HEAD_DIM
    n_grp = n3 // 3
    sub = tq // Q_ROWS

    def cur(off):
        return pl.BlockSpec((None, None, None, tq, width), lambda bb, r, i, gp: (bb, r, off + gp, i, 0))

    def prev(off):
        return pl.BlockSpec((None, None, None, Q_ROWS, width),
                            lambda bb, r, i, gp: (bb, r, off + gp, jnp.maximum(i * sub - 1, 0), 0))

    rows = (width // HEAD_DIM) * Q_ROWS
    dist = Q_ROWS + (np.arange(rows)[:, None] % Q_ROWS) - np.arange(2 * Q_ROWS)[None, :]
    bias = np.where((dist >= 0) & (dist <= n_back), 0.0, NEG).astype(np.float32)

    o, lse = pl.pallas_call(
        functools.partial(_dil_kernel, tq=tq),
        grid=(b, rate, l // tq, n_grp),
        in_specs=[pl.BlockSpec(memory_space=pltpu.SMEM),
                  pl.BlockSpec(bias.shape, lambda bb, r, i, gp: (0, 0)),
                  cur(0), prev(n_grp), cur(n_grp), prev(2 * n_grp), cur(2 * n_grp)],
        out_specs=[pl.BlockSpec((1, None, tq, width), lambda bb, r, i, gp: (bb, r, i, gp)),
                   pl.BlockSpec((1, None, tq, LANES), lambda bb, r, i, gp: (bb, r, i, 0))],
        out_shape=[jax.ShapeDtypeStruct((b, rate, l, hdim), BF16),
                   jax.ShapeDtypeStruct((b, rate, l, LANES), F32)],
        compiler_params=_params("parallel", "parallel", "parallel", "arbitrary"),
        name=f"dilated_attention_rate{rate}",
    )(small, bias, qkv_r, qkv_r, qkv_r, qkv_r, qkv_r)
    return o, lse


def _merge_out_kernel(o0_ref, o1_ref, o2_ref, l0_ref, l1_ref, l2_ref, p1_ref, p2_ref, e_ref, h_ref, gt_ref, w_ref,
                      out_ref):
    def positions(x_ref, p_ref):
        x = jnp.concatenate([x_ref[res] for res in range(x_ref.shape[0])], axis=0)
        if p_ref is None:
            return x.astype(F32)
        if x.dtype == BF16:
            return _dot(p_ref[...], x)
        hi = x.astype(BF16)
        return _dot(p_ref[...], hi) + _dot(p_ref[...], (x - hi.astype(F32)).astype(BF16))

    l0, l1, l2 = positions(l0_ref, None), positions(l1_ref, p1_ref), positions(l2_ref, p2_ref)
    m = jnp.maximum(jnp.maximum(l0, l1), l2)
    e0, e1, e2 = jnp.exp2(l0 - m), jnp.exp2(l1 - m), jnp.exp2(l2 - m)
    inv = 1.0 / (e0 + e1 + e2)
    spread = e_ref[...]
    o = (_dot((e0 * inv).astype(BF16), spread) * positions(o0_ref, None)
         + _dot((e1 * inv).astype(BF16), spread) * positions(o1_ref, p1_ref)
         + _dot((e2 * inv).astype(BF16), spread) * positions(o2_ref, p2_ref))
    out_ref[0] = h_ref[0] + (1.0 + gt_ref[...]) * _dot(o.astype(BF16), w_ref[...])


def _merge_out(os_, lses, h, mod5, w_o, l):
    b, s, d = h.shape
    hdim = w_o.shape[0]
    tm = min(512, s)
    row = lambda width: pl.BlockSpec((1, tm, width), lambda bb, i: (bb, i, 0))
    rep = LANES // N_HEADS
    lane = np.arange(LANES)[:, None]
    col = np.arange(hdim)[None, :]
    spread = ((lane % rep == 0) & (lane // rep == col // HEAD_DIM)).astype(BF16)

    def unpermute(rate):
        pos = np.arange(tm)
        return (((pos % rate) * (tm // rate) + pos // rate)[:, None] == np.arange(tm)[None, :]).astype(BF16)

    def residue_spec(t):
        rate, width = t.shape[1], t.shape[3]
        return pl.BlockSpec((None, rate, tm // rate, width), lambda bb, i: (bb, 0, i, 0))

    const = lambda shape: pl.BlockSpec(shape, lambda bb, i: (0, 0))
    return pl.pallas_call(
        _merge_out_kernel,
        grid=(b, s // tm),
        in_specs=[residue_spec(t) for t in (*os_, *lses)]
        + [const((tm, tm)), const((tm, tm)), const((LANES, hdim)), row(d), _mod_block(d, l, 5), const((hdim, d))],
        out_specs=row(d),
        out_shape=jax.ShapeDtypeStruct((b, s, d), F32),
        compiler_params=_params("parallel", "parallel"),
        name="merge_out_proj",
    )(*os_, *lses, unpermute(os_[1].shape[1]), unpermute(os_[2].shape[1]), spread, h, mod5, w_o)


def _out_kernel(o_ref, h_ref, gt_ref, w_ref, out_ref):
    out_ref[0] = h_ref[0] + (1.0 + gt_ref[...]) * _dot(o_ref[0], w_ref[...])


def _out_proj(o, h, mod5, w_o, l):
    b, s, d = h.shape
    hdim = w_o.shape[0]
    tm = min(1024, s)
    row = lambda width: pl.BlockSpec((1, tm, width), lambda bb, i: (bb, i, 0))
    return pl.pallas_call(
        _out_kernel,
        grid=(b, s // tm),
        in_specs=[row(hdim), row(d), _mod_block(d, l, 5), pl.BlockSpec((hdim, d), lambda bb, i: (0, 0))],
        out_specs=row(d),
        out_shape=jax.ShapeDtypeStruct((b, s, d), F32),
        compiler_params=_params("parallel", "parallel"),
        name="out_proj",
    )(o, h, mod5, w_o)


def _kv_kernel(h_ref, g_ref, sh_ref, sc_ref, w_ref, gain_ref, cos_ref, sin_ref, bd_ref,
               ks_ref, kw_ref, vs_ref, vw_ref, kvc_ref, *, tm):
    i = pl.program_id(1)
    u = _normmod(h_ref[0], g_ref[...], sh_ref[...], sc_ref[...]).astype(BF16)
    y = _dot(u, w_ref[...])
    cos = cos_ref[...]
    sin = sin_ref[...]
    bd = bd_ref[...]
    lane = lax.broadcasted_iota(jnp.int32, (tm, LANES), 1)
    pos = i * tm + lax.broadcasted_iota(jnp.int32, (tm, LANES), 0)
    onehot = jnp.where((pos >> SEL_SHIFT) == lane, 1.0, 0.0).astype(BF16)
    for j in range(4):
        z = _head_rms(y[:, j * LANES:(j + 1) * LANES], bd) * gain_ref[j:j + 1, :]
        zr = _rope(z, cos, sin).astype(BF16)
        if j < 2:
            ks_ref[0, :, 2 * j * LANES:(2 * j + 1) * LANES] = zr
            ks_ref[0, :, (2 * j + 1) * LANES:(2 * j + 2) * LANES] = onehot
        else:
            kw_ref[0, :, (j - 2) * LANES:(j - 1) * LANES] = zr
    for j in range(4):
        v = y[:, (4 + j) * LANES:(5 + j) * LANES]
        va = jnp.where(lane < HEAD_DIM, v, 1.0).astype(BF16)
        ref = vs_ref if j < 2 else vw_ref
        ref[0, :, (j % 2) * LANES:(j % 2 + 1) * LANES] = va
    kvc_ref[0] = y[:, 8 * LANES:10 * LANES]


def _kv_proj(h, kv_norm_g, kvmod5, w, gain, cos, sin, bd):
    b, s, d = h.shape
    n = w.shape[1]
    tm = min(512, s)
    row = lambda width, dt: (pl.BlockSpec((1, tm, width), lambda bb, i: (bb, i, 0)),
                             jax.ShapeDtypeStruct((b, s, width), dt))
    outs = [row(4 * LANES, BF16), row(2 * LANES, BF16), row(2 * LANES, BF16), row(2 * LANES, BF16),
            row(2 * LANES, F32)]
    return pl.pallas_call(
        functools.partial(_kv_kernel, tm=tm),
        grid=(b, s // tm),
        in_specs=[pl.BlockSpec((1, tm, d), lambda bb, i: (bb, i, 0)),
                  pl.BlockSpec((1, d), lambda bb, i: (0, 0)),
                  _mod_block(d, 0, 0),
                  _mod_block(d, 0, 1),
                  pl.BlockSpec((d, n), lambda bb, i: (0, 0)),
                  pl.BlockSpec((8, LANES), lambda bb, i: (0, 0)),
                  pl.BlockSpec((tm, LANES), lambda bb, i: (i, 0)),
                  pl.BlockSpec((tm, LANES), lambda bb, i: (i, 0)),
                  pl.BlockSpec((LANES, LANES), lambda bb, i: (0, 0))],
        out_specs=[o[0] for o in outs],
        out_shape=[o[1] for o in outs],
        compiler_params=_params("parallel", "parallel"),
        name="shared_kv_proj",
    )(h, kv_norm_g, kvmod5, kvmod5, w, gain, cos, sin, bd)


def _cmp_kernel(t_ref, w1_ref, w2_ref, pos_ref, gain_ref, o_ref):
    is_key = pl.program_id(0) < N_KV_HEADS
    t = t_ref[...]
    w1 = w1_ref[...]
    half = w1.shape[0] // 2
    first = _split_dot(t, w1[:half])
    second = _split_dot(t, w1[half:])
    second = pltpu.roll(second, second.shape[0] - 1, 0)
    posb = _split_dot(jnp.broadcast_to(pos_ref[...], (8, w1.shape[0])), w1)[0:1]
    hid = first + second + posb
    out = _split_dot(hid * _sigmoid(hid), w2_ref[...])
    key = out * lax.rsqrt(jnp.mean(out * out, axis=-1, keepdims=True) + EPS) * gain_ref[...]
    dup = jnp.concatenate([key, key], axis=-1)
    aug = jnp.concatenate([out, jnp.ones_like(out)], axis=-1)
    o_ref[...] = jnp.where(is_key, dup, aug).astype(BF16)


def _compress(t4, phi_w1, phi_w2, cmp_pos, gain):
    _, b, nsb, width = t4.shape
    hid = phi_w1.shape[2]
    return pl.pallas_call(
        _cmp_kernel,
        grid=(4, b),
        in_specs=[pl.BlockSpec((None, None, nsb, width), lambda t, bb: (t, bb, 0, 0)),
                  pl.BlockSpec((None, CMP_LEN * HEAD_DIM, hid), lambda t, bb: (t // N_KV_HEADS, 0, 0)),
                  pl.BlockSpec((None, hid, HEAD_DIM), lambda t, bb: (t // N_KV_HEADS, 0, 0)),
                  pl.BlockSpec((None, 1, CMP_LEN * HEAD_DIM), lambda t, bb: (t // N_KV_HEADS, 0, 0)),
                  pl.BlockSpec((1, HEAD_DIM), lambda t, bb: (0, 0))],
        out_specs=pl.BlockSpec((None, None, nsb, LANES), lambda t, bb: (t, bb, 0, 0)),
        out_shape=jax.ShapeDtypeStruct((4, b, nsb, LANES), BF16),
        compiler_params=_params("parallel", "parallel"),
        name="compress_blocks",
    )(t4, phi_w1, phi_w2, cmp_pos, gain)


def _qg_kernel(h_ref, g_ref, sh_ref, sc_ref, w_ref, gain_ref, cos_ref, sin_ref, bd_ref,
               qn_ref, qr_ref, gate_ref):
    u = _normmod(h_ref[0], g_ref[...], sh_ref[...], sc_ref[...]).astype(BF16)
    y = _dot(u, w_ref[...])
    cos = cos_ref[...]
    sin = sin_ref[...]
    bd = bd_ref[...]
    hdim = qn_ref.shape[2]
    for cb in range(hdim // LANES):
        sl = slice(cb * LANES, (cb + 1) * LANES)
        z = _head_rms(y[:, sl], bd) * gain_ref[:, sl]
        qn_ref[0, :, sl] = z.astype(BF16)
        qr_ref[0, :, sl] = _rope(z, cos, sin).astype(BF16)
    gate_ref[0] = _sigmoid(y[:, hdim:])


def _qg_proj(h, norm_g3, mod5, w, gain, cos, sin, bd, l):
    b, s, d = h.shape
    n = w.shape[1]
    hdim = N_HEADS * HEAD_DIM
    tm = min(512, s)
    row = lambda width, dt: (pl.BlockSpec((1, tm, width), lambda bb, i: (bb, i, 0)),
                             jax.ShapeDtypeStruct((b, s, width), dt))
    outs = [row(hdim, BF16), row(hdim, BF16), row(n - hdim, F32)]
    return pl.pallas_call(
        _qg_kernel,
        grid=(b, s // tm),
        in_specs=[pl.BlockSpec((1, tm, d), lambda bb, i: (bb, i, 0)),
                  pl.BlockSpec((None, 1, d), lambda bb, i: (l * 3 + 1, 0, 0)),
                  _mod_block(d, l, 3),
                  _mod_block(d, l, 4),
                  pl.BlockSpec((d, n), lambda bb, i: (0, 0)),
                  pl.BlockSpec((1, hdim), lambda bb, i: (0, 0)),
                  pl.BlockSpec((tm, LANES), lambda bb, i: (i, 0)),
                  pl.BlockSpec((tm, LANES), lambda bb, i: (i, 0)),
                  pl.BlockSpec((LANES, LANES), lambda bb, i: (0, 0))],
        out_specs=[o[0] for o in outs],
        out_shape=[o[1] for o in outs],
        compiler_params=_params("parallel", "parallel"),
        name="nsa_query_gate_proj",
    )(h, norm_g3, mod5, mod5, w, gain, cos, sin, bd)


def _nsa_body(qn_ref, qr_ref, gate_ref, kc_ref, vc_ref, ks_ref, vs_ref, kw_ref, vw_ref, wimpt_ref, gsel_ref,
              o_ref, qa_scr, m_scr, acc_scr, mw_scr, accw_scr, *, tq, shifted):
    i = pl.program_id(2)
    start = i * tq
    rows = GROUP * tq
    ncp = kc_ref.shape[0]
    low = lax.broadcasted_iota(jnp.int32, (tq, LANES), 1) < HEAD_DIM
    low_r = lax.broadcasted_iota(jnp.int32, (rows, LANES), 1) < HEAD_DIM

    def accumulate(s, v, mask, first, acc_ref=acc_scr, m_ref=m_scr):
        if not shifted:
            p = jnp.exp2(s)
            if mask is not None:
                p = jnp.where(mask, p, 0.0)
            pv = _dot(p.astype(BF16), v)
            acc_ref[...] = pv if first else acc_ref[...] + pv
            return
        if mask is not None:
            s = jnp.where(mask, s, NEG)
        if first:
            m_new = jnp.max(s, axis=-1, keepdims=True)
            acc_ref[...] = _dot(jnp.exp2(s - m_new).astype(BF16), v)
        else:
            m_old = m_ref[...]
            m_new = jnp.maximum(m_old, jnp.max(s, axis=-1, keepdims=True))
            acc_ref[...] = jnp.exp2(m_old - m_new) * acc_ref[...] + _dot(jnp.exp2(s - m_new).astype(BF16), v)
        m_ref[...] = m_new

    def stack(q):
        parts = []
        for g in range(GROUP):
            blk = q[:, (g // 2) * LANES:(g // 2 + 1) * LANES]
            parts.append(jnp.where(low if g % 2 == 0 else ~low, blk, jnp.zeros_like(blk)))
        return parts

    qn = jnp.concatenate(stack(qn_ref[0]), axis=0)
    sc = _dot_t(qn, kc_ref[...])
    c_end = lax.broadcasted_iota(jnp.int32, (tq, ncp), 1) * CMP_STRIDE + (CMP_LEN - 1)
    valid_t = c_end <= start + lax.broadcasted_iota(jnp.int32, (tq, ncp), 0)
    if shifted:
        valid = jnp.concatenate([valid_t] * GROUP, axis=0)
        sc = jnp.where(valid, sc, NEG)
        sc = sc - jnp.max(sc, axis=-1, keepdims=True)
        p = jnp.where(valid, jnp.exp2(sc), 0.0)
    else:
        p = jnp.exp2(sc + jnp.concatenate([jnp.where(valid_t, 0.0, NEG)] * GROUP, axis=0))
    oc = _dot(p.astype(BF16), vc_ref[...])
    inv = 1.0 / jnp.maximum(jnp.where(low_r, pltpu.roll(oc, HEAD_DIM, 1), oc), 1e-30)
    o_cmp = oc * inv
    psum_blocks = []
    for cb in range(ncp // LANES):
        cs = slice(cb * LANES, (cb + 1) * LANES)
        tot = p[0:tq, cs] * inv[0:tq]
        for g in range(1, GROUP):
            tot = tot + p[g * tq:(g + 1) * tq, cs] * inv[g * tq:(g + 1) * tq]
        psum_blocks.append(tot)
    psum = jnp.concatenate(psum_blocks, axis=1)

    qr_parts = stack(qr_ref[0])
    for g in range(GROUP):
        qa_scr[g * tq:(g + 1) * tq, 0:LANES] = qr_parts[g]
    qr = jnp.concatenate(qr_parts, axis=0)

    tri = (lax.broadcasted_iota(jnp.int32, (rows, tq), 1)
           <= (lax.broadcasted_iota(jnp.int32, (rows, tq), 0) & (tq - 1)))

    def key_block(kb):
        return pl.ds(pl.multiple_of(kb * tq, tq), tq)

    diag = key_block(i)
    accumulate(_dot_t(qr, kw_ref[diag, :]), vw_ref[diag, :], tri, True, accw_scr, mw_scr)

    p_hi = psum.astype(BF16)
    p_lo = (psum - p_hi.astype(F32)).astype(BF16)
    imp_t = _dot_t(wimpt_ref[...], p_hi) + _dot_t(wimpt_ref[...], p_lo)

    blk = key_block(jnp.maximum(i - 1, 0))
    accumulate(_dot_t(qr, kw_ref[blk, :]), vw_ref[blk, :], jnp.broadcast_to(i >= 1, tri.shape), False,
               accw_scr, mw_scr)
    blk = key_block(jnp.maximum(i - 2, 0))
    accumulate(_dot_t(qr, kw_ref[blk, :]), vw_ref[blk, :], (~tri) & (i >= 2), False, accw_scr, mw_scr)

    blk_id = lax.broadcasted_iota(jnp.int32, (LANES, tq), 0)
    jt = (start + lax.broadcasted_iota(jnp.int32, (LANES, tq), 1)) >> SEL_SHIFT
    forced = (blk_id == 0) | (blk_id == jt) | (blk_id == jt - 1)
    work = jnp.where(blk_id > jt, -1.0, jnp.where(forced, -2.0, imp_t))
    sel = forced
    for _ in range(N_SELECT - 3):
        mx = jnp.max(work, axis=0, keepdims=True)
        idx = jnp.min(jnp.where(work == mx, blk_id, LANES), axis=0, keepdims=True)
        pick = blk_id == idx
        sel = sel | pick
        work = jnp.where(pick, -2.0, work)
    sel_bias = jnp.transpose(jnp.where(sel, 0.0, NEG)).astype(BF16)

    for g in range(GROUP):
        qa_scr[g * tq:(g + 1) * tq, LANES:2 * LANES] = sel_bias

    accumulate(_dot_t(qa_scr[...], ks_ref[diag, :]), vs_ref[diag, :], tri, True)

    def slc_pair(kp, carry):
        blk = pl.ds(pl.multiple_of(kp * 2 * tq, 2 * tq), 2 * tq)
        accumulate(_dot_t(qa_scr[...], ks_ref[blk, :]), vs_ref[blk, :], None, False)
        return carry

    lax.fori_loop(0, i // 2, slc_pair, 0)

    @pl.when(i % 2 == 1)
    def _():
        blk = key_block(i - 1)
        accumulate(_dot_t(qa_scr[...], ks_ref[blk, :]), vs_ref[blk, :], None, False)

    gates = gate_ref[0]
    g_hi = gates.astype(BF16)
    g_lo = (gates - g_hi.astype(F32)).astype(BF16)
    gate_wide = _dot(g_hi, gsel_ref[...]) + _dot(g_lo, gsel_ref[...])
    for pr in range(GROUP // 2):
        ev = slice(2 * pr * tq, (2 * pr + 1) * tq)
        od = slice((2 * pr + 1) * tq, (2 * pr + 2) * tq)

        def pair(acc):
            a_e, a_o = acc[ev], acc[od]
            r_e = pltpu.roll(a_e, HEAD_DIM, 1)
            r_o = pltpu.roll(a_o, HEAD_DIM, 1)
            return jnp.where(low, a_e, r_o) / jnp.where(low, r_e, a_o)

        def gate(br):
            c = (br * (GROUP // 2) + pr) * LANES
            return gate_wide[:, c:c + LANES]

        o_c = jnp.where(low, o_cmp[ev], pltpu.roll(o_cmp[od], HEAD_DIM, 1))
        out = gate(0) * o_c + gate(1) * pair(acc_scr) + gate(2) * pair(accw_scr)
        o_ref[0, :, pr * LANES:(pr + 1) * LANES] = out.astype(BF16)


def _nsa_kernel(small_ref, *refs, tq):
    @pl.when(small_ref[0] != 0)
    def _():
        _nsa_body(*refs, tq=tq, shifted=False)

    @pl.when(small_ref[0] == 0)
    def _():
        _nsa_body(*refs, tq=tq, shifted=True)


def _nsa_attention(small, qn, qr, gates, cmp4, ks, vs, kw, vw, wimpt):
    b, s, hdim = qn.shape
    tq = min(256, s)
    assert s // SEL_BLOCK <= LANES and s % tq == 0 and SLIDE_WINDOW == 2 * tq
    ncp = cmp4.shape[2]
    gw = GROUP * HEAD_DIM
    rows = GROUP * tq
    qspec = pl.BlockSpec((1, tq, gw), lambda bb, kv, i: (bb, i, kv))
    full = lambda width: pl.BlockSpec((None, s, width), lambda bb, kv, i: (bb, 0, kv))
    col = np.arange(3 * (GROUP // 2) * LANES)[None, :]
    src = (col // LANES // (GROUP // 2)) * GROUP + 2 * ((col // LANES) % (GROUP // 2)) + (col % LANES) // HEAD_DIM
    gsel = (np.arange(LANES)[:, None] == src).astype(BF16)
    return pl.pallas_call(
        functools.partial(_nsa_kernel, tq=tq),
        grid=(b, N_KV_HEADS, s // tq),
        in_specs=[pl.BlockSpec(memory_space=pltpu.SMEM), qspec, qspec,
                  pl.BlockSpec((1, tq, LANES), lambda bb, kv, i: (bb, i, kv)),
                  pl.BlockSpec((None, None, ncp, LANES), lambda bb, kv, i: (kv, bb, 0, 0)),
                  pl.BlockSpec((None, None, ncp, LANES), lambda bb, kv, i: (N_KV_HEADS + kv, bb, 0, 0)),
                  full(2 * LANES), full(LANES), full(LANES), full(LANES),
                  pl.BlockSpec((LANES, ncp), lambda bb, kv, i: (0, 0)),
                  pl.BlockSpec(gsel.shape, lambda bb, kv, i: (0, 0))],
        out_specs=qspec,
        out_shape=jax.ShapeDtypeStruct((b, s, hdim), BF16),
        scratch_shapes=[pltpu.VMEM((rows, 2 * LANES), BF16),
                        pltpu.VMEM((rows, 1), F32), pltpu.VMEM((rows, LANES), F32),
                        pltpu.VMEM((rows, 1), F32), pltpu.VMEM((rows, LANES), F32)],
        compiler_params=_params("parallel", "parallel", "arbitrary"),
        name="nsa_attention",
    )(small, qn, qr, gates, cmp4, cmp4, ks, vs, kw, vw, wimpt, gsel)


def _rope_tables(s, tm=None, rate=1):
    half = HEAD_DIM // 2
    row = lax.broadcasted_iota(jnp.int32, (s, LANES), 0)
    lane = lax.broadcasted_iota(jnp.int32, (s, LANES), 1)
    if rate > 1:
        seg = tm // rate
        local = row % tm
        row = row - local + (local % seg) * rate + local // seg
    inv = jnp.tile(ROPE_THETA ** (-jnp.arange(half, dtype=F32) / half), LANES // half)
    ang = row.astype(F32) * inv[None, :]
    cos, sin = jnp.cos(ang), jnp.sin(ang)
    return cos, jnp.where(lane % HEAD_DIM < half, -sin, sin), sin


def _split_half(t, axis):
    hpg = MXU_WIDTH // HEAD_DIM
    half = HEAD_DIM // 2
    shape = t.shape
    t = t.reshape(shape[:axis] + (shape[axis] // MXU_WIDTH, hpg, 2, half) + shape[axis + 1:])
    return t.swapaxes(axis + 1, axis + 2).reshape(shape)


def _importance_weights(ncp):
    r = SEL_BLOCK // CMP_STRIDE
    nper = CMP_LEN // CMP_STRIDE
    o = np.arange(ncp)[:, None] - r * np.arange(LANES)[None, :]
    w = np.zeros((ncp, LANES), np.float32)
    for off in range(-(nper - 1), r):
        cnt = sum(1 for m in range(r) for n in range(nper) if m - n == off)
        w += np.where(o == off, float(cnt), 0.0)
    return w.astype(BF16)


def kernel(x, c, norm_g, w_ada, b_ada, ffn_w_in, ffn_w_out, a_w_qkv, a_q_gain, a_k_gain, a_w_o, kv_norm_g,
           w_ada_kv, b_ada_kv, w_kv, kv_k_gain, cmp_pos, phi_w1, phi_w2, b_w_qg, b_q_gain, b_w_o):
    b, s, d = x.shape
    depth = norm_g.shape[0]
    n_a = depth // 2
    hdim = N_HEADS * HEAD_DIM
    assert depth == 2 and b <= 8

    c8 = jnp.zeros((8, d), F32).at[:b].set(c)
    mod = _ada(c8, w_ada, b_ada.reshape(depth, 1, 9 * d))
    mod5 = mod[:, :b].reshape(depth, b, 9, 1, d)
    kvmod = _ada(c8, w_ada_kv[None], b_ada_kv.reshape(1, 1, 2 * d))
    kvmod5 = kvmod[:, :b].reshape(1, b, 2, 1, d)

    norm_g3 = norm_g.reshape(depth * 3, 1, d)
    w_in = ffn_w_in.astype(BF16)
    w_out = ffn_w_out.astype(BF16)
    cos, sin, _ = _rope_tables(s)
    blk = np.arange(LANES) // HEAD_DIM
    bd = (blk[:, None] == blk[None, :]).astype(BF16)
    blk = (np.arange(MXU_WIDTH) % LANES) // (HEAD_DIM // 2)
    bd_split = (blk[:, None] == blk[None, :]).astype(BF16)
    q_scale = HEAD_DIM ** -0.5 * LOG2E

    h = x
    for l in range(depth):
        h = _ffn(h, norm_g3, mod5, w_in, w_out, l, 0, 0)
        if l < n_a:
            gain = jnp.stack([_split_half(jnp.tile(a_q_gain[l] * q_scale, (1, N_HEADS)), 1),
                              _split_half(jnp.tile(a_k_gain[l], (1, N_HEADS)), 1),
                              jnp.ones((N_DIL, hdim), F32)], axis=1).reshape(3 * N_DIL, 1, hdim)
            cols = np.arange(N_DIL * 3 * hdim).reshape(N_DIL, 3, hdim)
            cols = np.concatenate([_split_half(cols[:, :2], 2), cols[:, 2:]], axis=1).reshape(-1)
            w_qkv = a_w_qkv[l][:, cols].astype(BF16)
            bound = HEAD_DIM * q_scale * jnp.max(jnp.abs(a_q_gain[l])) * jnp.max(jnp.abs(a_k_gain[l]))
            small = (bound <= SCORE_LIMIT).astype(jnp.int32).reshape(1)
            parts = [_dilated_attention(small, _qkv_proj(h, norm_g3, mod5, w_qkv, gain, bd_split, l, g), g)
                     for g in range(N_DIL)]
            h = _merge_out([p[0] for p in parts], [p[1] for p in parts], h, mod5, a_w_o[l].astype(BF16), l)
        else:
            j = l - n_a
            w_q = b_w_qg[j][:, :hdim]
            w_g = b_w_qg[j][:, hdim:].reshape(d, 3, N_KV_HEADS, GROUP).transpose(0, 2, 1, 3)
            w_g = jnp.pad(w_g.reshape(d, N_KV_HEADS, 3 * GROUP), ((0, 0), (0, 0), (0, LANES - 3 * GROUP)))
            w_qg = jnp.concatenate([w_q, w_g.reshape(d, N_KV_HEADS * LANES)], axis=1).astype(BF16)
            gain = jnp.tile(b_q_gain[j] * q_scale, N_HEADS)[None, :]
            qn, qr, gates = _qg_proj(h, norm_g3, mod5, w_qg, gain, cos, sin, bd, l)
            bound = HEAD_DIM * q_scale * jnp.max(jnp.abs(b_q_gain[j])) * jnp.max(jnp.abs(kv_k_gain))
            small = (bound <= SCORE_LIMIT).astype(jnp.int32).reshape(1)
            o = _nsa_attention(small, qn, qr, gates, *shared)
            h = _out_proj(o, h, mod5, b_w_o[j].astype(BF16), l)
        h = _ffn(h, norm_g3, mod5, w_in, w_out, l, 2, 1)
        if l == n_a - 1:
            col = lambda br, kv, kvh: br * 4 * HEAD_DIM + kv * 2 * HEAD_DIM + kvh * HEAD_DIM + jnp.arange(HEAD_DIM)
            dup = lambda br, kv, kvh: jnp.concatenate([col(br, kv, kvh)] * 2)
            cols = jnp.concatenate(
                [dup(br, 0, kvh) for br in (1, 2) for kvh in range(N_KV_HEADS)]
                + [dup(br, 1, kvh) for br in (1, 2) for kvh in range(N_KV_HEADS)]
                + [col(0, kv, kvh) for kv in range(2) for kvh in range(N_KV_HEADS)])
            w_kv_ext = w_kv[:, cols].astype(BF16)
            kgain = jnp.concatenate([jnp.tile(kv_k_gain[1], (2, 2)), jnp.tile(kv_k_gain[2], (2, 2)),
                                     jnp.ones((4, LANES), F32)], axis=0)
            ks, kw, vs, vw, kvc = _kv_proj(h, kv_norm_g[None, :], kvmod5, w_kv_ext, kgain, cos, sin, bd)
            nsb = s // CMP_STRIDE
            t4 = kvc.reshape(b, nsb, CMP_STRIDE, 4, HEAD_DIM).transpose(3, 0, 1, 2, 4)
            t4 = t4.reshape(4, b, nsb, CMP_STRIDE * HEAD_DIM)
            cmp4 = _compress(t4, phi_w1, phi_w2, cmp_pos.reshape(2, 1, CMP_LEN * HEAD_DIM), kv_k_gain[0][None, :])
            shared = (cmp4, ks, vs, kw, vw, _importance_weights(nsb).T)
    return h
```

```python
import functools

import jax
import jax.numpy as jnp
import numpy as np
from jax import lax
from jax.experimental import pallas as pl
from jax.experimental.pallas import tpu as pltpu

HEAD_DIM = 64
N_HEADS = 16
DIL_WINDOWS = (128, 512, 2048)
DIL_RATES = (1, 4, 16)
N_DIL = 3
N_KV_HEADS = 2
GROUP = N_HEADS // N_KV_HEADS
CMP_STRIDE = 16
CMP_LEN = 32
SEL_BLOCK = 64
SEL_SHIFT = SEL_BLOCK.bit_length() - 1
N_SELECT = 16
SLIDE_WINDOW = 512
ROPE_THETA = 10000.0
EPS = 1e-6

LANES = 128
MXU_WIDTH = 256
Q_ROWS = 128
NEG = -1e30
LOG2E = 1.4426950408889634
SCORE_LIMIT = 60.0
VMEM_LIMIT = 56 * 1024 * 1024

BF16 = jnp.bfloat16
F32 = jnp.float32


def _params(*sem):
    return pltpu.CompilerParams(dimension_semantics=sem, vmem_limit_bytes=VMEM_LIMIT)


def _dot(a, b):
    return jnp.dot(a, b, preferred_element_type=F32)


def _dot_t(a, b):
    return lax.dot_general(a, b, (((1,), (1,)), ((), ())), preferred_element_type=F32)


def _split_dot(a, b):
    a_hi = a.astype(BF16)
    a_lo = (a - a_hi.astype(F32)).astype(BF16)
    b_hi = b.astype(BF16)
    b_lo = (b - b_hi.astype(F32)).astype(BF16)
    return _dot(a_hi, b_hi) + (_dot(a_hi, b_lo) + _dot(a_lo, b_hi))


def _sigmoid(x):
    return 1.0 / (1.0 + jnp.exp(-x))


def _normmod(x, g, shift, scale):
    return x * lax.rsqrt(jnp.mean(x * x, axis=-1, keepdims=True) + EPS) * (g * (1.0 + scale)) + shift


def _head_rms(y, bd):
    ss = _dot((y * y).astype(BF16), bd)
    return y * lax.rsqrt(ss * (1.0 / HEAD_DIM) + EPS)


def _rope(z, cos, sin_signed):
    lane = lax.broadcasted_iota(jnp.int32, z.shape, 1)
    first_half = (lane & (HEAD_DIM - 1)) < HEAD_DIM // 2
    swapped = jnp.where(first_half, pltpu.roll(z, z.shape[1] - HEAD_DIM // 2, 1),
                        pltpu.roll(z, HEAD_DIM // 2, 1))
    return z * cos + swapped * sin_signed


def _ada_kernel(c_ref, w_ref, b_ref, o_ref):
    c = c_ref[...]
    a = (c * _sigmoid(c)).astype(BF16)
    o_ref[...] = _dot(a, w_ref[...].astype(BF16)) + b_ref[...]


def _ada(c8, w, b):
    nl, d, n = w.shape
    tn = 1024
    return pl.pallas_call(
        _ada_kernel,
        grid=(nl, n // tn),
        in_specs=[pl.BlockSpec((8, d), lambda l, j: (0, 0)),
                  pl.BlockSpec((None, d, tn), lambda l, j: (l, 0, j)),
                  pl.BlockSpec((None, 1, tn), lambda l, j: (l, 0, j))],
        out_specs=pl.BlockSpec((None, 8, tn), lambda l, j: (l, 0, j)),
        out_shape=jax.ShapeDtypeStruct((nl, 8, n), F32),
        compiler_params=_params("parallel", "parallel"),
        name="ada_table",
    )(c8, w, b)


def _mod_block(d, l, k):
    return pl.BlockSpec((None, None, None, 1, d), lambda b, *_: (l, b, k, 0, 0))


def _ffn_kernel(h_ref, g_ref, sh_ref, sc_ref, gt_ref, wg_ref, wu_ref, wo_ref, o_ref, u_scr, acc_scr):
    u_scr[...] = _normmod(h_ref[0], g_ref[...], sh_ref[...], sc_ref[...]).astype(BF16)

    u = u_scr[...]
    tf = wg_ref.shape[1]
    bounds = [(lo, min(lo + MXU_WIDTH, tf)) for lo in range(0, tf, MXU_WIDTH)]

    def gate_up(c):
        lo, hi = bounds[c]
        return _dot(u, wg_ref[:, lo:hi]), _dot(u, wu_ref[:, lo:hi])

    nxt = gate_up(0)
    for c, (lo, hi) in enumerate(bounds):
        gate, up = nxt
        if c + 1 < len(bounds):
            nxt = gate_up(c + 1)
        act = (gate * _sigmoid(gate) * up).astype(BF16)
        down = _dot(act, wo_ref[lo:hi, :])
        acc_scr[...] = down if c == 0 else acc_scr[...] + down

    o_ref[0] = h_ref[0] + (0.5 * (1.0 + gt_ref[...])) * acc_scr[...]


def _ffn(h, norm_g3, mod5, w_in, w_out, l, sub, which):
    b, s, d = h.shape
    f = w_out.shape[2]
    tm = min(1024, s)
    resident = pl.Buffered(1)
    return pl.pallas_call(
        _ffn_kernel,
        grid=(b, s // tm),
        in_specs=[pl.BlockSpec((1, tm, d), lambda bb, i: (bb, i, 0)),
                  pl.BlockSpec((None, 1, d), lambda bb, i: (l * 3 + sub, 0, 0)),
                  _mod_block(d, l, sub * 3 + 0),
                  _mod_block(d, l, sub * 3 + 1),
                  _mod_block(d, l, sub * 3 + 2),
                  pl.BlockSpec((None, None, d, f), lambda bb, i: (l, which, 0, 0), pipeline_mode=resident),
                  pl.BlockSpec((None, None, d, f), lambda bb, i: (l, which, 0, 1), pipeline_mode=resident),
                  pl.BlockSpec((None, None, f, d), lambda bb, i: (l, which, 0, 0), pipeline_mode=resident)],
        out_specs=pl.BlockSpec((1, tm, d), lambda bb, i: (bb, i, 0)),
        out_shape=jax.ShapeDtypeStruct((b, s, d), F32),
        scratch_shapes=[pltpu.VMEM((tm, d), BF16), pltpu.VMEM((tm, d), F32)],
        compiler_params=_params("parallel", "parallel"),
        name="swiglu_half_step",
    )(h, norm_g3, mod5, mod5, mod5, w_in, w_in, w_out)


def _qkv_kernel(h_ref, g_ref, sh_ref, sc_ref, w_ref, gain_ref, cos_ref, sin_ref, bd_ref, o_ref, u_scr, uf_scr,
                *, rate):
    n = pl.program_id(2)
    tm = u_scr.shape[0]
    seg = tm // rate

    @pl.when(n == 0)
    def _():
        u = _normmod(h_ref[0], g_ref[...], sh_ref[...], sc_ref[...])
        if rate == 1:
            u_scr[...] = u.astype(BF16)
        else:
            for cb in range(u.shape[1] // LANES):
                sl = slice(cb * LANES, (cb + 1) * LANES)
                uf_scr[cb] = u[:, sl]
                for res in range(rate):
                    u_scr[res * seg:(res + 1) * seg, sl] = uf_scr[cb, pl.ds(res, seg, stride=rate), :].astype(BF16)

    n_grp = o_ref.shape[1]

    def store(grp, z):
        for res in range(rate):
            o_ref[res, grp] = z[res * seg:(res + 1) * seg]

    @pl.when(n == 2)
    def _():
        for grp in range(n_grp):
            store(grp, _dot(u_scr[...], w_ref[:, grp * MXU_WIDTH:(grp + 1) * MXU_WIDTH]).astype(BF16))

    @pl.when(n != 2)
    def _():
        cos = cos_ref[...]
        sin = sin_ref[...]
        bd = bd_ref[...]
        def project(grp):
            return _dot(u_scr[...], w_ref[:, grp * MXU_WIDTH:(grp + 1) * MXU_WIDTH])

        y = project(0)
        for grp in range(n_grp):
            y_next = project(grp + 1) if grp + 1 < n_grp else None
            z = _head_rms(y, bd) * gain_ref[:, grp * MXU_WIDTH:(grp + 1) * MXU_WIDTH]
            x1, x2 = z[:, :LANES], z[:, LANES:]
            store(grp, jnp.concatenate([x1 * cos - x2 * sin, x1 * sin + x2 * cos], axis=1).astype(BF16))
            y = y_next


def _qkv_proj(h, norm_g3, mod5, w, gain, bd, l, g):
    b, s, d = h.shape
    rate = DIL_RATES[g]
    tm = min(1024, s)
    seg = tm // rate
    tn = N_HEADS * HEAD_DIM
    n_grp = tn // MXU_WIDTH
    cos, _, sin = _rope_tables(s, tm, rate)

    return pl.pallas_call(
        functools.partial(_qkv_kernel, rate=rate),
        grid=(b, s // tm, 3),
        in_specs=[pl.BlockSpec((1, tm, d), lambda bb, i, j: (bb, i, 0)),
                  pl.BlockSpec((None, 1, d), lambda bb, i, j: (l * 3 + 1, 0, 0)),
                  _mod_block(d, l, 3),
                  _mod_block(d, l, 4),
                  pl.BlockSpec((d, tn), lambda bb, i, j: (0, 3 * g + j)),
                  pl.BlockSpec((None, 1, tn), lambda bb, i, j: (3 * g + j, 0, 0)),
                  pl.BlockSpec((tm, LANES), lambda bb, i, j: (i, 0)),
                  pl.BlockSpec((tm, LANES), lambda bb, i, j: (i, 0)),
                  pl.BlockSpec(bd.shape, lambda bb, i, j: (0, 0))],
        out_specs=pl.BlockSpec((None, rate, n_grp, seg, MXU_WIDTH), lambda bb, i, j: (bb, 0, j, i, 0)),
        out_shape=jax.ShapeDtypeStruct((b, rate, 3 * n_grp, s // rate, MXU_WIDTH), BF16),
        scratch_shapes=[pltpu.VMEM((tm, d), BF16), pltpu.VMEM((d // LANES, tm, LANES), F32)],
        compiler_params=_params("parallel", "parallel", "arbitrary"),
        name=f"qkv_proj_rate{rate}",
    )(h, norm_g3, mod5, mod5, w, gain, cos, sin, bd)


def _dil_body(bias_ref, q_ref, kp_ref, kc_ref, vp_ref, vc_ref, o_ref, lse_ref, *, tq, shifted):
    i = pl.program_id(2)
    grp = pl.program_id(3)
    hpg = MXU_WIDTH // HEAD_DIM
    qlane = lax.broadcasted_iota(jnp.int32, (Q_ROWS, MXU_WIDTH), 1)
    q_head = (qlane & (LANES - 1)) // (HEAD_DIM // 2)
    v_head = qlane // HEAD_DIM
    seg = lax.broadcasted_iota(jnp.int32, (Q_ROWS, LANES), 1) // (LANES // N_HEADS)

    def scores(j):
        q = q_ref[j * Q_ROWS:(j + 1) * Q_ROWS, :]
        zero = jnp.zeros_like(q)
        q2 = jnp.concatenate([jnp.where(q_head == t, q, zero) for t in range(hpg)], axis=0)
        if j == 0:
            k = jnp.concatenate([kp_ref[...], kc_ref[0:Q_ROWS, :]], axis=0)
        else:
            k = kc_ref[(j - 1) * Q_ROWS:(j + 1) * Q_ROWS, :]
        return _dot_t(q2, k)

    for j in range(tq // Q_ROWS):
        rs = slice(j * Q_ROWS, (j + 1) * Q_ROWS)
        s = scores(j) + bias_ref[...]
        if j == 0:
            v = jnp.concatenate([vp_ref[...], vc_ref[0:Q_ROWS, :]], axis=0)
            prev_key = lax.broadcasted_iota(jnp.int32, s.shape, 1) < Q_ROWS
            s = jnp.where(prev_key & (i == 0), NEG, s)
        else:
            v = vc_ref[(j - 1) * Q_ROWS:(j + 1) * Q_ROWS, :]
        if shifted:
            m = jnp.max(s, axis=-1, keepdims=True)
            p = jnp.exp2(s - m)
        else:
            p = jnp.exp2(s)
        den = jnp.sum(p, axis=-1, keepdims=True)
        o2 = _dot(p.astype(BF16), v) / den
        lse2 = jnp.log(den) * LOG2E
        if shifted:
            lse2 = lse2 + m
        out = o2[0:Q_ROWS]
        new = jnp.where(grp == 0, 0.0, lse_ref[0, rs, :])
        for t in range(hpg):
            ts = slice(t * Q_ROWS, (t + 1) * Q_ROWS)
            if t:
                out = jnp.where(v_head == t, o2[ts], out)
            new = jnp.where(seg == hpg * grp + t, lse2[ts], new)
        o_ref[0, rs, :] = out.astype(BF16)
        lse_ref[0, rs, :] = new


def _dil_kernel(small_ref, *refs, tq):
    @pl.when(small_ref[0] != 0)
    def _():
        _dil_body(*refs, tq=tq, shifted=False)

    @pl.when(small_ref[0] == 0)
    def _():
        _dil_body(*refs, tq=tq, shifted=True)


def _dilated_attention(small, qkv_r, g):
    b, rate, n3, l, width = qkv_r.shape
    s = l * rate
    n_back = DIL_WINDOWS[g] // rate
    assert n_back <= Q_ROWS and rate == DIL_RATES[g] and width == MXU_WIDTH
    tq = min(512, l)
    hdim = N_HEADS * HEAD_DIM
    n_grp = n3 // 3
    sub = tq // Q_ROWS

    def cur(off):
        return pl.BlockSpec((None, None, None, tq, width), lambda bb, r, i, gp: (bb, r, off + gp, i, 0))

    def prev(off):
        return pl.BlockSpec((None, None, None, Q_ROWS, width),
                            lambda bb, r, i, gp: (bb, r, off + gp, jnp.maximum(i * sub - 1, 0), 0))

    rows = (width // HEAD_DIM) * Q_ROWS
    dist = Q_ROWS + (np.arange(rows)[:, None] % Q_ROWS) - np.arange(2 * Q_ROWS)[None, :]
    bias = np.where((dist >= 0) & (dist <= n_back), 0.0, NEG).astype(np.float32)

    o, lse = pl.pallas_call(
        functools.partial(_dil_kernel, tq=tq),
        grid=(b, rate, l // tq, n_grp),
        in_specs=[pl.BlockSpec(memory_space=pltpu.SMEM),
                  pl.BlockSpec(bias.shape, lambda bb, r, i, gp: (0, 0)),
                  cur(0), prev(n_grp), cur(n_grp), prev(2 * n_grp), cur(2 * n_grp)],
        out_specs=[pl.BlockSpec((1, None, tq, width), lambda bb, r, i, gp: (bb, r, i, gp)),
                   pl.BlockSpec((1, None, tq, LANES), lambda bb, r, i, gp: (bb, r, i, 0))],
        out_shape=[jax.ShapeDtypeStruct((b, rate, l, hdim), BF16),
                   jax.ShapeDtypeStruct((b, rate, l, LANES), F32)],
        compiler_params=_params("parallel", "parallel", "parallel", "arbitrary"),
        name=f"dilated_attention_rate{rate}",
    )(small, bias, qkv_r, qkv_r, qkv_r, qkv_r, qkv_r)
    return o, lse


def _merge_out_kernel(o0_ref, o1_ref, o2_ref, l0_ref, l1_ref, l2_ref, p1_ref, p2_ref, e_ref, h_ref, gt_ref, w_ref,
                      out_ref):
    def positions(x_ref, p_ref):
        x = jnp.concatenate([x_ref[res] for res in range(x_ref.shape[0])], axis=0)
        if p_ref is None:
            return x.astype(F32)
        if x.dtype == BF16:
            return _dot(p_ref[...], x)
        hi = x.astype(BF16)
        return _dot(p_ref[...], hi) + _dot(p_ref[...], (x - hi.astype(F32)).astype(BF16))

    l0, l1, l2 = positions(l0_ref, None), positions(l1_ref, p1_ref), positions(l2_ref, p2_ref)
    m = jnp.maximum(jnp.maximum(l0, l1), l2)
    e0, e1, e2 = jnp.exp2(l0 - m), jnp.exp2(l1 - m), jnp.exp2(l2 - m)
    inv = 1.0 / (e0 + e1 + e2)
    spread = e_ref[...]
    o = (_dot((e0 * inv).astype(BF16), spread) * positions(o0_ref, None)
         + _dot((e1 * inv).astype(BF16), spread) * positions(o1_ref, p1_ref)
         + _dot((e2 * inv).astype(BF16), spread) * positions(o2_ref, p2_ref))
    out_ref[0] = h_ref[0] + (1.0 + gt_ref[...]) * _dot(o.astype(BF16), w_ref[...])


def _merge_out(os_, lses, h, mod5, w_o, l):
    b, s, d = h.shape
    hdim = w_o.shape[0]
    tm = min(512, s)
    row = lambda width: pl.BlockSpec((1, tm, width), lambda bb, i: (bb, i, 0))
    rep = LANES // N_HEADS
    lane = np.arange(LANES)[:, None]
    col = np.arange(hdim)[None, :]
    spread = ((lane % rep == 0) & (lane // rep == col // HEAD_DIM)).astype(BF16)

    def unpermute(rate):
        pos = np.arange(tm)
        return (((pos % rate) * (tm // rate) + pos // rate)[:, None] == np.arange(tm)[None, :]).astype(BF16)

    def residue_spec(t):
        rate, width = t.shape[1], t.shape[3]
        return pl.BlockSpec((None, rate, tm // rate, width), lambda bb, i: (bb, 0, i, 0))

    const = lambda shape: pl.BlockSpec(shape, lambda bb, i: (0, 0))
    return pl.pallas_call(
        _merge_out_kernel,
        grid=(b, s // tm),
        in_specs=[residue_spec(t) for t in (*os_, *lses)]
        + [const((tm, tm)), const((tm, tm)), const((LANES, hdim)), row(d), _mod_block(d, l, 5), const((hdim, d))],
        out_specs=row(d),
        out_shape=jax.ShapeDtypeStruct((b, s, d), F32),
        compiler_params=_params("parallel", "parallel"),
        name="merge_out_proj",
    )(*os_, *lses, unpermute(os_[1].shape[1]), unpermute(os_[2].shape[1]), spread, h, mod5, w_o)


def _out_kernel(o_ref, h_ref, gt_ref, w_ref, out_ref):
    out_ref[0] = h_ref[0] + (1.0 + gt_ref[...]) * _dot(o_ref[0], w_ref[...])


def _out_proj(o, h, mod5, w_o, l):
    b, s, d = h.shape
    hdim = w_o.shape[0]
    tm = min(1024, s)
    row = lambda width: pl.BlockSpec((1, tm, width), lambda bb, i: (bb, i, 0))
    return pl.pallas_call(
        _out_kernel,
        grid=(b, s // tm),
        in_specs=[row(hdim), row(d), _mod_block(d, l, 5), pl.BlockSpec((hdim, d), lambda bb, i: (0, 0))],
        out_specs=row(d),
        out_shape=jax.ShapeDtypeStruct((b, s, d), F32),
        compiler_params=_params("parallel", "parallel"),
        name="out_proj",
    )(o, h, mod5, w_o)


def _kv_kernel(h_ref, g_ref, sh_ref, sc_ref, w_ref, gain_ref, cos_ref, sin_ref, bd_ref,
               ks_ref, kw_ref, vs_ref, vw_ref, kvc_ref, *, tm):
    i = pl.program_id(1)
    u = _normmod(h_ref[0], g_ref[...], sh_ref[...], sc_ref[...]).astype(BF16)
    y = _dot(u, w_ref[...])
    cos = cos_ref[...]
    sin = sin_ref[...]
    bd = bd_ref[...]
    lane = lax.broadcasted_iota(jnp.int32, (tm, LANES), 1)
    pos = i * tm + lax.broadcasted_iota(jnp.int32, (tm, LANES), 0)
    onehot = jnp.where((pos >> SEL_SHIFT) == lane, 1.0, 0.0).astype(BF16)
    for j in range(4):
        z = _head_rms(y[:, j * LANES:(j + 1) * LANES], bd) * gain_ref[j:j + 1, :]
        zr = _rope(z, cos, sin).astype(BF16)
        if j < 2:
            ks_ref[0, :, 2 * j * LANES:(2 * j + 1) * LANES] = zr
            ks_ref[0, :, (2 * j + 1) * LANES:(2 * j + 2) * LANES] = onehot
        else:
            kw_ref[0, :, (j - 2) * LANES:(j - 1) * LANES] = zr
    for j in range(4):
        v = y[:, (4 + j) * LANES:(5 + j) * LANES]
        va = jnp.where(lane < HEAD_DIM, v, 1.0).astype(BF16)
        ref = vs_ref if j < 2 else vw_ref
        ref[0, :, (j % 2) * LANES:(j % 2 + 1) * LANES] = va
    kvc_ref[0] = y[:, 8 * LANES:10 * LANES]


def _kv_proj(h, kv_norm_g, kvmod5, w, gain, cos, sin, bd):
    b, s, d = h.shape
    n = w.shape[1]
    tm = min(512, s)
    row = lambda width, dt: (pl.BlockSpec((1, tm, width), lambda bb, i: (bb, i, 0)),
                             jax.ShapeDtypeStruct((b, s, width), dt))
    outs = [row(4 * LANES, BF16), row(2 * LANES, BF16), row(2 * LANES, BF16), row(2 * LANES, BF16),
            row(2 * LANES, F32)]
    return pl.pallas_call(
        functools.partial(_kv_kernel, tm=tm),
        grid=(b, s // tm),
        in_specs=[pl.BlockSpec((1, tm, d), lambda bb, i: (bb, i, 0)),
                  pl.BlockSpec((1, d), lambda bb, i: (0, 0)),
                  _mod_block(d, 0, 0),
                  _mod_block(d, 0, 1),
                  pl.BlockSpec((d, n), lambda bb, i: (0, 0)),
                  pl.BlockSpec((8, LANES), lambda bb, i: (0, 0)),
                  pl.BlockSpec((tm, LANES), lambda bb, i: (i, 0)),
                  pl.BlockSpec((tm, LANES), lambda bb, i: (i, 0)),
                  pl.BlockSpec((LANES, LANES), lambda bb, i: (0, 0))],
        out_specs=[o[0] for o in outs],
        out_shape=[o[1] for o in outs],
        compiler_params=_params("parallel", "parallel"),
        name="shared_kv_proj",
    )(h, kv_norm_g, kvmod5, kvmod5, w, gain, cos, sin, bd)


def _cmp_kernel(t_ref, w1_ref, w2_ref, pos_ref, gain_ref, o_ref):
    is_key = pl.program_id(0) < N_KV_HEADS
    t = t_ref[...]
    w1 = w1_ref[...]
    half = w1.shape[0] // 2
    first = _split_dot(t, w1[:half])
    second = _split_dot(t, w1[half:])
    second = pltpu.roll(second, second.shape[0] - 1, 0)
    posb = _split_dot(jnp.broadcast_to(pos_ref[...], (8, w1.shape[0])), w1)[0:1]
    hid = first + second + posb
    out = _split_dot(hid * _sigmoid(hid), w2_ref[...])
    key = out * lax.rsqrt(jnp.mean(out * out, axis=-1, keepdims=True) + EPS) * gain_ref[...]
    dup = jnp.concatenate([key, key], axis=-1)
    aug = jnp.concatenate([out, jnp.ones_like(out)], axis=-1)
    o_ref[...] = jnp.where(is_key, dup, aug).astype(BF16)


def _compress(t4, phi_w1, phi_w2, cmp_pos, gain):
    _, b, nsb, width = t4.shape
    hid = phi_w1.shape[2]
    return pl.pallas_call(
        _cmp_kernel,
        grid=(4, b),
        in_specs=[pl.BlockSpec((None, None, nsb, width), lambda t, bb: (t, bb, 0, 0)),
                  pl.BlockSpec((None, CMP_LEN * HEAD_DIM, hid), lambda t, bb: (t // N_KV_HEADS, 0, 0)),
                  pl.BlockSpec((None, hid, HEAD_DIM), lambda t, bb: (t // N_KV_HEADS, 0, 0)),
                  pl.BlockSpec((None, 1, CMP_LEN * HEAD_DIM), lambda t, bb: (t // N_KV_HEADS, 0, 0)),
                  pl.BlockSpec((1, HEAD_DIM), lambda t, bb: (0, 0))],
        out_specs=pl.BlockSpec((None, None, nsb, LANES), lambda t, bb: (t, bb, 0, 0)),
        out_shape=jax.ShapeDtypeStruct((4, b, nsb, LANES), BF16),
        compiler_params=_params("parallel", "parallel"),
        name="compress_blocks",
    )(t4, phi_w1, phi_w2, cmp_pos, gain)


def _qg_kernel(h_ref, g_ref, sh_ref, sc_ref, w_ref, gain_ref, cos_ref, sin_ref, bd_ref,
               qn_ref, qr_ref, gate_ref):
    u = _normmod(h_ref[0], g_ref[...], sh_ref[...], sc_ref[...]).astype(BF16)
    y = _dot(u, w_ref[...])
    cos = cos_ref[...]
    sin = sin_ref[...]
    bd = bd_ref[...]
    hdim = qn_ref.shape[2]
    for cb in range(hdim // LANES):
        sl = slice(cb * LANES, (cb + 1) * LANES)
        z = _head_rms(y[:, sl], bd) * gain_ref[:, sl]
        qn_ref[0, :, sl] = z.astype(BF16)
        qr_ref[0, :, sl] = _rope(z, cos, sin).astype(BF16)
    gate_ref[0] = _sigmoid(y[:, hdim:])


def _qg_proj(h, norm_g3, mod5, w, gain, cos, sin, bd, l):
    b, s, d = h.shape
    n = w.shape[1]
    hdim = N_HEADS * HEAD_DIM
    tm = min(512, s)
    row = lambda width, dt: (pl.BlockSpec((1, tm, width), lambda bb, i: (bb, i, 0)),
                             jax.ShapeDtypeStruct((b, s, width), dt))
    outs = [row(hdim, BF16), row(hdim, BF16), row(n - hdim, F32)]
    return pl.pallas_call(
        _qg_kernel,
        grid=(b, s // tm),
        in_specs=[pl.BlockSpec((1, tm, d), lambda bb, i: (bb, i, 0)),
                  pl.BlockSpec((None, 1, d), lambda bb, i: (l * 3 + 1, 0, 0)),
                  _mod_block(d, l, 3),
                  _mod_block(d, l, 4),
                  pl.BlockSpec((d, n), lambda bb, i: (0, 0)),
                  pl.BlockSpec((1, hdim), lambda bb, i: (0, 0)),
                  pl.BlockSpec((tm, LANES), lambda bb, i: (i, 0)),
                  pl.BlockSpec((tm, LANES), lambda bb, i: (i, 0)),
                  pl.BlockSpec((LANES, LANES), lambda bb, i: (0, 0))],
        out_specs=[o[0] for o in outs],
        out_shape=[o[1] for o in outs],
        compiler_params=_params("parallel", "parallel"),
        name="nsa_query_gate_proj",
    )(h, norm_g3, mod5, mod5, w, gain, cos, sin, bd)


def _nsa_body(qn_ref, qr_ref, gate_ref, kc_ref, vc_ref, ks_ref, vs_ref, kw_ref, vw_ref, wimpt_ref, gsel_ref,
              o_ref, qa_scr, m_scr, acc_scr, mw_scr, accw_scr, *, tq, shifted):
    i = pl.program_id(2)
    start = i * tq
    rows = GROUP * tq
    ncp = kc_ref.shape[0]
    low = lax.broadcasted_iota(jnp.int32, (tq, LANES), 1) < HEAD_DIM
    low_r = lax.broadcasted_iota(jnp.int32, (rows, LANES), 1) < HEAD_DIM

    def accumulate(s, v, mask, first, acc_ref=acc_scr, m_ref=m_scr):
        if not shifted:
            p = jnp.exp2(s)
            if mask is not None:
                p = jnp.where(mask, p, 0.0)
            pv = _dot(p.astype(BF16), v)
            acc_ref[...] = pv if first else acc_ref[...] + pv
            return
        if mask is not None:
            s = jnp.where(mask, s, NEG)
        if first:
            m_new = jnp.max(s, axis=-1, keepdims=True)
            acc_ref[...] = _dot(jnp.exp2(s - m_new).astype(BF16), v)
        else:
            m_old = m_ref[...]
            m_new = jnp.maximum(m_old, jnp.max(s, axis=-1, keepdims=True))
            acc_ref[...] = jnp.exp2(m_old - m_new) * acc_ref[...] + _dot(jnp.exp2(s - m_new).astype(BF16), v)
        m_ref[...] = m_new

    def stack(q):
        parts = []
        for g in range(GROUP):
            blk = q[:, (g // 2) * LANES:(g // 2 + 1) * LANES]
            parts.append(jnp.where(low if g % 2 == 0 else ~low, blk, jnp.zeros_like(blk)))
        return parts

    qn = jnp.concatenate(stack(qn_ref[0]), axis=0)
    sc = _dot_t(qn, kc_ref[...])
    c_end = lax.broadcasted_iota(jnp.int32, (tq, ncp), 1) * CMP_STRIDE + (CMP_LEN - 1)
    valid_t = c_end <= start + lax.broadcasted_iota(jnp.int32, (tq, ncp), 0)
    if shifted:
        valid = jnp.concatenate([valid_t] * GROUP, axis=0)
        sc = jnp.where(valid, sc, NEG)
        sc = sc - jnp.max(sc, axis=-1, keepdims=True)
        p = jnp.where(valid, jnp.exp2(sc), 0.0)
    else:
        p = jnp.exp2(sc + jnp.concatenate([jnp.where(valid_t, 0.0, NEG)] * GROUP, axis=0))
    oc = _dot(p.astype(BF16), vc_ref[...])
    inv = 1.0 / jnp.maximum(jnp.where(low_r, pltpu.roll(oc, HEAD_DIM, 1), oc), 1e-30)
    o_cmp = oc * inv
    psum_blocks = []
    for cb in range(ncp // LANES):
        cs = slice(cb * LANES, (cb + 1) * LANES)
        tot = p[0:tq, cs] * inv[0:tq]
        for g in range(1, GROUP):
            tot = tot + p[g * tq:(g + 1) * tq, cs] * inv[g * tq:(g + 1) * tq]
        psum_blocks.append(tot)
    psum = jnp.concatenate(psum_blocks, axis=1)

    qr_parts = stack(qr_ref[0])
    for g in range(GROUP):
        qa_scr[g * tq:(g + 1) * tq, 0:LANES] = qr_parts[g]
    qr = jnp.concatenate(qr_parts, axis=0)

    tri = (lax.broadcasted_iota(jnp.int32, (rows, tq), 1)
           <= (lax.broadcasted_iota(jnp.int32, (rows, tq), 0) & (tq - 1)))

    def key_block(kb):
        return pl.ds(pl.multiple_of(kb * tq, tq), tq)

    diag = key_block(i)
    accumulate(_dot_t(qr, kw_ref[diag, :]), vw_ref[diag, :], tri, True, accw_scr, mw_scr)

    p_hi = psum.astype(BF16)
    p_lo = (psum - p_hi.astype(F32)).astype(BF16)
    imp_t = _dot_t(wimpt_ref[...], p_hi) + _dot_t(wimpt_ref[...], p_lo)

    blk = key_block(jnp.maximum(i - 1, 0))
    accumulate(_dot_t(qr, kw_ref[blk, :]), vw_ref[blk, :], jnp.broadcast_to(i >= 1, tri.shape), False,
               accw_scr, mw_scr)
    blk = key_block(jnp.maximum(i - 2, 0))
    accumulate(_dot_t(qr, kw_ref[blk, :]), vw_ref[blk, :], (~tri) & (i >= 2), False, accw_scr, mw_scr)

    blk_id = lax.broadcasted_iota(jnp.int32, (LANES, tq), 0)
    jt = (start + lax.broadcasted_iota(jnp.int32, (LANES, tq), 1)) >> SEL_SHIFT
    forced = (blk_id == 0) | (blk_id == jt) | (blk_id == jt - 1)
    work = jnp.where(blk_id > jt, -1.0, jnp.where(forced, -2.0, imp_t))
    sel = forced
    for _ in range(N_SELECT - 3):
        mx = jnp.max(work, axis=0, keepdims=True)
        idx = jnp.min(jnp.where(work == mx, blk_id, LANES), axis=0, keepdims=True)
        pick = blk_id == idx
        sel = sel | pick
        work = jnp.where(pick, -2.0, work)
    sel_bias = jnp.transpose(jnp.where(sel, 0.0, NEG)).astype(BF16)

    for g in range(GROUP):
        qa_scr[g * tq:(g + 1) * tq, LANES:2 * LANES] = sel_bias

    accumulate(_dot_t(qa_scr[...], ks_ref[diag, :]), vs_ref[diag, :], tri, True)

    def slc_pair(kp, carry):
        blk = pl.ds(pl.multiple_of(kp * 2 * tq, 2 * tq), 2 * tq)
        accumulate(_dot_t(qa_scr[...], ks_ref[blk, :]), vs_ref[blk, :], None, False)
        return carry

    lax.fori_loop(0, i // 2, slc_pair, 0)

    @pl.when(i % 2 == 1)
    def _():
        blk = key_block(i - 1)
        accumulate(_dot_t(qa_scr[...], ks_ref[blk, :]), vs_ref[blk, :], None, False)

    gates = gate_ref[0]
    g_hi = gates.astype(BF16)
    g_lo = (gates - g_hi.astype(F32)).astype(BF16)
    gate_wide = _dot(g_hi, gsel_ref[...]) + _dot(g_lo, gsel_ref[...])
    for pr in range(GROUP // 2):
        ev = slice(2 * pr * tq, (2 * pr + 1) * tq)
        od = slice((2 * pr + 1) * tq, (2 * pr + 2) * tq)

        def pair(acc):
            a_e, a_o = acc[ev], acc[od]
            r_e = pltpu.roll(a_e, HEAD_DIM, 1)
            r_o = pltpu.roll(a_o, HEAD_DIM, 1)
            return jnp.where(low, a_e, r_o) / jnp.where(low, r_e, a_o)

        def gate(br):
            c = (br * (GROUP // 2) + pr) * LANES
            return gate_wide[:, c:c + LANES]

        o_c = jnp.where(low, o_cmp[ev], pltpu.roll(o_cmp[od], HEAD_DIM, 1))
        out = gate(0) * o_c + gate(1) * pair(acc_scr) + gate(2) * pair(accw_scr)
        o_ref[0, :, pr * LANES:(pr + 1) * LANES] = out.astype(BF16)


def _nsa_kernel(small_ref, *refs, tq):
    @pl.when(small_ref[0] != 0)
    def _():
        _nsa_body(*refs, tq=tq, shifted=False)

    @pl.when(small_ref[0] == 0)
    def _():
        _nsa_body(*refs, tq=tq, shifted=True)


def _nsa_attention(small, qn, qr, gates, cmp4, ks, vs, kw, vw, wimpt):
    b, s, hdim = qn.shape
    tq = min(256, s)
    assert s // SEL_BLOCK <= LANES and s % tq == 0 and SLIDE_WINDOW == 2 * tq
    ncp = cmp4.shape[2]
    gw = GROUP * HEAD_DIM
    rows = GROUP * tq
    qspec = pl.BlockSpec((1, tq, gw), lambda bb, kv, i: (bb, i, kv))
    full = lambda width: pl.BlockSpec((None, s, width), lambda bb, kv, i: (bb, 0, kv))
    col = np.arange(3 * (GROUP // 2) * LANES)[None, :]
    src = (col // LANES // (GROUP // 2)) * GROUP + 2 * ((col // LANES) % (GROUP // 2)) + (col % LANES) // HEAD_DIM
    gsel = (np.arange(LANES)[:, None] == src).astype(BF16)
    return pl.pallas_call(
        functools.partial(_nsa_kernel, tq=tq),
        grid=(b, N_KV_HEADS, s // tq),
        in_specs=[pl.BlockSpec(memory_space=pltpu.SMEM), qspec, qspec,
                  pl.BlockSpec((1, tq, LANES), lambda bb, kv, i: (bb, i, kv)),
                  pl.BlockSpec((None, None, ncp, LANES), lambda bb, kv, i: (kv, bb, 0, 0)),
                  pl.BlockSpec((None, None, ncp, LANES), lambda bb, kv, i: (N_KV_HEADS + kv, bb, 0, 0)),
                  full(2 * LANES), full(LANES), full(LANES), full(LANES),
                  pl.BlockSpec((LANES, ncp), lambda bb, kv, i: (0, 0)),
                  pl.BlockSpec(gsel.shape, lambda bb, kv, i: (0, 0))],
        out_specs=qspec,
        out_shape=jax.ShapeDtypeStruct((b, s, hdim), BF16),
        scratch_shapes=[pltpu.VMEM((rows, 2 * LANES), BF16),
                        pltpu.VMEM((rows, 1), F32), pltpu.VMEM((rows, LANES), F32),
                        pltpu.VMEM((rows, 1), F32), pltpu.VMEM((rows, LANES), F32)],
        compiler_params=_params("parallel", "parallel", "arbitrary"),
        name="nsa_attention",
    )(small, qn, qr, gates, cmp4, cmp4, ks, vs, kw, vw, wimpt, gsel)


def _rope_tables(s, tm=None, rate=1):
    half = HEAD_DIM // 2
    row = lax.broadcasted_iota(jnp.int32, (s, LANES), 0)
    lane = lax.broadcasted_iota(jnp.int32, (s, LANES), 1)
    if rate > 1:
        seg = tm // rate
        local = row % tm
        row = row - local + (local % seg) * rate + local // seg
    inv = jnp.tile(ROPE_THETA ** (-jnp.arange(half, dtype=F32) / half), LANES // half)
    ang = row.astype(F32) * inv[None, :]
    cos, sin = jnp.cos(ang), jnp.sin(ang)
    return cos, jnp.where(lane % HEAD_DIM < half, -sin, sin), sin


def _split_half(t, axis):
    hpg = MXU_WIDTH // HEAD_DIM
    half = HEAD_DIM // 2
    shape = t.shape
    t = t.reshape(shape[:axis] + (shape[axis] // MXU_WIDTH, hpg, 2, half) + shape[axis + 1:])
    return t.swapaxes(axis + 1, axis + 2).reshape(shape)


def _importance_weights(ncp):
    r = SEL_BLOCK // CMP_STRIDE
    nper = CMP_LEN // CMP_STRIDE
    o = np.arange(ncp)[:, None] - r * np.arange(LANES)[None, :]
    w = np.zeros((ncp, LANES), np.float32)
    for off in range(-(nper - 1), r):
        cnt = sum(1 for m in range(r) for n in range(nper) if m - n == off)
        w += np.where(o == off, float(cnt), 0.0)
    return w.astype(BF16)


def kernel(x, c, norm_g, w_ada, b_ada, ffn_w_in, ffn_w_out, a_w_qkv, a_q_gain, a_k_gain, a_w_o, kv_norm_g,
           w_ada_kv, b_ada_kv, w_kv, kv_k_gain, cmp_pos, phi_w1, phi_w2, b_w_qg, b_q_gain, b_w_o):
    b, s, d = x.shape
    depth = norm_g.shape[0]
    n_a = depth // 2
    hdim = N_HEADS * HEAD_DIM
    assert depth == 2 and b <= 8

    c8 = jnp.zeros((8, d), F32).at[:b].set(c)
    mod = _ada(c8, w_ada, b_ada.reshape(depth, 1, 9 * d))
    mod5 = mod[:, :b].reshape(depth, b, 9, 1, d)
    kvmod = _ada(c8, w_ada_kv[None], b_ada_kv.reshape(1, 1, 2 * d))
    kvmod5 = kvmod[:, :b].reshape(1, b, 2, 1, d)

    norm_g3 = norm_g.reshape(depth * 3, 1, d)
    w_in = ffn_w_in.astype(BF16)
    w_out = ffn_w_out.astype(BF16)
    cos, sin, _ = _rope_tables(s)
    blk = np.arange(LANES) // HEAD_DIM
    bd = (blk[:, None] == blk[None, :]).astype(BF16)
    blk = (np.arange(MXU_WIDTH) % LANES) // (HEAD_DIM // 2)
    bd_split = (blk[:, None] == blk[None, :]).astype(BF16)
    q_scale = HEAD_DIM ** -0.5 * LOG2E

    h = x
    for l in range(depth):
        h = _ffn(h, norm_g3, mod5, w_in, w_out, l, 0, 0)
        if l < n_a:
            gain = jnp.stack([_split_half(jnp.tile(a_q_gain[l] * q_scale, (1, N_HEADS)), 1),
                              _split_half(jnp.tile(a_k_gain[l], (1, N_HEADS)), 1),
                              jnp.ones((N_DIL, hdim), F32)], axis=1).reshape(3 * N_DIL, 1, hdim)
            cols = np.arange(N_DIL * 3 * hdim).reshape(N_DIL, 3, hdim)
            cols = np.concatenate([_split_half(cols[:, :2], 2), cols[:, 2:]], axis=1).reshape(-1)
            w_qkv = a_w_qkv[l][:, cols].astype(BF16)
            bound = HEAD_DIM * q_scale * jnp.max(jnp.abs(a_q_gain[l])) * jnp.max(jnp.abs(a_k_gain[l]))
            small = (bound <= SCORE_LIMIT).astype(jnp.int32).reshape(1)
            parts = [_dilated_attention(small, _qkv_proj(h, norm_g3, mod5, w_qkv, gain, bd_split, l, g), g)
                     for g in range(N_DIL)]
            h = _merge_out([p[0] for p in parts], [p[1] for p in parts], h, mod5, a_w_o[l].astype(BF16), l)
        else:
            j = l - n_a
            w_q = b_w_qg[j][:, :hdim]
            w_g = b_w_qg[j][:, hdim:].reshape(d, 3, N_KV_HEADS, GROUP).transpose(0, 2, 1, 3)
            w_g = jnp.pad(w_g.reshape(d, N_KV_HEADS, 3 * GROUP), ((0, 0), (0, 0), (0, LANES - 3 * GROUP)))
            w_qg = jnp.concatenate([w_q, w_g.reshape(d, N_KV_HEADS * LANES)], axis=1).astype(BF16)
            gain = jnp.tile(b_q_gain[j] * q_scale, N_HEADS)[None, :]
            qn, qr, gates = _qg_proj(h, norm_g3, mod5, w_qg, gain, cos, sin, bd, l)
            bound = HEAD_DIM * q_scale * jnp.max(jnp.abs(b_q_gain[j])) * jnp.max(jnp.abs(kv_k_gain))
            small = (bound <= SCORE_LIMIT).astype(jnp.int32).reshape(1)
            o = _nsa_attention(small, qn, qr, gates, *shared)
            h = _out_proj(o, h, mod5, b_w_o[j].astype(BF16), l)
        h = _ffn(h, norm_g3, mod5, w_in, w_out, l, 2, 1)
        if l == n_a - 1:
            col = lambda br, kv, kvh: br * 4 * HEAD_DIM + kv * 2 * HEAD_DIM + kvh * HEAD_DIM + jnp.arange(HEAD_DIM)
            dup = lambda br, kv, kvh: jnp.concatenate([col(br, kv, kvh)] * 2)
            cols = jnp.concatenate(
                [dup(br, 0, kvh) for br in (1, 2) for kvh in range(N_KV_HEADS)]
                + [dup(br, 1, kvh) for br in (1, 2) for kvh in range(N_KV_HEADS)]
                + [col(0, kv, kvh) for kv in range(2) for kvh in range(N_KV_HEADS)])
            w_kv_ext = w_kv[:, cols].astype(BF16)
            kgain = jnp.concatenate([jnp.tile(kv_k_gain[1], (2, 2)), jnp.tile(kv_k_gain[2], (2, 2)),
                                     jnp.ones((4, LANES), F32)], axis=0)
            ks, kw, vs, vw, kvc = _kv_proj(h, kv_norm_g[None, :], kvmod5, w_kv_ext, kgain, cos, sin, bd)
            nsb = s // CMP_STRIDE
            t4 = kvc.reshape(b, nsb, CMP_STRIDE, 4, HEAD_DIM).transpose(3, 0, 1, 2, 4)
            t4 = t4.reshape(4, b, nsb, CMP_STRIDE * HEAD_DIM)
            cmp4 = _compress(t4, phi_w1, phi_w2, cmp_pos.reshape(2, 1, CMP_LEN * HEAD_DIM), kv_k_gain[0][None, :])
            shared = (cmp4, ks, vs, kw, vw, _importance_weights(nsb).T)
    return h
```

```python
import functools

import jax
import jax.numpy as jnp
import numpy as np
from jax import lax
from jax.experimental import pallas as pl
from jax.experimental.pallas import tpu as pltpu

HEAD_DIM = 64
N_HEADS = 16
DIL_WINDOWS = (128, 512, 2048)
DIL_RATES = (1, 4, 16)
N_DIL = 3
N_KV_HEADS = 2
GROUP = N_HEADS // N_KV_HEADS
CMP_STRIDE = 16
CMP_LEN = 32
SEL_BLOCK = 64
SEL_SHIFT = SEL_BLOCK.bit_length() - 1
N_SELECT = 16
SLIDE_WINDOW = 512
ROPE_THETA = 10000.0
EPS = 1e-6

LANES = 128
MXU_WIDTH = 256
Q_ROWS = 128
NEG = -1e30
LOG2E = 1.4426950408889634
SCORE_LIMIT = 60.0
VMEM_LIMIT = 56 * 1024 * 1024

BF16 = jnp.bfloat16
F32 = jnp.float32


def _params(*sem):
    return pltpu.CompilerParams(dimension_semantics=sem, vmem_limit_bytes=VMEM_LIMIT)


def _dot(a, b):
    return jnp.dot(a, b, preferred_element_type=F32)


def _dot_t(a, b):
    return lax.dot_general(a, b, (((1,), (1,)), ((), ())), preferred_element_type=F32)


def _split_dot(a, b):
    a_hi = a.astype(BF16)
    a_lo = (a - a_hi.astype(F32)).astype(BF16)
    b_hi = b.astype(BF16)
    b_lo = (b - b_hi.astype(F32)).astype(BF16)
    return _dot(a_hi, b_hi) + (_dot(a_hi, b_lo) + _dot(a_lo, b_hi))


def _sigmoid(x):
    return 1.0 / (1.0 + jnp.exp(-x))


def _normmod(x, g, shift, scale):
    return x * lax.rsqrt(jnp.mean(x * x, axis=-1, keepdims=True) + EPS) * (g * (1.0 + scale)) + shift


def _head_rms(y, bd):
    ss = _dot((y * y).astype(BF16), bd)
    return y * lax.rsqrt(ss * (1.0 / HEAD_DIM) + EPS)


def _rope(z, cos, sin_signed):
    lane = lax.broadcasted_iota(jnp.int32, z.shape, 1)
    first_half = (lane & (HEAD_DIM - 1)) < HEAD_DIM // 2
    swapped = jnp.where(first_half, pltpu.roll(z, z.shape[1] - HEAD_DIM // 2, 1),
                        pltpu.roll(z, HEAD_DIM // 2, 1))
    return z * cos + swapped * sin_signed


def _ada_kernel(c_ref, w_ref, b_ref, o_ref):
    c = c_ref[...]
    a = (c * _sigmoid(c)).astype(BF16)
    o_ref[...] = _dot(a, w_ref[...].astype(BF16)) + b_ref[...]


def _ada(c8, w, b):
    nl, d, n = w.shape
    tn = 1024
    return pl.pallas_call(
        _ada_kernel,
        grid=(nl, n // tn),
        in_specs=[pl.BlockSpec((8, d), lambda l, j: (0, 0)),
                  pl.BlockSpec((None, d, tn), lambda l, j: (l, 0, j)),
                  pl.BlockSpec((None, 1, tn), lambda l, j: (l, 0, j))],
        out_specs=pl.BlockSpec((None, 8, tn), lambda l, j: (l, 0, j)),
        out_shape=jax.ShapeDtypeStruct((nl, 8, n), F32),
        compiler_params=_params("parallel", "parallel"),
        name="ada_table",
    )(c8, w, b)


def _mod_block(d, l, k):
    return pl.BlockSpec((None, None, None, 1, d), lambda b, *_: (l, b, k, 0, 0))


def _ffn_kernel(h_ref, g_ref, sh_ref, sc_ref, gt_ref, wg_ref, wu_ref, wo_ref, o_ref, u_scr, acc_scr):
    u_scr[...] = _normmod(h_ref[0], g_ref[...], sh_ref[...], sc_ref[...]).astype(BF16)

    u = u_scr[...]
    tf = wg_ref.shape[1]
    bounds = [(lo, min(lo + MXU_WIDTH, tf)) for lo in range(0, tf, MXU_WIDTH)]

    def gate_up(c):
        lo, hi = bounds[c]
        return _dot(u, wg_ref[:, lo:hi]), _dot(u, wu_ref[:, lo:hi])

    nxt = gate_up(0)
    for c, (lo, hi) in enumerate(bounds):
        gate, up = nxt
        if c + 1 < len(bounds):
            nxt = gate_up(c + 1)
        act = (gate * _sigmoid(gate) * up).astype(BF16)
        down = _dot(act, wo_ref[lo:hi, :])
        acc_scr[...] = down if c == 0 else acc_scr[...] + down

    o_ref[0] = h_ref[0] + (0.5 * (1.0 + gt_ref[...])) * acc_scr[...]


def _ffn(h, norm_g3, mod5, w_in, w_out, l, sub, which):
    b, s, d = h.shape
    f = w_out.shape[2]
    tm = min(1024, s)
    resident = pl.Buffered(1)
    return pl.pallas_call(
        _ffn_kernel,
        grid=(b, s // tm),
        in_specs=[pl.BlockSpec((1, tm, d), lambda bb, i: (bb, i, 0)),
                  pl.BlockSpec((None, 1, d), lambda bb, i: (l * 3 + sub, 0, 0)),
                  _mod_block(d, l, sub * 3 + 0),
                  _mod_block(d, l, sub * 3 + 1),
                  _mod_block(d, l, sub * 3 + 2),
                  pl.BlockSpec((None, None, d, f), lambda bb, i: (l, which, 0, 0), pipeline_mode=resident),
                  pl.BlockSpec((None, None, d, f), lambda bb, i: (l, which, 0, 1), pipeline_mode=resident),
                  pl.BlockSpec((None, None, f, d), lambda bb, i: (l, which, 0, 0), pipeline_mode=resident)],
        out_specs=pl.BlockSpec((1, tm, d), lambda bb, i: (bb, i, 0)),
        out_shape=jax.ShapeDtypeStruct((b, s, d), F32),
        scratch_shapes=[pltpu.VMEM((tm, d), BF16), pltpu.VMEM((tm, d), F32)],
        compiler_params=_params("parallel", "parallel"),
        name="swiglu_half_step",
    )(h, norm_g3, mod5, mod5, mod5, w_in, w_in, w_out)


def _qkv_kernel(h_ref, g_ref, sh_ref, sc_ref, w_ref, gain_ref, cos_ref, sin_ref, bd_ref, o_ref, u_scr, uf_scr,
                *, rate):
    tm = u_scr.shape[0]
    seg = tm // rate
    u = _normmod(h_ref[0], g_ref[...], sh_ref[...], sc_ref[...])
    if rate == 1:
        u_scr[...] = u.astype(BF16)
    else:
        for cb in range(u.shape[1] // LANES):
            sl = slice(cb * LANES, (cb + 1) * LANES)
            uf_scr[cb] = u[:, sl]
            for res in range(rate):
                u_scr[res * seg:(res + 1) * seg, sl] = uf_scr[cb, pl.ds(res, seg, stride=rate), :].astype(BF16)

    n_grp = o_ref.shape[1]
    per_kind = n_grp // 3
    cos = cos_ref[...]
    sin = sin_ref[...]
    bd = bd_ref[...]

    def project(grp):
        return _dot(u_scr[...], w_ref[:, grp * MXU_WIDTH:(grp + 1) * MXU_WIDTH])

    y = project(0)
    for grp in range(n_grp):
        y_next = project(grp + 1) if grp + 1 < n_grp else None
        kind, sub = divmod(grp, per_kind)
        if kind < 2:
            z = _head_rms(y, bd) * gain_ref[kind, :, sub * MXU_WIDTH:(sub + 1) * MXU_WIDTH]
            x1, x2 = z[:, :LANES], z[:, LANES:]
            y = jnp.concatenate([x1 * cos - x2 * sin, x1 * sin + x2 * cos], axis=1)
        y = y.astype(BF16)
        for res in range(rate):
            o_ref[res, grp] = y[res * seg:(res + 1) * seg]
        y = y_next


def _qkv_proj(h, norm_g3, mod5, w, gain, bd, l, g):
    b, s, d = h.shape
    rate = DIL_RATES[g]
    tm = min(1024, s)
    seg = tm // rate
    tn = N_HEADS * HEAD_DIM
    n_grp = tn // MXU_WIDTH
    cos, _, sin = _rope_tables(s, tm, rate)

    return pl.pallas_call(
        functools.partial(_qkv_kernel, rate=rate),
        grid=(b, s // tm),
        in_specs=[pl.BlockSpec((1, tm, d), lambda bb, i: (bb, i, 0)),
                  pl.BlockSpec((None, 1, d), lambda bb, i: (l * 3 + 1, 0, 0)),
                  _mod_block(d, l, 3),
                  _mod_block(d, l, 4),
                  pl.BlockSpec((d, 3 * tn), lambda bb, i: (0, g), pipeline_mode=pl.Buffered(1)),
                  pl.BlockSpec((3, 1, tn), lambda bb, i: (g, 0, 0)),
                  pl.BlockSpec((tm, LANES), lambda bb, i: (i, 0)),
                  pl.BlockSpec((tm, LANES), lambda bb, i: (i, 0)),
                  pl.BlockSpec(bd.shape, lambda bb, i: (0, 0))],
        out_specs=pl.BlockSpec((None, rate, 3 * n_grp, seg, MXU_WIDTH), lambda bb, i: (bb, 0, 0, i, 0)),
        out_shape=jax.ShapeDtypeStruct((b, rate, 3 * n_grp, s // rate, MXU_WIDTH), BF16),
        scratch_shapes=[pltpu.VMEM((tm, d), BF16), pltpu.VMEM((d // LANES, tm, LANES), F32)],
        compiler_params=_params("parallel", "parallel"),
        name=f"qkv_proj_rate{rate}",
    )(h, norm_g3, mod5, mod5, w, gain, cos, sin, bd)


def _dil_body(bias_ref, q_ref, kp_ref, kc_ref, vp_ref, vc_ref, o_ref, lse_ref, *, tq, shifted):
    i = pl.program_id(2)
    grp = pl.program_id(3)
    hpg = MXU_WIDTH // HEAD_DIM
    qlane = lax.broadcasted_iota(jnp.int32, (Q_ROWS, MXU_WIDTH), 1)
    q_head = (qlane & (LANES - 1)) // (HEAD_DIM // 2)
    v_head = qlane // HEAD_DIM
    seg = lax.broadcasted_iota(jnp.int32, (Q_ROWS, LANES), 1) // (LANES // N_HEADS)

    def scores(j):
        q = q_ref[j * Q_ROWS:(j + 1) * Q_ROWS, :]
        zero = jnp.zeros_like(q)
        q2 = jnp.concatenate([jnp.where(q_head == t, q, zero) for t in range(hpg)], axis=0)
        if j == 0:
            k = jnp.concatenate([kp_ref[...], kc_ref[0:Q_ROWS, :]], axis=0)
        else:
            k = kc_ref[(j - 1) * Q_ROWS:(j + 1) * Q_ROWS, :]
        return _dot_t(q2, k)

    for j in range(tq // Q_ROWS):
        rs = slice(j * Q_ROWS, (j + 1) * Q_ROWS)
        s = scores(j) + bias_ref[...]
        if j == 0:
            v = jnp.concatenate([vp_ref[...], vc_ref[0:Q_ROWS, :]], axis=0)
            prev_key = lax.broadcasted_iota(jnp.int32, s.shape, 1) < Q_ROWS
            s = jnp.where(prev_key & (i == 0), NEG, s)
        else:
            v = vc_ref[(j - 1) * Q_ROWS:(j + 1) * Q_ROWS, :]
        if shifted:
            m = jnp.max(s, axis=-1, keepdims=True)
            p = jnp.exp2(s - m)
        else:
            p = jnp.exp2(s)
        den = jnp.sum(p, axis=-1, keepdims=True)
        o2 = _dot(p.astype(BF16), v) / den
        lse2 = jnp.log(den) * LOG2E
        if shifted:
            lse2 = lse2 + m
        out = o2[0:Q_ROWS]
        new = jnp.where(grp == 0, 0.0, lse_ref[0, rs, :])
        for t in range(hpg):
            ts = slice(t * Q_ROWS, (t + 1) * Q_ROWS)
            if t:
                out = jnp.where(v_head == t, o2[ts], out)
            new = jnp.where(seg == hpg * grp + t, lse2[ts], new)
        o_ref[0, rs, :] = out.astype(BF16)
        lse_ref[0, rs, :] = new


def _dil_kernel(small_ref, *refs, tq):
    @pl.when(small_ref[0] != 0)
    def _():
        _dil_body(*refs, tq=tq, shifted=False)

    @pl.when(small_ref[0] == 0)
    def _():
        _dil_body(*refs, tq=tq, shifted=True)


def _dilated_attention(small, qkv_r, g):
    b, rate, n3, l, width = qkv_r.shape
    s = l * rate
    n_back = DIL_WINDOWS[g] // rate
    assert n_back <= Q_ROWS and rate == DIL_RATES[g] and width == MXU_WIDTH
    tq = min(512, l)
    hdim = N_HEADS * HEAD_DIM
    n_grp = n3 // 3
    sub = tq // Q_ROWS

    def cur(off):
        return pl.BlockSpec((None, None, None, tq, width), lambda bb, r, i, gp: (bb, r, off + gp, i, 0))

    def prev(off):
        return pl.BlockSpec((None, None, None, Q_ROWS, width),
                            lambda bb, r, i, gp: (bb, r, off + gp, jnp.maximum(i * sub - 1, 0), 0))

    rows = (width // HEAD_DIM) * Q_ROWS
    dist = Q_ROWS + (np.arange(rows)[:, None] % Q_ROWS) - np.arange(2 * Q_ROWS)[None, :]
    bias = np.where((dist >= 0) & (dist <= n_back), 0.0, NEG).astype(np.float32)

    o, lse = pl.pallas_call(
        functools.partial(_dil_kernel, tq=tq),
        grid=(b, rate, l // tq, n_grp),
        in_specs=[pl.BlockSpec(memory_space=pltpu.SMEM),
                  pl.BlockSpec(bias.shape, lambda bb, r, i, gp: (0, 0)),
                  cur(0), prev(n_grp), cur(n_grp), prev(2 * n_grp), cur(2 * n_grp)],
        out_specs=[pl.BlockSpec((1, None, tq, width), lambda bb, r, i, gp: (bb, r, i, gp)),
                   pl.BlockSpec((1, None, tq, LANES), lambda bb, r, i, gp: (bb, r, i, 0))],
        out_shape=[jax.ShapeDtypeStruct((b, rate, l, hdim), BF16),
                   jax.ShapeDtypeStruct((b, rate, l, LANES), F32)],
        compiler_params=_params("parallel", "parallel", "parallel", "arbitrary"),
        name=f"dilated_attention_rate{rate}",
    )(small, bias, qkv_r, qkv_r, qkv_r, qkv_r, qkv_r)
    return o, lse


def _merge_out_kernel(o0_ref, o1_ref, o2_ref, l0_ref, l1_ref, l2_ref, p1_ref, p2_ref, e_ref, h_ref, gt_ref, w_ref,
                      out_ref):
    def positions(x_ref, p_ref):
        x = jnp.concatenate([x_ref[res] for res in range(x_ref.shape[0])], axis=0)
        if p_ref is None:
            return x.astype(F32)
        if x.dtype == BF16:
            return _dot(p_ref[...], x)
        hi = x.astype(BF16)
        return _dot(p_ref[...], hi) + _dot(p_ref[...], (x - hi.astype(F32)).astype(BF16))

    l0, l1, l2 = positions(l0_ref, None), positions(l1_ref, p1_ref), positions(l2_ref, p2_ref)
    m = jnp.maximum(jnp.maximum(l0, l1), l2)
    e0, e1, e2 = jnp.exp2(l0 - m), jnp.exp2(l1 - m), jnp.exp2(l2 - m)
    inv = 1.0 / (e0 + e1 + e2)
    spread = e_ref[...]
    o = (_dot((e0 * inv).astype(BF16), spread) * positions(o0_ref, None)
         + _dot((e1 * inv).astype(BF16), spread) * positions(o1_ref, p1_ref)
         + _dot((e2 * inv).astype(BF16), spread) * positions(o2_ref, p2_ref))
    out_ref[0] = h_ref[0] + (1.0 + gt_ref[...]) * _dot(o.astype(BF16), w_ref[...])


def _merge_out(os_, lses, h, mod5, w_o, l):
    b, s, d = h.shape
    hdim = w_o.shape[0]
    tm = min(512, s)
    row = lambda width: pl.BlockSpec((1, tm, width), lambda bb, i: (bb, i, 0))
    rep = LANES // N_HEADS
    lane = np.arange(LANES)[:, None]
    col = np.arange(hdim)[None, :]
    spread = ((lane % rep == 0) & (lane // rep == col // HEAD_DIM)).astype(BF16)

    def unpermute(rate):
        pos = np.arange(tm)
        return (((pos % rate) * (tm // rate) + pos // rate)[:, None] == np.arange(tm)[None, :]).astype(BF16)

    def residue_spec(t):
        rate, width = t.shape[1], t.shape[3]
        return pl.BlockSpec((None, rate, tm // rate, width), lambda bb, i: (bb, 0, i, 0))

    const = lambda shape: pl.BlockSpec(shape, lambda bb, i: (0, 0))
    return pl.pallas_call(
        _merge_out_kernel,
        grid=(b, s // tm),
        in_specs=[residue_spec(t) for t in (*os_, *lses)]
        + [const((tm, tm)), const((tm, tm)), const((LANES, hdim)), row(d), _mod_block(d, l, 5), const((hdim, d))],
        out_specs=row(d),
        out_shape=jax.ShapeDtypeStruct((b, s, d), F32),
        compiler_params=_params("parallel", "parallel"),
        name="merge_out_proj",
    )(*os_, *lses, unpermute(os_[1].shape[1]), unpermute(os_[2].shape[1]), spread, h, mod5, w_o)


def _out_kernel(o_ref, h_ref, gt_ref, w_ref, out_ref):
    out_ref[0] = h_ref[0] + (1.0 + gt_ref[...]) * _dot(o_ref[0], w_ref[...])


def _out_proj(o, h, mod5, w_o, l):
    b, s, d = h.shape
    hdim = w_o.shape[0]
    tm = min(1024, s)
    row = lambda width: pl.BlockSpec((1, tm, width), lambda bb, i: (bb, i, 0))
    return pl.pallas_call(
        _out_kernel,
        grid=(b, s // tm),
        in_specs=[row(hdim), row(d), _mod_block(d, l, 5), pl.BlockSpec((hdim, d), lambda bb, i: (0, 0))],
        out_specs=row(d),
        out_shape=jax.ShapeDtypeStruct((b, s, d), F32),
        compiler_params=_params("parallel", "parallel"),
        name="out_proj",
    )(o, h, mod5, w_o)


def _kv_kernel(h_ref, g_ref, sh_ref, sc_ref, w_ref, gain_ref, cos_ref, sin_ref, bd_ref,
               ks_ref, kw_ref, vs_ref, vw_ref, kvc_ref, *, tm):
    i = pl.program_id(1)
    u = _normmod(h_ref[0], g_ref[...], sh_ref[...], sc_ref[...]).astype(BF16)
    y = _dot(u, w_ref[...])
    cos = cos_ref[...]
    sin = sin_ref[...]
    bd = bd_ref[...]
    lane = lax.broadcasted_iota(jnp.int32, (tm, LANES), 1)
    pos = i * tm + lax.broadcasted_iota(jnp.int32, (tm, LANES), 0)
    onehot = jnp.where((pos >> SEL_SHIFT) == lane, 1.0, 0.0).astype(BF16)
    for j in range(4):
        z = _head_rms(y[:, j * LANES:(j + 1) * LANES], bd) * gain_ref[j:j + 1, :]
        zr = _rope(z, cos, sin).astype(BF16)
        if j < 2:
            ks_ref[0, :, 2 * j * LANES:(2 * j + 1) * LANES] = zr
            ks_ref[0, :, (2 * j + 1) * LANES:(2 * j + 2) * LANES] = onehot
        else:
            kw_ref[0, :, (j - 2) * LANES:(j - 1) * LANES] = zr
    for j in range(4):
        v = y[:, (4 + j) * LANES:(5 + j) * LANES]
        va = jnp.where(lane < HEAD_DIM, v, 1.0).astype(BF16)
        ref = vs_ref if j < 2 else vw_ref
        ref[0, :, (j % 2) * LANES:(j % 2 + 1) * LANES] = va
    kvc_ref[0] = y[:, 8 * LANES:10 * LANES]


def _kv_proj(h, kv_norm_g, kvmod5, w, gain, cos, sin, bd):
    b, s, d = h.shape
    n = w.shape[1]
    tm = min(512, s)
    row = lambda width, dt: (pl.BlockSpec((1, tm, width), lambda bb, i: (bb, i, 0)),
                             jax.ShapeDtypeStruct((b, s, width), dt))
    outs = [row(4 * LANES, BF16), row(2 * LANES, BF16), row(2 * LANES, BF16), row(2 * LANES, BF16),
            row(2 * LANES, F32)]
    return pl.pallas_call(
        functools.partial(_kv_kernel, tm=tm),
        grid=(b, s // tm),
        in_specs=[pl.BlockSpec((1, tm, d), lambda bb, i: (bb, i, 0)),
                  pl.BlockSpec((1, d), lambda bb, i: (0, 0)),
                  _mod_block(d, 0, 0),
                  _mod_block(d, 0, 1),
                  pl.BlockSpec((d, n), lambda bb, i: (0, 0)),
                  pl.BlockSpec((8, LANES), lambda bb, i: (0, 0)),
                  pl.BlockSpec((tm, LANES), lambda bb, i: (i, 0)),
                  pl.BlockSpec((tm, LANES), lambda bb, i: (i, 0)),
                  pl.BlockSpec((LANES, LANES), lambda bb, i: (0, 0))],
        out_specs=[o[0] for o in outs],
        out_shape=[o[1] for o in outs],
        compiler_params=_params("parallel", "parallel"),
        name="shared_kv_proj",
    )(h, kv_norm_g, kvmod5, kvmod5, w, gain, cos, sin, bd)


def _cmp_kernel(t_ref, w1_ref, w2_ref, pos_ref, gain_ref, o_ref):
    is_key = pl.program_id(0) < N_KV_HEADS
    t = t_ref[...]
    w1 = w1_ref[...]
    half = w1.shape[0] // 2
    first = _split_dot(t, w1[:half])
    second = _split_dot(t, w1[half:])
    second = pltpu.roll(second, second.shape[0] - 1, 0)
    posb = _split_dot(jnp.broadcast_to(pos_ref[...], (8, w1.shape[0])), w1)[0:1]
    hid = first + second + posb
    out = _split_dot(hid * _sigmoid(hid), w2_ref[...])
    key = out * lax.rsqrt(jnp.mean(out * out, axis=-1, keepdims=True) + EPS) * gain_ref[...]
    dup = jnp.concatenate([key, key], axis=-1)
    aug = jnp.concatenate([out, jnp.ones_like(out)], axis=-1)
    o_ref[...] = jnp.where(is_key, dup, aug).astype(BF16)


def _compress(t4, phi_w1, phi_w2, cmp_pos, gain):
    _, b, nsb, width = t4.shape
    hid = phi_w1.shape[2]
    return pl.pallas_call(
        _cmp_kernel,
        grid=(4, b),
        in_specs=[pl.BlockSpec((None, None, nsb, width), lambda t, bb: (t, bb, 0, 0)),
                  pl.BlockSpec((None, CMP_LEN * HEAD_DIM, hid), lambda t, bb: (t // N_KV_HEADS, 0, 0)),
                  pl.BlockSpec((None, hid, HEAD_DIM), lambda t, bb: (t // N_KV_HEADS, 0, 0)),
                  pl.BlockSpec((None, 1, CMP_LEN * HEAD_DIM), lambda t, bb: (t // N_KV_HEADS, 0, 0)),
                  pl.BlockSpec((1, HEAD_DIM), lambda t, bb: (0, 0))],
        out_specs=pl.BlockSpec((None, None, nsb, LANES), lambda t, bb: (t, bb, 0, 0)),
        out_shape=jax.ShapeDtypeStruct((4, b, nsb, LANES), BF16),
        compiler_params=_params("parallel", "parallel"),
        name="compress_blocks",
    )(t4, phi_w1, phi_w2, cmp_pos, gain)


def _qg_kernel(h_ref, g_ref, sh_ref, sc_ref, w_ref, gain_ref, cos_ref, sin_ref, bd_ref,
               qn_ref, qr_ref, gate_ref):
    u = _normmod(h_ref[0], g_ref[...], sh_ref[...], sc_ref[...]).astype(BF16)
    y = _dot(u, w_ref[...])
    cos = cos_ref[...]
    sin = sin_ref[...]
    bd = bd_ref[...]
    hdim = qn_ref.shape[2]
    for cb in range(hdim // LANES):
        sl = slice(cb * LANES, (cb + 1) * LANES)
        z = _head_rms(y[:, sl], bd) * gain_ref[:, sl]
        qn_ref[0, :, sl] = z.astype(BF16)
        qr_ref[0, :, sl] = _rope(z, cos, sin).astype(BF16)
    gate_ref[0] = _sigmoid(y[:, hdim:])


def _qg_proj(h, norm_g3, mod5, w, gain, cos, sin, bd, l):
    b, s, d = h.shape
    n = w.shape[1]
    hdim = N_HEADS * HEAD_DIM
    tm = min(512, s)
    row = lambda width, dt: (pl.BlockSpec((1, tm, width), lambda bb, i: (bb, i, 0)),
                             jax.ShapeDtypeStruct((b, s, width), dt))
    outs = [row(hdim, BF16), row(hdim, BF16), row(n - hdim, F32)]
    return pl.pallas_call(
        _qg_kernel,
        grid=(b, s // tm),
        in_specs=[pl.BlockSpec((1, tm, d), lambda bb, i: (bb, i, 0)),
                  pl.BlockSpec((None, 1, d), lambda bb, i: (l * 3 + 1, 0, 0)),
                  _mod_block(d, l, 3),
                  _mod_block(d, l, 4),
                  pl.BlockSpec((d, n), lambda bb, i: (0, 0)),
                  pl.BlockSpec((1, hdim), lambda bb, i: (0, 0)),
                  pl.BlockSpec((tm, LANES), lambda bb, i: (i, 0)),
                  pl.BlockSpec((tm, LANES), lambda bb, i: (i, 0)),
                  pl.BlockSpec((LANES, LANES), lambda bb, i: (0, 0))],
        out_specs=[o[0] for o in outs],
        out_shape=[o[1] for o in outs],
        compiler_params=_params("parallel", "parallel"),
        name="nsa_query_gate_proj",
    )(h, norm_g3, mod5, mod5, w, gain, cos, sin, bd)


def _nsa_body(qn_ref, qr_ref, gate_ref, kc_ref, vc_ref, ks_ref, vs_ref, kw_ref, vw_ref, wimpt_ref, gsel_ref,
              o_ref, qa_scr, m_scr, acc_scr, mw_scr, accw_scr, *, tq, shifted):
    i = pl.program_id(2)
    start = i * tq
    rows = GROUP * tq
    ncp = kc_ref.shape[0]
    low = lax.broadcasted_iota(jnp.int32, (tq, LANES), 1) < HEAD_DIM
    low_r = lax.broadcasted_iota(jnp.int32, (rows, LANES), 1) < HEAD_DIM

    def accumulate(s, v, mask, first, acc_ref=acc_scr, m_ref=m_scr):
        if not shifted:
            p = jnp.exp2(s)
            if mask is not None:
                p = jnp.where(mask, p, 0.0)
            pv = _dot(p.astype(BF16), v)
            acc_ref[...] = pv if first else acc_ref[...] + pv
            return
        if mask is not None:
            s = jnp.where(mask, s, NEG)
        if first:
            m_new = jnp.max(s, axis=-1, keepdims=True)
            acc_ref[...] = _dot(jnp.exp2(s - m_new).astype(BF16), v)
        else:
            m_old = m_ref[...]
            m_new = jnp.maximum(m_old, jnp.max(s, axis=-1, keepdims=True))
            acc_ref[...] = jnp.exp2(m_old - m_new) * acc_ref[...] + _dot(jnp.exp2(s - m_new).astype(BF16), v)
        m_ref[...] = m_new

    def stack(q):
        parts = []
        for g in range(GROUP):
            blk = q[:, (g // 2) * LANES:(g // 2 + 1) * LANES]
            parts.append(jnp.where(low if g % 2 == 0 else ~low, blk, jnp.zeros_like(blk)))
        return parts

    qn = jnp.concatenate(stack(qn_ref[0]), axis=0)
    sc = _dot_t(qn, kc_ref[...])
    c_end = lax.broadcasted_iota(jnp.int32, (tq, ncp), 1) * CMP_STRIDE + (CMP_LEN - 1)
    valid_t = c_end <= start + lax.broadcasted_iota(jnp.int32, (tq, ncp), 0)
    if shifted:
        valid = jnp.concatenate([valid_t] * GROUP, axis=0)
        sc = jnp.where(valid, sc, NEG)
        sc = sc - jnp.max(sc, axis=-1, keepdims=True)
        p = jnp.where(valid, jnp.exp2(sc), 0.0)
    else:
        p = jnp.exp2(sc + jnp.concatenate([jnp.where(valid_t, 0.0, NEG)] * GROUP, axis=0))
    oc = _dot(p.astype(BF16), vc_ref[...])
    inv = 1.0 / jnp.maximum(jnp.where(low_r, pltpu.roll(oc, HEAD_DIM, 1), oc), 1e-30)
    o_cmp = oc * inv
    psum_blocks = []
    for cb in range(ncp // LANES):
        cs = slice(cb * LANES, (cb + 1) * LANES)
        tot = p[0:tq, cs] * inv[0:tq]
        for g in range(1, GROUP):
            tot = tot + p[g * tq:(g + 1) * tq, cs] * inv[g * tq:(g + 1) * tq]
        psum_blocks.append(tot)
    psum = jnp.concatenate(psum_blocks, axis=1)

    qr_parts = stack(qr_ref[0])
    for g in range(GROUP):
        qa_scr[g * tq:(g + 1) * tq, 0:LANES] = qr_parts[g]
    qr = jnp.concatenate(qr_parts, axis=0)

    tri = (lax.broadcasted_iota(jnp.int32, (rows, tq), 1)
           <= (lax.broadcasted_iota(jnp.int32, (rows, tq), 0) & (tq - 1)))

    def key_block(kb):
        return pl.ds(pl.multiple_of(kb * tq, tq), tq)

    diag = key_block(i)
    accumulate(_dot_t(qr, kw_ref[diag, :]), vw_ref[diag, :], tri, True, accw_scr, mw_scr)

    p_hi = psum.astype(BF16)
    p_lo = (psum - p_hi.astype(F32)).astype(BF16)
    imp_t = _dot_t(wimpt_ref[...], p_hi) + _dot_t(wimpt_ref[...], p_lo)

    blk = key_block(jnp.maximum(i - 1, 0))
    accumulate(_dot_t(qr, kw_ref[blk, :]), vw_ref[blk, :], jnp.broadcast_to(i >= 1, tri.shape), False,
               accw_scr, mw_scr)
    blk = key_block(jnp.maximum(i - 2, 0))
    accumulate(_dot_t(qr, kw_ref[blk, :]), vw_ref[blk, :], (~tri) & (i >= 2), False, accw_scr, mw_scr)

    blk_id = lax.broadcasted_iota(jnp.int32, (LANES, tq), 0)
    jt = (start + lax.broadcasted_iota(jnp.int32, (LANES, tq), 1)) >> SEL_SHIFT
    forced = (blk_id == 0) | (blk_id == jt) | (blk_id == jt - 1)
    work = jnp.where(blk_id > jt, -1.0, jnp.where(forced, -2.0, imp_t))
    sel = forced
    for _ in range(N_SELECT - 3):
        mx = jnp.max(work, axis=0, keepdims=True)
        idx = jnp.min(jnp.where(work == mx, blk_id, LANES), axis=0, keepdims=True)
        pick = blk_id == idx
        sel = sel | pick
        work = jnp.where(pick, -2.0, work)
    sel_bias = jnp.transpose(jnp.where(sel, 0.0, NEG)).astype(BF16)

    for g in range(GROUP):
        qa_scr[g * tq:(g + 1) * tq, LANES:2 * LANES] = sel_bias

    accumulate(_dot_t(qa_scr[...], ks_ref[diag, :]), vs_ref[diag, :], tri, True)

    def slc_pair(kp, carry):
        blk = pl.ds(pl.multiple_of(kp * 2 * tq, 2 * tq), 2 * tq)
        accumulate(_dot_t(qa_scr[...], ks_ref[blk, :]), vs_ref[blk, :], None, False)
        return carry

    lax.fori_loop(0, i // 2, slc_pair, 0)

    @pl.when(i % 2 == 1)
    def _():
        blk = key_block(i - 1)
        accumulate(_dot_t(qa_scr[...], ks_ref[blk, :]), vs_ref[blk, :], None, False)

    gates = gate_ref[0]
    g_hi = gates.astype(BF16)
    g_lo = (gates - g_hi.astype(F32)).astype(BF16)
    gate_wide = _dot(g_hi, gsel_ref[...]) + _dot(g_lo, gsel_ref[...])
    for pr in range(GROUP // 2):
        ev = slice(2 * pr * tq, (2 * pr + 1) * tq)
        od = slice((2 * pr + 1) * tq, (2 * pr + 2) * tq)

        def pair(acc):
            a_e, a_o = acc[ev], acc[od]
            r_e = pltpu.roll(a_e, HEAD_DIM, 1)
            r_o = pltpu.roll(a_o, HEAD_DIM, 1)
            return jnp.where(low, a_e, r_o) / jnp.where(low, r_e, a_o)

        def gate(br):
            c = (br * (GROUP // 2) + pr) * LANES
            return gate_wide[:, c:c + LANES]

        o_c = jnp.where(low, o_cmp[ev], pltpu.roll(o_cmp[od], HEAD_DIM, 1))
        out = gate(0) * o_c + gate(1) * pair(acc_scr) + gate(2) * pair(accw_scr)
        o_ref[0, :, pr * LANES:(pr + 1) * LANES] = out.astype(BF16)


def _nsa_kernel(small_ref, *refs, tq):
    @pl.when(small_ref[0] != 0)
    def _():
        _nsa_body(*refs, tq=tq, shifted=False)

    @pl.when(small_ref[0] == 0)
    def _():
        _nsa_body(*refs, tq=tq, shifted=True)


def _nsa_attention(small, qn, qr, gates, cmp4, ks, vs, kw, vw, wimpt):
    b, s, hdim = qn.shape
    tq = min(256, s)
    assert s // SEL_BLOCK <= LANES and s % tq == 0 and SLIDE_WINDOW == 2 * tq
    ncp = cmp4.shape[2]
    gw = GROUP * HEAD_DIM
    rows = GROUP * tq
    qspec = pl.BlockSpec((1, tq, gw), lambda bb, kv, i: (bb, i, kv))
    full = lambda width: pl.BlockSpec((None, s, width), lambda bb, kv, i: (bb, 0, kv))
    col = np.arange(3 * (GROUP // 2) * LANES)[None, :]
    src = (col // LANES // (GROUP // 2)) * GROUP + 2 * ((col // LANES) % (GROUP // 2)) + (col % LANES) // HEAD_DIM
    gsel = (np.arange(LANES)[:, None] == src).astype(BF16)
    return pl.pallas_call(
        functools.partial(_nsa_kernel, tq=tq),
        grid=(b, N_KV_HEADS, s // tq),
        in_specs=[pl.BlockSpec(memory_space=pltpu.SMEM), qspec, qspec,
                  pl.BlockSpec((1, tq, LANES), lambda bb, kv, i: (bb, i, kv)),
                  pl.BlockSpec((None, None, ncp, LANES), lambda bb, kv, i: (kv, bb, 0, 0)),
                  pl.BlockSpec((None, None, ncp, LANES), lambda bb, kv, i: (N_KV_HEADS + kv, bb, 0, 0)),
                  full(2 * LANES), full(LANES), full(LANES), full(LANES),
                  pl.BlockSpec((LANES, ncp), lambda bb, kv, i: (0, 0)),
                  pl.BlockSpec(gsel.shape, lambda bb, kv, i: (0, 0))],
        out_specs=qspec,
        out_shape=jax.ShapeDtypeStruct((b, s, hdim), BF16),
        scratch_shapes=[pltpu.VMEM((rows, 2 * LANES), BF16),
                        pltpu.VMEM((rows, 1), F32), pltpu.VMEM((rows, LANES), F32),
                        pltpu.VMEM((rows, 1), F32), pltpu.VMEM((rows, LANES), F32)],
        compiler_params=_params("parallel", "parallel", "arbitrary"),
        name="nsa_attention",
    )(small, qn, qr, gates, cmp4, cmp4, ks, vs, kw, vw, wimpt, gsel)


def _rope_tables(s, tm=None, rate=1):
    half = HEAD_DIM // 2
    row = lax.broadcasted_iota(jnp.int32, (s, LANES), 0)
    lane = lax.broadcasted_iota(jnp.int32, (s, LANES), 1)
    if rate > 1:
        seg = tm // rate
        local = row % tm
        row = row - local + (local % seg) * rate + local // seg
    inv = jnp.tile(ROPE_THETA ** (-jnp.arange(half, dtype=F32) / half), LANES // half)
    ang = row.astype(F32) * inv[None, :]
    cos, sin = jnp.cos(ang), jnp.sin(ang)
    return cos, jnp.where(lane % HEAD_DIM < half, -sin, sin), sin


def _split_half(t, axis):
    hpg = MXU_WIDTH // HEAD_DIM
    half = HEAD_DIM // 2
    shape = t.shape
    t = t.reshape(shape[:axis] + (shape[axis] // MXU_WIDTH, hpg, 2, half) + shape[axis + 1:])
    return t.swapaxes(axis + 1, axis + 2).reshape(shape)


def _importance_weights(ncp):
    r = SEL_BLOCK // CMP_STRIDE
    nper = CMP_LEN // CMP_STRIDE
    o = np.arange(ncp)[:, None] - r * np.arange(LANES)[None, :]
    w = np.zeros((ncp, LANES), np.float32)
    for off in range(-(nper - 1), r):
        cnt = sum(1 for m in range(r) for n in range(nper) if m - n == off)
        w += np.where(o == off, float(cnt), 0.0)
    return w.astype(BF16)


def kernel(x, c, norm_g, w_ada, b_ada, ffn_w_in, ffn_w_out, a_w_qkv, a_q_gain, a_k_gain, a_w_o, kv_norm_g,
           w_ada_kv, b_ada_kv, w_kv, kv_k_gain, cmp_pos, phi_w1, phi_w2, b_w_qg, b_q_gain, b_w_o):
    b, s, d = x.shape
    depth = norm_g.shape[0]
    n_a = depth // 2
    hdim = N_HEADS * HEAD_DIM
    assert depth == 2 and b <= 8

    c8 = jnp.zeros((8, d), F32).at[:b].set(c)
    mod = _ada(c8, w_ada, b_ada.reshape(depth, 1, 9 * d))
    mod5 = mod[:, :b].reshape(depth, b, 9, 1, d)
    kvmod = _ada(c8, w_ada_kv[None], b_ada_kv.reshape(1, 1, 2 * d))
    kvmod5 = kvmod[:, :b].reshape(1, b, 2, 1, d)

    norm_g3 = norm_g.reshape(depth * 3, 1, d)
    w_in = ffn_w_in.astype(BF16)
    w_out = ffn_w_out.astype(BF16)
    cos, sin, _ = _rope_tables(s)
    blk = np.arange(LANES) // HEAD_DIM
    bd = (blk[:, None] == blk[None, :]).astype(BF16)
    blk = (np.arange(MXU_WIDTH) % LANES) // (HEAD_DIM // 2)
    bd_split = (blk[:, None] == blk[None, :]).astype(BF16)
    q_scale = HEAD_DIM ** -0.5 * LOG2E

    h = x
    for l in range(depth):
        h = _ffn(h, norm_g3, mod5, w_in, w_out, l, 0, 0)
        if l < n_a:
            gain = jnp.stack([_split_half(jnp.tile(a_q_gain[l] * q_scale, (1, N_HEADS)), 1),
                              _split_half(jnp.tile(a_k_gain[l], (1, N_HEADS)), 1),
                              jnp.ones((N_DIL, hdim), F32)], axis=1).reshape(3 * N_DIL, 1, hdim)
            cols = np.arange(N_DIL * 3 * hdim).reshape(N_DIL, 3, hdim)
            cols = np.concatenate([_split_half(cols[:, :2], 2), cols[:, 2:]], axis=1).reshape(-1)
            w_qkv = a_w_qkv[l][:, cols].astype(BF16)
            bound = HEAD_DIM * q_scale * jnp.max(jnp.abs(a_q_gain[l])) * jnp.max(jnp.abs(a_k_gain[l]))
            small = (bound <= SCORE_LIMIT).astype(jnp.int32).reshape(1)
            parts = [_dilated_attention(small, _qkv_proj(h, norm_g3, mod5, w_qkv, gain, bd_split, l, g), g)
                     for g in range(N_DIL)]
            h = _merge_out([p[0] for p in parts], [p[1] for p in parts], h, mod5, a_w_o[l].astype(BF16), l)
        else:
            j = l - n_a
            w_q = b_w_qg[j][:, :hdim]
            w_g = b_w_qg[j][:, hdim:].reshape(d, 3, N_KV_HEADS, GROUP).transpose(0, 2, 1, 3)
            w_g = jnp.pad(w_g.reshape(d, N_KV_HEADS, 3 * GROUP), ((0, 0), (0, 0), (0, LANES - 3 * GROUP)))
            w_qg = jnp.concatenate([w_q, w_g.reshape(d, N_KV_HEADS * LANES)], axis=1).astype(BF16)
            gain = jnp.tile(b_q_gain[j] * q_scale, N_HEADS)[None, :]
            qn, qr, gates = _qg_proj(h, norm_g3, mod5, w_qg, gain, cos, sin, bd, l)
            bound = HEAD_DIM * q_scale * jnp.max(jnp.abs(b_q_gain[j])) * jnp.max(jnp.abs(kv_k_gain))
            small = (bound <= SCORE_LIMIT).astype(jnp.int32).reshape(1)
            o = _nsa_attention(small, qn, qr, gates, *shared)
            h = _out_proj(o, h, mod5, b_w_o[j].astype(BF16), l)
        h = _ffn(h, norm_g3, mod5, w_in, w_out, l, 2, 1)
        if l == n_a - 1:
            col = lambda br, kv, kvh: br * 4 * HEAD_DIM + kv * 2 * HEAD_DIM + kvh * HEAD_DIM + jnp.arange(HEAD_DIM)
            dup = lambda br, kv, kvh: jnp.concatenate([col(br, kv, kvh)] * 2)
            cols = jnp.concatenate(
                [dup(br, 0, kvh) for br in (1, 2) for kvh in range(N_KV_HEADS)]
                + [dup(br, 1, kvh) for br in (1, 2) for kvh in range(N_KV_HEADS)]
                + [col(0, kv, kvh) for kv in range(2) for kvh in range(N_KV_HEADS)])
            w_kv_ext = w_kv[:, cols].astype(BF16)
            kgain = jnp.concatenate([jnp.tile(kv_k_gain[1], (2, 2)), jnp.tile(kv_k_gain[2], (2, 2)),
                                     jnp.ones((4, LANES), F32)], axis=0)
            ks, kw, vs, vw, kvc = _kv_proj(h, kv_norm_g[None, :], kvmod5, w_kv_ext, kgain, cos, sin, bd)
            nsb = s // CMP_STRIDE
            t4 = kvc.reshape(b, nsb, CMP_STRIDE, 4, HEAD_DIM).transpose(3, 0, 1, 2, 4)
            t4 = t4.reshape(4, b, nsb, CMP_STRIDE * HEAD_DIM)
            cmp4 = _compress(t4, phi_w1, phi_w2, cmp_pos.reshape(2, 1, CMP_LEN * HEAD_DIM), kv_k_gain[0][None, :])
            shared = (cmp4, ks, vs, kw, vw, _importance_weights(nsb).T)
    return h
```

```python
import functools

import jax
import jax.numpy as jnp
import numpy as np
from jax import lax
from jax.experimental import pallas as pl
from jax.experimental.pallas import tpu as pltpu

HEAD_DIM = 64
N_HEADS = 16
DIL_WINDOWS = (128, 512, 2048)
DIL_RATES = (1, 4, 16)
N_DIL = 3
N_KV_HEADS = 2
GROUP = N_HEADS // N_KV_HEADS
CMP_STRIDE = 16
CMP_LEN = 32
SEL_BLOCK = 64
SEL_SHIFT = SEL_BLOCK.bit_length() - 1
N_SELECT = 16
SLIDE_WINDOW = 512
ROPE_THETA = 10000.0
EPS = 1e-6

LANES = 128
MXU_WIDTH = 256
Q_ROWS = 128
NEG = -1e30
LOG2E = 1.4426950408889634
SCORE_LIMIT = 60.0
VMEM_LIMIT = 56 * 1024 * 1024

BF16 = jnp.bfloat16
F32 = jnp.float32


def _params(*sem):
    return pltpu.CompilerParams(dimension_semantics=sem, vmem_limit_bytes=VMEM_LIMIT)


def _dot(a, b):
    return jnp.dot(a, b, preferred_element_type=F32)


def _dot_t(a, b):
    return lax.dot_general(a, b, (((1,), (1,)), ((), ())), preferred_element_type=F32)


def _split_dot(a, b):
    a_hi = a.astype(BF16)
    a_lo = (a - a_hi.astype(F32)).astype(BF16)
    b_hi = b.astype(BF16)
    b_lo = (b - b_hi.astype(F32)).astype(BF16)
    return _dot(a_hi, b_hi) + (_dot(a_hi, b_lo) + _dot(a_lo, b_hi))


def _sigmoid(x):
    return 1.0 / (1.0 + jnp.exp(-x))


def _normmod(x, g, shift, scale):
    return x * lax.rsqrt(jnp.mean(x * x, axis=-1, keepdims=True) + EPS) * (g * (1.0 + scale)) + shift


def _head_rms(y, bd):
    ss = _dot((y * y).astype(BF16), bd)
    return y * lax.rsqrt(ss * (1.0 / HEAD_DIM) + EPS)


def _rope(z, cos, sin_signed):
    lane = lax.broadcasted_iota(jnp.int32, z.shape, 1)
    first_half = (lane & (HEAD_DIM - 1)) < HEAD_DIM // 2
    swapped = jnp.where(first_half, pltpu.roll(z, z.shape[1] - HEAD_DIM // 2, 1),
                        pltpu.roll(z, HEAD_DIM // 2, 1))
    return z * cos + swapped * sin_signed


def _ada_kernel(c_ref, w_ref, b_ref, o_ref):
    c = c_ref[...]
    a = (c * _sigmoid(c)).astype(BF16)
    o_ref[...] = _dot(a, w_ref[...].astype(BF16)) + b_ref[...]


def _ada(c8, w, b):
    nl, d, n = w.shape
    tn = 1024
    return pl.pallas_call(
        _ada_kernel,
        grid=(nl, n // tn),
        in_specs=[pl.BlockSpec((8, d), lambda l, j: (0, 0)),
                  pl.BlockSpec((None, d, tn), lambda l, j: (l, 0, j)),
                  pl.BlockSpec((None, 1, tn), lambda l, j: (l, 0, j))],
        out_specs=pl.BlockSpec((None, 8, tn), lambda l, j: (l, 0, j)),
        out_shape=jax.ShapeDtypeStruct((nl, 8, n), F32),
        compiler_params=_params("parallel", "parallel"),
        name="ada_table",
    )(c8, w, b)


def _mod_block(d, l, k):
    return pl.BlockSpec((None, None, None, 1, d), lambda b, *_: (l, b, k, 0, 0))


def _ffn_kernel(h_ref, g_ref, sh_ref, sc_ref, gt_ref, wg_ref, wu_ref, wo_ref, o_ref, u_scr, acc_scr):
    u_scr[...] = _normmod(h_ref[0], g_ref[...], sh_ref[...], sc_ref[...]).astype(BF16)

    u = u_scr[...]
    tf = wg_ref.shape[1]
    bounds = [(lo, min(lo + MXU_WIDTH, tf)) for lo in range(0, tf, MXU_WIDTH)]

    def gate_up(c):
        lo, hi = bounds[c]
        return _dot(u, wg_ref[:, lo:hi]), _dot(u, wu_ref[:, lo:hi])

    nxt = gate_up(0)
    for c, (lo, hi) in enumerate(bounds):
        gate, up = nxt
        if c + 1 < len(bounds):
            nxt = gate_up(c + 1)
        act = (gate * _sigmoid(gate) * up).astype(BF16)
        down = _dot(act, wo_ref[lo:hi, :])
        acc_scr[...] = down if c == 0 else acc_scr[...] + down

    o_ref[0] = h_ref[0] + (0.5 * (1.0 + gt_ref[...])) * acc_scr[...]


def _ffn(h, norm_g3, mod5, w_in, w_out, l, sub, which):
    b, s, d = h.shape
    f = w_out.shape[2]
    tm = min(1024, s)
    resident = pl.Buffered(1)
    return pl.pallas_call(
        _ffn_kernel,
        grid=(b, s // tm),
        in_specs=[pl.BlockSpec((1, tm, d), lambda bb, i: (bb, i, 0)),
                  pl.BlockSpec((None, 1, d), lambda bb, i: (l * 3 + sub, 0, 0)),
                  _mod_block(d, l, sub * 3 + 0),
                  _mod_block(d, l, sub * 3 + 1),
                  _mod_block(d, l, sub * 3 + 2),
                  pl.BlockSpec((None, None, d, f), lambda bb, i: (l, which, 0, 0), pipeline_mode=resident),
                  pl.BlockSpec((None, None, d, f), lambda bb, i: (l, which, 0, 1), pipeline_mode=resident),
                  pl.BlockSpec((None, None, f, d), lambda bb, i: (l, which, 0, 0), pipeline_mode=resident)],
        out_specs=pl.BlockSpec((1, tm, d), lambda bb, i: (bb, i, 0)),
        out_shape=jax.ShapeDtypeStruct((b, s, d), F32),
        scratch_shapes=[pltpu.VMEM((tm, d), BF16), pltpu.VMEM((tm, d), F32)],
        compiler_params=_params("parallel", "parallel"),
        name="swiglu_half_step",
    )(h, norm_g3, mod5, mod5, mod5, w_in, w_in, w_out)


def _qkv_kernel(h_ref, g_ref, sh_ref, sc_ref, w_ref, gain_ref, cos_ref, sin_ref, bd_ref, o_ref, u_scr, uf_scr,
                *, rate):
    tm = u_scr.shape[0]
    seg = tm // rate
    u = _normmod(h_ref[0], g_ref[...], sh_ref[...], sc_ref[...])
    if rate == 1:
        u_scr[...] = u.astype(BF16)
    else:
        for cb in range(u.shape[1] // LANES):
            sl = slice(cb * LANES, (cb + 1) * LANES)
            uf_scr[cb] = u[:, sl]
            for res in range(rate):
                u_scr[res * seg:(res + 1) * seg, sl] = uf_scr[cb, pl.ds(res, seg, stride=rate), :].astype(BF16)

    n_grp = o_ref.shape[1]
    per_kind = n_grp // 3
    cos = cos_ref[...]
    sin = sin_ref[...]
    bd = bd_ref[...]

    def project(grp):
        return _dot(u_scr[...], w_ref[:, grp * MXU_WIDTH:(grp + 1) * MXU_WIDTH])

    y = project(0)
    for grp in range(n_grp):
        y_next = project(grp + 1) if grp + 1 < n_grp else None
        kind, sub = divmod(grp, per_kind)
        if kind < 2:
            z = _head_rms(y, bd) * gain_ref[kind, :, sub * MXU_WIDTH:(sub + 1) * MXU_WIDTH]
            x1, x2 = z[:, :LANES], z[:, LANES:]
            y = jnp.concatenate([x1 * cos - x2 * sin, x1 * sin + x2 * cos], axis=1)
        y = y.astype(BF16)
        for res in range(rate):
            o_ref[res, grp] = y[res * seg:(res + 1) * seg]
        y = y_next


def _qkv_proj(h, norm_g3, mod5, w, gain, bd, l, g):
    b, s, d = h.shape
    rate = DIL_RATES[g]
    tm = min(1024, s)
    seg = tm // rate
    tn = N_HEADS * HEAD_DIM
    n_grp = tn // MXU_WIDTH
    cos, _, sin = _rope_tables(s, tm, rate)

    return pl.pallas_call(
        functools.partial(_qkv_kernel, rate=rate),
        grid=(b, s // tm),
        in_specs=[pl.BlockSpec((1, tm, d), lambda bb, i: (bb, i, 0)),
                  pl.BlockSpec((None, 1, d), lambda bb, i: (l * 3 + 1, 0, 0)),
                  _mod_block(d, l, 3),
                  _mod_block(d, l, 4),
                  pl.BlockSpec((d, 3 * tn), lambda bb, i: (0, g), pipeline_mode=pl.Buffered(1)),
                  pl.BlockSpec((3, 1, tn), lambda bb, i: (g, 0, 0)),
                  pl.BlockSpec((tm, LANES), lambda bb, i: (i, 0)),
                  pl.BlockSpec((tm, LANES), lambda bb, i: (i, 0)),
                  pl.BlockSpec(bd.shape, lambda bb, i: (0, 0))],
        out_specs=pl.BlockSpec((None, rate, 3 * n_grp, seg, MXU_WIDTH), lambda bb, i: (bb, 0, 0, i, 0)),
        out_shape=jax.ShapeDtypeStruct((b, rate, 3 * n_grp, s // rate, MXU_WIDTH), BF16),
        scratch_shapes=[pltpu.VMEM((tm, d), BF16), pltpu.VMEM((d // LANES, tm, LANES), F32)],
        compiler_params=_params("parallel", "parallel"),
        name=f"qkv_proj_rate{rate}",
    )(h, norm_g3, mod5, mod5, w, gain, cos, sin, bd)


def _dil_body(bias_ref, q_ref, kp_ref, kc_ref, vp_ref, vc_ref, o_ref, lse_ref, *, tq, shifted):
    i = pl.program_id(2)
    n_grp = q_ref.shape[0]
    hpg = MXU_WIDTH // HEAD_DIM
    qlane = lax.broadcasted_iota(jnp.int32, (Q_ROWS, MXU_WIDTH), 1)
    q_head = (qlane & (LANES - 1)) // (HEAD_DIM // 2)
    v_head = qlane // HEAD_DIM
    seg = lax.broadcasted_iota(jnp.int32, (Q_ROWS, LANES), 1) // (LANES // N_HEADS)

    def scores(grp, j):
        q = q_ref[grp, j * Q_ROWS:(j + 1) * Q_ROWS, :]
        zero = jnp.zeros_like(q)
        q2 = jnp.concatenate([jnp.where(q_head == t, q, zero) for t in range(hpg)], axis=0)
        if j == 0:
            k = jnp.concatenate([kp_ref[grp], kc_ref[grp, 0:Q_ROWS, :]], axis=0)
        else:
            k = kc_ref[grp, (j - 1) * Q_ROWS:(j + 1) * Q_ROWS, :]
        return _dot_t(q2, k)

    for j in range(tq // Q_ROWS):
        rs = slice(j * Q_ROWS, (j + 1) * Q_ROWS)
        lse_tile = None
        for grp in range(n_grp):
            s = scores(grp, j) + bias_ref[...]
            if j == 0:
                v = jnp.concatenate([vp_ref[grp], vc_ref[grp, 0:Q_ROWS, :]], axis=0)
                prev_key = lax.broadcasted_iota(jnp.int32, s.shape, 1) < Q_ROWS
                s = jnp.where(prev_key & (i == 0), NEG, s)
            else:
                v = vc_ref[grp, (j - 1) * Q_ROWS:(j + 1) * Q_ROWS, :]
            if shifted:
                m = jnp.max(s, axis=-1, keepdims=True)
                p = jnp.exp2(s - m)
            else:
                p = jnp.exp2(s)
            den = jnp.sum(p, axis=-1, keepdims=True)
            o2 = _dot(p.astype(BF16), v) / den
            lse2 = jnp.log(den) * LOG2E
            if shifted:
                lse2 = lse2 + m
            out = o2[0:Q_ROWS]
            for t in range(hpg):
                ts = slice(t * Q_ROWS, (t + 1) * Q_ROWS)
                if t:
                    out = jnp.where(v_head == t, o2[ts], out)
                head_lse = jnp.broadcast_to(lse2[ts], (Q_ROWS, LANES))
                lse_tile = head_lse if lse_tile is None else jnp.where(seg == hpg * grp + t, head_lse, lse_tile)
            o_ref[0, rs, grp * MXU_WIDTH:(grp + 1) * MXU_WIDTH] = out.astype(BF16)
        lse_ref[0, rs, :] = lse_tile


def _dil_kernel(small_ref, *refs, tq):
    @pl.when(small_ref[0] != 0)
    def _():
        _dil_body(*refs, tq=tq, shifted=False)

    @pl.when(small_ref[0] == 0)
    def _():
        _dil_body(*refs, tq=tq, shifted=True)


def _dilated_attention(small, qkv_r, g):
    b, rate, n3, l, width = qkv_r.shape
    s = l * rate
    n_back = DIL_WINDOWS[g] // rate
    assert n_back <= Q_ROWS and rate == DIL_RATES[g] and width == MXU_WIDTH
    tq = min(512, l)
    hdim = N_HEADS * HEAD_DIM
    n_grp = n3 // 3
    sub = tq // Q_ROWS

    def cur(kind):
        return pl.BlockSpec((None, None, n_grp, tq, width), lambda bb, r, i: (bb, r, kind, i, 0))

    def prev(kind):
        return pl.BlockSpec((None, None, n_grp, Q_ROWS, width),
                            lambda bb, r, i: (bb, r, kind, jnp.maximum(i * sub - 1, 0), 0))

    rows = (width // HEAD_DIM) * Q_ROWS
    dist = Q_ROWS + (np.arange(rows)[:, None] % Q_ROWS) - np.arange(2 * Q_ROWS)[None, :]
    bias = np.where((dist >= 0) & (dist <= n_back), 0.0, NEG).astype(np.float32)

    o, lse = pl.pallas_call(
        functools.partial(_dil_kernel, tq=tq),
        grid=(b, rate, l // tq),
        in_specs=[pl.BlockSpec(memory_space=pltpu.SMEM),
                  pl.BlockSpec(bias.shape, lambda bb, r, i: (0, 0)),
                  cur(0), prev(1), cur(1), prev(2), cur(2)],
        out_specs=[pl.BlockSpec((1, None, tq, hdim), lambda bb, r, i: (bb, r, i, 0)),
                   pl.BlockSpec((1, None, tq, LANES), lambda bb, r, i: (bb, r, i, 0))],
        out_shape=[jax.ShapeDtypeStruct((b, rate, l, hdim), BF16),
                   jax.ShapeDtypeStruct((b, rate, l, LANES), F32)],
        compiler_params=_params("parallel", "parallel", "parallel"),
        name=f"dilated_attention_rate{rate}",
    )(small, bias, qkv_r, qkv_r, qkv_r, qkv_r, qkv_r)
    return o, lse


def _merge_out_kernel(o0_ref, o1_ref, o2_ref, l0_ref, l1_ref, l2_ref, p1_ref, p2_ref, e_ref, h_ref, gt_ref, w_ref,
                      out_ref):
    def positions(x_ref, p_ref):
        x = jnp.concatenate([x_ref[res] for res in range(x_ref.shape[0])], axis=0)
        if p_ref is None:
            return x.astype(F32)
        if x.dtype == BF16:
            return _dot(p_ref[...], x)
        hi = x.astype(BF16)
        return _dot(p_ref[...], hi) + _dot(p_ref[...], (x - hi.astype(F32)).astype(BF16))

    l0, l1, l2 = positions(l0_ref, None), positions(l1_ref, p1_ref), positions(l2_ref, p2_ref)
    m = jnp.maximum(jnp.maximum(l0, l1), l2)
    e0, e1, e2 = jnp.exp2(l0 - m), jnp.exp2(l1 - m), jnp.exp2(l2 - m)
    inv = 1.0 / (e0 + e1 + e2)
    spread = e_ref[...]
    o = (_dot((e0 * inv).astype(BF16), spread) * positions(o0_ref, None)
         + _dot((e1 * inv).astype(BF16), spread) * positions(o1_ref, p1_ref)
         + _dot((e2 * inv).astype(BF16), spread) * positions(o2_ref, p2_ref))
    out_ref[0] = h_ref[0] + (1.0 + gt_ref[...]) * _dot(o.astype(BF16), w_ref[...])


def _merge_out(os_, lses, h, mod5, w_o, l):
    b, s, d = h.shape
    hdim = w_o.shape[0]
    tm = min(512, s)
    row = lambda width: pl.BlockSpec((1, tm, width), lambda bb, i: (bb, i, 0))
    rep = LANES // N_HEADS
    lane = np.arange(LANES)[:, None]
    col = np.arange(hdim)[None, :]
    spread = ((lane % rep == 0) & (lane // rep == col // HEAD_DIM)).astype(BF16)

    def unpermute(rate):
        pos = np.arange(tm)
        return (((pos % rate) * (tm // rate) + pos // rate)[:, None] == np.arange(tm)[None, :]).astype(BF16)

    def residue_spec(t):
        rate, width = t.shape[1], t.shape[3]
        return pl.BlockSpec((None, rate, tm // rate, width), lambda bb, i: (bb, 0, i, 0))

    const = lambda shape: pl.BlockSpec(shape, lambda bb, i: (0, 0))
    return pl.pallas_call(
        _merge_out_kernel,
        grid=(b, s // tm),
        in_specs=[residue_spec(t) for t in (*os_, *lses)]
        + [const((tm, tm)), const((tm, tm)), const((LANES, hdim)), row(d), _mod_block(d, l, 5), const((hdim, d))],
        out_specs=row(d),
        out_shape=jax.ShapeDtypeStruct((b, s, d), F32),
        compiler_params=_params("parallel", "parallel"),
        name="merge_out_proj",
    )(*os_, *lses, unpermute(os_[1].shape[1]), unpermute(os_[2].shape[1]), spread, h, mod5, w_o)


def _out_kernel(o_ref, h_ref, gt_ref, w_ref, out_ref):
    out_ref[0] = h_ref[0] + (1.0 + gt_ref[...]) * _dot(o_ref[0], w_ref[...])


def _out_proj(o, h, mod5, w_o, l):
    b, s, d = h.shape
    hdim = w_o.shape[0]
    tm = min(1024, s)
    row = lambda width: pl.BlockSpec((1, tm, width), lambda bb, i: (bb, i, 0))
    return pl.pallas_call(
        _out_kernel,
        grid=(b, s // tm),
        in_specs=[row(hdim), row(d), _mod_block(d, l, 5), pl.BlockSpec((hdim, d), lambda bb, i: (0, 0))],
        out_specs=row(d),
        out_shape=jax.ShapeDtypeStruct((b, s, d), F32),
        compiler_params=_params("parallel", "parallel"),
        name="out_proj",
    )(o, h, mod5, w_o)


def _kv_kernel(h_ref, g_ref, sh_ref, sc_ref, w_ref, gain_ref, cos_ref, sin_ref, bd_ref,
               ks_ref, kw_ref, vs_ref, vw_ref, kvc_ref, *, tm):
    i = pl.program_id(1)
    u = _normmod(h_ref[0], g_ref[...], sh_ref[...], sc_ref[...]).astype(BF16)
    y = _dot(u, w_ref[...])
    cos = cos_ref[...]
    sin = sin_ref[...]
    bd = bd_ref[...]
    lane = lax.broadcasted_iota(jnp.int32, (tm, LANES), 1)
    pos = i * tm + lax.broadcasted_iota(jnp.int32, (tm, LANES), 0)
    onehot = jnp.where((pos >> SEL_SHIFT) == lane, 1.0, 0.0).astype(BF16)
    for j in range(4):
        z = _head_rms(y[:, j * LANES:(j + 1) * LANES], bd) * gain_ref[j:j + 1, :]
        zr = _rope(z, cos, sin).astype(BF16)
        if j < 2:
            ks_ref[0, :, 2 * j * LANES:(2 * j + 1) * LANES] = zr
            ks_ref[0, :, (2 * j + 1) * LANES:(2 * j + 2) * LANES] = onehot
        else:
            kw_ref[0, :, (j - 2) * LANES:(j - 1) * LANES] = zr
    for j in range(4):
        v = y[:, (4 + j) * LANES:(5 + j) * LANES]
        va = jnp.where(lane < HEAD_DIM, v, 1.0).astype(BF16)
        ref = vs_ref if j < 2 else vw_ref
        ref[0, :, (j % 2) * LANES:(j % 2 + 1) * LANES] = va
    kvc_ref[0] = y[:, 8 * LANES:10 * LANES]


def _kv_proj(h, kv_norm_g, kvmod5, w, gain, cos, sin, bd):
    b, s, d = h.shape
    n = w.shape[1]
    tm = min(512, s)
    row = lambda width, dt: (pl.BlockSpec((1, tm, width), lambda bb, i: (bb, i, 0)),
                             jax.ShapeDtypeStruct((b, s, width), dt))
    outs = [row(4 * LANES, BF16), row(2 * LANES, BF16), row(2 * LANES, BF16), row(2 * LANES, BF16),
            row(2 * LANES, F32)]
    return pl.pallas_call(
        functools.partial(_kv_kernel, tm=tm),
        grid=(b, s // tm),
        in_specs=[pl.BlockSpec((1, tm, d), lambda bb, i: (bb, i, 0)),
                  pl.BlockSpec((1, d), lambda bb, i: (0, 0)),
                  _mod_block(d, 0, 0),
                  _mod_block(d, 0, 1),
                  pl.BlockSpec((d, n), lambda bb, i: (0, 0)),
                  pl.BlockSpec((8, LANES), lambda bb, i: (0, 0)),
                  pl.BlockSpec((tm, LANES), lambda bb, i: (i, 0)),
                  pl.BlockSpec((tm, LANES), lambda bb, i: (i, 0)),
                  pl.BlockSpec((LANES, LANES), lambda bb, i: (0, 0))],
        out_specs=[o[0] for o in outs],
        out_shape=[o[1] for o in outs],
        compiler_params=_params("parallel", "parallel"),
        name="shared_kv_proj",
    )(h, kv_norm_g, kvmod5, kvmod5, w, gain, cos, sin, bd)


def _cmp_kernel(t_ref, w1_ref, w2_ref, pos_ref, gain_ref, o_ref):
    is_key = pl.program_id(0) < N_KV_HEADS
    t = t_ref[...]
    w1 = w1_ref[...]
    half = w1.shape[0] // 2
    first = _split_dot(t, w1[:half])
    second = _split_dot(t, w1[half:])
    second = pltpu.roll(second, second.shape[0] - 1, 0)
    posb = _split_dot(jnp.broadcast_to(pos_ref[...], (8, w1.shape[0])), w1)[0:1]
    hid = first + second + posb
    out = _split_dot(hid * _sigmoid(hid), w2_ref[...])
    key = out * lax.rsqrt(jnp.mean(out * out, axis=-1, keepdims=True) + EPS) * gain_ref[...]
    dup = jnp.concatenate([key, key], axis=-1)
    aug = jnp.concatenate([out, jnp.ones_like(out)], axis=-1)
    o_ref[...] = jnp.where(is_key, dup, aug).astype(BF16)


def _compress(t4, phi_w1, phi_w2, cmp_pos, gain):
    _, b, nsb, width = t4.shape
    hid = phi_w1.shape[2]
    return pl.pallas_call(
        _cmp_kernel,
        grid=(4, b),
        in_specs=[pl.BlockSpec((None, None, nsb, width), lambda t, bb: (t, bb, 0, 0)),
                  pl.BlockSpec((None, CMP_LEN * HEAD_DIM, hid), lambda t, bb: (t // N_KV_HEADS, 0, 0)),
                  pl.BlockSpec((None, hid, HEAD_DIM), lambda t, bb: (t // N_KV_HEADS, 0, 0)),
                  pl.BlockSpec((None, 1, CMP_LEN * HEAD_DIM), lambda t, bb: (t // N_KV_HEADS, 0, 0)),
                  pl.BlockSpec((1, HEAD_DIM), lambda t, bb: (0, 0))],
        out_specs=pl.BlockSpec((None, None, nsb, LANES), lambda t, bb: (t, bb, 0, 0)),
        out_shape=jax.ShapeDtypeStruct((4, b, nsb, LANES), BF16),
        compiler_params=_params("parallel", "parallel"),
        name="compress_blocks",
    )(t4, phi_w1, phi_w2, cmp_pos, gain)


def _qg_kernel(h_ref, g_ref, sh_ref, sc_ref, w_ref, gain_ref, cos_ref, sin_ref, bd_ref,
               qn_ref, qr_ref, gate_ref):
    u = _normmod(h_ref[0], g_ref[...], sh_ref[...], sc_ref[...]).astype(BF16)
    y = _dot(u, w_ref[...])
    cos = cos_ref[...]
    sin = sin_ref[...]
    bd = bd_ref[...]
    hdim = qn_ref.shape[2]
    for cb in range(hdim // LANES):
        sl = slice(cb * LANES, (cb + 1) * LANES)
        z = _head_rms(y[:, sl], bd) * gain_ref[:, sl]
        qn_ref[0, :, sl] = z.astype(BF16)
        qr_ref[0, :, sl] = _rope(z, cos, sin).astype(BF16)
    gate_ref[0] = _sigmoid(y[:, hdim:])


def _qg_proj(h, norm_g3, mod5, w, gain, cos, sin, bd, l):
    b, s, d = h.shape
    n = w.shape[1]
    hdim = N_HEADS * HEAD_DIM
    tm = min(512, s)
    row = lambda width, dt: (pl.BlockSpec((1, tm, width), lambda bb, i: (bb, i, 0)),
                             jax.ShapeDtypeStruct((b, s, width), dt))
    outs = [row(hdim, BF16), row(hdim, BF16), row(n - hdim, F32)]
    return pl.pallas_call(
        _qg_kernel,
        grid=(b, s // tm),
        in_specs=[pl.BlockSpec((1, tm, d), lambda bb, i: (bb, i, 0)),
                  pl.BlockSpec((None, 1, d), lambda bb, i: (l * 3 + 1, 0, 0)),
                  _mod_block(d, l, 3),
                  _mod_block(d, l, 4),
                  pl.BlockSpec((d, n), lambda bb, i: (0, 0)),
                  pl.BlockSpec((1, hdim), lambda bb, i: (0, 0)),
                  pl.BlockSpec((tm, LANES), lambda bb, i: (i, 0)),
                  pl.BlockSpec((tm, LANES), lambda bb, i: (i, 0)),
                  pl.BlockSpec((LANES, LANES), lambda bb, i: (0, 0))],
        out_specs=[o[0] for o in outs],
        out_shape=[o[1] for o in outs],
        compiler_params=_params("parallel", "parallel"),
        name="nsa_query_gate_proj",
    )(h, norm_g3, mod5, mod5, w, gain, cos, sin, bd)


def _nsa_body(qn_ref, qr_ref, gate_ref, kc_ref, vc_ref, ks_ref, vs_ref, kw_ref, vw_ref, wimpt_ref, gsel_ref,
              o_ref, qa_scr, m_scr, acc_scr, mw_scr, accw_scr, *, tq, shifted):
    i = pl.program_id(2)
    start = i * tq
    rows = GROUP * tq
    ncp = kc_ref.shape[0]
    low = lax.broadcasted_iota(jnp.int32, (tq, LANES), 1) < HEAD_DIM
    low_r = lax.broadcasted_iota(jnp.int32, (rows, LANES), 1) < HEAD_DIM

    def accumulate(s, v, mask, first, acc_ref=acc_scr, m_ref=m_scr):
        if not shifted:
            p = jnp.exp2(s)
            if mask is not None:
                p = jnp.where(mask, p, 0.0)
            pv = _dot(p.astype(BF16), v)
            acc_ref[...] = pv if first else acc_ref[...] + pv
            return
        if mask is not None:
            s = jnp.where(mask, s, NEG)
        if first:
            m_new = jnp.max(s, axis=-1, keepdims=True)
            acc_ref[...] = _dot(jnp.exp2(s - m_new).astype(BF16), v)
        else:
            m_old = m_ref[...]
            m_new = jnp.maximum(m_old, jnp.max(s, axis=-1, keepdims=True))
            acc_ref[...] = jnp.exp2(m_old - m_new) * acc_ref[...] + _dot(jnp.exp2(s - m_new).astype(BF16), v)
        m_ref[...] = m_new

    def stack(q):
        parts = []
        for g in range(GROUP):
            blk = q[:, (g // 2) * LANES:(g // 2 + 1) * LANES]
            parts.append(jnp.where(low if g % 2 == 0 else ~low, blk, jnp.zeros_like(blk)))
        return parts

    qn = jnp.concatenate(stack(qn_ref[0]), axis=0)
    sc = _dot_t(qn, kc_ref[...])
    c_end = lax.broadcasted_iota(jnp.int32, (tq, ncp), 1) * CMP_STRIDE + (CMP_LEN - 1)
    valid_t = c_end <= start + lax.broadcasted_iota(jnp.int32, (tq, ncp), 0)
    if shifted:
        valid = jnp.concatenate([valid_t] * GROUP, axis=0)
        sc = jnp.where(valid, sc, NEG)
        sc = sc - jnp.max(sc, axis=-1, keepdims=True)
        p = jnp.where(valid, jnp.exp2(sc), 0.0)
    else:
        p = jnp.exp2(sc + jnp.concatenate([jnp.where(valid_t, 0.0, NEG)] * GROUP, axis=0))
    oc = _dot(p.astype(BF16), vc_ref[...])
    inv = 1.0 / jnp.maximum(jnp.where(low_r, pltpu.roll(oc, HEAD_DIM, 1), oc), 1e-30)
    o_cmp = oc * inv
    psum_blocks = []
    for cb in range(ncp // LANES):
        cs = slice(cb * LANES, (cb + 1) * LANES)
        tot = p[0:tq, cs] * inv[0:tq]
        for g in range(1, GROUP):
            tot = tot + p[g * tq:(g + 1) * tq, cs] * inv[g * tq:(g + 1) * tq]
        psum_blocks.append(tot)
    psum = jnp.concatenate(psum_blocks, axis=1)

    qr_parts = stack(qr_ref[0])
    for g in range(GROUP):
        qa_scr[g * tq:(g + 1) * tq, 0:LANES] = qr_parts[g]
    qr = jnp.concatenate(qr_parts, axis=0)

    tri = (lax.broadcasted_iota(jnp.int32, (rows, tq), 1)
           <= (lax.broadcasted_iota(jnp.int32, (rows, tq), 0) & (tq - 1)))

    def key_block(kb):
        return pl.ds(pl.multiple_of(kb * tq, tq), tq)

    diag = key_block(i)
    accumulate(_dot_t(qr, kw_ref[diag, :]), vw_ref[diag, :], tri, True, accw_scr, mw_scr)

    p_hi = psum.astype(BF16)
    p_lo = (psum - p_hi.astype(F32)).astype(BF16)
    imp_t = _dot_t(wimpt_ref[...], p_hi) + _dot_t(wimpt_ref[...], p_lo)

    blk = key_block(jnp.maximum(i - 1, 0))
    accumulate(_dot_t(qr, kw_ref[blk, :]), vw_ref[blk, :], jnp.broadcast_to(i >= 1, tri.shape), False,
               accw_scr, mw_scr)
    blk = key_block(jnp.maximum(i - 2, 0))
    accumulate(_dot_t(qr, kw_ref[blk, :]), vw_ref[blk, :], (~tri) & (i >= 2), False, accw_scr, mw_scr)

    blk_id = lax.broadcasted_iota(jnp.int32, (LANES, tq), 0)
    jt = (start + lax.broadcasted_iota(jnp.int32, (LANES, tq), 1)) >> SEL_SHIFT
    forced = (blk_id == 0) | (blk_id == jt) | (blk_id == jt - 1)
    work = jnp.where(blk_id > jt, -1.0, jnp.where(forced, -2.0, imp_t))
    sel = forced
    for _ in range(N_SELECT - 3):
        mx = jnp.max(work, axis=0, keepdims=True)
        idx = jnp.min(jnp.where(work == mx, blk_id, LANES), axis=0, keepdims=True)
        pick = blk_id == idx
        sel = sel | pick
        work = jnp.where(pick, -2.0, work)
    sel_bias = jnp.transpose(jnp.where(sel, 0.0, NEG)).astype(BF16)

    for g in range(GROUP):
        qa_scr[g * tq:(g + 1) * tq, LANES:2 * LANES] = sel_bias

    accumulate(_dot_t(qa_scr[...], ks_ref[diag, :]), vs_ref[diag, :], tri, True)

    def slc_pair(kp, carry):
        blk = pl.ds(pl.multiple_of(kp * 2 * tq, 2 * tq), 2 * tq)
        accumulate(_dot_t(qa_scr[...], ks_ref[blk, :]), vs_ref[blk, :], None, False)
        return carry

    lax.fori_loop(0, i // 2, slc_pair, 0)

    @pl.when(i % 2 == 1)
    def _():
        blk = key_block(i - 1)
        accumulate(_dot_t(qa_scr[...], ks_ref[blk, :]), vs_ref[blk, :], None, False)

    gates = gate_ref[0]
    g_hi = gates.astype(BF16)
    g_lo = (gates - g_hi.astype(F32)).astype(BF16)
    gate_wide = _dot(g_hi, gsel_ref[...]) + _dot(g_lo, gsel_ref[...])
    for pr in range(GROUP // 2):
        ev = slice(2 * pr * tq, (2 * pr + 1) * tq)
        od = slice((2 * pr + 1) * tq, (2 * pr + 2) * tq)

        def pair(acc):
            a_e, a_o = acc[ev], acc[od]
            r_e = pltpu.roll(a_e, HEAD_DIM, 1)
            r_o = pltpu.roll(a_o, HEAD_DIM, 1)
            return jnp.where(low, a_e, r_o) / jnp.where(low, r_e, a_o)

        def gate(br):
            c = (br * (GROUP // 2) + pr) * LANES
            return gate_wide[:, c:c + LANES]

        o_c = jnp.where(low, o_cmp[ev], pltpu.roll(o_cmp[od], HEAD_DIM, 1))
        out = gate(0) * o_c + gate(1) * pair(acc_scr) + gate(2) * pair(accw_scr)
        o_ref[0, :, pr * LANES:(pr + 1) * LANES] = out.astype(BF16)


def _nsa_kernel(small_ref, *refs, tq):
    @pl.when(small_ref[0] != 0)
    def _():
        _nsa_body(*refs, tq=tq, shifted=False)

    @pl.when(small_ref[0] == 0)
    def _():
        _nsa_body(*refs, tq=tq, shifted=True)


def _nsa_attention(small, qn, qr, gates, cmp4, ks, vs, kw, vw, wimpt):
    b, s, hdim = qn.shape
    tq = min(256, s)
    assert s // SEL_BLOCK <= LANES and s % tq == 0 and SLIDE_WINDOW == 2 * tq
    ncp = cmp4.shape[2]
    gw = GROUP * HEAD_DIM
    rows = GROUP * tq
    qspec = pl.BlockSpec((1, tq, gw), lambda bb, kv, i: (bb, i, kv))
    full = lambda width: pl.BlockSpec((None, s, width), lambda bb, kv, i: (bb, 0, kv))
    col = np.arange(3 * (GROUP // 2) * LANES)[None, :]
    src = (col // LANES // (GROUP // 2)) * GROUP + 2 * ((col // LANES) % (GROUP // 2)) + (col % LANES) // HEAD_DIM
    gsel = (np.arange(LANES)[:, None] == src).astype(BF16)
    return pl.pallas_call(
        functools.partial(_nsa_kernel, tq=tq),
        grid=(b, N_KV_HEADS, s // tq),
        in_specs=[pl.BlockSpec(memory_space=pltpu.SMEM), qspec, qspec,
                  pl.BlockSpec((1, tq, LANES), lambda bb, kv, i: (bb, i, kv)),
                  pl.BlockSpec((None, None, ncp, LANES), lambda bb, kv, i: (kv, bb, 0, 0)),
                  pl.BlockSpec((None, None, ncp, LANES), lambda bb, kv, i: (N_KV_HEADS + kv, bb, 0, 0)),
                  full(2 * LANES), full(LANES), full(LANES), full(LANES),
                  pl.BlockSpec((LANES, ncp), lambda bb, kv, i: (0, 0)),
                  pl.BlockSpec(gsel.shape, lambda bb, kv, i: (0, 0))],
        out_specs=qspec,
        out_shape=jax.ShapeDtypeStruct((b, s, hdim), BF16),
        scratch_shapes=[pltpu.VMEM((rows, 2 * LANES), BF16),
                        pltpu.VMEM((rows, 1), F32), pltpu.VMEM((rows, LANES), F32),
                        pltpu.VMEM((rows, 1), F32), pltpu.VMEM((rows, LANES), F32)],
        compiler_params=_params("parallel", "parallel", "arbitrary"),
        name="nsa_attention",
    )(small, qn, qr, gates, cmp4, cmp4, ks, vs, kw, vw, wimpt, gsel)


def _rope_tables(s, tm=None, rate=1):
    half = HEAD_DIM // 2
    row = lax.broadcasted_iota(jnp.int32, (s, LANES), 0)
    lane = lax.broadcasted_iota(jnp.int32, (s, LANES), 1)
    if rate > 1:
        seg = tm // rate
        local = row % tm
        row = row - local + (local % seg) * rate + local // seg
    inv = jnp.tile(ROPE_THETA ** (-jnp.arange(half, dtype=F32) / half), LANES // half)
    ang = row.astype(F32) * inv[None, :]
    cos, sin = jnp.cos(ang), jnp.sin(ang)
    return cos, jnp.where(lane % HEAD_DIM < half, -sin, sin), sin


def _split_half(t, axis):
    hpg = MXU_WIDTH // HEAD_DIM
    half = HEAD_DIM // 2
    shape = t.shape
    t = t.reshape(shape[:axis] + (shape[axis] // MXU_WIDTH, hpg, 2, half) + shape[axis + 1:])
    return t.swapaxes(axis + 1, axis + 2).reshape(shape)


def _importance_weights(ncp):
    r = SEL_BLOCK // CMP_STRIDE
    nper = CMP_LEN // CMP_STRIDE
    o = np.arange(ncp)[:, None] - r * np.arange(LANES)[None, :]
    w = np.zeros((ncp, LANES), np.float32)
    for off in range(-(nper - 1), r):
        cnt = sum(1 for m in range(r) for n in range(nper) if m - n == off)
        w += np.where(o == off, float(cnt), 0.0)
    return w.astype(BF16)


def kernel(x, c, norm_g, w_ada, b_ada, ffn_w_in, ffn_w_out, a_w_qkv, a_q_gain, a_k_gain, a_w_o, kv_norm_g,
           w_ada_kv, b_ada_kv, w_kv, kv_k_gain, cmp_pos, phi_w1, phi_w2, b_w_qg, b_q_gain, b_w_o):
    b, s, d = x.shape
    depth = norm_g.shape[0]
    n_a = depth // 2
    hdim = N_HEADS * HEAD_DIM
    assert depth == 2 and b <= 8

    c8 = jnp.zeros((8, d), F32).at[:b].set(c)
    mod = _ada(c8, w_ada, b_ada.reshape(depth, 1, 9 * d))
    mod5 = mod[:, :b].reshape(depth, b, 9, 1, d)
    kvmod = _ada(c8, w_ada_kv[None], b_ada_kv.reshape(1, 1, 2 * d))
    kvmod5 = kvmod[:, :b].reshape(1, b, 2, 1, d)

    norm_g3 = norm_g.reshape(depth * 3, 1, d)
    w_in = ffn_w_in.astype(BF16)
    w_out = ffn_w_out.astype(BF16)
    cos, sin, _ = _rope_tables(s)
    blk = np.arange(LANES) // HEAD_DIM
    bd = (blk[:, None] == blk[None, :]).astype(BF16)
    blk = (np.arange(MXU_WIDTH) % LANES) // (HEAD_DIM // 2)
    bd_split = (blk[:, None] == blk[None, :]).astype(BF16)
    q_scale = HEAD_DIM ** -0.5 * LOG2E

    h = x
    for l in range(depth):
        h = _ffn(h, norm_g3, mod5, w_in, w_out, l, 0, 0)
        if l < n_a:
            gain = jnp.stack([_split_half(jnp.tile(a_q_gain[l] * q_scale, (1, N_HEADS)), 1),
                              _split_half(jnp.tile(a_k_gain[l], (1, N_HEADS)), 1),
                              jnp.ones((N_DIL, hdim), F32)], axis=1).reshape(3 * N_DIL, 1, hdim)
            cols = np.arange(N_DIL * 3 * hdim).reshape(N_DIL, 3, hdim)
            cols = np.concatenate([_split_half(cols[:, :2], 2), cols[:, 2:]], axis=1).reshape(-1)
            w_qkv = a_w_qkv[l][:, cols].astype(BF16)
            bound = HEAD_DIM * q_scale * jnp.max(jnp.abs(a_q_gain[l])) * jnp.max(jnp.abs(a_k_gain[l]))
            small = (bound <= SCORE_LIMIT).astype(jnp.int32).reshape(1)
            parts = [_dilated_attention(small, _qkv_proj(h, norm_g3, mod5, w_qkv, gain, bd_split, l, g), g)
                     for g in range(N_DIL)]
            h = _merge_out([p[0] for p in parts], [p[1] for p in parts], h, mod5, a_w_o[l].astype(BF16), l)
        else:
            j = l - n_a
            w_q = b_w_qg[j][:, :hdim]
            w_g = b_w_qg[j][:, hdim:].reshape(d, 3, N_KV_HEADS, GROUP).transpose(0, 2, 1, 3)
            w_g = jnp.pad(w_g.reshape(d, N_KV_HEADS, 3 * GROUP), ((0, 0), (0, 0), (0, LANES - 3 * GROUP)))
            w_qg = jnp.concatenate([w_q, w_g.reshape(d, N_KV_HEADS * LANES)], axis=1).astype(BF16)
            gain = jnp.tile(b_q_gain[j] * q_scale, N_HEADS)[None, :]
            qn, qr, gates = _qg_proj(h, norm_g3, mod5, w_qg, gain, cos, sin, bd, l)
            bound = HEAD_DIM * q_scale * jnp.max(jnp.abs(b_q_gain[j])) * jnp.max(jnp.abs(kv_k_gain))
            small = (bound <= SCORE_LIMIT).astype(jnp.int32).reshape(1)
            o = _nsa_attention(small, qn, qr, gates, *shared)
            h = _out_proj(o, h, mod5, b_w_o[j].astype(BF16), l)
        h = _ffn(h, norm_g3, mod5, w_in, w_out, l, 2, 1)
        if l == n_a - 1:
            col = lambda br, kv, kvh: br * 4 * HEAD_DIM + kv * 2 * HEAD_DIM + kvh * HEAD_DIM + jnp.arange(HEAD_DIM)
            dup = lambda br, kv, kvh: jnp.concatenate([col(br, kv, kvh)] * 2)
            cols = jnp.concatenate(
                [dup(br, 0, kvh) for br in (1, 2) for kvh in range(N_KV_HEADS)]
                + [dup(br, 1, kvh) for br in (1, 2) for kvh in range(N_KV_HEADS)]
                + [col(0, kv, kvh) for kv in range(2) for kvh in range(N_KV_HEADS)])
            w_kv_ext = w_kv[:, cols].astype(BF16)
            kgain = jnp.concatenate([jnp.tile(kv_k_gain[1], (2, 2)), jnp.tile(kv_k_gain[2], (2, 2)),
                                     jnp.ones((4, LANES), F32)], axis=0)
            ks, kw, vs, vw, kvc = _kv_proj(h, kv_norm_g[None, :], kvmod5, w_kv_ext, kgain, cos, sin, bd)
            nsb = s // CMP_STRIDE
            t4 = kvc.reshape(b, nsb, CMP_STRIDE, 4, HEAD_DIM).transpose(3, 0, 1, 2, 4)
            t4 = t4.reshape(4, b, nsb, CMP_STRIDE * HEAD_DIM)
            cmp4 = _compress(t4, phi_w1, phi_w2, cmp_pos.reshape(2, 1, CMP_LEN * HEAD_DIM), kv_k_gain[0][None, :])
            shared = (cmp4, ks, vs, kw, vw, _importance_weights(nsb).T)
    return h
```

```python
import functools

import jax
import jax.numpy as jnp
import numpy as np
from jax import lax
from jax.experimental import pallas as pl
from jax.experimental.pallas import tpu as pltpu

HEAD_DIM = 64
N_HEADS = 16
DIL_WINDOWS = (128, 512, 2048)
DIL_RATES = (1, 4, 16)
N_DIL = 3
N_KV_HEADS = 2
GROUP = N_HEADS // N_KV_HEADS
CMP_STRIDE = 16
CMP_LEN = 32
SEL_BLOCK = 64
SEL_SHIFT = SEL_BLOCK.bit_length() - 1
N_SELECT = 16
SLIDE_WINDOW = 512
ROPE_THETA = 10000.0
EPS = 1e-6

LANES = 128
MXU_WIDTH = 256
Q_ROWS = 128
NEG = -1e30
LOG2E = 1.4426950408889634
SCORE_LIMIT = 60.0
VMEM_LIMIT = 56 * 1024 * 1024

BF16 = jnp.bfloat16
F32 = jnp.float32


def _params(*sem):
    return pltpu.CompilerParams(dimension_semantics=sem, vmem_limit_bytes=VMEM_LIMIT)


def _dot(a, b):
    return jnp.dot(a, b, preferred_element_type=F32)


def _dot_t(a, b):
    return lax.dot_general(a, b, (((1,), (1,)), ((), ())), preferred_element_type=F32)


def _split_dot(a, b):
    a_hi = a.astype(BF16)
    a_lo = (a - a_hi.astype(F32)).astype(BF16)
    b_hi = b.astype(BF16)
    b_lo = (b - b_hi.astype(F32)).astype(BF16)
    return _dot(a_hi, b_hi) + (_dot(a_hi, b_lo) + _dot(a_lo, b_hi))


def _sigmoid(x):
    return 1.0 / (1.0 + jnp.exp(-x))


def _normmod(x, g, shift, scale):
    return x * lax.rsqrt(jnp.mean(x * x, axis=-1, keepdims=True) + EPS) * (g * (1.0 + scale)) + shift


def _head_rms(y, bd):
    ss = _dot((y * y).astype(BF16), bd)
    return y * lax.rsqrt(ss * (1.0 / HEAD_DIM) + EPS)


def _rope(z, cos, sin_signed):
    lane = lax.broadcasted_iota(jnp.int32, z.shape, 1)
    first_half = (lane & (HEAD_DIM - 1)) < HEAD_DIM // 2
    swapped = jnp.where(first_half, pltpu.roll(z, z.shape[1] - HEAD_DIM // 2, 1),
                        pltpu.roll(z, HEAD_DIM // 2, 1))
    return z * cos + swapped * sin_signed


def _ada_kernel(c_ref, w_ref, b_ref, o_ref):
    c = c_ref[...]
    a = (c * _sigmoid(c)).astype(BF16)
    o_ref[...] = _dot(a, w_ref[...].astype(BF16)) + b_ref[...]


def _ada(c8, w, b):
    nl, d, n = w.shape
    tn = 1024
    return pl.pallas_call(
        _ada_kernel,
        grid=(nl, n // tn),
        in_specs=[pl.BlockSpec((8, d), lambda l, j: (0, 0)),
                  pl.BlockSpec((None, d, tn), lambda l, j: (l, 0, j)),
                  pl.BlockSpec((None, 1, tn), lambda l, j: (l, 0, j))],
        out_specs=pl.BlockSpec((None, 8, tn), lambda l, j: (l, 0, j)),
        out_shape=jax.ShapeDtypeStruct((nl, 8, n), F32),
        compiler_params=_params("parallel", "parallel"),
        name="ada_table",
    )(c8, w, b)


def _mod_block(d, l, k):
    return pl.BlockSpec((None, None, None, 1, d), lambda b, *_: (l, b, k, 0, 0))


def _ffn_kernel(h_ref, g_ref, sh_ref, sc_ref, gt_ref, wg_ref, wu_ref, wo_ref, o_ref, u_scr, acc_scr):
    u_scr[...] = _normmod(h_ref[0], g_ref[...], sh_ref[...], sc_ref[...]).astype(BF16)

    u = u_scr[...]
    tf = wg_ref.shape[1]
    bounds = [(lo, min(lo + MXU_WIDTH, tf)) for lo in range(0, tf, MXU_WIDTH)]

    def gate_up(c):
        lo, hi = bounds[c]
        return _dot(u, wg_ref[:, lo:hi]), _dot(u, wu_ref[:, lo:hi])

    nxt = gate_up(0)
    for c, (lo, hi) in enumerate(bounds):
        gate, up = nxt
        if c + 1 < len(bounds):
            nxt = gate_up(c + 1)
        act = (gate * _sigmoid(gate) * up).astype(BF16)
        down = _dot(act, wo_ref[lo:hi, :])
        acc_scr[...] = down if c == 0 else acc_scr[...] + down

    o_ref[0] = h_ref[0] + (0.5 * (1.0 + gt_ref[...])) * acc_scr[...]


def _ffn(h, norm_g3, mod5, w_in, w_out, l, sub, which):
    b, s, d = h.shape
    f = w_out.shape[2]
    tm = min(1024, s)
    resident = pl.Buffered(1)
    return pl.pallas_call(
        _ffn_kernel,
        grid=(b, s // tm),
        in_specs=[pl.BlockSpec((1, tm, d), lambda bb, i: (bb, i, 0)),
                  pl.BlockSpec((None, 1, d), lambda bb, i: (l * 3 + sub, 0, 0)),
                  _mod_block(d, l, sub * 3 + 0),
                  _mod_block(d, l, sub * 3 + 1),
                  _mod_block(d, l, sub * 3 + 2),
                  pl.BlockSpec((None, None, d, f), lambda bb, i: (l, which, 0, 0), pipeline_mode=resident),
                  pl.BlockSpec((None, None, d, f), lambda bb, i: (l, which, 0, 1), pipeline_mode=resident),
                  pl.BlockSpec((None, None, f, d), lambda bb, i: (l, which, 0, 0), pipeline_mode=resident)],
        out_specs=pl.BlockSpec((1, tm, d), lambda bb, i: (bb, i, 0)),
        out_shape=jax.ShapeDtypeStruct((b, s, d), F32),
        scratch_shapes=[pltpu.VMEM((tm, d), BF16), pltpu.VMEM((tm, d), F32)],
        compiler_params=_params("parallel", "parallel"),
        name="swiglu_half_step",
    )(h, norm_g3, mod5, mod5, mod5, w_in, w_in, w_out)


def _qkv_kernel(h_ref, g_ref, sh_ref, sc_ref, w_ref, gain_ref, cos_ref, sin_ref, bd_ref, o_ref, u_scr, uf_scr,
                *, rate):
    tm = u_scr.shape[0]
    seg = tm // rate
    u = _normmod(h_ref[0], g_ref[...], sh_ref[...], sc_ref[...])
    if rate == 1:
        u_scr[...] = u.astype(BF16)
    else:
        for cb in range(u.shape[1] // LANES):
            sl = slice(cb * LANES, (cb + 1) * LANES)
            uf_scr[cb] = u[:, sl]
            for res in range(rate):
                u_scr[res * seg:(res + 1) * seg, sl] = uf_scr[cb, pl.ds(res, seg, stride=rate), :].astype(BF16)

    n_grp = o_ref.shape[1]
    per_kind = n_grp // 3
    cos = cos_ref[...]
    sin = sin_ref[...]
    bd = bd_ref[...]

    def project(grp):
        return _dot(u_scr[...], w_ref[:, grp * MXU_WIDTH:(grp + 1) * MXU_WIDTH])

    y = project(0)
    for grp in range(n_grp):
        y_next = project(grp + 1) if grp + 1 < n_grp else None
        kind, sub = divmod(grp, per_kind)
        if kind < 2:
            z = _head_rms(y, bd) * gain_ref[kind, :, sub * MXU_WIDTH:(sub + 1) * MXU_WIDTH]
            x1, x2 = z[:, :LANES], z[:, LANES:]
            y = jnp.concatenate([x1 * cos - x2 * sin, x1 * sin + x2 * cos], axis=1)
        y = y.astype(BF16)
        for res in range(rate):
            o_ref[res, grp] = y[res * seg:(res + 1) * seg]
        y = y_next


def _qkv_proj(h, norm_g3, mod5, w, gain, bd, l, g):
    b, s, d = h.shape
    rate = DIL_RATES[g]
    tm = min(1024, s)
    seg = tm // rate
    tn = N_HEADS * HEAD_DIM
    n_grp = tn // MXU_WIDTH
    cos, _, sin = _rope_tables(s, tm, rate)

    return pl.pallas_call(
        functools.partial(_qkv_kernel, rate=rate),
        grid=(b, s // tm),
        in_specs=[pl.BlockSpec((1, tm, d), lambda bb, i: (bb, i, 0)),
                  pl.BlockSpec((None, 1, d), lambda bb, i: (l * 3 + 1, 0, 0)),
                  _mod_block(d, l, 3),
                  _mod_block(d, l, 4),
                  pl.BlockSpec((d, 3 * tn), lambda bb, i: (0, g), pipeline_mode=pl.Buffered(1)),
                  pl.BlockSpec((3, 1, tn), lambda bb, i: (g, 0, 0)),
                  pl.BlockSpec((tm, LANES), lambda bb, i: (i, 0)),
                  pl.BlockSpec((tm, LANES), lambda bb, i: (i, 0)),
                  pl.BlockSpec(bd.shape, lambda bb, i: (0, 0))],
        out_specs=pl.BlockSpec((None, rate, 3 * n_grp, seg, MXU_WIDTH), lambda bb, i: (bb, 0, 0, i, 0)),
        out_shape=jax.ShapeDtypeStruct((b, rate, 3 * n_grp, s // rate, MXU_WIDTH), BF16),
        scratch_shapes=[pltpu.VMEM((tm, d), BF16), pltpu.VMEM((d // LANES, tm, LANES), F32)],
        compiler_params=_params("parallel", "parallel"),
        name=f"qkv_proj_rate{rate}",
    )(h, norm_g3, mod5, mod5, w, gain, cos, sin, bd)


def _dil_body(bias_ref, q_ref, kp_ref, kc_ref, vp_ref, vc_ref, o_ref, lse_ref, *, tq, shifted):
    i = pl.program_id(2)
    n_grp = q_ref.shape[0]
    hpg = MXU_WIDTH // HEAD_DIM
    qlane = lax.broadcasted_iota(jnp.int32, (Q_ROWS, MXU_WIDTH), 1)
    q_head = (qlane & (LANES - 1)) // (HEAD_DIM // 2)
    v_head = qlane // HEAD_DIM
    seg = lax.broadcasted_iota(jnp.int32, (Q_ROWS, LANES), 1) // (LANES // N_HEADS)

    def scores(grp, j):
        q = q_ref[grp, j * Q_ROWS:(j + 1) * Q_ROWS, :]
        zero = jnp.zeros_like(q)
        q2 = jnp.concatenate([jnp.where(q_head == t, q, zero) for t in range(hpg)], axis=0)
        if j == 0:
            k = jnp.concatenate([kp_ref[grp], kc_ref[grp, 0:Q_ROWS, :]], axis=0)
        else:
            k = kc_ref[grp, (j - 1) * Q_ROWS:(j + 1) * Q_ROWS, :]
        return _dot_t(q2, k)

    for j in range(tq // Q_ROWS):
        rs = slice(j * Q_ROWS, (j + 1) * Q_ROWS)
        lse_tile = None
        for grp in range(n_grp):
            s = scores(grp, j) + bias_ref[...]
            if j == 0:
                v = jnp.concatenate([vp_ref[grp], vc_ref[grp, 0:Q_ROWS, :]], axis=0)
                prev_key = lax.broadcasted_iota(jnp.int32, s.shape, 1) < Q_ROWS
                s = jnp.where(prev_key & (i == 0), NEG, s)
            else:
                v = vc_ref[grp, (j - 1) * Q_ROWS:(j + 1) * Q_ROWS, :]
            if shifted:
                m = jnp.max(s, axis=-1, keepdims=True)
                p = jnp.exp2(s - m)
            else:
                p = jnp.exp2(s)
            den = jnp.sum(p, axis=-1, keepdims=True)
            o2 = _dot(p.astype(BF16), v) / den
            lse2 = jnp.log(den) * LOG2E
            if shifted:
                lse2 = lse2 + m
            out = o2[0:Q_ROWS]
            for t in range(hpg):
                ts = slice(t * Q_ROWS, (t + 1) * Q_ROWS)
                if t:
                    out = jnp.where(v_head == t, o2[ts], out)
                head_lse = jnp.broadcast_to(lse2[ts], (Q_ROWS, LANES))
                lse_tile = head_lse if lse_tile is None else jnp.where(seg == hpg * grp + t, head_lse, lse_tile)
            o_ref[0, rs, grp * MXU_WIDTH:(grp + 1) * MXU_WIDTH] = out.astype(BF16)
        lse_ref[0, rs, :] = lse_tile


def _dil_kernel(small_ref, *refs, tq):
    @pl.when(small_ref[0] != 0)
    def _():
        _dil_body(*refs, tq=tq, shifted=False)

    @pl.when(small_ref[0] == 0)
    def _():
        _dil_body(*refs, tq=tq, shifted=True)


def _dilated_attention(small, qkv_r, g):
    b, rate, n3, l, width = qkv_r.shape
    s = l * rate
    n_back = DIL_WINDOWS[g] // rate
    assert n_back <= Q_ROWS and rate == DIL_RATES[g] and width == MXU_WIDTH
    tq = min(512, l)
    hdim = N_HEADS * HEAD_DIM
    n_grp = n3 // 3
    sub = tq // Q_ROWS

    def cur(kind):
        return pl.BlockSpec((None, None, n_grp, tq, width), lambda bb, r, i: (bb, r, kind, i, 0))

    def prev(kind):
        return pl.BlockSpec((None, None, n_grp, Q_ROWS, width),
                            lambda bb, r, i: (bb, r, kind, jnp.maximum(i * sub - 1, 0), 0))

    rows = (width // HEAD_DIM) * Q_ROWS
    dist = Q_ROWS + (np.arange(rows)[:, None] % Q_ROWS) - np.arange(2 * Q_ROWS)[None, :]
    bias = np.where((dist >= 0) & (dist <= n_back), 0.0, NEG).astype(np.float32)

    o, lse = pl.pallas_call(
        functools.partial(_dil_kernel, tq=tq),
        grid=(b, rate, l // tq),
        in_specs=[pl.BlockSpec(memory_space=pltpu.SMEM),
                  pl.BlockSpec(bias.shape, lambda bb, r, i: (0, 0)),
                  cur(0), prev(1), cur(1), prev(2), cur(2)],
        out_specs=[pl.BlockSpec((1, None, tq, hdim), lambda bb, r, i: (bb, r, i, 0)),
                   pl.BlockSpec((1, None, tq, LANES), lambda bb, r, i: (bb, r, i, 0))],
        out_shape=[jax.ShapeDtypeStruct((b, rate, l, hdim), BF16),
                   jax.ShapeDtypeStruct((b, rate, l, LANES), F32)],
        compiler_params=_params("parallel", "parallel", "parallel"),
        name=f"dilated_attention_rate{rate}",
    )(small, bias, qkv_r, qkv_r, qkv_r, qkv_r, qkv_r)
    return o, lse


def _merge_out_kernel(o0_ref, o1_ref, o2_ref, l0_ref, l1_ref, l2_ref, p1_ref, p2_ref, e_ref, h_ref, gt_ref, w_ref,
                      out_ref):
    def positions(x_ref, p_ref):
        x = jnp.concatenate([x_ref[res] for res in range(x_ref.shape[0])], axis=0)
        if p_ref is None:
            return x.astype(F32)
        if x.dtype == BF16:
            return _dot(p_ref[...], x)
        hi = x.astype(BF16)
        return _dot(p_ref[...], hi) + _dot(p_ref[...], (x - hi.astype(F32)).astype(BF16))

    l0, l1, l2 = positions(l0_ref, None), positions(l1_ref, p1_ref), positions(l2_ref, p2_ref)
    m = jnp.maximum(jnp.maximum(l0, l1), l2)
    e0, e1, e2 = jnp.exp2(l0 - m), jnp.exp2(l1 - m), jnp.exp2(l2 - m)
    inv = 1.0 / (e0 + e1 + e2)
    spread = e_ref[...]
    o = (_dot((e0 * inv).astype(BF16), spread) * positions(o0_ref, None)
         + _dot((e1 * inv).astype(BF16), spread) * positions(o1_ref, p1_ref)
         + _dot((e2 * inv).astype(BF16), spread) * positions(o2_ref, p2_ref))
    out_ref[0] = h_ref[0] + (1.0 + gt_ref[...]) * _dot(o.astype(BF16), w_ref[...])


def _merge_out(os_, lses, h, mod5, w_o, l):
    b, s, d = h.shape
    hdim = w_o.shape[0]
    tm = min(512, s)
    row = lambda width: pl.BlockSpec((1, tm, width), lambda bb, i: (bb, i, 0))
    rep = LANES // N_HEADS
    lane = np.arange(LANES)[:, None]
    col = np.arange(hdim)[None, :]
    spread = ((lane % rep == 0) & (lane // rep == col // HEAD_DIM)).astype(BF16)

    def unpermute(rate):
        pos = np.arange(tm)
        return (((pos % rate) * (tm // rate) + pos // rate)[:, None] == np.arange(tm)[None, :]).astype(BF16)

    def residue_spec(t):
        rate, width = t.shape[1], t.shape[3]
        return pl.BlockSpec((None, rate, tm // rate, width), lambda bb, i: (bb, 0, i, 0))

    const = lambda shape: pl.BlockSpec(shape, lambda bb, i: (0, 0))
    return pl.pallas_call(
        _merge_out_kernel,
        grid=(b, s // tm),
        in_specs=[residue_spec(t) for t in (*os_, *lses)]
        + [const((tm, tm)), const((tm, tm)), const((LANES, hdim)), row(d), _mod_block(d, l, 5), const((hdim, d))],
        out_specs=row(d),
        out_shape=jax.ShapeDtypeStruct((b, s, d), F32),
        compiler_params=_params("parallel", "parallel"),
        name="merge_out_proj",
    )(*os_, *lses, unpermute(os_[1].shape[1]), unpermute(os_[2].shape[1]), spread, h, mod5, w_o)


def _out_kernel(o_ref, h_ref, gt_ref, w_ref, out_ref):
    out_ref[0] = h_ref[0] + (1.0 + gt_ref[...]) * _dot(o_ref[0], w_ref[...])


def _out_proj(o, h, mod5, w_o, l):
    b, s, d = h.shape
    hdim = w_o.shape[0]
    tm = min(1024, s)
    row = lambda width: pl.BlockSpec((1, tm, width), lambda bb, i: (bb, i, 0))
    return pl.pallas_call(
        _out_kernel,
        grid=(b, s // tm),
        in_specs=[row(hdim), row(d), _mod_block(d, l, 5), pl.BlockSpec((hdim, d), lambda bb, i: (0, 0))],
        out_specs=row(d),
        out_shape=jax.ShapeDtypeStruct((b, s, d), F32),
        compiler_params=_params("parallel", "parallel"),
        name="out_proj",
    )(o, h, mod5, w_o)


def _kv_kernel(h_ref, g_ref, sh_ref, sc_ref, w_ref, gain_ref, cos_ref, sin_ref, bd_ref,
               ks_ref, kw_ref, vs_ref, vw_ref, kvc_ref, *, tm):
    i = pl.program_id(1)
    u = _normmod(h_ref[0], g_ref[...], sh_ref[...], sc_ref[...]).astype(BF16)
    y = _dot(u, w_ref[...])
    cos = cos_ref[...]
    sin = sin_ref[...]
    bd = bd_ref[...]
    lane = lax.broadcasted_iota(jnp.int32, (tm, LANES), 1)
    pos = i * tm + lax.broadcasted_iota(jnp.int32, (tm, LANES), 0)
    onehot = jnp.where((pos >> SEL_SHIFT) == lane, 1.0, 0.0).astype(BF16)
    for j in range(4):
        z = _head_rms(y[:, j * LANES:(j + 1) * LANES], bd) * gain_ref[j:j + 1, :]
        zr = _rope(z, cos, sin).astype(BF16)
        if j < 2:
            ks_ref[0, :, 2 * j * LANES:(2 * j + 1) * LANES] = zr
            ks_ref[0, :, (2 * j + 1) * LANES:(2 * j + 2) * LANES] = onehot
        else:
            kw_ref[0, :, (j - 2) * LANES:(j - 1) * LANES] = zr
    for j in range(4):
        v = y[:, (4 + j) * LANES:(5 + j) * LANES]
        va = jnp.where(lane < HEAD_DIM, v, 1.0).astype(BF16)
        ref = vs_ref if j < 2 else vw_ref
        ref[0, :, (j % 2) * LANES:(j % 2 + 1) * LANES] = va
    kvc_ref[0] = y[:, 8 * LANES:10 * LANES]


def _kv_proj(h, kv_norm_g, kvmod5, w, gain, cos, sin, bd):
    b, s, d = h.shape
    n = w.shape[1]
    tm = min(512, s)
    row = lambda width, dt: (pl.BlockSpec((1, tm, width), lambda bb, i: (bb, i, 0)),
                             jax.ShapeDtypeStruct((b, s, width), dt))
    outs = [row(4 * LANES, BF16), row(2 * LANES, BF16), row(2 * LANES, BF16), row(2 * LANES, BF16),
            row(2 * LANES, F32)]
    return pl.pallas_call(
        functools.partial(_kv_kernel, tm=tm),
        grid=(b, s // tm),
        in_specs=[pl.BlockSpec((1, tm, d), lambda bb, i: (bb, i, 0)),
                  pl.BlockSpec((1, d), lambda bb, i: (0, 0)),
                  _mod_block(d, 0, 0),
                  _mod_block(d, 0, 1),
                  pl.BlockSpec((d, n), lambda bb, i: (0, 0)),
                  pl.BlockSpec((8, LANES), lambda bb, i: (0, 0)),
                  pl.BlockSpec((tm, LANES), lambda bb, i: (i, 0)),
                  pl.BlockSpec((tm, LANES), lambda bb, i: (i, 0)),
                  pl.BlockSpec((LANES, LANES), lambda bb, i: (0, 0))],
        out_specs=[o[0] for o in outs],
        out_shape=[o[1] for o in outs],
        compiler_params=_params("parallel", "parallel"),
        name="shared_kv_proj",
    )(h, kv_norm_g, kvmod5, kvmod5, w, gain, cos, sin, bd)


def _cmp_kernel(t_ref, w1_ref, w2_ref, pos_ref, gain_ref, o_ref):
    is_key = pl.program_id(0) < N_KV_HEADS
    t = t_ref[...]
    w1 = w1_ref[...]
    half = w1.shape[0] // 2
    first = _split_dot(t, w1[:half])
    second = _split_dot(t, w1[half:])
    second = pltpu.roll(second, second.shape[0] - 1, 0)
    posb = _split_dot(jnp.broadcast_to(pos_ref[...], (8, w1.shape[0])), w1)[0:1]
    hid = first + second + posb
    out = _split_dot(hid * _sigmoid(hid), w2_ref[...])
    key = out * lax.rsqrt(jnp.mean(out * out, axis=-1, keepdims=True) + EPS) * gain_ref[...]
    dup = jnp.concatenate([key, key], axis=-1)
    aug = jnp.concatenate([out, jnp.ones_like(out)], axis=-1)
    o_ref[...] = jnp.where(is_key, dup, aug).astype(BF16)


def _compress(t4, phi_w1, phi_w2, cmp_pos, gain):
    _, b, nsb, width = t4.shape
    hid = phi_w1.shape[2]
    return pl.pallas_call(
        _cmp_kernel,
        grid=(4, b),
        in_specs=[pl.BlockSpec((None, None, nsb, width), lambda t, bb: (t, bb, 0, 0)),
                  pl.BlockSpec((None, CMP_LEN * HEAD_DIM, hid), lambda t, bb: (t // N_KV_HEADS, 0, 0)),
                  pl.BlockSpec((None, hid, HEAD_DIM), lambda t, bb: (t // N_KV_HEADS, 0, 0)),
                  pl.BlockSpec((None, 1, CMP_LEN * HEAD_DIM), lambda t, bb: (t // N_KV_HEADS, 0, 0)),
                  pl.BlockSpec((1, HEAD_DIM), lambda t, bb: (0, 0))],
        out_specs=pl.BlockSpec((None, None, nsb, LANES), lambda t, bb: (t, bb, 0, 0)),
        out_shape=jax.ShapeDtypeStruct((4, b, nsb, LANES), BF16),
        compiler_params=_params("parallel", "parallel"),
        name="compress_blocks",
    )(t4, phi_w1, phi_w2, cmp_pos, gain)


def _qg_kernel(h_ref, g_ref, sh_ref, sc_ref, w_ref, gain_ref, cos_ref, sin_ref, bd_ref,
               qn_ref, qr_ref, gate_ref):
    u = _normmod(h_ref[0], g_ref[...], sh_ref[...], sc_ref[...]).astype(BF16)
    y = _dot(u, w_ref[...])
    cos = cos_ref[...]
    sin = sin_ref[...]
    bd = bd_ref[...]
    hdim = qn_ref.shape[2]
    for cb in range(hdim // LANES):
        sl = slice(cb * LANES, (cb + 1) * LANES)
        z = _head_rms(y[:, sl], bd) * gain_ref[:, sl]
        qn_ref[0, :, sl] = z.astype(BF16)
        qr_ref[0, :, sl] = _rope(z, cos, sin).astype(BF16)
    gate_ref[0] = _sigmoid(y[:, hdim:])


def _qg_proj(h, norm_g3, mod5, w, gain, cos, sin, bd, l):
    b, s, d = h.shape
    n = w.shape[1]
    hdim = N_HEADS * HEAD_DIM
    tm = min(512, s)
    row = lambda width, dt: (pl.BlockSpec((1, tm, width), lambda bb, i: (bb, i, 0)),
                             jax.ShapeDtypeStruct((b, s, width), dt))
    outs = [row(hdim, BF16), row(hdim, BF16), row(n - hdim, F32)]
    return pl.pallas_call(
        _qg_kernel,
        grid=(b, s // tm),
        in_specs=[pl.BlockSpec((1, tm, d), lambda bb, i: (bb, i, 0)),
                  pl.BlockSpec((None, 1, d), lambda bb, i: (l * 3 + 1, 0, 0)),
                  _mod_block(d, l, 3),
                  _mod_block(d, l, 4),
                  pl.BlockSpec((d, n), lambda bb, i: (0, 0)),
                  pl.BlockSpec((1, hdim), lambda bb, i: (0, 0)),
                  pl.BlockSpec((tm, LANES), lambda bb, i: (i, 0)),
                  pl.BlockSpec((tm, LANES), lambda bb, i: (i, 0)),
                  pl.BlockSpec((LANES, LANES), lambda bb, i: (0, 0))],
        out_specs=[o[0] for o in outs],
        out_shape=[o[1] for o in outs],
        compiler_params=_params("parallel", "parallel"),
        name="nsa_query_gate_proj",
    )(h, norm_g3, mod5, mod5, w, gain, cos, sin, bd)


def _nsa_body(qn_ref, qr_ref, gate_ref, kc_ref, vc_ref, ks_ref, vs_ref, kw_ref, vw_ref, wimpt_ref, gsel_ref,
              o_ref, qa_scr, m_scr, acc_scr, mw_scr, accw_scr, *, tq, shifted):
    i = pl.program_id(2)
    start = i * tq
    rows = GROUP * tq
    ncp = kc_ref.shape[0]
    low = lax.broadcasted_iota(jnp.int32, (tq, LANES), 1) < HEAD_DIM
    low_r = lax.broadcasted_iota(jnp.int32, (rows, LANES), 1) < HEAD_DIM

    def accumulate(s, v, mask, first, acc_ref=acc_scr, m_ref=m_scr):
        if not shifted:
            p = jnp.exp2(s)
            if mask is not None:
                p = jnp.where(mask, p, 0.0)
            pv = _dot(p.astype(BF16), v)
            acc_ref[...] = pv if first else acc_ref[...] + pv
            return
        if mask is not None:
            s = jnp.where(mask, s, NEG)
        if first:
            m_new = jnp.max(s, axis=-1, keepdims=True)
            acc_ref[...] = _dot(jnp.exp2(s - m_new).astype(BF16), v)
        else:
            m_old = m_ref[...]
            m_new = jnp.maximum(m_old, jnp.max(s, axis=-1, keepdims=True))
            acc_ref[...] = jnp.exp2(m_old - m_new) * acc_ref[...] + _dot(jnp.exp2(s - m_new).astype(BF16), v)
        m_ref[...] = m_new

    def stack(q):
        parts = []
        for g in range(GROUP):
            blk = q[:, (g // 2) * LANES:(g // 2 + 1) * LANES]
            parts.append(jnp.where(low if g % 2 == 0 else ~low, blk, jnp.zeros_like(blk)))
        return parts

    qn = jnp.concatenate(stack(qn_ref[0]), axis=0)
    sc = _dot_t(qn, kc_ref[...])
    c_end = lax.broadcasted_iota(jnp.int32, (tq, ncp), 1) * CMP_STRIDE + (CMP_LEN - 1)
    valid_t = c_end <= start + lax.broadcasted_iota(jnp.int32, (tq, ncp), 0)
    if shifted:
        valid = jnp.concatenate([valid_t] * GROUP, axis=0)
        sc = jnp.where(valid, sc, NEG)
        sc = sc - jnp.max(sc, axis=-1, keepdims=True)
        p = jnp.where(valid, jnp.exp2(sc), 0.0)
    else:
        p = jnp.exp2(sc + jnp.concatenate([jnp.where(valid_t, 0.0, NEG)] * GROUP, axis=0))
    oc = _dot(p.astype(BF16), vc_ref[...])
    inv = 1.0 / jnp.maximum(jnp.where(low_r, pltpu.roll(oc, HEAD_DIM, 1), oc), 1e-30)
    o_cmp = oc * inv
    psum_blocks = []
    for cb in range(ncp // LANES):
        cs = slice(cb * LANES, (cb + 1) * LANES)
        tot = p[0:tq, cs] * inv[0:tq]
        for g in range(1, GROUP):
            tot = tot + p[g * tq:(g + 1) * tq, cs] * inv[g * tq:(g + 1) * tq]
        psum_blocks.append(tot)
    psum = jnp.concatenate(psum_blocks, axis=1)

    qr_parts = stack(qr_ref[0])
    for g in range(GROUP):
        qa_scr[g * tq:(g + 1) * tq, 0:LANES] = qr_parts[g]
    qr = jnp.concatenate(qr_parts, axis=0)

    tri = (lax.broadcasted_iota(jnp.int32, (rows, tq), 1)
           <= (lax.broadcasted_iota(jnp.int32, (rows, tq), 0) & (tq - 1)))

    def key_block(kb):
        return pl.ds(pl.multiple_of(kb * tq, tq), tq)

    diag = key_block(i)
    accumulate(_dot_t(qr, kw_ref[diag, :]), vw_ref[diag, :], tri, True, accw_scr, mw_scr)

    p_hi = psum.astype(BF16)
    p_lo = (psum - p_hi.astype(F32)).astype(BF16)
    imp_t = _dot_t(wimpt_ref[...], p_hi) + _dot_t(wimpt_ref[...], p_lo)

    blk = key_block(jnp.maximum(i - 1, 0))
    accumulate(_dot_t(qr, kw_ref[blk, :]), vw_ref[blk, :], jnp.broadcast_to(i >= 1, tri.shape), False,
               accw_scr, mw_scr)
    blk = key_block(jnp.maximum(i - 2, 0))
    accumulate(_dot_t(qr, kw_ref[blk, :]), vw_ref[blk, :], (~tri) & (i >= 2), False, accw_scr, mw_scr)

    blk_id = lax.broadcasted_iota(jnp.int32, (LANES, tq), 0)
    jt = (start + lax.broadcasted_iota(jnp.int32, (LANES, tq), 1)) >> SEL_SHIFT
    forced = (blk_id == 0) | (blk_id == jt) | (blk_id == jt - 1)
    work = jnp.where(blk_id > jt, -1.0, jnp.where(forced, -2.0, imp_t))
    sel = forced
    for _ in range(N_SELECT - 3):
        mx = jnp.max(work, axis=0, keepdims=True)
        idx = jnp.min(jnp.where(work == mx, blk_id, LANES), axis=0, keepdims=True)
        pick = blk_id == idx
        sel = sel | pick
        work = jnp.where(pick, -2.0, work)
    sel_bias = jnp.transpose(jnp.where(sel, 0.0, NEG)).astype(BF16)

    for g in range(GROUP):
        qa_scr[g * tq:(g + 1) * tq, LANES:2 * LANES] = sel_bias

    accumulate(_dot_t(qa_scr[...], ks_ref[diag, :]), vs_ref[diag, :], tri, True)

    def key_span(first_block, n_blocks):
        return pl.ds(pl.multiple_of(first_block * tq, tq), n_blocks * tq)

    def slc_quad(kq, carry):
        span = key_span(4 * kq, 4)
        accumulate(_dot_t(qa_scr[...], ks_ref[span, :]), vs_ref[span, :], None, False)
        return carry

    lax.fori_loop(0, i // 4, slc_quad, 0)

    @pl.when(i % 4 >= 2)
    def _():
        span = key_span((i // 4) * 4, 2)
        accumulate(_dot_t(qa_scr[...], ks_ref[span, :]), vs_ref[span, :], None, False)

    @pl.when(i % 2 == 1)
    def _():
        blk = key_block(i - 1)
        accumulate(_dot_t(qa_scr[...], ks_ref[blk, :]), vs_ref[blk, :], None, False)

    gates = gate_ref[0]
    g_hi = gates.astype(BF16)
    g_lo = (gates - g_hi.astype(F32)).astype(BF16)
    gate_wide = _dot(g_hi, gsel_ref[...]) + _dot(g_lo, gsel_ref[...])
    for pr in range(GROUP // 2):
        ev = slice(2 * pr * tq, (2 * pr + 1) * tq)
        od = slice((2 * pr + 1) * tq, (2 * pr + 2) * tq)

        def pair(acc):
            a_e, a_o = acc[ev], acc[od]
            r_e = pltpu.roll(a_e, HEAD_DIM, 1)
            r_o = pltpu.roll(a_o, HEAD_DIM, 1)
            return jnp.where(low, a_e, r_o) / jnp.where(low, r_e, a_o)

        def gate(br):
            c = (br * (GROUP // 2) + pr) * LANES
            return gate_wide[:, c:c + LANES]

        o_c = jnp.where(low, o_cmp[ev], pltpu.roll(o_cmp[od], HEAD_DIM, 1))
        out = gate(0) * o_c + gate(1) * pair(acc_scr) + gate(2) * pair(accw_scr)
        o_ref[0, :, pr * LANES:(pr + 1) * LANES] = out.astype(BF16)


def _nsa_kernel(small_ref, *refs, tq):
    @pl.when(small_ref[0] != 0)
    def _():
        _nsa_body(*refs, tq=tq, shifted=False)

    @pl.when(small_ref[0] == 0)
    def _():
        _nsa_body(*refs, tq=tq, shifted=True)


def _nsa_attention(small, qn, qr, gates, cmp4, ks, vs, kw, vw, wimpt):
    b, s, hdim = qn.shape
    tq = min(256, s)
    assert s // SEL_BLOCK <= LANES and s % tq == 0 and SLIDE_WINDOW == 2 * tq
    ncp = cmp4.shape[2]
    gw = GROUP * HEAD_DIM
    rows = GROUP * tq
    qspec = pl.BlockSpec((1, tq, gw), lambda bb, kv, i: (bb, i, kv))
    full = lambda width: pl.BlockSpec((None, s, width), lambda bb, kv, i: (bb, 0, kv))
    col = np.arange(3 * (GROUP // 2) * LANES)[None, :]
    src = (col // LANES // (GROUP // 2)) * GROUP + 2 * ((col // LANES) % (GROUP // 2)) + (col % LANES) // HEAD_DIM
    gsel = (np.arange(LANES)[:, None] == src).astype(BF16)
    return pl.pallas_call(
        functools.partial(_nsa_kernel, tq=tq),
        grid=(b, N_KV_HEADS, s // tq),
        in_specs=[pl.BlockSpec(memory_space=pltpu.SMEM), qspec, qspec,
                  pl.BlockSpec((1, tq, LANES), lambda bb, kv, i: (bb, i, kv)),
                  pl.BlockSpec((None, None, ncp, LANES), lambda bb, kv, i: (kv, bb, 0, 0)),
                  pl.BlockSpec((None, None, ncp, LANES), lambda bb, kv, i: (N_KV_HEADS + kv, bb, 0, 0)),
                  full(2 * LANES), full(LANES), full(LANES), full(LANES),
                  pl.BlockSpec((LANES, ncp), lambda bb, kv, i: (0, 0)),
                  pl.BlockSpec(gsel.shape, lambda bb, kv, i: (0, 0))],
        out_specs=qspec,
        out_shape=jax.ShapeDtypeStruct((b, s, hdim), BF16),
        scratch_shapes=[pltpu.VMEM((rows, 2 * LANES), BF16),
                        pltpu.VMEM((rows, 1), F32), pltpu.VMEM((rows, LANES), F32),
                        pltpu.VMEM((rows, 1), F32), pltpu.VMEM((rows, LANES), F32)],
        compiler_params=_params("parallel", "parallel", "arbitrary"),
        name="nsa_attention",
    )(small, qn, qr, gates, cmp4, cmp4, ks, vs, kw, vw, wimpt, gsel)


def _rope_tables(s, tm=None, rate=1):
    half = HEAD_DIM // 2
    row = lax.broadcasted_iota(jnp.int32, (s, LANES), 0)
    lane = lax.broadcasted_iota(jnp.int32, (s, LANES), 1)
    if rate > 1:
        seg = tm // rate
        local = row % tm
        row = row - local + (local % seg) * rate + local // seg
    inv = jnp.tile(ROPE_THETA ** (-jnp.arange(half, dtype=F32) / half), LANES // half)
    ang = row.astype(F32) * inv[None, :]
    cos, sin = jnp.cos(ang), jnp.sin(ang)
    return cos, jnp.where(lane % HEAD_DIM < half, -sin, sin), sin


def _split_half(t, axis):
    hpg = MXU_WIDTH // HEAD_DIM
    half = HEAD_DIM // 2
    shape = t.shape
    t = t.reshape(shape[:axis] + (shape[axis] // MXU_WIDTH, hpg, 2, half) + shape[axis + 1:])
    return t.swapaxes(axis + 1, axis + 2).reshape(shape)


def _importance_weights(ncp):
    r = SEL_BLOCK // CMP_STRIDE
    nper = CMP_LEN // CMP_STRIDE
    o = np.arange(ncp)[:, None] - r * np.arange(LANES)[None, :]
    w = np.zeros((ncp, LANES), np.float32)
    for off in range(-(nper - 1), r):
        cnt = sum(1 for m in range(r) for n in range(nper) if m - n == off)
        w += np.where(o == off, float(cnt), 0.0)
    return w.astype(BF16)


def kernel(x, c, norm_g, w_ada, b_ada, ffn_w_in, ffn_w_out, a_w_qkv, a_q_gain, a_k_gain, a_w_o, kv_norm_g,
           w_ada_kv, b_ada_kv, w_kv, kv_k_gain, cmp_pos, phi_w1, phi_w2, b_w_qg, b_q_gain, b_w_o):
    b, s, d = x.shape
    depth = norm_g.shape[0]
    n_a = depth // 2
    hdim = N_HEADS * HEAD_DIM
    assert depth == 2 and b <= 8

    c8 = jnp.zeros((8, d), F32).at[:b].set(c)
    mod = _ada(c8, w_ada, b_ada.reshape(depth, 1, 9 * d))
    mod5 = mod[:, :b].reshape(depth, b, 9, 1, d)
    kvmod = _ada(c8, w_ada_kv[None], b_ada_kv.reshape(1, 1, 2 * d))
    kvmod5 = kvmod[:, :b].reshape(1, b, 2, 1, d)

    norm_g3 = norm_g.reshape(depth * 3, 1, d)
    w_in = ffn_w_in.astype(BF16)
    w_out = ffn_w_out.astype(BF16)
    cos, sin, _ = _rope_tables(s)
    blk = np.arange(LANES) // HEAD_DIM
    bd = (blk[:, None] == blk[None, :]).astype(BF16)
    blk = (np.arange(MXU_WIDTH) % LANES) // (HEAD_DIM // 2)
    bd_split = (blk[:, None] == blk[None, :]).astype(BF16)
    q_scale = HEAD_DIM ** -0.5 * LOG2E

    h = x
    for l in range(depth):
        h = _ffn(h, norm_g3, mod5, w_in, w_out, l, 0, 0)
        if l < n_a:
            gain = jnp.stack([_split_half(jnp.tile(a_q_gain[l] * q_scale, (1, N_HEADS)), 1),
                              _split_half(jnp.tile(a_k_gain[l], (1, N_HEADS)), 1),
                              jnp.ones((N_DIL, hdim), F32)], axis=1).reshape(3 * N_DIL, 1, hdim)
            cols = np.arange(N_DIL * 3 * hdim).reshape(N_DIL, 3, hdim)
            cols = np.concatenate([_split_half(cols[:, :2], 2), cols[:, 2:]], axis=1).reshape(-1)
            w_qkv = a_w_qkv[l][:, cols].astype(BF16)
            bound = HEAD_DIM * q_scale * jnp.max(jnp.abs(a_q_gain[l])) * jnp.max(jnp.abs(a_k_gain[l]))
            small = (bound <= SCORE_LIMIT).astype(jnp.int32).reshape(1)
            parts = [_dilated_attention(small, _qkv_proj(h, norm_g3, mod5, w_qkv, gain, bd_split, l, g), g)
                     for g in range(N_DIL)]
            h = _merge_out([p[0] for p in parts], [p[1] for p in parts], h, mod5, a_w_o[l].astype(BF16), l)
        else:
            j = l - n_a
            w_q = b_w_qg[j][:, :hdim]
            w_g = b_w_qg[j][:, hdim:].reshape(d, 3, N_KV_HEADS, GROUP).transpose(0, 2, 1, 3)
            w_g = jnp.pad(w_g.reshape(d, N_KV_HEADS, 3 * GROUP), ((0, 0), (0, 0), (0, LANES - 3 * GROUP)))
            w_qg = jnp.concatenate([w_q, w_g.reshape(d, N_KV_HEADS * LANES)], axis=1).astype(BF16)
            gain = jnp.tile(b_q_gain[j] * q_scale, N_HEADS)[None, :]
            qn, qr, gates = _qg_proj(h, norm_g3, mod5, w_qg, gain, cos, sin, bd, l)
            bound = HEAD_DIM * q_scale * jnp.max(jnp.abs(b_q_gain[j])) * jnp.max(jnp.abs(kv_k_gain))
            small = (bound <= SCORE_LIMIT).astype(jnp.int32).reshape(1)
            o = _nsa_attention(small, qn, qr, gates, *shared)
            h = _out_proj(o, h, mod5, b_w_o[j].astype(BF16), l)
        h = _ffn(h, norm_g3, mod5, w_in, w_out, l, 2, 1)
        if l == n_a - 1:
            col = lambda br, kv, kvh: br * 4 * HEAD_DIM + kv * 2 * HEAD_DIM + kvh * HEAD_DIM + jnp.arange(HEAD_DIM)
            dup = lambda br, kv, kvh: jnp.concatenate([col(br, kv, kvh)] * 2)
            cols = jnp.concatenate(
                [dup(br, 0, kvh) for br in (1, 2) for kvh in range(N_KV_HEADS)]
                + [dup(br, 1, kvh) for br in (1, 2) for kvh in range(N_KV_HEADS)]
                + [col(0, kv, kvh) for kv in range(2) for kvh in range(N_KV_HEADS)])
            w_kv_ext = w_kv[:, cols].astype(BF16)
            kgain = jnp.concatenate([jnp.tile(kv_k_gain[1], (2, 2)), jnp.tile(kv_k_gain[2], (2, 2)),
                                     jnp.ones((4, LANES), F32)], axis=0)
            ks, kw, vs, vw, kvc = _kv_proj(h, kv_norm_g[None, :], kvmod5, w_kv_ext, kgain, cos, sin, bd)
            nsb = s // CMP_STRIDE
            t4 = kvc.reshape(b, nsb, CMP_STRIDE, 4, HEAD_DIM).transpose(3, 0, 1, 2, 4)
            t4 = t4.reshape(4, b, nsb, CMP_STRIDE * HEAD_DIM)
            cmp4 = _compress(t4, phi_w1, phi_w2, cmp_pos.reshape(2, 1, CMP_LEN * HEAD_DIM), kv_k_gain[0][None, :])
            shared = (cmp4, ks, vs, kw, vw, _importance_weights(nsb).T)
    return h
```

```python
import functools

import jax
import jax.numpy as jnp
import numpy as np
from jax import lax
from jax.experimental import pallas as pl
from jax.experimental.pallas import tpu as pltpu

HEAD_DIM = 64
N_HEADS = 16
DIL_WINDOWS = (128, 512, 2048)
DIL_RATES = (1, 4, 16)
N_DIL = 3
N_KV_HEADS = 2
GROUP = N_HEADS // N_KV_HEADS
CMP_STRIDE = 16
CMP_LEN = 32
SEL_BLOCK = 64
SEL_SHIFT = SEL_BLOCK.bit_length() - 1
N_SELECT = 16
SLIDE_WINDOW = 512
ROPE_THETA = 10000.0
EPS = 1e-6

LANES = 128
MXU_WIDTH = 256
Q_ROWS = 128
NEG = -1e30
LOG2E = 1.4426950408889634
SCORE_LIMIT = 60.0
VMEM_LIMIT = 56 * 1024 * 1024

BF16 = jnp.bfloat16
F32 = jnp.float32


def _params(*sem):
    return pltpu.CompilerParams(dimension_semantics=sem, vmem_limit_bytes=VMEM_LIMIT)


def _dot(a, b):
    return jnp.dot(a, b, preferred_element_type=F32)


def _dot_t(a, b):
    return lax.dot_general(a, b, (((1,), (1,)), ((), ())), preferred_element_type=F32)


def _split_dot(a, b):
    a_hi = a.astype(BF16)
    a_lo = (a - a_hi.astype(F32)).astype(BF16)
    b_hi = b.astype(BF16)
    b_lo = (b - b_hi.astype(F32)).astype(BF16)
    return _dot(a_hi, b_hi) + (_dot(a_hi, b_lo) + _dot(a_lo, b_hi))


def _sigmoid(x):
    return 1.0 / (1.0 + jnp.exp(-x))


def _normmod(x, g, shift, scale):
    return x * lax.rsqrt(jnp.mean(x * x, axis=-1, keepdims=True) + EPS) * (g * (1.0 + scale)) + shift


def _head_rms(y, bd):
    ss = _dot((y * y).astype(BF16), bd)
    return y * lax.rsqrt(ss * (1.0 / HEAD_DIM) + EPS)


def _rope(z, cos, sin_signed):
    lane = lax.broadcasted_iota(jnp.int32, z.shape, 1)
    first_half = (lane & (HEAD_DIM - 1)) < HEAD_DIM // 2
    swapped = jnp.where(first_half, pltpu.roll(z, z.shape[1] - HEAD_DIM // 2, 1),
                        pltpu.roll(z, HEAD_DIM // 2, 1))
    return z * cos + swapped * sin_signed


def _ada_kernel(c_ref, w_ref, b_ref, o_ref):
    c = c_ref[...]
    a = (c * _sigmoid(c)).astype(BF16)
    o_ref[...] = _dot(a, w_ref[...].astype(BF16)) + b_ref[...]


def _ada(c8, w, b):
    nl, d, n = w.shape
    tn = 1024
    return pl.pallas_call(
        _ada_kernel,
        grid=(nl, n // tn),
        in_specs=[pl.BlockSpec((8, d), lambda l, j: (0, 0)),
                  pl.BlockSpec((None, d, tn), lambda l, j: (l, 0, j)),
                  pl.BlockSpec((None, 1, tn), lambda l, j: (l, 0, j))],
        out_specs=pl.BlockSpec((None, 8, tn), lambda l, j: (l, 0, j)),
        out_shape=jax.ShapeDtypeStruct((nl, 8, n), F32),
        compiler_params=_params("parallel", "parallel"),
        name="ada_table",
    )(c8, w, b)


def _mod_block(d, l, k):
    return pl.BlockSpec((None, None, None, 1, d), lambda b, *_: (l, b, k, 0, 0))


def _ffn_kernel(h_ref, g_ref, sh_ref, sc_ref, gt_ref, wg_ref, wu_ref, wo_ref, o_ref, u_scr, acc_scr):
    u_scr[...] = _normmod(h_ref[0], g_ref[...], sh_ref[...], sc_ref[...]).astype(BF16)

    u = u_scr[...]
    tf = wg_ref.shape[1]
    bounds = [(lo, min(lo + MXU_WIDTH, tf)) for lo in range(0, tf, MXU_WIDTH)]

    def gate_up(c):
        lo, hi = bounds[c]
        return _dot(u, wg_ref[:, lo:hi]), _dot(u, wu_ref[:, lo:hi])

    nxt = gate_up(0)
    for c, (lo, hi) in enumerate(bounds):
        gate, up = nxt
        if c + 1 < len(bounds):
            nxt = gate_up(c + 1)
        act = (gate * _sigmoid(gate) * up).astype(BF16)
        down = _dot(act, wo_ref[lo:hi, :])
        acc_scr[...] = down if c == 0 else acc_scr[...] + down

    o_ref[0] = h_ref[0] + (0.5 * (1.0 + gt_ref[...])) * acc_scr[...]


def _ffn(h, norm_g3, mod5, w_in, w_out, l, sub, which):
    b, s, d = h.shape
    f = w_out.shape[2]
    tm = min(1024, s)
    resident = pl.Buffered(1)
    return pl.pallas_call(
        _ffn_kernel,
        grid=(b, s // tm),
        in_specs=[pl.BlockSpec((1, tm, d), lambda bb, i: (bb, i, 0)),
                  pl.BlockSpec((None, 1, d), lambda bb, i: (l * 3 + sub, 0, 0)),
                  _mod_block(d, l, sub * 3 + 0),
                  _mod_block(d, l, sub * 3 + 1),
                  _mod_block(d, l, sub * 3 + 2),
                  pl.BlockSpec((None, None, d, f), lambda bb, i: (l, which, 0, 0), pipeline_mode=resident),
                  pl.BlockSpec((None, None, d, f), lambda bb, i: (l, which, 0, 1), pipeline_mode=resident),
                  pl.BlockSpec((None, None, f, d), lambda bb, i: (l, which, 0, 0), pipeline_mode=resident)],
        out_specs=pl.BlockSpec((1, tm, d), lambda bb, i: (bb, i, 0)),
        out_shape=jax.ShapeDtypeStruct((b, s, d), F32),
        scratch_shapes=[pltpu.VMEM((tm, d), BF16), pltpu.VMEM((tm, d), F32)],
        compiler_params=_params("parallel", "parallel"),
        name="swiglu_half_step",
    )(h, norm_g3, mod5, mod5, mod5, w_in, w_in, w_out)


def _qkv_kernel(h_ref, g_ref, sh_ref, sc_ref, w_ref, gain_ref, cos_ref, sin_ref, bd_ref, o_ref, u_scr, uf_scr,
                *, rate):
    tm = u_scr.shape[0]
    seg = tm // rate
    u = _normmod(h_ref[0], g_ref[...], sh_ref[...], sc_ref[...])
    if rate == 1:
        u_scr[...] = u.astype(BF16)
    else:
        for cb in range(u.shape[1] // LANES):
            sl = slice(cb * LANES, (cb + 1) * LANES)
            uf_scr[cb] = u[:, sl]
            for res in range(rate):
                u_scr[res * seg:(res + 1) * seg, sl] = uf_scr[cb, pl.ds(res, seg, stride=rate), :].astype(BF16)

    n_grp = o_ref.shape[1]
    per_kind = n_grp // 3
    cos = cos_ref[...]
    sin = sin_ref[...]
    bd = bd_ref[...]

    def project(grp):
        return _dot(u_scr[...], w_ref[:, grp * MXU_WIDTH:(grp + 1) * MXU_WIDTH])

    y = project(0)
    for grp in range(n_grp):
        y_next = project(grp + 1) if grp + 1 < n_grp else None
        kind, sub = divmod(grp, per_kind)
        if kind < 2:
            z = _head_rms(y, bd) * gain_ref[kind, :, sub * MXU_WIDTH:(sub + 1) * MXU_WIDTH]
            x1, x2 = z[:, :LANES], z[:, LANES:]
            y = jnp.concatenate([x1 * cos - x2 * sin, x1 * sin + x2 * cos], axis=1)
        y = y.astype(BF16)
        for res in range(rate):
            o_ref[res, grp] = y[res * seg:(res + 1) * seg]
        y = y_next


def _qkv_proj(h, norm_g3, mod5, w, gain, bd, l, g):
    b, s, d = h.shape
    rate = DIL_RATES[g]
    tm = min(1024, s)
    seg = tm // rate
    tn = N_HEADS * HEAD_DIM
    n_grp = tn // MXU_WIDTH
    cos, _, sin = _rope_tables(s, tm, rate)

    return pl.pallas_call(
        functools.partial(_qkv_kernel, rate=rate),
        grid=(b, s // tm),
        in_specs=[pl.BlockSpec((1, tm, d), lambda bb, i: (bb, i, 0)),
                  pl.BlockSpec((None, 1, d), lambda bb, i: (l * 3 + 1, 0, 0)),
                  _mod_block(d, l, 3),
                  _mod_block(d, l, 4),
                  pl.BlockSpec((d, 3 * tn), lambda bb, i: (0, g), pipeline_mode=pl.Buffered(1)),
                  pl.BlockSpec((3, 1, tn), lambda bb, i: (g, 0, 0)),
                  pl.BlockSpec((tm, LANES), lambda bb, i: (i, 0)),
                  pl.BlockSpec((tm, LANES), lambda bb, i: (i, 0)),
                  pl.BlockSpec(bd.shape, lambda bb, i: (0, 0))],
        out_specs=pl.BlockSpec((None, rate, 3 * n_grp, seg, MXU_WIDTH), lambda bb, i: (bb, 0, 0, i, 0)),
        out_shape=jax.ShapeDtypeStruct((b, rate, 3 * n_grp, s // rate, MXU_WIDTH), BF16),
        scratch_shapes=[pltpu.VMEM((tm, d), BF16), pltpu.VMEM((d // LANES, tm, LANES), F32)],
        compiler_params=_params("parallel", "parallel"),
        name=f"qkv_proj_rate{rate}",
    )(h, norm_g3, mod5, mod5, w, gain, cos, sin, bd)


def _dil_body(bias_ref, q_ref, kp_ref, kc_ref, vp_ref, vc_ref, o_ref, lse_ref, *, tq, shifted):
    i = pl.program_id(2)
    n_grp = q_ref.shape[0]
    hpg = MXU_WIDTH // HEAD_DIM
    qlane = lax.broadcasted_iota(jnp.int32, (Q_ROWS, MXU_WIDTH), 1)
    q_head = (qlane & (LANES - 1)) // (HEAD_DIM // 2)
    v_head = qlane // HEAD_DIM
    seg = lax.broadcasted_iota(jnp.int32, (Q_ROWS, LANES), 1) // (LANES // N_HEADS)

    def scores(grp, j):
        q = q_ref[grp, j * Q_ROWS:(j + 1) * Q_ROWS, :]
        zero = jnp.zeros_like(q)
        q2 = jnp.concatenate([jnp.where(q_head == t, q, zero) for t in range(hpg)], axis=0)
        if j == 0:
            k = jnp.concatenate([kp_ref[grp], kc_ref[grp, 0:Q_ROWS, :]], axis=0)
        else:
            k = kc_ref[grp, (j - 1) * Q_ROWS:(j + 1) * Q_ROWS, :]
        return _dot_t(q2, k)

    for j in range(tq // Q_ROWS):
        rs = slice(j * Q_ROWS, (j + 1) * Q_ROWS)
        lse_tile = None
        for grp in range(n_grp):
            s = scores(grp, j) + bias_ref[...]
            if j == 0:
                v = jnp.concatenate([vp_ref[grp], vc_ref[grp, 0:Q_ROWS, :]], axis=0)
                prev_key = lax.broadcasted_iota(jnp.int32, s.shape, 1) < Q_ROWS
                s = jnp.where(prev_key & (i == 0), NEG, s)
            else:
                v = vc_ref[grp, (j - 1) * Q_ROWS:(j + 1) * Q_ROWS, :]
            if shifted:
                m = jnp.max(s, axis=-1, keepdims=True)
                p = jnp.exp2(s - m)
            else:
                p = jnp.exp2(s)
            den = jnp.sum(p, axis=-1, keepdims=True)
            o2 = _dot(p.astype(BF16), v) / den
            lse2 = jnp.log(den) * LOG2E
            if shifted:
                lse2 = lse2 + m
            out = o2[0:Q_ROWS]
            for t in range(hpg):
                ts = slice(t * Q_ROWS, (t + 1) * Q_ROWS)
                if t:
                    out = jnp.where(v_head == t, o2[ts], out)
                head_lse = jnp.broadcast_to(lse2[ts], (Q_ROWS, LANES))
                lse_tile = head_lse if lse_tile is None else jnp.where(seg == hpg * grp + t, head_lse, lse_tile)
            o_ref[0, rs, grp * MXU_WIDTH:(grp + 1) * MXU_WIDTH] = out.astype(BF16)
        lse_ref[0, rs, :] = lse_tile


def _dil_kernel(small_ref, *refs, tq):
    @pl.when(small_ref[0] != 0)
    def _():
        _dil_body(*refs, tq=tq, shifted=False)

    @pl.when(small_ref[0] == 0)
    def _():
        _dil_body(*refs, tq=tq, shifted=True)


def _dilated_attention(small, qkv_r, g):
    b, rate, n3, l, width = qkv_r.shape
    s = l * rate
    n_back = DIL_WINDOWS[g] // rate
    assert n_back <= Q_ROWS and rate == DIL_RATES[g] and width == MXU_WIDTH
    tq = min(512, l)
    hdim = N_HEADS * HEAD_DIM
    n_grp = n3 // 3
    sub = tq // Q_ROWS

    def cur(kind):
        return pl.BlockSpec((None, None, n_grp, tq, width), lambda bb, r, i: (bb, r, kind, i, 0))

    def prev(kind):
        return pl.BlockSpec((None, None, n_grp, Q_ROWS, width),
                            lambda bb, r, i: (bb, r, kind, jnp.maximum(i * sub - 1, 0), 0))

    rows = (width // HEAD_DIM) * Q_ROWS
    dist = Q_ROWS + (np.arange(rows)[:, None] % Q_ROWS) - np.arange(2 * Q_ROWS)[None, :]
    bias = np.where((dist >= 0) & (dist <= n_back), 0.0, NEG).astype(np.float32)

    o, lse = pl.pallas_call(
        functools.partial(_dil_kernel, tq=tq),
        grid=(b, rate, l // tq),
        in_specs=[pl.BlockSpec(memory_space=pltpu.SMEM),
                  pl.BlockSpec(bias.shape, lambda bb, r, i: (0, 0)),
                  cur(0), prev(1), cur(1), prev(2), cur(2)],
        out_specs=[pl.BlockSpec((1, None, tq, hdim), lambda bb, r, i: (bb, r, i, 0)),
                   pl.BlockSpec((1, None, tq, LANES), lambda bb, r, i: (bb, r, i, 0))],
        out_shape=[jax.ShapeDtypeStruct((b, rate, l, hdim), BF16),
                   jax.ShapeDtypeStruct((b, rate, l, LANES), F32)],
        compiler_params=_params("parallel", "parallel", "parallel"),
        name=f"dilated_attention_rate{rate}",
    )(small, bias, qkv_r, qkv_r, qkv_r, qkv_r, qkv_r)
    return o, lse


def _merge_out_kernel(o0_ref, o1_ref, o2_ref, l0_ref, l1_ref, l2_ref, p1_ref, p2_ref, e_ref, h_ref, gt_ref, w_ref,
                      out_ref):
    def positions(x_ref, p_ref):
        x = jnp.concatenate([x_ref[res] for res in range(x_ref.shape[0])], axis=0)
        if p_ref is None:
            return x.astype(F32)
        if x.dtype == BF16:
            return _dot(p_ref[...], x)
        hi = x.astype(BF16)
        return _dot(p_ref[...], hi) + _dot(p_ref[...], (x - hi.astype(F32)).astype(BF16))

    l0, l1, l2 = positions(l0_ref, None), positions(l1_ref, p1_ref), positions(l2_ref, p2_ref)
    m = jnp.maximum(jnp.maximum(l0, l1), l2)
    e0, e1, e2 = jnp.exp2(l0 - m), jnp.exp2(l1 - m), jnp.exp2(l2 - m)
    inv = 1.0 / (e0 + e1 + e2)
    spread = e_ref[...]
    o = (_dot((e0 * inv).astype(BF16), spread) * positions(o0_ref, None)
         + _dot((e1 * inv).astype(BF16), spread) * positions(o1_ref, p1_ref)
         + _dot((e2 * inv).astype(BF16), spread) * positions(o2_ref, p2_ref))
    out_ref[0] = h_ref[0] + (1.0 + gt_ref[...]) * _dot(o.astype(BF16), w_ref[...])


def _merge_out(os_, lses, h, mod5, w_o, l):
    b, s, d = h.shape
    hdim = w_o.shape[0]
    tm = min(512, s)
    row = lambda width: pl.BlockSpec((1, tm, width), lambda bb, i: (bb, i, 0))
    rep = LANES // N_HEADS
    lane = np.arange(LANES)[:, None]
    col = np.arange(hdim)[None, :]
    spread = ((lane % rep == 0) & (lane // rep == col // HEAD_DIM)).astype(BF16)

    def unpermute(rate):
        pos = np.arange(tm)
        return (((pos % rate) * (tm // rate) + pos // rate)[:, None] == np.arange(tm)[None, :]).astype(BF16)

    def residue_spec(t):
        rate, width = t.shape[1], t.shape[3]
        return pl.BlockSpec((None, rate, tm // rate, width), lambda bb, i: (bb, 0, i, 0))

    const = lambda shape: pl.BlockSpec(shape, lambda bb, i: (0, 0))
    return pl.pallas_call(
        _merge_out_kernel,
        grid=(b, s // tm),
        in_specs=[residue_spec(t) for t in (*os_, *lses)]
        + [const((tm, tm)), const((tm, tm)), const((LANES, hdim)), row(d), _mod_block(d, l, 5), const((hdim, d))],
        out_specs=row(d),
        out_shape=jax.ShapeDtypeStruct((b, s, d), F32),
        compiler_params=_params("parallel", "parallel"),
        name="merge_out_proj",
    )(*os_, *lses, unpermute(os_[1].shape[1]), unpermute(os_[2].shape[1]), spread, h, mod5, w_o)


def _out_kernel(o_ref, h_ref, gt_ref, w_ref, out_ref):
    out_ref[0] = h_ref[0] + (1.0 + gt_ref[...]) * _dot(o_ref[0], w_ref[...])


def _out_proj(o, h, mod5, w_o, l):
    b, s, d = h.shape
    hdim = w_o.shape[0]
    tm = min(1024, s)
    row = lambda width: pl.BlockSpec((1, tm, width), lambda bb, i: (bb, i, 0))
    return pl.pallas_call(
        _out_kernel,
        grid=(b, s // tm),
        in_specs=[row(hdim), row(d), _mod_block(d, l, 5), pl.BlockSpec((hdim, d), lambda bb, i: (0, 0))],
        out_specs=row(d),
        out_shape=jax.ShapeDtypeStruct((b, s, d), F32),
        compiler_params=_params("parallel", "parallel"),
        name="out_proj",
    )(o, h, mod5, w_o)


def _kv_kernel(h_ref, g_ref, sh_ref, sc_ref, w_ref, gain_ref, cos_ref, sin_ref, bd_ref,
               ks_ref, kw_ref, vs_ref, vw_ref, kvc_ref, *, tm):
    i = pl.program_id(1)
    u = _normmod(h_ref[0], g_ref[...], sh_ref[...], sc_ref[...]).astype(BF16)
    bd = bd_ref[...]
    width = bd.shape[0]
    cos = jnp.concatenate([cos_ref[...]] * (width // LANES), axis=1)
    sin = jnp.concatenate([sin_ref[...]] * (width // LANES), axis=1)
    lane = lax.broadcasted_iota(jnp.int32, (tm, LANES), 1)
    pos = i * tm + lax.broadcasted_iota(jnp.int32, (tm, LANES), 0)
    onehot = jnp.where((pos >> SEL_SHIFT) == lane, 1.0, 0.0).astype(BF16)
    wide_lane = lax.broadcasted_iota(jnp.int32, (tm, width), 1)

    def project(c):
        return _dot(u, w_ref[:, c * width:(c + 1) * width])

    y = project(0)
    for c in range(5):
        y_next = project(c + 1) if c < 4 else None
        if c < 2:
            z = _rope(_head_rms(y, bd) * gain_ref[c:c + 1, :], cos, sin).astype(BF16)
            if c == 0:
                for kvh in range(N_KV_HEADS):
                    ks_ref[0, :, 2 * kvh * LANES:(2 * kvh + 1) * LANES] = z[:, kvh * LANES:(kvh + 1) * LANES]
                    ks_ref[0, :, (2 * kvh + 1) * LANES:(2 * kvh + 2) * LANES] = onehot
            else:
                kw_ref[0] = z
        elif c < 4:
            va = jnp.where((wide_lane & (LANES - 1)) < HEAD_DIM, y, 1.0).astype(BF16)
            if c == 2:
                vs_ref[0] = va
            else:
                vw_ref[0] = va
        else:
            kvc_ref[0] = y
        y = y_next


def _kv_proj(h, kv_norm_g, kvmod5, w, gain, cos, sin, bd):
    b, s, d = h.shape
    n = w.shape[1]
    tm = min(512, s)
    row = lambda width, dt: (pl.BlockSpec((1, tm, width), lambda bb, i: (bb, i, 0)),
                             jax.ShapeDtypeStruct((b, s, width), dt))
    outs = [row(4 * LANES, BF16), row(2 * LANES, BF16), row(2 * LANES, BF16), row(2 * LANES, BF16),
            row(2 * LANES, F32)]
    return pl.pallas_call(
        functools.partial(_kv_kernel, tm=tm),
        grid=(b, s // tm),
        in_specs=[pl.BlockSpec((1, tm, d), lambda bb, i: (bb, i, 0)),
                  pl.BlockSpec((1, d), lambda bb, i: (0, 0)),
                  _mod_block(d, 0, 0),
                  _mod_block(d, 0, 1),
                  pl.BlockSpec((d, n), lambda bb, i: (0, 0)),
                  pl.BlockSpec(gain.shape, lambda bb, i: (0, 0)),
                  pl.BlockSpec((tm, LANES), lambda bb, i: (i, 0)),
                  pl.BlockSpec((tm, LANES), lambda bb, i: (i, 0)),
                  pl.BlockSpec(bd.shape, lambda bb, i: (0, 0))],
        out_specs=[o[0] for o in outs],
        out_shape=[o[1] for o in outs],
        compiler_params=_params("parallel", "parallel"),
        name="shared_kv_proj",
    )(h, kv_norm_g, kvmod5, kvmod5, w, gain, cos, sin, bd)


def _cmp_kernel(t_ref, w1_ref, w2_ref, pos_ref, gain_ref, o_ref):
    is_key = pl.program_id(0) < N_KV_HEADS
    t = t_ref[...]
    w1 = w1_ref[...]
    half = w1.shape[0] // 2
    first = _split_dot(t, w1[:half])
    second = _split_dot(t, w1[half:])
    second = pltpu.roll(second, second.shape[0] - 1, 0)
    posb = _split_dot(jnp.broadcast_to(pos_ref[...], (8, w1.shape[0])), w1)[0:1]
    hid = first + second + posb
    out = _split_dot(hid * _sigmoid(hid), w2_ref[...])
    key = out * lax.rsqrt(jnp.mean(out * out, axis=-1, keepdims=True) + EPS) * gain_ref[...]
    dup = jnp.concatenate([key, key], axis=-1)
    aug = jnp.concatenate([out, jnp.ones_like(out)], axis=-1)
    o_ref[...] = jnp.where(is_key, dup, aug).astype(BF16)


def _compress(t4, phi_w1, phi_w2, cmp_pos, gain):
    _, b, nsb, width = t4.shape
    hid = phi_w1.shape[2]
    return pl.pallas_call(
        _cmp_kernel,
        grid=(4, b),
        in_specs=[pl.BlockSpec((None, None, nsb, width), lambda t, bb: (t, bb, 0, 0)),
                  pl.BlockSpec((None, CMP_LEN * HEAD_DIM, hid), lambda t, bb: (t // N_KV_HEADS, 0, 0)),
                  pl.BlockSpec((None, hid, HEAD_DIM), lambda t, bb: (t // N_KV_HEADS, 0, 0)),
                  pl.BlockSpec((None, 1, CMP_LEN * HEAD_DIM), lambda t, bb: (t // N_KV_HEADS, 0, 0)),
                  pl.BlockSpec((1, HEAD_DIM), lambda t, bb: (0, 0))],
        out_specs=pl.BlockSpec((None, None, nsb, LANES), lambda t, bb: (t, bb, 0, 0)),
        out_shape=jax.ShapeDtypeStruct((4, b, nsb, LANES), BF16),
        compiler_params=_params("parallel", "parallel"),
        name="compress_blocks",
    )(t4, phi_w1, phi_w2, cmp_pos, gain)


def _qg_kernel(h_ref, g_ref, sh_ref, sc_ref, w_ref, gain_ref, cos_ref, sin_ref, bd_ref,
               qn_ref, qr_ref, gate_ref):
    u = _normmod(h_ref[0], g_ref[...], sh_ref[...], sc_ref[...]).astype(BF16)
    bd = bd_ref[...]
    width = bd.shape[0]
    cos = jnp.concatenate([cos_ref[...]] * (width // LANES), axis=1)
    sin = jnp.concatenate([sin_ref[...]] * (width // LANES), axis=1)
    hdim = qn_ref.shape[2]

    def project(c):
        return _dot(u, w_ref[:, c * width:(c + 1) * width])

    y = project(0)
    for c in range(hdim // width):
        y_next = project(c + 1)
        sl = slice(c * width, (c + 1) * width)
        z = _head_rms(y, bd) * gain_ref[:, sl]
        qn_ref[0, :, sl] = z.astype(BF16)
        qr_ref[0, :, sl] = _rope(z, cos, sin).astype(BF16)
        y = y_next
    gate_ref[0] = _sigmoid(y)


def _qg_proj(h, norm_g3, mod5, w, gain, cos, sin, bd, l):
    b, s, d = h.shape
    n = w.shape[1]
    hdim = N_HEADS * HEAD_DIM
    tm = min(512, s)
    row = lambda width, dt: (pl.BlockSpec((1, tm, width), lambda bb, i: (bb, i, 0)),
                             jax.ShapeDtypeStruct((b, s, width), dt))
    outs = [row(hdim, BF16), row(hdim, BF16), row(n - hdim, F32)]
    return pl.pallas_call(
        _qg_kernel,
        grid=(b, s // tm),
        in_specs=[pl.BlockSpec((1, tm, d), lambda bb, i: (bb, i, 0)),
                  pl.BlockSpec((None, 1, d), lambda bb, i: (l * 3 + 1, 0, 0)),
                  _mod_block(d, l, 3),
                  _mod_block(d, l, 4),
                  pl.BlockSpec((d, n), lambda bb, i: (0, 0)),
                  pl.BlockSpec((1, hdim), lambda bb, i: (0, 0)),
                  pl.BlockSpec((tm, LANES), lambda bb, i: (i, 0)),
                  pl.BlockSpec((tm, LANES), lambda bb, i: (i, 0)),
                  pl.BlockSpec(bd.shape, lambda bb, i: (0, 0))],
        out_specs=[o[0] for o in outs],
        out_shape=[o[1] for o in outs],
        compiler_params=_params("parallel", "parallel"),
        name="nsa_query_gate_proj",
    )(h, norm_g3, mod5, mod5, w, gain, cos, sin, bd)


def _nsa_body(qn_ref, qr_ref, gate_ref, kc_ref, vc_ref, ks_ref, vs_ref, kw_ref, vw_ref, wimpt_ref, gsel_ref,
              o_ref, qa_scr, m_scr, acc_scr, mw_scr, accw_scr, *, tq, shifted):
    i = pl.program_id(2)
    start = i * tq
    rows = GROUP * tq
    ncp = kc_ref.shape[0]
    low = lax.broadcasted_iota(jnp.int32, (tq, LANES), 1) < HEAD_DIM
    low_r = lax.broadcasted_iota(jnp.int32, (rows, LANES), 1) < HEAD_DIM

    def accumulate(s, v, mask, first, acc_ref=acc_scr, m_ref=m_scr):
        if not shifted:
            p = jnp.exp2(s)
            if mask is not None:
                p = jnp.where(mask, p, 0.0)
            pv = _dot(p.astype(BF16), v)
            acc_ref[...] = pv if first else acc_ref[...] + pv
            return
        if mask is not None:
            s = jnp.where(mask, s, NEG)
        if first:
            m_new = jnp.max(s, axis=-1, keepdims=True)
            acc_ref[...] = _dot(jnp.exp2(s - m_new).astype(BF16), v)
        else:
            m_old = m_ref[...]
            m_new = jnp.maximum(m_old, jnp.max(s, axis=-1, keepdims=True))
            acc_ref[...] = jnp.exp2(m_old - m_new) * acc_ref[...] + _dot(jnp.exp2(s - m_new).astype(BF16), v)
        m_ref[...] = m_new

    def stack(q):
        parts = []
        for g in range(GROUP):
            blk = q[:, (g // 2) * LANES:(g // 2 + 1) * LANES]
            parts.append(jnp.where(low if g % 2 == 0 else ~low, blk, jnp.zeros_like(blk)))
        return parts

    qn = jnp.concatenate(stack(qn_ref[0]), axis=0)
    sc = _dot_t(qn, kc_ref[...])
    c_end = lax.broadcasted_iota(jnp.int32, (tq, ncp), 1) * CMP_STRIDE + (CMP_LEN - 1)
    valid_t = c_end <= start + lax.broadcasted_iota(jnp.int32, (tq, ncp), 0)
    if shifted:
        valid = jnp.concatenate([valid_t] * GROUP, axis=0)
        sc = jnp.where(valid, sc, NEG)
        sc = sc - jnp.max(sc, axis=-1, keepdims=True)
        p = jnp.where(valid, jnp.exp2(sc), 0.0)
    else:
        p = jnp.exp2(sc + jnp.concatenate([jnp.where(valid_t, 0.0, NEG)] * GROUP, axis=0))
    oc = _dot(p.astype(BF16), vc_ref[...])
    inv = 1.0 / jnp.maximum(jnp.where(low_r, pltpu.roll(oc, HEAD_DIM, 1), oc), 1e-30)
    o_cmp = oc * inv
    psum_blocks = []
    for cb in range(ncp // LANES):
        cs = slice(cb * LANES, (cb + 1) * LANES)
        tot = p[0:tq, cs] * inv[0:tq]
        for g in range(1, GROUP):
            tot = tot + p[g * tq:(g + 1) * tq, cs] * inv[g * tq:(g + 1) * tq]
        psum_blocks.append(tot)
    psum = jnp.concatenate(psum_blocks, axis=1)

    qr_parts = stack(qr_ref[0])
    for g in range(GROUP):
        qa_scr[g * tq:(g + 1) * tq, 0:LANES] = qr_parts[g]
    qr = jnp.concatenate(qr_parts, axis=0)

    tri = (lax.broadcasted_iota(jnp.int32, (rows, tq), 1)
           <= (lax.broadcasted_iota(jnp.int32, (rows, tq), 0) & (tq - 1)))

    def key_block(kb):
        return pl.ds(pl.multiple_of(kb * tq, tq), tq)

    diag = key_block(i)
    accumulate(_dot_t(qr, kw_ref[diag, :]), vw_ref[diag, :], tri, True, accw_scr, mw_scr)

    p_hi = psum.astype(BF16)
    p_lo = (psum - p_hi.astype(F32)).astype(BF16)
    imp_t = _dot_t(wimpt_ref[...], p_hi) + _dot_t(wimpt_ref[...], p_lo)

    blk = key_block(jnp.maximum(i - 1, 0))
    accumulate(_dot_t(qr, kw_ref[blk, :]), vw_ref[blk, :], jnp.broadcast_to(i >= 1, tri.shape), False,
               accw_scr, mw_scr)
    blk = key_block(jnp.maximum(i - 2, 0))
    accumulate(_dot_t(qr, kw_ref[blk, :]), vw_ref[blk, :], (~tri) & (i >= 2), False, accw_scr, mw_scr)

    blk_id = lax.broadcasted_iota(jnp.int32, (LANES, tq), 0)
    jt = (start + lax.broadcasted_iota(jnp.int32, (LANES, tq), 1)) >> SEL_SHIFT
    forced = (blk_id == 0) | (blk_id == jt) | (blk_id == jt - 1)
    work = jnp.where(blk_id > jt, -1.0, jnp.where(forced, -2.0, imp_t))
    sel = forced
    for _ in range(N_SELECT - 3):
        mx = jnp.max(work, axis=0, keepdims=True)
        idx = jnp.min(jnp.where(work == mx, blk_id, LANES), axis=0, keepdims=True)
        pick = blk_id == idx
        sel = sel | pick
        work = jnp.where(pick, -2.0, work)
    sel_bias = jnp.transpose(jnp.where(sel, 0.0, NEG)).astype(BF16)

    for g in range(GROUP):
        qa_scr[g * tq:(g + 1) * tq, LANES:2 * LANES] = sel_bias

    accumulate(_dot_t(qa_scr[...], ks_ref[diag, :]), vs_ref[diag, :], tri, True)

    def key_span(first_block, n_blocks):
        return pl.ds(pl.multiple_of(first_block * tq, tq), n_blocks * tq)

    def slc_quad(kq, carry):
        span = key_span(4 * kq, 4)
        accumulate(_dot_t(qa_scr[...], ks_ref[span, :]), vs_ref[span, :], None, False)
        return carry

    lax.fori_loop(0, i // 4, slc_quad, 0)

    @pl.when(i % 4 >= 2)
    def _():
        span = key_span((i // 4) * 4, 2)
        accumulate(_dot_t(qa_scr[...], ks_ref[span, :]), vs_ref[span, :], None, False)

    @pl.when(i % 2 == 1)
    def _():
        blk = key_block(i - 1)
        accumulate(_dot_t(qa_scr[...], ks_ref[blk, :]), vs_ref[blk, :], None, False)

    gates = gate_ref[0]
    g_hi = gates.astype(BF16)
    g_lo = (gates - g_hi.astype(F32)).astype(BF16)
    gate_wide = _dot(g_hi, gsel_ref[...]) + _dot(g_lo, gsel_ref[...])
    for pr in range(GROUP // 2):
        ev = slice(2 * pr * tq, (2 * pr + 1) * tq)
        od = slice((2 * pr + 1) * tq, (2 * pr + 2) * tq)

        def pair(acc):
            a_e, a_o = acc[ev], acc[od]
            r_e = pltpu.roll(a_e, HEAD_DIM, 1)
            r_o = pltpu.roll(a_o, HEAD_DIM, 1)
            return jnp.where(low, a_e, r_o) / jnp.where(low, r_e, a_o)

        def gate(br):
            c = (br * (GROUP // 2) + pr) * LANES
            return gate_wide[:, c:c + LANES]

        o_c = jnp.where(low, o_cmp[ev], pltpu.roll(o_cmp[od], HEAD_DIM, 1))
        out = gate(0) * o_c + gate(1) * pair(acc_scr) + gate(2) * pair(accw_scr)
        o_ref[0, :, pr * LANES:(pr + 1) * LANES] = out.astype(BF16)


def _nsa_kernel(small_ref, *refs, tq):
    @pl.when(small_ref[0] != 0)
    def _():
        _nsa_body(*refs, tq=tq, shifted=False)

    @pl.when(small_ref[0] == 0)
    def _():
        _nsa_body(*refs, tq=tq, shifted=True)


def _nsa_attention(small, qn, qr, gates, cmp4, ks, vs, kw, vw, wimpt):
    b, s, hdim = qn.shape
    tq = min(256, s)
    assert s // SEL_BLOCK <= LANES and s % tq == 0 and SLIDE_WINDOW == 2 * tq
    ncp = cmp4.shape[2]
    gw = GROUP * HEAD_DIM
    rows = GROUP * tq
    qspec = pl.BlockSpec((1, tq, gw), lambda bb, kv, i: (bb, i, kv))
    full = lambda width: pl.BlockSpec((None, s, width), lambda bb, kv, i: (bb, 0, kv))
    col = np.arange(3 * (GROUP // 2) * LANES)[None, :]
    src = (col // LANES // (GROUP // 2)) * GROUP + 2 * ((col // LANES) % (GROUP // 2)) + (col % LANES) // HEAD_DIM
    gsel = (np.arange(LANES)[:, None] == src).astype(BF16)
    return pl.pallas_call(
        functools.partial(_nsa_kernel, tq=tq),
        grid=(b, N_KV_HEADS, s // tq),
        in_specs=[pl.BlockSpec(memory_space=pltpu.SMEM), qspec, qspec,
                  pl.BlockSpec((1, tq, LANES), lambda bb, kv, i: (bb, i, kv)),
                  pl.BlockSpec((None, None, ncp, LANES), lambda bb, kv, i: (kv, bb, 0, 0)),
                  pl.BlockSpec((None, None, ncp, LANES), lambda bb, kv, i: (N_KV_HEADS + kv, bb, 0, 0)),
                  full(2 * LANES), full(LANES), full(LANES), full(LANES),
                  pl.BlockSpec((LANES, ncp), lambda bb, kv, i: (0, 0)),
                  pl.BlockSpec(gsel.shape, lambda bb, kv, i: (0, 0))],
        out_specs=qspec,
        out_shape=jax.ShapeDtypeStruct((b, s, hdim), BF16),
        scratch_shapes=[pltpu.VMEM((rows, 2 * LANES), BF16),
                        pltpu.VMEM((rows, 1), F32), pltpu.VMEM((rows, LANES), F32),
                        pltpu.VMEM((rows, 1), F32), pltpu.VMEM((rows, LANES), F32)],
        compiler_params=_params("parallel", "parallel", "arbitrary"),
        name="nsa_attention",
    )(small, qn, qr, gates, cmp4, cmp4, ks, vs, kw, vw, wimpt, gsel)


def _rope_tables(s, tm=None, rate=1):
    half = HEAD_DIM // 2
    row = lax.broadcasted_iota(jnp.int32, (s, LANES), 0)
    lane = lax.broadcasted_iota(jnp.int32, (s, LANES), 1)
    if rate > 1:
        seg = tm // rate
        local = row % tm
        row = row - local + (local % seg) * rate + local // seg
    inv = jnp.tile(ROPE_THETA ** (-jnp.arange(half, dtype=F32) / half), LANES // half)
    ang = row.astype(F32) * inv[None, :]
    cos, sin = jnp.cos(ang), jnp.sin(ang)
    return cos, jnp.where(lane % HEAD_DIM < half, -sin, sin), sin


def _split_half(t, axis):
    hpg = MXU_WIDTH // HEAD_DIM
    half = HEAD_DIM // 2
    shape = t.shape
    t = t.reshape(shape[:axis] + (shape[axis] // MXU_WIDTH, hpg, 2, half) + shape[axis + 1:])
    return t.swapaxes(axis + 1, axis + 2).reshape(shape)


def _importance_weights(ncp):
    r = SEL_BLOCK // CMP_STRIDE
    nper = CMP_LEN // CMP_STRIDE
    o = np.arange(ncp)[:, None] - r * np.arange(LANES)[None, :]
    w = np.zeros((ncp, LANES), np.float32)
    for off in range(-(nper - 1), r):
        cnt = sum(1 for m in range(r) for n in range(nper) if m - n == off)
        w += np.where(o == off, float(cnt), 0.0)
    return w.astype(BF16)


def kernel(x, c, norm_g, w_ada, b_ada, ffn_w_in, ffn_w_out, a_w_qkv, a_q_gain, a_k_gain, a_w_o, kv_norm_g,
           w_ada_kv, b_ada_kv, w_kv, kv_k_gain, cmp_pos, phi_w1, phi_w2, b_w_qg, b_q_gain, b_w_o):
    b, s, d = x.shape
    depth = norm_g.shape[0]
    n_a = depth // 2
    hdim = N_HEADS * HEAD_DIM
    assert depth == 2 and b <= 8

    c8 = jnp.zeros((8, d), F32).at[:b].set(c)
    mod = _ada(c8, w_ada, b_ada.reshape(depth, 1, 9 * d))
    mod5 = mod[:, :b].reshape(depth, b, 9, 1, d)
    kvmod = _ada(c8, w_ada_kv[None], b_ada_kv.reshape(1, 1, 2 * d))
    kvmod5 = kvmod[:, :b].reshape(1, b, 2, 1, d)

    norm_g3 = norm_g.reshape(depth * 3, 1, d)
    w_in = ffn_w_in.astype(BF16)
    w_out = ffn_w_out.astype(BF16)
    cos, sin, _ = _rope_tables(s)
    blk = np.arange(MXU_WIDTH) // HEAD_DIM
    bd_wide = (blk[:, None] == blk[None, :]).astype(BF16)
    blk = (np.arange(MXU_WIDTH) % LANES) // (HEAD_DIM // 2)
    bd_split = (blk[:, None] == blk[None, :]).astype(BF16)
    q_scale = HEAD_DIM ** -0.5 * LOG2E

    h = x
    for l in range(depth):
        h = _ffn(h, norm_g3, mod5, w_in, w_out, l, 0, 0)
        if l < n_a:
            gain = jnp.stack([_split_half(jnp.tile(a_q_gain[l] * q_scale, (1, N_HEADS)), 1),
                              _split_half(jnp.tile(a_k_gain[l], (1, N_HEADS)), 1),
                              jnp.ones((N_DIL, hdim), F32)], axis=1).reshape(3 * N_DIL, 1, hdim)
            cols = np.arange(N_DIL * 3 * hdim).reshape(N_DIL, 3, hdim)
            cols = np.concatenate([_split_half(cols[:, :2], 2), cols[:, 2:]], axis=1).reshape(-1)
            w_qkv = a_w_qkv[l][:, cols].astype(BF16)
            bound = HEAD_DIM * q_scale * jnp.max(jnp.abs(a_q_gain[l])) * jnp.max(jnp.abs(a_k_gain[l]))
            small = (bound <= SCORE_LIMIT).astype(jnp.int32).reshape(1)
            parts = [_dilated_attention(small, _qkv_proj(h, norm_g3, mod5, w_qkv, gain, bd_split, l, g), g)
                     for g in range(N_DIL)]
            h = _merge_out([p[0] for p in parts], [p[1] for p in parts], h, mod5, a_w_o[l].astype(BF16), l)
        else:
            j = l - n_a
            w_q = b_w_qg[j][:, :hdim]
            w_g = b_w_qg[j][:, hdim:].reshape(d, 3, N_KV_HEADS, GROUP).transpose(0, 2, 1, 3)
            w_g = jnp.pad(w_g.reshape(d, N_KV_HEADS, 3 * GROUP), ((0, 0), (0, 0), (0, LANES - 3 * GROUP)))
            w_qg = jnp.concatenate([w_q, w_g.reshape(d, N_KV_HEADS * LANES)], axis=1).astype(BF16)
            gain = jnp.tile(b_q_gain[j] * q_scale, N_HEADS)[None, :]
            qn, qr, gates = _qg_proj(h, norm_g3, mod5, w_qg, gain, cos, sin, bd_wide, l)
            bound = HEAD_DIM * q_scale * jnp.max(jnp.abs(b_q_gain[j])) * jnp.max(jnp.abs(kv_k_gain))
            small = (bound <= SCORE_LIMIT).astype(jnp.int32).reshape(1)
            o = _nsa_attention(small, qn, qr, gates, *shared)
            h = _out_proj(o, h, mod5, b_w_o[j].astype(BF16), l)
        h = _ffn(h, norm_g3, mod5, w_in, w_out, l, 2, 1)
        if l == n_a - 1:
            col = lambda br, kv, kvh: br * 4 * HEAD_DIM + kv * 2 * HEAD_DIM + kvh * HEAD_DIM + jnp.arange(HEAD_DIM)
            dup = lambda br, kv, kvh: jnp.concatenate([col(br, kv, kvh)] * 2)
            cols = jnp.concatenate(
                [dup(br, 0, kvh) for br in (1, 2) for kvh in range(N_KV_HEADS)]
                + [dup(br, 1, kvh) for br in (1, 2) for kvh in range(N_KV_HEADS)]
                + [col(0, kv, kvh) for kv in range(2) for kvh in range(N_KV_HEADS)])
            w_kv_ext = w_kv[:, cols].astype(BF16)
            reps = MXU_WIDTH // HEAD_DIM
            kgain = jnp.concatenate([jnp.tile(kv_k_gain[1], (1, reps)), jnp.tile(kv_k_gain[2], (1, reps)),
                                     jnp.ones((6, MXU_WIDTH), F32)], axis=0)
            ks, kw, vs, vw, kvc = _kv_proj(h, kv_norm_g[None, :], kvmod5, w_kv_ext, kgain, cos, sin, bd_wide)
            nsb = s // CMP_STRIDE
            t4 = kvc.reshape(b, nsb, CMP_STRIDE, 4, HEAD_DIM).transpose(3, 0, 1, 2, 4)
            t4 = t4.reshape(4, b, nsb, CMP_STRIDE * HEAD_DIM)
            cmp4 = _compress(t4, phi_w1, phi_w2, cmp_pos.reshape(2, 1, CMP_LEN * HEAD_DIM), kv_k_gain[0][None, :])
            shared = (cmp4, ks, vs, kw, vw, _importance_weights(nsb).T)
    return h
```

```python
import functools

import jax
import jax.numpy as jnp
import numpy as np
from jax import lax
from jax.experimental import pallas as pl
from jax.experimental.pallas import tpu as pltpu

HEAD_DIM = 64
N_HEADS = 16
DIL_WINDOWS = (128, 512, 2048)
DIL_RATES = (1, 4, 16)
N_DIL = 3
N_KV_HEADS = 2
GROUP = N_HEADS // N_KV_HEADS
CMP_STRIDE = 16
CMP_LEN = 32
SEL_BLOCK = 64
SEL_SHIFT = SEL_BLOCK.bit_length() - 1
N_SELECT = 16
SLIDE_WINDOW = 512
ROPE_THETA = 10000.0
EPS = 1e-6

LANES = 128
MXU_WIDTH = 256
Q_ROWS = 128
NEG = -1e30
LOG2E = 1.4426950408889634
SCORE_LIMIT = 60.0
VMEM_LIMIT = 56 * 1024 * 1024

BF16 = jnp.bfloat16
F32 = jnp.float32


def _params(*sem):
    return pltpu.CompilerParams(dimension_semantics=sem, vmem_limit_bytes=VMEM_LIMIT)


def _dot(a, b):
    return jnp.dot(a, b, preferred_element_type=F32)


def _dot_t(a, b):
    return lax.dot_general(a, b, (((1,), (1,)), ((), ())), preferred_element_type=F32)


def _split_dot(a, b):
    a_hi = a.astype(BF16)
    a_lo = (a - a_hi.astype(F32)).astype(BF16)
    b_hi = b.astype(BF16)
    b_lo = (b - b_hi.astype(F32)).astype(BF16)
    return _dot(a_hi, b_hi) + (_dot(a_hi, b_lo) + _dot(a_lo, b_hi))


def _sigmoid(x):
    return 1.0 / (1.0 + jnp.exp(-x))


def _normmod(x, g, shift, scale):
    return x * lax.rsqrt(jnp.mean(x * x, axis=-1, keepdims=True) + EPS) * (g * (1.0 + scale)) + shift


def _head_rms(y, bd):
    ss = _dot((y * y).astype(BF16), bd)
    return y * lax.rsqrt(ss * (1.0 / HEAD_DIM) + EPS)


def _rope(z, cos, sin_signed):
    lane = lax.broadcasted_iota(jnp.int32, z.shape, 1)
    first_half = (lane & (HEAD_DIM - 1)) < HEAD_DIM // 2
    swapped = jnp.where(first_half, pltpu.roll(z, z.shape[1] - HEAD_DIM // 2, 1),
                        pltpu.roll(z, HEAD_DIM // 2, 1))
    return z * cos + swapped * sin_signed


def _ada_kernel(c_ref, w_ref, b_ref, o_ref):
    c = c_ref[...]
    a = (c * _sigmoid(c)).astype(BF16)
    o_ref[...] = _dot(a, w_ref[...].astype(BF16)) + b_ref[...]


def _ada(c8, w, b):
    nl, d, n = w.shape
    tn = 1024
    return pl.pallas_call(
        _ada_kernel,
        grid=(nl, n // tn),
        in_specs=[pl.BlockSpec((8, d), lambda l, j: (0, 0)),
                  pl.BlockSpec((None, d, tn), lambda l, j: (l, 0, j)),
                  pl.BlockSpec((None, 1, tn), lambda l, j: (l, 0, j))],
        out_specs=pl.BlockSpec((None, 8, tn), lambda l, j: (l, 0, j)),
        out_shape=jax.ShapeDtypeStruct((nl, 8, n), F32),
        compiler_params=_params("parallel", "parallel"),
        name="ada_table",
    )(c8, w, b)


def _mod_block(d, l, k):
    return pl.BlockSpec((None, None, None, 1, d), lambda b, *_: (l, b, k, 0, 0))


def _ffn_kernel(h_ref, g_ref, sh_ref, sc_ref, gt_ref, wg_ref, wu_ref, wo_ref, o_ref, u_scr, acc_scr):
    u_scr[...] = _normmod(h_ref[0], g_ref[...], sh_ref[...], sc_ref[...]).astype(BF16)

    u = u_scr[...]
    tf = wg_ref.shape[1]
    bounds = [(lo, min(lo + MXU_WIDTH, tf)) for lo in range(0, tf, MXU_WIDTH)]

    def gate_up(c):
        lo, hi = bounds[c]
        return _dot(u, wg_ref[:, lo:hi]), _dot(u, wu_ref[:, lo:hi])

    nxt = gate_up(0)
    for c, (lo, hi) in enumerate(bounds):
        gate, up = nxt
        if c + 1 < len(bounds):
            nxt = gate_up(c + 1)
        act = (gate * _sigmoid(gate) * up).astype(BF16)
        down = _dot(act, wo_ref[lo:hi, :])
        acc_scr[...] = down if c == 0 else acc_scr[...] + down

    o_ref[0] = h_ref[0] + (0.5 * (1.0 + gt_ref[...])) * acc_scr[...]


def _ffn(h, norm_g3, mod5, w_in, w_out, l, sub, which):
    b, s, d = h.shape
    f = w_out.shape[2]
    tm = min(1024, s)
    resident = pl.Buffered(1)
    return pl.pallas_call(
        _ffn_kernel,
        grid=(b, s // tm),
        in_specs=[pl.BlockSpec((1, tm, d), lambda bb, i: (bb, i, 0)),
                  pl.BlockSpec((None, 1, d), lambda bb, i: (l * 3 + sub, 0, 0)),
                  _mod_block(d, l, sub * 3 + 0),
                  _mod_block(d, l, sub * 3 + 1),
                  _mod_block(d, l, sub * 3 + 2),
                  pl.BlockSpec((None, None, d, f), lambda bb, i: (l, which, 0, 0), pipeline_mode=resident),
                  pl.BlockSpec((None, None, d, f), lambda bb, i: (l, which, 0, 1), pipeline_mode=resident),
                  pl.BlockSpec((None, None, f, d), lambda bb, i: (l, which, 0, 0), pipeline_mode=resident)],
        out_specs=pl.BlockSpec((1, tm, d), lambda bb, i: (bb, i, 0)),
        out_shape=jax.ShapeDtypeStruct((b, s, d), F32),
        scratch_shapes=[pltpu.VMEM((tm, d), BF16), pltpu.VMEM((tm, d), F32)],
        compiler_params=_params("parallel", "parallel"),
        name="swiglu_half_step",
    )(h, norm_g3, mod5, mod5, mod5, w_in, w_in, w_out)


def _qkv_kernel(h_ref, g_ref, sh_ref, sc_ref, w_ref, gain_ref, cos_ref, sin_ref, bd_ref, o_ref, u_scr, uf_scr,
                *, rate):
    tm = u_scr.shape[0]
    seg = tm // rate
    u = _normmod(h_ref[0], g_ref[...], sh_ref[...], sc_ref[...])
    if rate == 1:
        u_scr[...] = u.astype(BF16)
    else:
        for cb in range(u.shape[1] // LANES):
            sl = slice(cb * LANES, (cb + 1) * LANES)
            uf_scr[cb] = u[:, sl]
            for res in range(rate):
                u_scr[res * seg:(res + 1) * seg, sl] = uf_scr[cb, pl.ds(res, seg, stride=rate), :].astype(BF16)

    n_grp = o_ref.shape[1]
    per_kind = n_grp // 3
    cos = cos_ref[...]
    sin = sin_ref[...]
    bd = bd_ref[...]

    def project(grp):
        return _dot(u_scr[...], w_ref[:, grp * MXU_WIDTH:(grp + 1) * MXU_WIDTH])

    y = project(0)
    for grp in range(n_grp):
        y_next = project(grp + 1) if grp + 1 < n_grp else None
        kind, sub = divmod(grp, per_kind)
        if kind < 2:
            z = _head_rms(y, bd) * gain_ref[kind, :, sub * MXU_WIDTH:(sub + 1) * MXU_WIDTH]
            x1, x2 = z[:, :LANES], z[:, LANES:]
            y = jnp.concatenate([x1 * cos - x2 * sin, x1 * sin + x2 * cos], axis=1)
        y = y.astype(BF16)
        for res in range(rate):
            o_ref[res, grp] = y[res * seg:(res + 1) * seg]
        y = y_next


def _qkv_proj(h, norm_g3, mod5, w, gain, bd, l, g):
    b, s, d = h.shape
    rate = DIL_RATES[g]
    tm = min(1024, s)
    seg = tm // rate
    tn = N_HEADS * HEAD_DIM
    n_grp = tn // MXU_WIDTH
    cos, _, sin = _rope_tables(s, tm, rate)

    return pl.pallas_call(
        functools.partial(_qkv_kernel, rate=rate),
        grid=(b, s // tm),
        in_specs=[pl.BlockSpec((1, tm, d), lambda bb, i: (bb, i, 0)),
                  pl.BlockSpec((None, 1, d), lambda bb, i: (l * 3 + 1, 0, 0)),
                  _mod_block(d, l, 3),
                  _mod_block(d, l, 4),
                  pl.BlockSpec((d, 3 * tn), lambda bb, i: (0, g), pipeline_mode=pl.Buffered(1)),
                  pl.BlockSpec((3, 1, tn), lambda bb, i: (g, 0, 0)),
                  pl.BlockSpec((tm, LANES), lambda bb, i: (i, 0)),
                  pl.BlockSpec((tm, LANES), lambda bb, i: (i, 0)),
                  pl.BlockSpec(bd.shape, lambda bb, i: (0, 0))],
        out_specs=pl.BlockSpec((None, rate, 3 * n_grp, seg, MXU_WIDTH), lambda bb, i: (bb, 0, 0, i, 0)),
        out_shape=jax.ShapeDtypeStruct((b, rate, 3 * n_grp, s // rate, MXU_WIDTH), BF16),
        scratch_shapes=[pltpu.VMEM((tm, d), BF16), pltpu.VMEM((d // LANES, tm, LANES), F32)],
        compiler_params=_params("parallel", "parallel"),
        name=f"qkv_proj_rate{rate}",
    )(h, norm_g3, mod5, mod5, w, gain, cos, sin, bd)


def _dil_body(bias_ref, q_ref, kp_ref, kc_ref, vp_ref, vc_ref, o_ref, lse_ref, *, tq, shifted):
    i = pl.program_id(2)
    n_grp = q_ref.shape[0]
    hpg = MXU_WIDTH // HEAD_DIM
    qlane = lax.broadcasted_iota(jnp.int32, (Q_ROWS, MXU_WIDTH), 1)
    q_head = (qlane & (LANES - 1)) // (HEAD_DIM // 2)
    v_head = qlane // HEAD_DIM
    seg = lax.broadcasted_iota(jnp.int32, (Q_ROWS, LANES), 1) // (LANES // N_HEADS)

    def scores(grp, j):
        q = q_ref[grp, j * Q_ROWS:(j + 1) * Q_ROWS, :]
        zero = jnp.zeros_like(q)
        q2 = jnp.concatenate([jnp.where(q_head == t, q, zero) for t in range(hpg)], axis=0)
        if j == 0:
            k = jnp.concatenate([kp_ref[grp], kc_ref[grp, 0:Q_ROWS, :]], axis=0)
        else:
            k = kc_ref[grp, (j - 1) * Q_ROWS:(j + 1) * Q_ROWS, :]
        return _dot_t(q2, k)

    for j in range(tq // Q_ROWS):
        rs = slice(j * Q_ROWS, (j + 1) * Q_ROWS)
        lse_tile = None
        for grp in range(n_grp):
            s = scores(grp, j) + bias_ref[...]
            if j == 0:
                v = jnp.concatenate([vp_ref[grp], vc_ref[grp, 0:Q_ROWS, :]], axis=0)
                prev_key = lax.broadcasted_iota(jnp.int32, s.shape, 1) < Q_ROWS
                s = jnp.where(prev_key & (i == 0), NEG, s)
            else:
                v = vc_ref[grp, (j - 1) * Q_ROWS:(j + 1) * Q_ROWS, :]
            if shifted:
                m = jnp.max(s, axis=-1, keepdims=True)
                p = jnp.exp2(s - m)
            else:
                p = jnp.exp2(s)
            den = jnp.sum(p, axis=-1, keepdims=True)
            o2 = _dot(p.astype(BF16), v) / den
            lse2 = jnp.log(den) * LOG2E
            if shifted:
                lse2 = lse2 + m
            out = o2[0:Q_ROWS]
            for t in range(hpg):
                ts = slice(t * Q_ROWS, (t + 1) * Q_ROWS)
                if t:
                    out = jnp.where(v_head == t, o2[ts], out)
                head_lse = jnp.broadcast_to(lse2[ts], (Q_ROWS, LANES))
                lse_tile = head_lse if lse_tile is None else jnp.where(seg == hpg * grp + t, head_lse, lse_tile)
            o_ref[0, rs, grp * MXU_WIDTH:(grp + 1) * MXU_WIDTH] = out.astype(BF16)
        lse_ref[0, rs, :] = lse_tile


def _dil_kernel(small_ref, *refs, tq):
    @pl.when(small_ref[0] != 0)
    def _():
        _dil_body(*refs, tq=tq, shifted=False)

    @pl.when(small_ref[0] == 0)
    def _():
        _dil_body(*refs, tq=tq, shifted=True)


def _dilated_attention(small, qkv_r, g):
    b, rate, n3, l, width = qkv_r.shape
    s = l * rate
    n_back = DIL_WINDOWS[g] // rate
    assert n_back <= Q_ROWS and rate == DIL_RATES[g] and width == MXU_WIDTH
    tq = min(512, l)
    hdim = N_HEADS * HEAD_DIM
    n_grp = n3 // 3
    sub = tq // Q_ROWS

    def cur(kind):
        return pl.BlockSpec((None, None, n_grp, tq, width), lambda bb, r, i: (bb, r, kind, i, 0))

    def prev(kind):
        return pl.BlockSpec((None, None, n_grp, Q_ROWS, width),
                            lambda bb, r, i: (bb, r, kind, jnp.maximum(i * sub - 1, 0), 0))

    rows = (width // HEAD_DIM) * Q_ROWS
    dist = Q_ROWS + (np.arange(rows)[:, None] % Q_ROWS) - np.arange(2 * Q_ROWS)[None, :]
    bias = np.where((dist >= 0) & (dist <= n_back), 0.0, NEG).astype(np.float32)

    o, lse = pl.pallas_call(
        functools.partial(_dil_kernel, tq=tq),
        grid=(b, rate, l // tq),
        in_specs=[pl.BlockSpec(memory_space=pltpu.SMEM),
                  pl.BlockSpec(bias.shape, lambda bb, r, i: (0, 0)),
                  cur(0), prev(1), cur(1), prev(2), cur(2)],
        out_specs=[pl.BlockSpec((1, None, tq, hdim), lambda bb, r, i: (bb, r, i, 0)),
                   pl.BlockSpec((1, None, tq, LANES), lambda bb, r, i: (bb, r, i, 0))],
        out_shape=[jax.ShapeDtypeStruct((b, rate, l, hdim), BF16),
                   jax.ShapeDtypeStruct((b, rate, l, LANES), F32)],
        compiler_params=_params("parallel", "parallel", "parallel"),
        name=f"dilated_attention_rate{rate}",
    )(small, bias, qkv_r, qkv_r, qkv_r, qkv_r, qkv_r)
    return o, lse


def _merge_out_kernel(o0_ref, o1_ref, o2_ref, l0_ref, l1_ref, l2_ref, p1_ref, p2_ref, e_ref, h_ref, gt_ref, w_ref,
                      out_ref):
    def positions(x_ref, p_ref):
        x = jnp.concatenate([x_ref[res] for res in range(x_ref.shape[0])], axis=0)
        if p_ref is None:
            return x.astype(F32)
        if x.dtype == BF16:
            return _dot(p_ref[...], x)
        hi = x.astype(BF16)
        return _dot(p_ref[...], hi) + _dot(p_ref[...], (x - hi.astype(F32)).astype(BF16))

    l0, l1, l2 = positions(l0_ref, None), positions(l1_ref, p1_ref), positions(l2_ref, p2_ref)
    m = jnp.maximum(jnp.maximum(l0, l1), l2)
    e0, e1, e2 = jnp.exp2(l0 - m), jnp.exp2(l1 - m), jnp.exp2(l2 - m)
    inv = 1.0 / (e0 + e1 + e2)
    spread = e_ref[...]
    o = (_dot((e0 * inv).astype(BF16), spread) * positions(o0_ref, None)
         + _dot((e1 * inv).astype(BF16), spread) * positions(o1_ref, p1_ref)
         + _dot((e2 * inv).astype(BF16), spread) * positions(o2_ref, p2_ref))
    out_ref[0] = h_ref[0] + (1.0 + gt_ref[...]) * _dot(o.astype(BF16), w_ref[...])


def _merge_out(os_, lses, h, mod5, w_o, l):
    b, s, d = h.shape
    hdim = w_o.shape[0]
    tm = min(512, s)
    row = lambda width: pl.BlockSpec((1, tm, width), lambda bb, i: (bb, i, 0))
    rep = LANES // N_HEADS
    lane = np.arange(LANES)[:, None]
    col = np.arange(hdim)[None, :]
    spread = ((lane % rep == 0) & (lane // rep == col // HEAD_DIM)).astype(BF16)

    def unpermute(rate):
        pos = np.arange(tm)
        return (((pos % rate) * (tm // rate) + pos // rate)[:, None] == np.arange(tm)[None, :]).astype(BF16)

    def residue_spec(t):
        rate, width = t.shape[1], t.shape[3]
        return pl.BlockSpec((None, rate, tm // rate, width), lambda bb, i: (bb, 0, i, 0))

    const = lambda shape: pl.BlockSpec(shape, lambda bb, i: (0, 0))
    return pl.pallas_call(
        _merge_out_kernel,
        grid=(b, s // tm),
        in_specs=[residue_spec(t) for t in (*os_, *lses)]
        + [const((tm, tm)), const((tm, tm)), const((LANES, hdim)), row(d), _mod_block(d, l, 5), const((hdim, d))],
        out_specs=row(d),
        out_shape=jax.ShapeDtypeStruct((b, s, d), F32),
        compiler_params=_params("parallel", "parallel"),
        name="merge_out_proj",
    )(*os_, *lses, unpermute(os_[1].shape[1]), unpermute(os_[2].shape[1]), spread, h, mod5, w_o)


def _out_kernel(o_ref, h_ref, gt_ref, w_ref, out_ref):
    out_ref[0] = h_ref[0] + (1.0 + gt_ref[...]) * _dot(o_ref[0], w_ref[...])


def _out_proj(o, h, mod5, w_o, l):
    b, s, d = h.shape
    hdim = w_o.shape[0]
    tm = min(1024, s)
    row = lambda width: pl.BlockSpec((1, tm, width), lambda bb, i: (bb, i, 0))
    return pl.pallas_call(
        _out_kernel,
        grid=(b, s // tm),
        in_specs=[row(hdim), row(d), _mod_block(d, l, 5), pl.BlockSpec((hdim, d), lambda bb, i: (0, 0))],
        out_specs=row(d),
        out_shape=jax.ShapeDtypeStruct((b, s, d), F32),
        compiler_params=_params("parallel", "parallel"),
        name="out_proj",
    )(o, h, mod5, w_o)


def _kv_kernel(h_ref, g_ref, sh_ref, sc_ref, w_ref, gain_ref, cos_ref, sin_ref, bd_ref,
               ks_ref, kw_ref, vs_ref, vw_ref, kvc_ref, *, tm):
    i = pl.program_id(1)
    u = _normmod(h_ref[0], g_ref[...], sh_ref[...], sc_ref[...]).astype(BF16)
    bd = bd_ref[...]
    width = bd.shape[0]
    cos = jnp.concatenate([cos_ref[...]] * (width // LANES), axis=1)
    sin = jnp.concatenate([sin_ref[...]] * (width // LANES), axis=1)
    lane = lax.broadcasted_iota(jnp.int32, (tm, LANES), 1)
    pos = i * tm + lax.broadcasted_iota(jnp.int32, (tm, LANES), 0)
    onehot = jnp.where((pos >> SEL_SHIFT) == lane, 1.0, 0.0).astype(BF16)
    wide_lane = lax.broadcasted_iota(jnp.int32, (tm, width), 1)

    def project(c):
        return _dot(u, w_ref[:, c * width:(c + 1) * width])

    y = project(0)
    for c in range(5):
        y_next = project(c + 1) if c < 4 else None
        if c < 2:
            z = _rope(_head_rms(y, bd) * gain_ref[c:c + 1, :], cos, sin).astype(BF16)
            if c == 0:
                for kvh in range(N_KV_HEADS):
                    ks_ref[0, :, 2 * kvh * LANES:(2 * kvh + 1) * LANES] = z[:, kvh * LANES:(kvh + 1) * LANES]
                    ks_ref[0, :, (2 * kvh + 1) * LANES:(2 * kvh + 2) * LANES] = onehot
            else:
                kw_ref[0] = z
        elif c < 4:
            va = jnp.where((wide_lane & (LANES - 1)) < HEAD_DIM, y, 1.0).astype(BF16)
            if c == 2:
                vs_ref[0] = va
            else:
                vw_ref[0] = va
        else:
            kvc_ref[0] = y
        y = y_next


def _kv_proj(h, kv_norm_g, kvmod5, w, gain, cos, sin, bd):
    b, s, d = h.shape
    n = w.shape[1]
    tm = min(512, s)
    row = lambda width, dt: (pl.BlockSpec((1, tm, width), lambda bb, i: (bb, i, 0)),
                             jax.ShapeDtypeStruct((b, s, width), dt))
    outs = [row(4 * LANES, BF16), row(2 * LANES, BF16), row(2 * LANES, BF16), row(2 * LANES, BF16),
            row(2 * LANES, F32)]
    return pl.pallas_call(
        functools.partial(_kv_kernel, tm=tm),
        grid=(b, s // tm),
        in_specs=[pl.BlockSpec((1, tm, d), lambda bb, i: (bb, i, 0)),
                  pl.BlockSpec((1, d), lambda bb, i: (0, 0)),
                  _mod_block(d, 0, 0),
                  _mod_block(d, 0, 1),
                  pl.BlockSpec((d, n), lambda bb, i: (0, 0)),
                  pl.BlockSpec(gain.shape, lambda bb, i: (0, 0)),
                  pl.BlockSpec((tm, LANES), lambda bb, i: (i, 0)),
                  pl.BlockSpec((tm, LANES), lambda bb, i: (i, 0)),
                  pl.BlockSpec(bd.shape, lambda bb, i: (0, 0))],
        out_specs=[o[0] for o in outs],
        out_shape=[o[1] for o in outs],
        compiler_params=_params("parallel", "parallel"),
        name="shared_kv_proj",
    )(h, kv_norm_g, kvmod5, kvmod5, w, gain, cos, sin, bd)


def _cmp_kernel(t_ref, w1_ref, w2_ref, pos_ref, gain_ref, o_ref):
    is_key = pl.program_id(0) < N_KV_HEADS
    t = t_ref[...]
    w1 = w1_ref[...]
    half = w1.shape[0] // 2
    first = _split_dot(t, w1[:half])
    second = _split_dot(t, w1[half:])
    second = pltpu.roll(second, second.shape[0] - 1, 0)
    posb = _split_dot(jnp.broadcast_to(pos_ref[...], (8, w1.shape[0])), w1)[0:1]
    hid = first + second + posb
    out = _split_dot(hid * _sigmoid(hid), w2_ref[...])
    key = out * lax.rsqrt(jnp.mean(out * out, axis=-1, keepdims=True) + EPS) * gain_ref[...]
    dup = jnp.concatenate([key, key], axis=-1)
    aug = jnp.concatenate([out, jnp.ones_like(out)], axis=-1)
    o_ref[...] = jnp.where(is_key, dup, aug).astype(BF16)


def _compress(t4, phi_w1, phi_w2, cmp_pos, gain):
    _, b, nsb, width = t4.shape
    hid = phi_w1.shape[2]
    return pl.pallas_call(
        _cmp_kernel,
        grid=(4, b),
        in_specs=[pl.BlockSpec((None, None, nsb, width), lambda t, bb: (t, bb, 0, 0)),
                  pl.BlockSpec((None, CMP_LEN * HEAD_DIM, hid), lambda t, bb: (t // N_KV_HEADS, 0, 0)),
                  pl.BlockSpec((None, hid, HEAD_DIM), lambda t, bb: (t // N_KV_HEADS, 0, 0)),
                  pl.BlockSpec((None, 1, CMP_LEN * HEAD_DIM), lambda t, bb: (t // N_KV_HEADS, 0, 0)),
                  pl.BlockSpec((1, HEAD_DIM), lambda t, bb: (0, 0))],
        out_specs=pl.BlockSpec((None, None, nsb, LANES), lambda t, bb: (t, bb, 0, 0)),
        out_shape=jax.ShapeDtypeStruct((4, b, nsb, LANES), BF16),
        compiler_params=_params("parallel", "parallel"),
        name="compress_blocks",
    )(t4, phi_w1, phi_w2, cmp_pos, gain)


def _qg_kernel(h_ref, g_ref, sh_ref, sc_ref, w_ref, gain_ref, cos_ref, sin_ref, bd_ref,
               qn_ref, qr_ref, gate_ref):
    u = _normmod(h_ref[0], g_ref[...], sh_ref[...], sc_ref[...]).astype(BF16)
    bd = bd_ref[...]
    width = bd.shape[0]
    cos = jnp.concatenate([cos_ref[...]] * (width // LANES), axis=1)
    sin = jnp.concatenate([sin_ref[...]] * (width // LANES), axis=1)
    hdim = qn_ref.shape[2]

    def project(c):
        return _dot(u, w_ref[:, c * width:(c + 1) * width])

    y = project(0)
    for c in range(hdim // width):
        y_next = project(c + 1)
        sl = slice(c * width, (c + 1) * width)
        z = _head_rms(y, bd) * gain_ref[:, sl]
        qn_ref[0, :, sl] = z.astype(BF16)
        qr_ref[0, :, sl] = _rope(z, cos, sin).astype(BF16)
        y = y_next
    gate_ref[0] = _sigmoid(y)


def _qg_proj(h, norm_g3, mod5, w, gain, cos, sin, bd, l):
    b, s, d = h.shape
    n = w.shape[1]
    hdim = N_HEADS * HEAD_DIM
    tm = min(512, s)
    row = lambda width, dt: (pl.BlockSpec((1, tm, width), lambda bb, i: (bb, i, 0)),
                             jax.ShapeDtypeStruct((b, s, width), dt))
    outs = [row(hdim, BF16), row(hdim, BF16), row(n - hdim, F32)]
    return pl.pallas_call(
        _qg_kernel,
        grid=(b, s // tm),
        in_specs=[pl.BlockSpec((1, tm, d), lambda bb, i: (bb, i, 0)),
                  pl.BlockSpec((None, 1, d), lambda bb, i: (l * 3 + 1, 0, 0)),
                  _mod_block(d, l, 3),
                  _mod_block(d, l, 4),
                  pl.BlockSpec((d, n), lambda bb, i: (0, 0)),
                  pl.BlockSpec((1, hdim), lambda bb, i: (0, 0)),
                  pl.BlockSpec((tm, LANES), lambda bb, i: (i, 0)),
                  pl.BlockSpec((tm, LANES), lambda bb, i: (i, 0)),
                  pl.BlockSpec(bd.shape, lambda bb, i: (0, 0))],
        out_specs=[o[0] for o in outs],
        out_shape=[o[1] for o in outs],
        compiler_params=_params("parallel", "parallel"),
        name="nsa_query_gate_proj",
    )(h, norm_g3, mod5, mod5, w, gain, cos, sin, bd)


def _nsa_body(qn_ref, qr_ref, gate_ref, kc_ref, vc_ref, ks_ref, vs_ref, kw_ref, vw_ref, wimpt_ref, gsel_ref,
              o_ref, qa_scr, m_scr, acc_scr, mw_scr, accw_scr, *, tq, shifted, ncp):
    i = pl.program_id(2)
    start = i * tq
    rows = GROUP * tq
    low = lax.broadcasted_iota(jnp.int32, (tq, LANES), 1) < HEAD_DIM
    low_r = lax.broadcasted_iota(jnp.int32, (rows, LANES), 1) < HEAD_DIM

    def accumulate(s, v, mask, first, acc_ref=acc_scr, m_ref=m_scr):
        if not shifted:
            p = jnp.exp2(s)
            if mask is not None:
                p = jnp.where(mask, p, 0.0)
            pv = _dot(p.astype(BF16), v)
            acc_ref[...] = pv if first else acc_ref[...] + pv
            return
        if mask is not None:
            s = jnp.where(mask, s, NEG)
        if first:
            m_new = jnp.max(s, axis=-1, keepdims=True)
            acc_ref[...] = _dot(jnp.exp2(s - m_new).astype(BF16), v)
        else:
            m_old = m_ref[...]
            m_new = jnp.maximum(m_old, jnp.max(s, axis=-1, keepdims=True))
            acc_ref[...] = jnp.exp2(m_old - m_new) * acc_ref[...] + _dot(jnp.exp2(s - m_new).astype(BF16), v)
        m_ref[...] = m_new

    def stack(q):
        parts = []
        for g in range(GROUP):
            blk = q[:, (g // 2) * LANES:(g // 2 + 1) * LANES]
            parts.append(jnp.where(low if g % 2 == 0 else ~low, blk, jnp.zeros_like(blk)))
        return parts

    qn = jnp.concatenate(stack(qn_ref[0]), axis=0)
    sc = _dot_t(qn, kc_ref[0:ncp, :])
    c_end = lax.broadcasted_iota(jnp.int32, (tq, ncp), 1) * CMP_STRIDE + (CMP_LEN - 1)
    valid_t = c_end <= start + lax.broadcasted_iota(jnp.int32, (tq, ncp), 0)
    if shifted:
        valid = jnp.concatenate([valid_t] * GROUP, axis=0)
        sc = jnp.where(valid, sc, NEG)
        sc = sc - jnp.max(sc, axis=-1, keepdims=True)
        p = jnp.where(valid, jnp.exp2(sc), 0.0)
    else:
        p = jnp.exp2(sc + jnp.concatenate([jnp.where(valid_t, 0.0, NEG)] * GROUP, axis=0))
    oc = _dot(p.astype(BF16), vc_ref[0:ncp, :])
    inv = 1.0 / jnp.maximum(jnp.where(low_r, pltpu.roll(oc, HEAD_DIM, 1), oc), 1e-30)
    o_cmp = oc * inv
    psum_blocks = []
    for cb in range(ncp // LANES):
        cs = slice(cb * LANES, (cb + 1) * LANES)
        tot = p[0:tq, cs] * inv[0:tq]
        for g in range(1, GROUP):
            tot = tot + p[g * tq:(g + 1) * tq, cs] * inv[g * tq:(g + 1) * tq]
        psum_blocks.append(tot)
    psum = jnp.concatenate(psum_blocks, axis=1)

    qr_parts = stack(qr_ref[0])
    for g in range(GROUP):
        qa_scr[g * tq:(g + 1) * tq, 0:LANES] = qr_parts[g]
    qr = jnp.concatenate(qr_parts, axis=0)

    tri = (lax.broadcasted_iota(jnp.int32, (rows, tq), 1)
           <= (lax.broadcasted_iota(jnp.int32, (rows, tq), 0) & (tq - 1)))

    def key_block(kb):
        return pl.ds(pl.multiple_of(kb * tq, tq), tq)

    diag = key_block(i)
    accumulate(_dot_t(qr, kw_ref[diag, :]), vw_ref[diag, :], tri, True, accw_scr, mw_scr)

    p_hi = psum.astype(BF16)
    p_lo = (psum - p_hi.astype(F32)).astype(BF16)
    wimpt = wimpt_ref[:, 0:ncp]
    imp_t = _dot_t(wimpt, p_hi) + _dot_t(wimpt, p_lo)

    blk = key_block(jnp.maximum(i - 1, 0))
    accumulate(_dot_t(qr, kw_ref[blk, :]), vw_ref[blk, :], jnp.broadcast_to(i >= 1, tri.shape), False,
               accw_scr, mw_scr)
    blk = key_block(jnp.maximum(i - 2, 0))
    accumulate(_dot_t(qr, kw_ref[blk, :]), vw_ref[blk, :], (~tri) & (i >= 2), False, accw_scr, mw_scr)

    blk_id = lax.broadcasted_iota(jnp.int32, (LANES, tq), 0)
    jt = (start + lax.broadcasted_iota(jnp.int32, (LANES, tq), 1)) >> SEL_SHIFT
    forced = (blk_id == 0) | (blk_id == jt) | (blk_id == jt - 1)
    work = jnp.where(blk_id > jt, -1.0, jnp.where(forced, -2.0, imp_t))
    sel = forced
    for _ in range(N_SELECT - 3):
        mx = jnp.max(work, axis=0, keepdims=True)
        idx = jnp.min(jnp.where(work == mx, blk_id, LANES), axis=0, keepdims=True)
        pick = blk_id == idx
        sel = sel | pick
        work = jnp.where(pick, -2.0, work)
    sel_bias = jnp.transpose(jnp.where(sel, 0.0, NEG)).astype(BF16)

    for g in range(GROUP):
        qa_scr[g * tq:(g + 1) * tq, LANES:2 * LANES] = sel_bias

    accumulate(_dot_t(qa_scr[...], ks_ref[diag, :]), vs_ref[diag, :], tri, True)

    def key_span(first_block, n_blocks):
        return pl.ds(pl.multiple_of(first_block * tq, tq), n_blocks * tq)

    def slc_quad(kq, carry):
        span = key_span(4 * kq, 4)
        accumulate(_dot_t(qa_scr[...], ks_ref[span, :]), vs_ref[span, :], None, False)
        return carry

    lax.fori_loop(0, i // 4, slc_quad, 0)

    @pl.when(i % 4 >= 2)
    def _():
        span = key_span((i // 4) * 4, 2)
        accumulate(_dot_t(qa_scr[...], ks_ref[span, :]), vs_ref[span, :], None, False)

    @pl.when(i % 2 == 1)
    def _():
        blk = key_block(i - 1)
        accumulate(_dot_t(qa_scr[...], ks_ref[blk, :]), vs_ref[blk, :], None, False)

    gates = gate_ref[0]
    g_hi = gates.astype(BF16)
    g_lo = (gates - g_hi.astype(F32)).astype(BF16)
    gate_wide = _dot(g_hi, gsel_ref[...]) + _dot(g_lo, gsel_ref[...])
    for pr in range(GROUP // 2):
        ev = slice(2 * pr * tq, (2 * pr + 1) * tq)
        od = slice((2 * pr + 1) * tq, (2 * pr + 2) * tq)

        def pair(acc):
            a_e, a_o = acc[ev], acc[od]
            r_e = pltpu.roll(a_e, HEAD_DIM, 1)
            r_o = pltpu.roll(a_o, HEAD_DIM, 1)
            return jnp.where(low, a_e, r_o) / jnp.where(low, r_e, a_o)

        def gate(br):
            c = (br * (GROUP // 2) + pr) * LANES
            return gate_wide[:, c:c + LANES]

        o_c = jnp.where(low, o_cmp[ev], pltpu.roll(o_cmp[od], HEAD_DIM, 1))
        out = gate(0) * o_c + gate(1) * pair(acc_scr) + gate(2) * pair(accw_scr)
        o_ref[0, :, pr * LANES:(pr + 1) * LANES] = out.astype(BF16)


def _nsa_kernel(small_ref, *refs, tq):
    ncp = refs[3].shape[0]
    half = ncp // 2 if (ncp // 2) % LANES == 0 else ncp
    early = pl.program_id(2) < pl.num_programs(2) // 2
    if half < ncp:
        @pl.when((small_ref[0] != 0) & early)
        def _():
            _nsa_body(*refs, tq=tq, shifted=False, ncp=half)

    @pl.when((small_ref[0] != 0) & jnp.logical_not(early & (half < ncp)))
    def _():
        _nsa_body(*refs, tq=tq, shifted=False, ncp=ncp)

    @pl.when(small_ref[0] == 0)
    def _():
        _nsa_body(*refs, tq=tq, shifted=True, ncp=ncp)


def _nsa_attention(small, qn, qr, gates, cmp4, ks, vs, kw, vw, wimpt):
    b, s, hdim = qn.shape
    tq = min(256, s)
    assert s // SEL_BLOCK <= LANES and s % tq == 0 and SLIDE_WINDOW == 2 * tq
    ncp = cmp4.shape[2]
    gw = GROUP * HEAD_DIM
    rows = GROUP * tq
    qspec = pl.BlockSpec((1, tq, gw), lambda bb, kv, i: (bb, i, kv))
    full = lambda width: pl.BlockSpec((None, s, width), lambda bb, kv, i: (bb, 0, kv))
    col = np.arange(3 * (GROUP // 2) * LANES)[None, :]
    src = (col // LANES // (GROUP // 2)) * GROUP + 2 * ((col // LANES) % (GROUP // 2)) + (col % LANES) // HEAD_DIM
    gsel = (np.arange(LANES)[:, None] == src).astype(BF16)
    return pl.pallas_call(
        functools.partial(_nsa_kernel, tq=tq),
        grid=(b, N_KV_HEADS, s // tq),
        in_specs=[pl.BlockSpec(memory_space=pltpu.SMEM), qspec, qspec,
                  pl.BlockSpec((1, tq, LANES), lambda bb, kv, i: (bb, i, kv)),
                  pl.BlockSpec((None, None, ncp, LANES), lambda bb, kv, i: (kv, bb, 0, 0)),
                  pl.BlockSpec((None, None, ncp, LANES), lambda bb, kv, i: (N_KV_HEADS + kv, bb, 0, 0)),
                  full(2 * LANES), full(LANES), full(LANES), full(LANES),
                  pl.BlockSpec((LANES, ncp), lambda bb, kv, i: (0, 0)),
                  pl.BlockSpec(gsel.shape, lambda bb, kv, i: (0, 0))],
        out_specs=qspec,
        out_shape=jax.ShapeDtypeStruct((b, s, hdim), BF16),
        scratch_shapes=[pltpu.VMEM((rows, 2 * LANES), BF16),
                        pltpu.VMEM((rows, 1), F32), pltpu.VMEM((rows, LANES), F32),
                        pltpu.VMEM((rows, 1), F32), pltpu.VMEM((rows, LANES), F32)],
        compiler_params=_params("parallel", "parallel", "arbitrary"),
        name="nsa_attention",
    )(small, qn, qr, gates, cmp4, cmp4, ks, vs, kw, vw, wimpt, gsel)


def _rope_tables(s, tm=None, rate=1):
    half = HEAD_DIM // 2
    row = lax.broadcasted_iota(jnp.int32, (s, LANES), 0)
    lane = lax.broadcasted_iota(jnp.int32, (s, LANES), 1)
    if rate > 1:
        seg = tm // rate
        local = row % tm
        row = row - local + (local % seg) * rate + local // seg
    inv = jnp.tile(ROPE_THETA ** (-jnp.arange(half, dtype=F32) / half), LANES // half)
    ang = row.astype(F32) * inv[None, :]
    cos, sin = jnp.cos(ang), jnp.sin(ang)
    return cos, jnp.where(lane % HEAD_DIM < half, -sin, sin), sin


def _split_half(t, axis):
    hpg = MXU_WIDTH // HEAD_DIM
    half = HEAD_DIM // 2
    shape = t.shape
    t = t.reshape(shape[:axis] + (shape[axis] // MXU_WIDTH, hpg, 2, half) + shape[axis + 1:])
    return t.swapaxes(axis + 1, axis + 2).reshape(shape)


def _importance_weights(ncp):
    r = SEL_BLOCK // CMP_STRIDE
    nper = CMP_LEN // CMP_STRIDE
    o = np.arange(ncp)[:, None] - r * np.arange(LANES)[None, :]
    w = np.zeros((ncp, LANES), np.float32)
    for off in range(-(nper - 1), r):
        cnt = sum(1 for m in range(r) for n in range(nper) if m - n == off)
        w += np.where(o == off, float(cnt), 0.0)
    return w.astype(BF16)


def kernel(x, c, norm_g, w_ada, b_ada, ffn_w_in, ffn_w_out, a_w_qkv, a_q_gain, a_k_gain, a_w_o, kv_norm_g,
           w_ada_kv, b_ada_kv, w_kv, kv_k_gain, cmp_pos, phi_w1, phi_w2, b_w_qg, b_q_gain, b_w_o):
    b, s, d = x.shape
    depth = norm_g.shape[0]
    n_a = depth // 2
    hdim = N_HEADS * HEAD_DIM
    assert depth == 2 and b <= 8

    c8 = jnp.zeros((8, d), F32).at[:b].set(c)
    mod = _ada(c8, w_ada, b_ada.reshape(depth, 1, 9 * d))
    mod5 = mod[:, :b].reshape(depth, b, 9, 1, d)
    kvmod = _ada(c8, w_ada_kv[None], b_ada_kv.reshape(1, 1, 2 * d))
    kvmod5 = kvmod[:, :b].reshape(1, b, 2, 1, d)

    norm_g3 = norm_g.reshape(depth * 3, 1, d)
    w_in = ffn_w_in.astype(BF16)
    w_out = ffn_w_out.astype(BF16)
    cos, sin, _ = _rope_tables(s)
    blk = np.arange(MXU_WIDTH) // HEAD_DIM
    bd_wide = (blk[:, None] == blk[None, :]).astype(BF16)
    blk = (np.arange(MXU_WIDTH) % LANES) // (HEAD_DIM // 2)
    bd_split = (blk[:, None] == blk[None, :]).astype(BF16)
    q_scale = HEAD_DIM ** -0.5 * LOG2E

    h = x
    for l in range(depth):
        h = _ffn(h, norm_g3, mod5, w_in, w_out, l, 0, 0)
        if l < n_a:
            gain = jnp.stack([_split_half(jnp.tile(a_q_gain[l] * q_scale, (1, N_HEADS)), 1),
                              _split_half(jnp.tile(a_k_gain[l], (1, N_HEADS)), 1),
                              jnp.ones((N_DIL, hdim), F32)], axis=1).reshape(3 * N_DIL, 1, hdim)
            cols = np.arange(N_DIL * 3 * hdim).reshape(N_DIL, 3, hdim)
            cols = np.concatenate([_split_half(cols[:, :2], 2), cols[:, 2:]], axis=1).reshape(-1)
            w_qkv = a_w_qkv[l][:, cols].astype(BF16)
            bound = HEAD_DIM * q_scale * jnp.max(jnp.abs(a_q_gain[l])) * jnp.max(jnp.abs(a_k_gain[l]))
            small = (bound <= SCORE_LIMIT).astype(jnp.int32).reshape(1)
            parts = [_dilated_attention(small, _qkv_proj(h, norm_g3, mod5, w_qkv, gain, bd_split, l, g), g)
                     for g in range(N_DIL)]
            h = _merge_out([p[0] for p in parts], [p[1] for p in parts], h, mod5, a_w_o[l].astype(BF16), l)
        else:
            j = l - n_a
            w_q = b_w_qg[j][:, :hdim]
            w_g = b_w_qg[j][:, hdim:].reshape(d, 3, N_KV_HEADS, GROUP).transpose(0, 2, 1, 3)
            w_g = jnp.pad(w_g.reshape(d, N_KV_HEADS, 3 * GROUP), ((0, 0), (0, 0), (0, LANES - 3 * GROUP)))
            w_qg = jnp.concatenate([w_q, w_g.reshape(d, N_KV_HEADS * LANES)], axis=1).astype(BF16)
            gain = jnp.tile(b_q_gain[j] * q_scale, N_HEADS)[None, :]
            qn, qr, gates = _qg_proj(h, norm_g3, mod5, w_qg, gain, cos, sin, bd_wide, l)
            bound = HEAD_DIM * q_scale * jnp.max(jnp.abs(b_q_gain[j])) * jnp.max(jnp.abs(kv_k_gain))
            small = (bound <= SCORE_LIMIT).astype(jnp.int32).reshape(1)
            o = _nsa_attention(small, qn, qr, gates, *shared)
            h = _out_proj(o, h, mod5, b_w_o[j].astype(BF16), l)
        h = _ffn(h, norm_g3, mod5, w_in, w_out, l, 2, 1)
        if l == n_a - 1:
            col = lambda br, kv, kvh: br * 4 * HEAD_DIM + kv * 2 * HEAD_DIM + kvh * HEAD_DIM + jnp.arange(HEAD_DIM)
            dup = lambda br, kv, kvh: jnp.concatenate([col(br, kv, kvh)] * 2)
            cols = jnp.concatenate(
                [dup(br, 0, kvh) for br in (1, 2) for kvh in range(N_KV_HEADS)]
                + [dup(br, 1, kvh) for br in (1, 2) for kvh in range(N_KV_HEADS)]
                + [col(0, kv, kvh) for kv in range(2) for kvh in range(N_KV_HEADS)])
            w_kv_ext = w_kv[:, cols].astype(BF16)
            reps = MXU_WIDTH // HEAD_DIM
            kgain = jnp.concatenate([jnp.tile(kv_k_gain[1], (1, reps)), jnp.tile(kv_k_gain[2], (1, reps)),
                                     jnp.ones((6, MXU_WIDTH), F32)], axis=0)
            ks, kw, vs, vw, kvc = _kv_proj(h, kv_norm_g[None, :], kvmod5, w_kv_ext, kgain, cos, sin, bd_wide)
            nsb = s // CMP_STRIDE
            t4 = kvc.reshape(b, nsb, CMP_STRIDE, 4, HEAD_DIM).transpose(3, 0, 1, 2, 4)
            t4 = t4.reshape(4, b, nsb, CMP_STRIDE * HEAD_DIM)
            cmp4 = _compress(t4, phi_w1, phi_w2, cmp_pos.reshape(2, 1, CMP_LEN * HEAD_DIM), kv_k_gain[0][None, :])
            shared = (cmp4, ks, vs, kw, vw, _importance_weights(nsb).T)
    return h
```

```python
import functools

import jax
import jax.numpy as jnp
import numpy as np
from jax import lax
from jax.experimental import pallas as pl
from jax.experimental.pallas import tpu as pltpu

HEAD_DIM = 64
N_HEADS = 16
DIL_WINDOWS = (128, 512, 2048)
DIL_RATES = (1, 4, 16)
N_DIL = 3
N_KV_HEADS = 2
GROUP = N_HEADS // N_KV_HEADS
CMP_STRIDE = 16
CMP_LEN = 32
SEL_BLOCK = 64
SEL_SHIFT = SEL_BLOCK.bit_length() - 1
N_SELECT = 16
SLIDE_WINDOW = 512
ROPE_THETA = 10000.0
EPS = 1e-6

LANES = 128
MXU_WIDTH = 256
Q_ROWS = 128
NEG = -1e30
LOG2E = 1.4426950408889634
SCORE_LIMIT = 60.0
VMEM_LIMIT = 56 * 1024 * 1024

BF16 = jnp.bfloat16
F32 = jnp.float32


def _params(*sem):
    return pltpu.CompilerParams(dimension_semantics=sem, vmem_limit_bytes=VMEM_LIMIT)


def _dot(a, b):
    return jnp.dot(a, b, preferred_element_type=F32)


def _dot_t(a, b):
    return lax.dot_general(a, b, (((1,), (1,)), ((), ())), preferred_element_type=F32)


def _split_dot(a, b):
    a_hi = a.astype(BF16)
    a_lo = (a - a_hi.astype(F32)).astype(BF16)
    b_hi = b.astype(BF16)
    b_lo = (b - b_hi.astype(F32)).astype(BF16)
    return _dot(a_hi, b_hi) + (_dot(a_hi, b_lo) + _dot(a_lo, b_hi))


def _sigmoid(x):
    return 1.0 / (1.0 + jnp.exp(-x))


def _normmod(x, g, shift, scale):
    return x * lax.rsqrt(jnp.mean(x * x, axis=-1, keepdims=True) + EPS) * (g * (1.0 + scale)) + shift


def _head_rms(y, bd):
    ss = _dot((y * y).astype(BF16), bd)
    return y * lax.rsqrt(ss * (1.0 / HEAD_DIM) + EPS)


def _rope(z, cos, sin_signed):
    lane = lax.broadcasted_iota(jnp.int32, z.shape, 1)
    first_half = (lane & (HEAD_DIM - 1)) < HEAD_DIM // 2
    swapped = jnp.where(first_half, pltpu.roll(z, z.shape[1] - HEAD_DIM // 2, 1),
                        pltpu.roll(z, HEAD_DIM // 2, 1))
    return z * cos + swapped * sin_signed


def _ada_kernel(c_ref, w_ref, b_ref, o_ref):
    c = c_ref[...]
    a = (c * _sigmoid(c)).astype(BF16)
    o_ref[...] = _dot(a, w_ref[...].astype(BF16)) + b_ref[...]


def _ada(c8, w, b):
    nl, d, n = w.shape
    tn = 1024
    return pl.pallas_call(
        _ada_kernel,
        grid=(nl, n // tn),
        in_specs=[pl.BlockSpec((8, d), lambda l, j: (0, 0)),
                  pl.BlockSpec((None, d, tn), lambda l, j: (l, 0, j)),
                  pl.BlockSpec((None, 1, tn), lambda l, j: (l, 0, j))],
        out_specs=pl.BlockSpec((None, 8, tn), lambda l, j: (l, 0, j)),
        out_shape=jax.ShapeDtypeStruct((nl, 8, n), F32),
        compiler_params=_params("parallel", "parallel"),
        name="ada_table",
    )(c8, w, b)


def _mod_block(d, l, k):
    return pl.BlockSpec((None, None, None, 1, d), lambda b, *_: (l, b, k, 0, 0))


def _ffn_kernel(h_ref, g_ref, sh_ref, sc_ref, gt_ref, wg_ref, wu_ref, wo_ref, o_ref, u_scr, acc_scr):
    u_scr[...] = _normmod(h_ref[0], g_ref[...], sh_ref[...], sc_ref[...]).astype(BF16)

    u = u_scr[...]
    tf = wg_ref.shape[1]
    bounds = [(lo, min(lo + MXU_WIDTH, tf)) for lo in range(0, tf, MXU_WIDTH)]

    def gate_up(c):
        lo, hi = bounds[c]
        return _dot(u, wg_ref[:, lo:hi]), _dot(u, wu_ref[:, lo:hi])

    nxt = gate_up(0)
    for c, (lo, hi) in enumerate(bounds):
        gate, up = nxt
        if c + 1 < len(bounds):
            nxt = gate_up(c + 1)
        act = (gate * _sigmoid(gate) * up).astype(BF16)
        down = _dot(act, wo_ref[lo:hi, :])
        acc_scr[...] = down if c == 0 else acc_scr[...] + down

    o_ref[0] = h_ref[0] + (0.5 * (1.0 + gt_ref[...])) * acc_scr[...]


def _ffn(h, norm_g3, mod5, w_in, w_out, l, sub, which):
    b, s, d = h.shape
    f = w_out.shape[2]
    tm = min(1024, s)
    resident = pl.Buffered(1)
    return pl.pallas_call(
        _ffn_kernel,
        grid=(b, s // tm),
        in_specs=[pl.BlockSpec((1, tm, d), lambda bb, i: (bb, i, 0)),
                  pl.BlockSpec((None, 1, d), lambda bb, i: (l * 3 + sub, 0, 0)),
                  _mod_block(d, l, sub * 3 + 0),
                  _mod_block(d, l, sub * 3 + 1),
                  _mod_block(d, l, sub * 3 + 2),
                  pl.BlockSpec((None, None, d, f), lambda bb, i: (l, which, 0, 0), pipeline_mode=resident),
                  pl.BlockSpec((None, None, d, f), lambda bb, i: (l, which, 0, 1), pipeline_mode=resident),
                  pl.BlockSpec((None, None, f, d), lambda bb, i: (l, which, 0, 0), pipeline_mode=resident)],
        out_specs=pl.BlockSpec((1, tm, d), lambda bb, i: (bb, i, 0)),
        out_shape=jax.ShapeDtypeStruct((b, s, d), F32),
        scratch_shapes=[pltpu.VMEM((tm, d), BF16), pltpu.VMEM((tm, d), F32)],
        compiler_params=_params("parallel", "parallel"),
        name="swiglu_half_step",
    )(h, norm_g3, mod5, mod5, mod5, w_in, w_in, w_out)


def _qkv_kernel(h_ref, g_ref, sh_ref, sc_ref, w_ref, gain_ref, cos_ref, sin_ref, bd_ref, o_ref, u_scr, uf_scr,
                *, rate):
    tm = u_scr.shape[0]
    seg = tm // rate
    u = _normmod(h_ref[0], g_ref[...], sh_ref[...], sc_ref[...])
    if rate == 1:
        u_scr[...] = u.astype(BF16)
    else:
        for cb in range(u.shape[1] // LANES):
            sl = slice(cb * LANES, (cb + 1) * LANES)
            uf_scr[cb] = u[:, sl]
            for res in range(rate):
                u_scr[res * seg:(res + 1) * seg, sl] = uf_scr[cb, pl.ds(res, seg, stride=rate), :].astype(BF16)

    n_grp = o_ref.shape[1]
    per_kind = n_grp // 3
    cos = cos_ref[...]
    sin = sin_ref[...]
    bd = bd_ref[...]

    def project(grp):
        return _dot(u_scr[...], w_ref[:, grp * MXU_WIDTH:(grp + 1) * MXU_WIDTH])

    y = project(0)
    for grp in range(n_grp):
        y_next = project(grp + 1) if grp + 1 < n_grp else None
        kind, sub = divmod(grp, per_kind)
        if kind < 2:
            z = _head_rms(y, bd) * gain_ref[kind, :, sub * MXU_WIDTH:(sub + 1) * MXU_WIDTH]
            x1, x2 = z[:, :LANES], z[:, LANES:]
            y = jnp.concatenate([x1 * cos - x2 * sin, x1 * sin + x2 * cos], axis=1)
        y = y.astype(BF16)
        for res in range(rate):
            o_ref[res, grp] = y[res * seg:(res + 1) * seg]
        y = y_next


def _qkv_proj(h, norm_g3, mod5, w, gain, bd, l, g):
    b, s, d = h.shape
    rate = DIL_RATES[g]
    tm = min(1024, s)
    seg = tm // rate
    tn = N_HEADS * HEAD_DIM
    n_grp = tn // MXU_WIDTH
    cos, _, sin = _rope_tables(s, tm, rate)

    return pl.pallas_call(
        functools.partial(_qkv_kernel, rate=rate),
        grid=(b, s // tm),
        in_specs=[pl.BlockSpec((1, tm, d), lambda bb, i: (bb, i, 0)),
                  pl.BlockSpec((None, 1, d), lambda bb, i: (l * 3 + 1, 0, 0)),
                  _mod_block(d, l, 3),
                  _mod_block(d, l, 4),
                  pl.BlockSpec((d, 3 * tn), lambda bb, i: (0, g), pipeline_mode=pl.Buffered(1)),
                  pl.BlockSpec((3, 1, tn), lambda bb, i: (g, 0, 0)),
                  pl.BlockSpec((tm, LANES), lambda bb, i: (i, 0)),
                  pl.BlockSpec((tm, LANES), lambda bb, i: (i, 0)),
                  pl.BlockSpec(bd.shape, lambda bb, i: (0, 0))],
        out_specs=pl.BlockSpec((None, rate, 3 * n_grp, seg, MXU_WIDTH), lambda bb, i: (bb, 0, 0, i, 0)),
        out_shape=jax.ShapeDtypeStruct((b, rate, 3 * n_grp, s // rate, MXU_WIDTH), BF16),
        scratch_shapes=[pltpu.VMEM((tm, d), BF16), pltpu.VMEM((d // LANES, tm, LANES), F32)],
        compiler_params=_params("parallel", "parallel"),
        name=f"qkv_proj_rate{rate}",
    )(h, norm_g3, mod5, mod5, w, gain, cos, sin, bd)


def _dil_body(bias_ref, q_ref, kp_ref, kc_ref, vp_ref, vc_ref, o_ref, lse_ref, *, tq, shifted):
    i = pl.program_id(2)
    n_grp = q_ref.shape[0]
    hpg = MXU_WIDTH // HEAD_DIM
    qlane = lax.broadcasted_iota(jnp.int32, (Q_ROWS, MXU_WIDTH), 1)
    q_head = (qlane & (LANES - 1)) // (HEAD_DIM // 2)
    v_head = qlane // HEAD_DIM
    seg = lax.broadcasted_iota(jnp.int32, (Q_ROWS, LANES), 1) // (LANES // N_HEADS)

    def scores(grp, j):
        q = q_ref[grp, j * Q_ROWS:(j + 1) * Q_ROWS, :]
        zero = jnp.zeros_like(q)
        q2 = jnp.concatenate([jnp.where(q_head == t, q, zero) for t in range(hpg)], axis=0)
        if j == 0:
            k = jnp.concatenate([kp_ref[grp], kc_ref[grp, 0:Q_ROWS, :]], axis=0)
        else:
            k = kc_ref[grp, (j - 1) * Q_ROWS:(j + 1) * Q_ROWS, :]
        return _dot_t(q2, k)

    for j in range(tq // Q_ROWS):
        rs = slice(j * Q_ROWS, (j + 1) * Q_ROWS)
        lse_tile = None
        for grp in range(n_grp):
            s = scores(grp, j) + bias_ref[...]
            if j == 0:
                v = jnp.concatenate([vp_ref[grp], vc_ref[grp, 0:Q_ROWS, :]], axis=0)
                prev_key = lax.broadcasted_iota(jnp.int32, s.shape, 1) < Q_ROWS
                s = jnp.where(prev_key & (i == 0), NEG, s)
            else:
                v = vc_ref[grp, (j - 1) * Q_ROWS:(j + 1) * Q_ROWS, :]
            if shifted:
                m = jnp.max(s, axis=-1, keepdims=True)
                p = jnp.exp2(s - m)
            else:
                p = jnp.exp2(s)
            den = jnp.sum(p, axis=-1, keepdims=True)
            o2 = _dot(p.astype(BF16), v) / den
            lse2 = jnp.log(den) * LOG2E
            if shifted:
                lse2 = lse2 + m
            out = o2[0:Q_ROWS]
            for t in range(hpg):
                ts = slice(t * Q_ROWS, (t + 1) * Q_ROWS)
                if t:
                    out = jnp.where(v_head == t, o2[ts], out)
                head_lse = jnp.broadcast_to(lse2[ts], (Q_ROWS, LANES))
                lse_tile = head_lse if lse_tile is None else jnp.where(seg == hpg * grp + t, head_lse, lse_tile)
            o_ref[0, rs, grp * MXU_WIDTH:(grp + 1) * MXU_WIDTH] = out.astype(BF16)
        lse_ref[0, rs, :] = lse_tile


def _dil_kernel(small_ref, *refs, tq):
    @pl.when(small_ref[0] != 0)
    def _():
        _dil_body(*refs, tq=tq, shifted=False)

    @pl.when(small_ref[0] == 0)
    def _():
        _dil_body(*refs, tq=tq, shifted=True)


def _dilated_attention(small, qkv_r, g):
    b, rate, n3, l, width = qkv_r.shape
    s = l * rate
    n_back = DIL_WINDOWS[g] // rate
    assert n_back <= Q_ROWS and rate == DIL_RATES[g] and width == MXU_WIDTH
    tq = min(512, l)
    hdim = N_HEADS * HEAD_DIM
    n_grp = n3 // 3
    sub = tq // Q_ROWS

    def cur(kind):
        return pl.BlockSpec((None, None, n_grp, tq, width), lambda bb, r, i: (bb, r, kind, i, 0))

    def prev(kind):
        return pl.BlockSpec((None, None, n_grp, Q_ROWS, width),
                            lambda bb, r, i: (bb, r, kind, jnp.maximum(i * sub - 1, 0), 0))

    rows = (width // HEAD_DIM) * Q_ROWS
    dist = Q_ROWS + (np.arange(rows)[:, None] % Q_ROWS) - np.arange(2 * Q_ROWS)[None, :]
    bias = np.where((dist >= 0) & (dist <= n_back), 0.0, NEG).astype(np.float32)

    o, lse = pl.pallas_call(
        functools.partial(_dil_kernel, tq=tq),
        grid=(b, rate, l // tq),
        in_specs=[pl.BlockSpec(memory_space=pltpu.SMEM),
                  pl.BlockSpec(bias.shape, lambda bb, r, i: (0, 0)),
                  cur(0), prev(1), cur(1), prev(2), cur(2)],
        out_specs=[pl.BlockSpec((1, None, tq, hdim), lambda bb, r, i: (bb, r, i, 0)),
                   pl.BlockSpec((1, None, tq, LANES), lambda bb, r, i: (bb, r, i, 0))],
        out_shape=[jax.ShapeDtypeStruct((b, rate, l, hdim), BF16),
                   jax.ShapeDtypeStruct((b, rate, l, LANES), F32)],
        compiler_params=_params("parallel", "parallel", "parallel"),
        name=f"dilated_attention_rate{rate}",
    )(small, bias, qkv_r, qkv_r, qkv_r, qkv_r, qkv_r)
    return o, lse


def _merge_out_kernel(o0_ref, o1_ref, o2_ref, l0_ref, l1_ref, l2_ref, p1_ref, p2_ref, e_ref, h_ref, gt_ref, w_ref,
                      out_ref):
    def positions(x_ref, p_ref):
        x = jnp.concatenate([x_ref[res] for res in range(x_ref.shape[0])], axis=0)
        if p_ref is None:
            return x.astype(F32)
        if x.dtype == BF16:
            return _dot(p_ref[...], x)
        hi = x.astype(BF16)
        return _dot(p_ref[...], hi) + _dot(p_ref[...], (x - hi.astype(F32)).astype(BF16))

    l0, l1, l2 = positions(l0_ref, None), positions(l1_ref, p1_ref), positions(l2_ref, p2_ref)
    m = jnp.maximum(jnp.maximum(l0, l1), l2)
    e0, e1, e2 = jnp.exp2(l0 - m), jnp.exp2(l1 - m), jnp.exp2(l2 - m)
    inv = 1.0 / (e0 + e1 + e2)
    spread = e_ref[...]
    o = (_dot((e0 * inv).astype(BF16), spread) * positions(o0_ref, None)
         + _dot((e1 * inv).astype(BF16), spread) * positions(o1_ref, p1_ref)
         + _dot((e2 * inv).astype(BF16), spread) * positions(o2_ref, p2_ref))
    out_ref[0] = h_ref[0] + (1.0 + gt_ref[...]) * _dot(o.astype(BF16), w_ref[...])


def _merge_out(os_, lses, h, mod5, w_o, l):
    b, s, d = h.shape
    hdim = w_o.shape[0]
    tm = min(512, s)
    row = lambda width: pl.BlockSpec((1, tm, width), lambda bb, i: (bb, i, 0))
    rep = LANES // N_HEADS
    lane = np.arange(LANES)[:, None]
    col = np.arange(hdim)[None, :]
    spread = ((lane % rep == 0) & (lane // rep == col // HEAD_DIM)).astype(BF16)

    def unpermute(rate):
        pos = np.arange(tm)
        return (((pos % rate) * (tm // rate) + pos // rate)[:, None] == np.arange(tm)[None, :]).astype(BF16)

    def residue_spec(t):
        rate, width = t.shape[1], t.shape[3]
        return pl.BlockSpec((None, rate, tm // rate, width), lambda bb, i: (bb, 0, i, 0))

    const = lambda shape: pl.BlockSpec(shape, lambda bb, i: (0, 0))
    return pl.pallas_call(
        _merge_out_kernel,
        grid=(b, s // tm),
        in_specs=[residue_spec(t) for t in (*os_, *lses)]
        + [const((tm, tm)), const((tm, tm)), const((LANES, hdim)), row(d), _mod_block(d, l, 5), const((hdim, d))],
        out_specs=row(d),
        out_shape=jax.ShapeDtypeStruct((b, s, d), F32),
        compiler_params=_params("parallel", "parallel"),
        name="merge_out_proj",
    )(*os_, *lses, unpermute(os_[1].shape[1]), unpermute(os_[2].shape[1]), spread, h, mod5, w_o)


def _out_kernel(o_ref, h_ref, gt_ref, w_ref, out_ref):
    out_ref[0] = h_ref[0] + (1.0 + gt_ref[...]) * _dot(o_ref[0], w_ref[...])


def _out_proj(o, h, mod5, w_o, l):
    b, s, d = h.shape
    hdim = w_o.shape[0]
    tm = min(1024, s)
    row = lambda width: pl.BlockSpec((1, tm, width), lambda bb, i: (bb, i, 0))
    return pl.pallas_call(
        _out_kernel,
        grid=(b, s // tm),
        in_specs=[row(hdim), row(d), _mod_block(d, l, 5), pl.BlockSpec((hdim, d), lambda bb, i: (0, 0))],
        out_specs=row(d),
        out_shape=jax.ShapeDtypeStruct((b, s, d), F32),
        compiler_params=_params("parallel", "parallel"),
        name="out_proj",
    )(o, h, mod5, w_o)


def _kv_kernel(h_ref, g_ref, sh_ref, sc_ref, w_ref, gain_ref, cos_ref, sin_ref, bd_ref,
               ks_ref, kw_ref, vs_ref, vw_ref, kvc_ref, *, tm):
    i = pl.program_id(1)
    u = _normmod(h_ref[0], g_ref[...], sh_ref[...], sc_ref[...]).astype(BF16)
    bd = bd_ref[...]
    width = bd.shape[0]
    cos = jnp.concatenate([cos_ref[...]] * (width // LANES), axis=1)
    sin = jnp.concatenate([sin_ref[...]] * (width // LANES), axis=1)
    lane = lax.broadcasted_iota(jnp.int32, (tm, LANES), 1)
    pos = i * tm + lax.broadcasted_iota(jnp.int32, (tm, LANES), 0)
    onehot = jnp.where((pos >> SEL_SHIFT) == lane, 1.0, 0.0).astype(BF16)
    wide_lane = lax.broadcasted_iota(jnp.int32, (tm, width), 1)

    def project(c):
        return _dot(u, w_ref[:, c * width:(c + 1) * width])

    y = project(0)
    for c in range(5):
        y_next = project(c + 1) if c < 4 else None
        if c < 2:
            z = _rope(_head_rms(y, bd) * gain_ref[c:c + 1, :], cos, sin).astype(BF16)
            if c == 0:
                for kvh in range(N_KV_HEADS):
                    ks_ref[0, :, 2 * kvh * LANES:(2 * kvh + 1) * LANES] = z[:, kvh * LANES:(kvh + 1) * LANES]
                    ks_ref[0, :, (2 * kvh + 1) * LANES:(2 * kvh + 2) * LANES] = onehot
            else:
                kw_ref[0] = z
        elif c < 4:
            va = jnp.where((wide_lane & (LANES - 1)) < HEAD_DIM, y, 1.0).astype(BF16)
            if c == 2:
                vs_ref[0] = va
            else:
                vw_ref[0] = va
        else:
            kvc_ref[0] = y
        y = y_next


def _kv_proj(h, kv_norm_g, kvmod5, w, gain, cos, sin, bd):
    b, s, d = h.shape
    n = w.shape[1]
    tm = min(512, s)
    row = lambda width, dt: (pl.BlockSpec((1, tm, width), lambda bb, i: (bb, i, 0)),
                             jax.ShapeDtypeStruct((b, s, width), dt))
    outs = [row(4 * LANES, BF16), row(2 * LANES, BF16), row(2 * LANES, BF16), row(2 * LANES, BF16),
            row(2 * LANES, F32)]
    return pl.pallas_call(
        functools.partial(_kv_kernel, tm=tm),
        grid=(b, s // tm),
        in_specs=[pl.BlockSpec((1, tm, d), lambda bb, i: (bb, i, 0)),
                  pl.BlockSpec((1, d), lambda bb, i: (0, 0)),
                  _mod_block(d, 0, 0),
                  _mod_block(d, 0, 1),
                  pl.BlockSpec((d, n), lambda bb, i: (0, 0)),
                  pl.BlockSpec(gain.shape, lambda bb, i: (0, 0)),
                  pl.BlockSpec((tm, LANES), lambda bb, i: (i, 0)),
                  pl.BlockSpec((tm, LANES), lambda bb, i: (i, 0)),
                  pl.BlockSpec(bd.shape, lambda bb, i: (0, 0))],
        out_specs=[o[0] for o in outs],
        out_shape=[o[1] for o in outs],
        compiler_params=_params("parallel", "parallel"),
        name="shared_kv_proj",
    )(h, kv_norm_g, kvmod5, kvmod5, w, gain, cos, sin, bd)


def _cmp_kernel(t_ref, w1_ref, w2_ref, pos_ref, gain_ref, o_ref):
    is_key = pl.program_id(0) < N_KV_HEADS
    t = t_ref[...]
    w1 = w1_ref[...]
    half = w1.shape[0] // 2
    first = _split_dot(t, w1[:half])
    second = _split_dot(t, w1[half:])
    second = pltpu.roll(second, second.shape[0] - 1, 0)
    posb = _split_dot(jnp.broadcast_to(pos_ref[...], (8, w1.shape[0])), w1)[0:1]
    hid = first + second + posb
    out = _split_dot(hid * _sigmoid(hid), w2_ref[...])
    key = out * lax.rsqrt(jnp.mean(out * out, axis=-1, keepdims=True) + EPS) * gain_ref[...]
    dup = jnp.concatenate([key, key], axis=-1)
    aug = jnp.concatenate([out, jnp.ones_like(out)], axis=-1)
    o_ref[...] = jnp.where(is_key, dup, aug).astype(BF16)


def _compress(t4, phi_w1, phi_w2, cmp_pos, gain):
    _, b, nsb, width = t4.shape
    hid = phi_w1.shape[2]
    return pl.pallas_call(
        _cmp_kernel,
        grid=(4, b),
        in_specs=[pl.BlockSpec((None, None, nsb, width), lambda t, bb: (t, bb, 0, 0)),
                  pl.BlockSpec((None, CMP_LEN * HEAD_DIM, hid), lambda t, bb: (t // N_KV_HEADS, 0, 0)),
                  pl.BlockSpec((None, hid, HEAD_DIM), lambda t, bb: (t // N_KV_HEADS, 0, 0)),
                  pl.BlockSpec((None, 1, CMP_LEN * HEAD_DIM), lambda t, bb: (t // N_KV_HEADS, 0, 0)),
                  pl.BlockSpec((1, HEAD_DIM), lambda t, bb: (0, 0))],
        out_specs=pl.BlockSpec((None, None, nsb, LANES), lambda t, bb: (t, bb, 0, 0)),
        out_shape=jax.ShapeDtypeStruct((4, b, nsb, LANES), BF16),
        compiler_params=_params("parallel", "parallel"),
        name="compress_blocks",
    )(t4, phi_w1, phi_w2, cmp_pos, gain)


def _qg_kernel(h_ref, g_ref, sh_ref, sc_ref, w_ref, gain_ref, cos_ref, sin_ref, bd_ref,
               qn_ref, qr_ref, gate_ref):
    u = _normmod(h_ref[0], g_ref[...], sh_ref[...], sc_ref[...]).astype(BF16)
    bd = bd_ref[...]
    width = bd.shape[0]
    cos = jnp.concatenate([cos_ref[...]] * (width // LANES), axis=1)
    sin = jnp.concatenate([sin_ref[...]] * (width // LANES), axis=1)
    hdim = qn_ref.shape[2]

    def project(c):
        return _dot(u, w_ref[:, c * width:(c + 1) * width])

    y = project(0)
    for c in range(hdim // width):
        y_next = project(c + 1)
        sl = slice(c * width, (c + 1) * width)
        z = _head_rms(y, bd) * gain_ref[:, sl]
        qn_ref[0, :, sl] = z.astype(BF16)
        qr_ref[0, :, sl] = _rope(z, cos, sin).astype(BF16)
        y = y_next
    gate_ref[0] = _sigmoid(y)


def _qg_proj(h, norm_g3, mod5, w, gain, cos, sin, bd, l):
    b, s, d = h.shape
    n = w.shape[1]
    hdim = N_HEADS * HEAD_DIM
    tm = min(512, s)
    row = lambda width, dt: (pl.BlockSpec((1, tm, width), lambda bb, i: (bb, i, 0)),
                             jax.ShapeDtypeStruct((b, s, width), dt))
    outs = [row(hdim, BF16), row(hdim, BF16), row(n - hdim, F32)]
    return pl.pallas_call(
        _qg_kernel,
        grid=(b, s // tm),
        in_specs=[pl.BlockSpec((1, tm, d), lambda bb, i: (bb, i, 0)),
                  pl.BlockSpec((None, 1, d), lambda bb, i: (l * 3 + 1, 0, 0)),
                  _mod_block(d, l, 3),
                  _mod_block(d, l, 4),
                  pl.BlockSpec((d, n), lambda bb, i: (0, 0)),
                  pl.BlockSpec((1, hdim), lambda bb, i: (0, 0)),
                  pl.BlockSpec((tm, LANES), lambda bb, i: (i, 0)),
                  pl.BlockSpec((tm, LANES), lambda bb, i: (i, 0)),
                  pl.BlockSpec(bd.shape, lambda bb, i: (0, 0))],
        out_specs=[o[0] for o in outs],
        out_shape=[o[1] for o in outs],
        compiler_params=_params("parallel", "parallel"),
        name="nsa_query_gate_proj",
    )(h, norm_g3, mod5, mod5, w, gain, cos, sin, bd)


def _nsa_body(qn_ref, qr_ref, gate_ref, kc_ref, vc_ref, ks_ref, vs_ref, kw_ref, vw_ref, wimpt_ref, gsel_ref,
              o_ref, qa_scr, m_scr, acc_scr, mw_scr, accw_scr, *, tq, shifted, ncp):
    i = pl.program_id(2)
    start = i * tq
    rows = GROUP * tq
    low = lax.broadcasted_iota(jnp.int32, (tq, LANES), 1) < HEAD_DIM
    low_r = lax.broadcasted_iota(jnp.int32, (rows, LANES), 1) < HEAD_DIM

    def accumulate(s, v, mask, first, acc_ref=acc_scr, m_ref=m_scr):
        if not shifted:
            p = jnp.exp2(s)
            if mask is not None:
                p = jnp.where(mask, p, 0.0)
            pv = _dot(p.astype(BF16), v)
            acc_ref[...] = pv if first else acc_ref[...] + pv
            return
        if mask is not None:
            s = jnp.where(mask, s, NEG)
        if first:
            m_new = jnp.max(s, axis=-1, keepdims=True)
            acc_ref[...] = _dot(jnp.exp2(s - m_new).astype(BF16), v)
        else:
            m_old = m_ref[...]
            m_new = jnp.maximum(m_old, jnp.max(s, axis=-1, keepdims=True))
            acc_ref[...] = jnp.exp2(m_old - m_new) * acc_ref[...] + _dot(jnp.exp2(s - m_new).astype(BF16), v)
        m_ref[...] = m_new

    def stack(q):
        parts = []
        for g in range(GROUP):
            blk = q[:, (g // 2) * LANES:(g // 2 + 1) * LANES]
            parts.append(jnp.where(low if g % 2 == 0 else ~low, blk, jnp.zeros_like(blk)))
        return parts

    qn = jnp.concatenate(stack(qn_ref[0]), axis=0)
    sc = _dot_t(qn, kc_ref[0:ncp, :])
    c_end = lax.broadcasted_iota(jnp.int32, (tq, ncp), 1) * CMP_STRIDE + (CMP_LEN - 1)
    valid_t = c_end <= start + lax.broadcasted_iota(jnp.int32, (tq, ncp), 0)
    if shifted:
        valid = jnp.concatenate([valid_t] * GROUP, axis=0)
        sc = jnp.where(valid, sc, NEG)
        sc = sc - jnp.max(sc, axis=-1, keepdims=True)
        p = jnp.where(valid, jnp.exp2(sc), 0.0)
    else:
        p = jnp.exp2(sc + jnp.concatenate([jnp.where(valid_t, 0.0, NEG)] * GROUP, axis=0))
    oc = _dot(p.astype(BF16), vc_ref[0:ncp, :])
    inv = 1.0 / jnp.maximum(jnp.where(low_r, pltpu.roll(oc, HEAD_DIM, 1), oc), 1e-30)
    o_cmp = oc * inv
    psum_blocks = []
    for cb in range(ncp // LANES):
        cs = slice(cb * LANES, (cb + 1) * LANES)
        tot = p[0:tq, cs] * inv[0:tq]
        for g in range(1, GROUP):
            tot = tot + p[g * tq:(g + 1) * tq, cs] * inv[g * tq:(g + 1) * tq]
        psum_blocks.append(tot)
    psum = jnp.concatenate(psum_blocks, axis=1)

    qr_parts = stack(qr_ref[0])
    for g in range(GROUP):
        qa_scr[g * tq:(g + 1) * tq, 0:LANES] = qr_parts[g]
    qr = jnp.concatenate(qr_parts, axis=0)

    tri = (lax.broadcasted_iota(jnp.int32, (rows, tq), 1)
           <= (lax.broadcasted_iota(jnp.int32, (rows, tq), 0) & (tq - 1)))

    def key_block(kb):
        return pl.ds(pl.multiple_of(kb * tq, tq), tq)

    diag = key_block(i)
    accumulate(_dot_t(qr, kw_ref[diag, :]), vw_ref[diag, :], tri, True, accw_scr, mw_scr)

    p_hi = psum.astype(BF16)
    p_lo = (psum - p_hi.astype(F32)).astype(BF16)
    wimpt = wimpt_ref[:, 0:ncp]
    imp_t = _dot_t(wimpt, p_hi) + _dot_t(wimpt, p_lo)

    blk = key_block(jnp.maximum(i - 1, 0))
    accumulate(_dot_t(qr, kw_ref[blk, :]), vw_ref[blk, :], jnp.broadcast_to(i >= 1, tri.shape), False,
               accw_scr, mw_scr)
    blk = key_block(jnp.maximum(i - 2, 0))
    accumulate(_dot_t(qr, kw_ref[blk, :]), vw_ref[blk, :], (~tri) & (i >= 2), False, accw_scr, mw_scr)

    blk_id = lax.broadcasted_iota(jnp.int32, (LANES, tq), 0)
    jt = (start + lax.broadcasted_iota(jnp.int32, (LANES, tq), 1)) >> SEL_SHIFT
    forced = (blk_id == 0) | (blk_id == jt) | (blk_id == jt - 1)
    work = jnp.where(blk_id > jt, -1.0, jnp.where(forced, -2.0, imp_t))
    sel = forced
    for _ in range(N_SELECT - 3):
        mx = jnp.max(work, axis=0, keepdims=True)
        idx = jnp.min(jnp.where(work == mx, blk_id, LANES), axis=0, keepdims=True)
        pick = blk_id == idx
        sel = sel | pick
        work = jnp.where(pick, -2.0, work)
    sel_bias = jnp.transpose(jnp.where(sel, 0.0, NEG)).astype(BF16)

    for g in range(GROUP):
        qa_scr[g * tq:(g + 1) * tq, LANES:2 * LANES] = sel_bias

    accumulate(_dot_t(qa_scr[...], ks_ref[diag, :]), vs_ref[diag, :], tri, True)

    def key_span(first_block, n_blocks):
        return pl.ds(pl.multiple_of(first_block * tq, tq), n_blocks * tq)

    def slc_quad(kq, carry):
        span = key_span(4 * kq, 4)
        accumulate(_dot_t(qa_scr[...], ks_ref[span, :]), vs_ref[span, :], None, False)
        return carry

    lax.fori_loop(0, i // 4, slc_quad, 0)

    @pl.when(i % 4 >= 2)
    def _():
        span = key_span((i // 4) * 4, 2)
        accumulate(_dot_t(qa_scr[...], ks_ref[span, :]), vs_ref[span, :], None, False)

    @pl.when(i % 2 == 1)
    def _():
        blk = key_block(i - 1)
        accumulate(_dot_t(qa_scr[...], ks_ref[blk, :]), vs_ref[blk, :], None, False)

    gates = gate_ref[0]
    g_hi = gates.astype(BF16)
    g_lo = (gates - g_hi.astype(F32)).astype(BF16)
    gate_wide = _dot(g_hi, gsel_ref[...]) + _dot(g_lo, gsel_ref[...])
    for pr in range(GROUP // 2):
        ev = slice(2 * pr * tq, (2 * pr + 1) * tq)
        od = slice((2 * pr + 1) * tq, (2 * pr + 2) * tq)

        def pair(acc):
            a_e, a_o = acc[ev], acc[od]
            r_e = pltpu.roll(a_e, HEAD_DIM, 1)
            r_o = pltpu.roll(a_o, HEAD_DIM, 1)
            return jnp.where(low, a_e, r_o) / jnp.where(low, r_e, a_o)

        def gate(br):
            c = (br * (GROUP // 2) + pr) * LANES
            return gate_wide[:, c:c + LANES]

        o_c = jnp.where(low, o_cmp[ev], pltpu.roll(o_cmp[od], HEAD_DIM, 1))
        out = gate(0) * o_c + gate(1) * pair(acc_scr) + gate(2) * pair(accw_scr)
        o_ref[0, :, pr * LANES:(pr + 1) * LANES] = out.astype(BF16)


def _nsa_kernel(*refs, tq, shifted):
    ncp = refs[3].shape[0]
    half = ncp // 2
    if shifted or half % LANES:
        _nsa_body(*refs, tq=tq, shifted=shifted, ncp=ncp)
        return
    early = pl.program_id(2) < pl.num_programs(2) // 2

    @pl.when(early)
    def _():
        _nsa_body(*refs, tq=tq, shifted=False, ncp=half)

    @pl.when(jnp.logical_not(early))
    def _():
        _nsa_body(*refs, tq=tq, shifted=False, ncp=ncp)


def _nsa_attention(small, qn, qr, gates, cmp4, ks, vs, kw, vw, wimpt):
    b, s, hdim = qn.shape
    tq = min(256, s)
    assert s // SEL_BLOCK <= LANES and s % tq == 0 and SLIDE_WINDOW == 2 * tq
    ncp = cmp4.shape[2]
    gw = GROUP * HEAD_DIM
    rows = GROUP * tq
    qspec = pl.BlockSpec((1, tq, gw), lambda bb, kv, i: (bb, i, kv))
    full = lambda width: pl.BlockSpec((None, s, width), lambda bb, kv, i: (bb, 0, kv))
    col = np.arange(3 * (GROUP // 2) * LANES)[None, :]
    src = (col // LANES // (GROUP // 2)) * GROUP + 2 * ((col // LANES) % (GROUP // 2)) + (col % LANES) // HEAD_DIM
    gsel = (np.arange(LANES)[:, None] == src).astype(BF16)

    def run(shifted):
        return lambda: pl.pallas_call(
            functools.partial(_nsa_kernel, tq=tq, shifted=shifted),
            **common)(qn, qr, gates, cmp4, cmp4, ks, vs, kw, vw, wimpt, gsel)

    common = dict(
        grid=(b, N_KV_HEADS, s // tq),
        in_specs=[qspec, qspec,
                  pl.BlockSpec((1, tq, LANES), lambda bb, kv, i: (bb, i, kv)),
                  pl.BlockSpec((None, None, ncp, LANES), lambda bb, kv, i: (kv, bb, 0, 0)),
                  pl.BlockSpec((None, None, ncp, LANES), lambda bb, kv, i: (N_KV_HEADS + kv, bb, 0, 0)),
                  full(2 * LANES), full(LANES), full(LANES), full(LANES),
                  pl.BlockSpec((LANES, ncp), lambda bb, kv, i: (0, 0)),
                  pl.BlockSpec(gsel.shape, lambda bb, kv, i: (0, 0))],
        out_specs=qspec,
        out_shape=jax.ShapeDtypeStruct((b, s, hdim), BF16),
        scratch_shapes=[pltpu.VMEM((rows, 2 * LANES), BF16),
                        pltpu.VMEM((rows, 1), F32), pltpu.VMEM((rows, LANES), F32),
                        pltpu.VMEM((rows, 1), F32), pltpu.VMEM((rows, LANES), F32)],
        compiler_params=_params("parallel", "parallel", "arbitrary"),
        name="nsa_attention",
    )
    return lax.cond(small[0] != 0, run(False), run(True))


def _rope_tables(s, tm=None, rate=1):
    half = HEAD_DIM // 2
    row = lax.broadcasted_iota(jnp.int32, (s, LANES), 0)
    lane = lax.broadcasted_iota(jnp.int32, (s, LANES), 1)
    if rate > 1:
        seg = tm // rate
        local = row % tm
        row = row - local + (local % seg) * rate + local // seg
    inv = jnp.tile(ROPE_THETA ** (-jnp.arange(half, dtype=F32) / half), LANES // half)
    ang = row.astype(F32) * inv[None, :]
    cos, sin = jnp.cos(ang), jnp.sin(ang)
    return cos, jnp.where(lane % HEAD_DIM < half, -sin, sin), sin


def _split_half(t, axis):
    hpg = MXU_WIDTH // HEAD_DIM
    half = HEAD_DIM // 2
    shape = t.shape
    t = t.reshape(shape[:axis] + (shape[axis] // MXU_WIDTH, hpg, 2, half) + shape[axis + 1:])
    return t.swapaxes(axis + 1, axis + 2).reshape(shape)


def _importance_weights(ncp):
    r = SEL_BLOCK // CMP_STRIDE
    nper = CMP_LEN // CMP_STRIDE
    o = np.arange(ncp)[:, None] - r * np.arange(LANES)[None, :]
    w = np.zeros((ncp, LANES), np.float32)
    for off in range(-(nper - 1), r):
        cnt = sum(1 for m in range(r) for n in range(nper) if m - n == off)
        w += np.where(o == off, float(cnt), 0.0)
    return w.astype(BF16)


def kernel(x, c, norm_g, w_ada, b_ada, ffn_w_in, ffn_w_out, a_w_qkv, a_q_gain, a_k_gain, a_w_o, kv_norm_g,
           w_ada_kv, b_ada_kv, w_kv, kv_k_gain, cmp_pos, phi_w1, phi_w2, b_w_qg, b_q_gain, b_w_o):
    b, s, d = x.shape
    depth = norm_g.shape[0]
    n_a = depth // 2
    hdim = N_HEADS * HEAD_DIM
    assert depth == 2 and b <= 8

    c8 = jnp.zeros((8, d), F32).at[:b].set(c)
    mod = _ada(c8, w_ada, b_ada.reshape(depth, 1, 9 * d))
    mod5 = mod[:, :b].reshape(depth, b, 9, 1, d)
    kvmod = _ada(c8, w_ada_kv[None], b_ada_kv.reshape(1, 1, 2 * d))
    kvmod5 = kvmod[:, :b].reshape(1, b, 2, 1, d)

    norm_g3 = norm_g.reshape(depth * 3, 1, d)
    w_in = ffn_w_in.astype(BF16)
    w_out = ffn_w_out.astype(BF16)
    cos, sin, _ = _rope_tables(s)
    blk = np.arange(MXU_WIDTH) // HEAD_DIM
    bd_wide = (blk[:, None] == blk[None, :]).astype(BF16)
    blk = (np.arange(MXU_WIDTH) % LANES) // (HEAD_DIM // 2)
    bd_split = (blk[:, None] == blk[None, :]).astype(BF16)
    q_scale = HEAD_DIM ** -0.5 * LOG2E

    h = x
    for l in range(depth):
        h = _ffn(h, norm_g3, mod5, w_in, w_out, l, 0, 0)
        if l < n_a:
            gain = jnp.stack([_split_half(jnp.tile(a_q_gain[l] * q_scale, (1, N_HEADS)), 1),
                              _split_half(jnp.tile(a_k_gain[l], (1, N_HEADS)), 1),
                              jnp.ones((N_DIL, hdim), F32)], axis=1).reshape(3 * N_DIL, 1, hdim)
            cols = np.arange(N_DIL * 3 * hdim).reshape(N_DIL, 3, hdim)
            cols = np.concatenate([_split_half(cols[:, :2], 2), cols[:, 2:]], axis=1).reshape(-1)
            w_qkv = a_w_qkv[l][:, cols].astype(BF16)
            bound = HEAD_DIM * q_scale * jnp.max(jnp.abs(a_q_gain[l])) * jnp.max(jnp.abs(a_k_gain[l]))
            small = (bound <= SCORE_LIMIT).astype(jnp.int32).reshape(1)
            parts = [_dilated_attention(small, _qkv_proj(h, norm_g3, mod5, w_qkv, gain, bd_split, l, g), g)
                     for g in range(N_DIL)]
            h = _merge_out([p[0] for p in parts], [p[1] for p in parts], h, mod5, a_w_o[l].astype(BF16), l)
        else:
            j = l - n_a
            w_q = b_w_qg[j][:, :hdim]
            w_g = b_w_qg[j][:, hdim:].reshape(d, 3, N_KV_HEADS, GROUP).transpose(0, 2, 1, 3)
            w_g = jnp.pad(w_g.reshape(d, N_KV_HEADS, 3 * GROUP), ((0, 0), (0, 0), (0, LANES - 3 * GROUP)))
            w_qg = jnp.concatenate([w_q, w_g.reshape(d, N_KV_HEADS * LANES)], axis=1).astype(BF16)
            gain = jnp.tile(b_q_gain[j] * q_scale, N_HEADS)[None, :]
            qn, qr, gates = _qg_proj(h, norm_g3, mod5, w_qg, gain, cos, sin, bd_wide, l)
            bound = HEAD_DIM * q_scale * jnp.max(jnp.abs(b_q_gain[j])) * jnp.max(jnp.abs(kv_k_gain))
            small = (bound <= SCORE_LIMIT).astype(jnp.int32).reshape(1)
            o = _nsa_attention(small, qn, qr, gates, *shared)
            h = _out_proj(o, h, mod5, b_w_o[j].astype(BF16), l)
        h = _ffn(h, norm_g3, mod5, w_in, w_out, l, 2, 1)
        if l == n_a - 1:
            col = lambda br, kv, kvh: br * 4 * HEAD_DIM + kv * 2 * HEAD_DIM + kvh * HEAD_DIM + jnp.arange(HEAD_DIM)
            dup = lambda br, kv, kvh: jnp.concatenate([col(br, kv, kvh)] * 2)
            cols = jnp.concatenate(
                [dup(br, 0, kvh) for br in (1, 2) for kvh in range(N_KV_HEADS)]
                + [dup(br, 1, kvh) for br in (1, 2) for kvh in range(N_KV_HEADS)]
                + [col(0, kv, kvh) for kv in range(2) for kvh in range(N_KV_HEADS)])
            w_kv_ext = w_kv[:, cols].astype(BF16)
            reps = MXU_WIDTH // HEAD_DIM
            kgain = jnp.concatenate([jnp.tile(kv_k_gain[1], (1, reps)), jnp.tile(kv_k_gain[2], (1, reps)),
                                     jnp.ones((6, MXU_WIDTH), F32)], axis=0)
            ks, kw, vs, vw, kvc = _kv_proj(h, kv_norm_g[None, :], kvmod5, w_kv_ext, kgain, cos, sin, bd_wide)
            nsb = s // CMP_STRIDE
            t4 = kvc.reshape(b, nsb, CMP_STRIDE, 4, HEAD_DIM).transpose(3, 0, 1, 2, 4)
            t4 = t4.reshape(4, b, nsb, CMP_STRIDE * HEAD_DIM)
            cmp4 = _compress(t4, phi_w1, phi_w2, cmp_pos.reshape(2, 1, CMP_LEN * HEAD_DIM), kv_k_gain[0][None, :])
            shared = (cmp4, ks, vs, kw, vw, _importance_weights(nsb).T)
    return h
```

```python
import functools

import jax
import jax.numpy as jnp
import numpy as np
from jax import lax
from jax.experimental import pallas as pl
from jax.experimental.pallas import tpu as pltpu

HEAD_DIM = 64
N_HEADS = 16
DIL_WINDOWS = (128, 512, 2048)
DIL_RATES = (1, 4, 16)
N_DIL = 3
N_KV_HEADS = 2
GROUP = N_HEADS // N_KV_HEADS
CMP_STRIDE = 16
CMP_LEN = 32
SEL_BLOCK = 64
SEL_SHIFT = SEL_BLOCK.bit_length() - 1
N_SELECT = 16
SLIDE_WINDOW = 512
ROPE_THETA = 10000.0
EPS = 1e-6

LANES = 128
MXU_WIDTH = 256
Q_ROWS = 128
NEG = -1e30
LOG2E = 1.4426950408889634
SCORE_LIMIT = 60.0
VMEM_LIMIT = 56 * 1024 * 1024

BF16 = jnp.bfloat16
F32 = jnp.float32


def _params(*sem):
    return pltpu.CompilerParams(dimension_semantics=sem, vmem_limit_bytes=VMEM_LIMIT)


def _dot(a, b):
    return jnp.dot(a, b, preferred_element_type=F32)


def _dot_t(a, b):
    return lax.dot_general(a, b, (((1,), (1,)), ((), ())), preferred_element_type=F32)


def _split_dot(a, b):
    a_hi = a.astype(BF16)
    a_lo = (a - a_hi.astype(F32)).astype(BF16)
    b_hi = b.astype(BF16)
    b_lo = (b - b_hi.astype(F32)).astype(BF16)
    return _dot(a_hi, b_hi) + (_dot(a_hi, b_lo) + _dot(a_lo, b_hi))


def _sigmoid(x):
    return 1.0 / (1.0 + jnp.exp(-x))


def _normmod(x, g, shift, scale):
    return x * lax.rsqrt(jnp.mean(x * x, axis=-1, keepdims=True) + EPS) * (g * (1.0 + scale)) + shift


def _head_rms(y, bd):
    ss = _dot((y * y).astype(BF16), bd)
    return y * lax.rsqrt(ss * (1.0 / HEAD_DIM) + EPS)


def _rope(z, cos, sin_signed):
    lane = lax.broadcasted_iota(jnp.int32, z.shape, 1)
    first_half = (lane & (HEAD_DIM - 1)) < HEAD_DIM // 2
    swapped = jnp.where(first_half, pltpu.roll(z, z.shape[1] - HEAD_DIM // 2, 1),
                        pltpu.roll(z, HEAD_DIM // 2, 1))
    return z * cos + swapped * sin_signed


def _ada_kernel(c_ref, w_ref, b_ref, o_ref):
    c = c_ref[...]
    a = (c * _sigmoid(c)).astype(BF16)
    o_ref[...] = _dot(a, w_ref[...].astype(BF16)) + b_ref[...]


def _ada(c8, w, b):
    nl, d, n = w.shape
    tn = 1024
    return pl.pallas_call(
        _ada_kernel,
        grid=(nl, n // tn),
        in_specs=[pl.BlockSpec((8, d), lambda l, j: (0, 0)),
                  pl.BlockSpec((None, d, tn), lambda l, j: (l, 0, j)),
                  pl.BlockSpec((None, 1, tn), lambda l, j: (l, 0, j))],
        out_specs=pl.BlockSpec((None, 8, tn), lambda l, j: (l, 0, j)),
        out_shape=jax.ShapeDtypeStruct((nl, 8, n), F32),
        compiler_params=_params("parallel", "parallel"),
        name="ada_table",
    )(c8, w, b)


def _mod_block(d, l, k):
    return pl.BlockSpec((None, None, None, 1, d), lambda b, *_: (l, b, k, 0, 0))


def _ffn_kernel(*refs, fused_attn):
    if fused_attn:
        attn_ref, ga_ref, wa_ref, *refs = refs
    h_ref, g_ref, sh_ref, sc_ref, gt_ref, wg_ref, wu_ref, wo_ref, o_ref, u_scr, acc_scr = refs
    o_ref[0] = h_ref[0]
    if fused_attn:
        o_ref[0] += (1.0 + ga_ref[...]) * _dot(attn_ref[0], wa_ref[...])
    u_scr[...] = _normmod(o_ref[0], g_ref[...], sh_ref[...], sc_ref[...]).astype(BF16)

    u = u_scr[...]
    tf = wg_ref.shape[1]
    bounds = [(lo, min(lo + MXU_WIDTH, tf)) for lo in range(0, tf, MXU_WIDTH)]

    def gate_up(c):
        lo, hi = bounds[c]
        return _dot(u, wg_ref[:, lo:hi]), _dot(u, wu_ref[:, lo:hi])

    nxt = gate_up(0)
    for c, (lo, hi) in enumerate(bounds):
        gate, up = nxt
        if c + 1 < len(bounds):
            nxt = gate_up(c + 1)
        act = (gate * _sigmoid(gate) * up).astype(BF16)
        down = _dot(act, wo_ref[lo:hi, :])
        acc_scr[...] = down if c == 0 else acc_scr[...] + down

    o_ref[0] += (0.5 * (1.0 + gt_ref[...])) * acc_scr[...]


def _ffn(h, norm_g3, mod5, w_in, w_out, l, sub, which, attn=None, w_attn=None):
    b, s, d = h.shape
    f = w_out.shape[2]
    tm = min(1024, s)
    resident = pl.Buffered(1)
    fused = attn is not None
    extra_specs, extra_args = [], []
    if fused:
        hdim = w_attn.shape[0]
        extra_specs = [pl.BlockSpec((1, tm, hdim), lambda bb, i: (bb, i, 0)),
                       _mod_block(d, l, 5),
                       pl.BlockSpec((hdim, d), lambda bb, i: (0, 0), pipeline_mode=resident)]
        extra_args = [attn, mod5, w_attn]
    return pl.pallas_call(
        functools.partial(_ffn_kernel, fused_attn=fused),
        grid=(b, s // tm),
        in_specs=extra_specs + [
                  pl.BlockSpec((1, tm, d), lambda bb, i: (bb, i, 0)),
                  pl.BlockSpec((None, 1, d), lambda bb, i: (l * 3 + sub, 0, 0)),
                  _mod_block(d, l, sub * 3 + 0),
                  _mod_block(d, l, sub * 3 + 1),
                  _mod_block(d, l, sub * 3 + 2),
                  pl.BlockSpec((None, None, d, f), lambda bb, i: (l, which, 0, 0), pipeline_mode=resident),
                  pl.BlockSpec((None, None, d, f), lambda bb, i: (l, which, 0, 1), pipeline_mode=resident),
                  pl.BlockSpec((None, None, f, d), lambda bb, i: (l, which, 0, 0), pipeline_mode=resident)],
        out_specs=pl.BlockSpec((1, tm, d), lambda bb, i: (bb, i, 0)),
        out_shape=jax.ShapeDtypeStruct((b, s, d), F32),
        scratch_shapes=[pltpu.VMEM((tm, d), BF16), pltpu.VMEM((tm, d), F32)],
        compiler_params=_params("parallel", "parallel"),
        name="swiglu_half_step",
    )(*extra_args, h, norm_g3, mod5, mod5, mod5, w_in, w_in, w_out)


def _qkv_kernel(h_ref, g_ref, sh_ref, sc_ref, w_ref, gain_ref, cos_ref, sin_ref, bd_ref, o_ref, u_scr, uf_scr,
                *, rate):
    tm = u_scr.shape[0]
    seg = tm // rate
    u = _normmod(h_ref[0], g_ref[...], sh_ref[...], sc_ref[...])
    if rate == 1:
        u_scr[...] = u.astype(BF16)
    else:
        for cb in range(u.shape[1] // LANES):
            sl = slice(cb * LANES, (cb + 1) * LANES)
            uf_scr[cb] = u[:, sl]
            for res in range(rate):
                u_scr[res * seg:(res + 1) * seg, sl] = uf_scr[cb, pl.ds(res, seg, stride=rate), :].astype(BF16)

    n_grp = o_ref.shape[1]
    per_kind = n_grp // 3
    cos = cos_ref[...]
    sin = sin_ref[...]
    bd = bd_ref[...]

    def project(grp):
        return _dot(u_scr[...], w_ref[:, grp * MXU_WIDTH:(grp + 1) * MXU_WIDTH])

    y = project(0)
    for grp in range(n_grp):
        y_next = project(grp + 1) if grp + 1 < n_grp else None
        kind, sub = divmod(grp, per_kind)
        if kind < 2:
            z = _head_rms(y, bd) * gain_ref[kind, :, sub * MXU_WIDTH:(sub + 1) * MXU_WIDTH]
            x1, x2 = z[:, :LANES], z[:, LANES:]
            y = jnp.concatenate([x1 * cos - x2 * sin, x1 * sin + x2 * cos], axis=1)
        y = y.astype(BF16)
        for res in range(rate):
            o_ref[res, grp] = y[res * seg:(res + 1) * seg]
        y = y_next


def _qkv_proj(h, norm_g3, mod5, w, gain, bd, l, g):
    b, s, d = h.shape
    rate = DIL_RATES[g]
    tm = min(1024, s)
    seg = tm // rate
    tn = N_HEADS * HEAD_DIM
    n_grp = tn // MXU_WIDTH
    cos, _, sin = _rope_tables(s, tm, rate)

    return pl.pallas_call(
        functools.partial(_qkv_kernel, rate=rate),
        grid=(b, s // tm),
        in_specs=[pl.BlockSpec((1, tm, d), lambda bb, i: (bb, i, 0)),
                  pl.BlockSpec((None, 1, d), lambda bb, i: (l * 3 + 1, 0, 0)),
                  _mod_block(d, l, 3),
                  _mod_block(d, l, 4),
                  pl.BlockSpec((d, 3 * tn), lambda bb, i: (0, g), pipeline_mode=pl.Buffered(1)),
                  pl.BlockSpec((3, 1, tn), lambda bb, i: (g, 0, 0)),
                  pl.BlockSpec((tm, LANES), lambda bb, i: (i, 0)),
                  pl.BlockSpec((tm, LANES), lambda bb, i: (i, 0)),
                  pl.BlockSpec(bd.shape, lambda bb, i: (0, 0))],
        out_specs=pl.BlockSpec((None, rate, 3 * n_grp, seg, MXU_WIDTH), lambda bb, i: (bb, 0, 0, i, 0)),
        out_shape=jax.ShapeDtypeStruct((b, rate, 3 * n_grp, s // rate, MXU_WIDTH), BF16),
        scratch_shapes=[pltpu.VMEM((tm, d), BF16), pltpu.VMEM((d // LANES, tm, LANES), F32)],
        compiler_params=_params("parallel", "parallel"),
        name=f"qkv_proj_rate{rate}",
    )(h, norm_g3, mod5, mod5, w, gain, cos, sin, bd)


def _dil_body(bias_ref, q_ref, kp_ref, kc_ref, vp_ref, vc_ref, o_ref, lse_ref, *, tq, shifted):
    i = pl.program_id(2)
    n_grp = q_ref.shape[0]
    hpg = MXU_WIDTH // HEAD_DIM
    qlane = lax.broadcasted_iota(jnp.int32, (Q_ROWS, MXU_WIDTH), 1)
    q_head = (qlane & (LANES - 1)) // (HEAD_DIM // 2)
    v_head = qlane // HEAD_DIM
    seg = lax.broadcasted_iota(jnp.int32, (Q_ROWS, LANES), 1) // (LANES // N_HEADS)

    def scores(grp, j):
        q = q_ref[grp, j * Q_ROWS:(j + 1) * Q_ROWS, :]
        zero = jnp.zeros_like(q)
        q2 = jnp.concatenate([jnp.where(q_head == t, q, zero) for t in range(hpg)], axis=0)
        if j == 0:
            k = jnp.concatenate([kp_ref[grp], kc_ref[grp, 0:Q_ROWS, :]], axis=0)
        else:
            k = kc_ref[grp, (j - 1) * Q_ROWS:(j + 1) * Q_ROWS, :]
        return _dot_t(q2, k)

    for j in range(tq // Q_ROWS):
        rs = slice(j * Q_ROWS, (j + 1) * Q_ROWS)
        lse_tile = None
        for grp in range(n_grp):
            s = scores(grp, j) + bias_ref[...]
            if j == 0:
                v = jnp.concatenate([vp_ref[grp], vc_ref[grp, 0:Q_ROWS, :]], axis=0)
                prev_key = lax.broadcasted_iota(jnp.int32, s.shape, 1) < Q_ROWS
                s = jnp.where(prev_key & (i == 0), NEG, s)
            else:
                v = vc_ref[grp, (j - 1) * Q_ROWS:(j + 1) * Q_ROWS, :]
            if shifted:
                m = jnp.max(s, axis=-1, keepdims=True)
                p = jnp.exp2(s - m)
            else:
                p = jnp.exp2(s)
            den = jnp.sum(p, axis=-1, keepdims=True)
            o2 = _dot(p.astype(BF16), v) / den
            lse2 = jnp.log(den) * LOG2E
            if shifted:
                lse2 = lse2 + m
            out = o2[0:Q_ROWS]
            for t in range(hpg):
                ts = slice(t * Q_ROWS, (t + 1) * Q_ROWS)
                if t:
                    out = jnp.where(v_head == t, o2[ts], out)
                head_lse = jnp.broadcast_to(lse2[ts], (Q_ROWS, LANES))
                lse_tile = head_lse if lse_tile is None else jnp.where(seg == hpg * grp + t, head_lse, lse_tile)
            o_ref[0, rs, grp * MXU_WIDTH:(grp + 1) * MXU_WIDTH] = out.astype(BF16)
        lse_ref[0, rs, :] = lse_tile


def _dil_kernel(small_ref, *refs, tq):
    @pl.when(small_ref[0] != 0)
    def _():
        _dil_body(*refs, tq=tq, shifted=False)

    @pl.when(small_ref[0] == 0)
    def _():
        _dil_body(*refs, tq=tq, shifted=True)


def _dilated_attention(small, qkv_r, g):
    b, rate, n3, l, width = qkv_r.shape
    s = l * rate
    n_back = DIL_WINDOWS[g] // rate
    assert n_back <= Q_ROWS and rate == DIL_RATES[g] and width == MXU_WIDTH
    tq = min(512, l)
    hdim = N_HEADS * HEAD_DIM
    n_grp = n3 // 3
    sub = tq // Q_ROWS

    def cur(kind):
        return pl.BlockSpec((None, None, n_grp, tq, width), lambda bb, r, i: (bb, r, kind, i, 0))

    def prev(kind):
        return pl.BlockSpec((None, None, n_grp, Q_ROWS, width),
                            lambda bb, r, i: (bb, r, kind, jnp.maximum(i * sub - 1, 0), 0))

    rows = (width // HEAD_DIM) * Q_ROWS
    dist = Q_ROWS + (np.arange(rows)[:, None] % Q_ROWS) - np.arange(2 * Q_ROWS)[None, :]
    bias = np.where((dist >= 0) & (dist <= n_back), 0.0, NEG).astype(np.float32)

    o, lse = pl.pallas_call(
        functools.partial(_dil_kernel, tq=tq),
        grid=(b, rate, l // tq),
        in_specs=[pl.BlockSpec(memory_space=pltpu.SMEM),
                  pl.BlockSpec(bias.shape, lambda bb, r, i: (0, 0)),
                  cur(0), prev(1), cur(1), prev(2), cur(2)],
        out_specs=[pl.BlockSpec((1, None, tq, hdim), lambda bb, r, i: (bb, r, i, 0)),
                   pl.BlockSpec((1, None, tq, LANES), lambda bb, r, i: (bb, r, i, 0))],
        out_shape=[jax.ShapeDtypeStruct((b, rate, l, hdim), BF16),
                   jax.ShapeDtypeStruct((b, rate, l, LANES), F32)],
        compiler_params=_params("parallel", "parallel", "parallel"),
        name=f"dilated_attention_rate{rate}",
    )(small, bias, qkv_r, qkv_r, qkv_r, qkv_r, qkv_r)
    return o, lse


def _merge_out_kernel(o0_ref, o1_ref, o2_ref, l0_ref, l1_ref, l2_ref, p1_ref, p2_ref, e_ref, h_ref, gt_ref, w_ref,
                      out_ref):
    def positions(x_ref, p_ref):
        x = jnp.concatenate([x_ref[res] for res in range(x_ref.shape[0])], axis=0)
        if p_ref is None:
            return x.astype(F32)
        if x.dtype == BF16:
            return _dot(p_ref[...], x)
        hi = x.astype(BF16)
        return _dot(p_ref[...], hi) + _dot(p_ref[...], (x - hi.astype(F32)).astype(BF16))

    l0, l1, l2 = positions(l0_ref, None), positions(l1_ref, p1_ref), positions(l2_ref, p2_ref)
    m = jnp.maximum(jnp.maximum(l0, l1), l2)
    e0, e1, e2 = jnp.exp2(l0 - m), jnp.exp2(l1 - m), jnp.exp2(l2 - m)
    inv = 1.0 / (e0 + e1 + e2)
    spread = e_ref[...]
    o = (_dot((e0 * inv).astype(BF16), spread) * positions(o0_ref, None)
         + _dot((e1 * inv).astype(BF16), spread) * positions(o1_ref, p1_ref)
         + _dot((e2 * inv).astype(BF16), spread) * positions(o2_ref, p2_ref))
    out_ref[0] = h_ref[0] + (1.0 + gt_ref[...]) * _dot(o.astype(BF16), w_ref[...])


def _merge_out(os_, lses, h, mod5, w_o, l):
    b, s, d = h.shape
    hdim = w_o.shape[0]
    tm = min(512, s)
    row = lambda width: pl.BlockSpec((1, tm, width), lambda bb, i: (bb, i, 0))
    rep = LANES // N_HEADS
    lane = np.arange(LANES)[:, None]
    col = np.arange(hdim)[None, :]
    spread = ((lane % rep == 0) & (lane // rep == col // HEAD_DIM)).astype(BF16)

    def unpermute(rate):
        pos = np.arange(tm)
        return (((pos % rate) * (tm // rate) + pos // rate)[:, None] == np.arange(tm)[None, :]).astype(BF16)

    def residue_spec(t):
        rate, width = t.shape[1], t.shape[3]
        return pl.BlockSpec((None, rate, tm // rate, width), lambda bb, i: (bb, 0, i, 0))

    const = lambda shape: pl.BlockSpec(shape, lambda bb, i: (0, 0))
    return pl.pallas_call(
        _merge_out_kernel,
        grid=(b, s // tm),
        in_specs=[residue_spec(t) for t in (*os_, *lses)]
        + [const((tm, tm)), const((tm, tm)), const((LANES, hdim)), row(d), _mod_block(d, l, 5), const((hdim, d))],
        out_specs=row(d),
        out_shape=jax.ShapeDtypeStruct((b, s, d), F32),
        compiler_params=_params("parallel", "parallel"),
        name="merge_out_proj",
    )(*os_, *lses, unpermute(os_[1].shape[1]), unpermute(os_[2].shape[1]), spread, h, mod5, w_o)


def _out_kernel(o_ref, h_ref, gt_ref, w_ref, out_ref):
    out_ref[0] = h_ref[0] + (1.0 + gt_ref[...]) * _dot(o_ref[0], w_ref[...])


def _out_proj(o, h, mod5, w_o, l):
    b, s, d = h.shape
    hdim = w_o.shape[0]
    tm = min(1024, s)
    row = lambda width: pl.BlockSpec((1, tm, width), lambda bb, i: (bb, i, 0))
    return pl.pallas_call(
        _out_kernel,
        grid=(b, s // tm),
        in_specs=[row(hdim), row(d), _mod_block(d, l, 5), pl.BlockSpec((hdim, d), lambda bb, i: (0, 0))],
        out_specs=row(d),
        out_shape=jax.ShapeDtypeStruct((b, s, d), F32),
        compiler_params=_params("parallel", "parallel"),
        name="out_proj",
    )(o, h, mod5, w_o)


def _kv_kernel(h_ref, g_ref, sh_ref, sc_ref, w_ref, gain_ref, cos_ref, sin_ref, bd_ref,
               ks_ref, kw_ref, vs_ref, vw_ref, kvc_ref, *, tm):
    i = pl.program_id(1)
    u = _normmod(h_ref[0], g_ref[...], sh_ref[...], sc_ref[...]).astype(BF16)
    bd = bd_ref[...]
    width = bd.shape[0]
    cos = jnp.concatenate([cos_ref[...]] * (width // LANES), axis=1)
    sin = jnp.concatenate([sin_ref[...]] * (width // LANES), axis=1)
    lane = lax.broadcasted_iota(jnp.int32, (tm, LANES), 1)
    pos = i * tm + lax.broadcasted_iota(jnp.int32, (tm, LANES), 0)
    onehot = jnp.where((pos >> SEL_SHIFT) == lane, 1.0, 0.0).astype(BF16)
    wide_lane = lax.broadcasted_iota(jnp.int32, (tm, width), 1)

    def project(c):
        return _dot(u, w_ref[:, c * width:(c + 1) * width])

    y = project(0)
    for c in range(5):
        y_next = project(c + 1) if c < 4 else None
        if c < 2:
            z = _rope(_head_rms(y, bd) * gain_ref[c:c + 1, :], cos, sin).astype(BF16)
            if c == 0:
                for kvh in range(N_KV_HEADS):
                    ks_ref[0, :, 2 * kvh * LANES:(2 * kvh + 1) * LANES] = z[:, kvh * LANES:(kvh + 1) * LANES]
                    ks_ref[0, :, (2 * kvh + 1) * LANES:(2 * kvh + 2) * LANES] = onehot
            else:
                kw_ref[0] = z
        elif c < 4:
            va = jnp.where((wide_lane & (LANES - 1)) < HEAD_DIM, y, 1.0).astype(BF16)
            if c == 2:
                vs_ref[0] = va
            else:
                vw_ref[0] = va
        else:
            kvc_ref[0] = y
        y = y_next


def _kv_proj(h, kv_norm_g, kvmod5, w, gain, cos, sin, bd):
    b, s, d = h.shape
    n = w.shape[1]
    tm = min(512, s)
    row = lambda width, dt: (pl.BlockSpec((1, tm, width), lambda bb, i: (bb, i, 0)),
                             jax.ShapeDtypeStruct((b, s, width), dt))
    outs = [row(4 * LANES, BF16), row(2 * LANES, BF16), row(2 * LANES, BF16), row(2 * LANES, BF16),
            row(2 * LANES, F32)]
    return pl.pallas_call(
        functools.partial(_kv_kernel, tm=tm),
        grid=(b, s // tm),
        in_specs=[pl.BlockSpec((1, tm, d), lambda bb, i: (bb, i, 0)),
                  pl.BlockSpec((1, d), lambda bb, i: (0, 0)),
                  _mod_block(d, 0, 0),
                  _mod_block(d, 0, 1),
                  pl.BlockSpec((d, n), lambda bb, i: (0, 0)),
                  pl.BlockSpec(gain.shape, lambda bb, i: (0, 0)),
                  pl.BlockSpec((tm, LANES), lambda bb, i: (i, 0)),
                  pl.BlockSpec((tm, LANES), lambda bb, i: (i, 0)),
                  pl.BlockSpec(bd.shape, lambda bb, i: (0, 0))],
        out_specs=[o[0] for o in outs],
        out_shape=[o[1] for o in outs],
        compiler_params=_params("parallel", "parallel"),
        name="shared_kv_proj",
    )(h, kv_norm_g, kvmod5, kvmod5, w, gain, cos, sin, bd)


def _cmp_kernel(t_ref, w1_ref, w2_ref, pos_ref, gain_ref, o_ref):
    is_key = pl.program_id(0) < N_KV_HEADS
    t = t_ref[...]
    w1 = w1_ref[...]
    half = w1.shape[0] // 2
    first = _split_dot(t, w1[:half])
    second = _split_dot(t, w1[half:])
    second = pltpu.roll(second, second.shape[0] - 1, 0)
    posb = _split_dot(jnp.broadcast_to(pos_ref[...], (8, w1.shape[0])), w1)[0:1]
    hid = first + second + posb
    out = _split_dot(hid * _sigmoid(hid), w2_ref[...])
    key = out * lax.rsqrt(jnp.mean(out * out, axis=-1, keepdims=True) + EPS) * gain_ref[...]
    dup = jnp.concatenate([key, key], axis=-1)
    aug = jnp.concatenate([out, jnp.ones_like(out)], axis=-1)
    o_ref[...] = jnp.where(is_key, dup, aug).astype(BF16)


def _compress(t4, phi_w1, phi_w2, cmp_pos, gain):
    _, b, nsb, width = t4.shape
    hid = phi_w1.shape[2]
    return pl.pallas_call(
        _cmp_kernel,
        grid=(4, b),
        in_specs=[pl.BlockSpec((None, None, nsb, width), lambda t, bb: (t, bb, 0, 0)),
                  pl.BlockSpec((None, CMP_LEN * HEAD_DIM, hid), lambda t, bb: (t // N_KV_HEADS, 0, 0)),
                  pl.BlockSpec((None, hid, HEAD_DIM), lambda t, bb: (t // N_KV_HEADS, 0, 0)),
                  pl.BlockSpec((None, 1, CMP_LEN * HEAD_DIM), lambda t, bb: (t // N_KV_HEADS, 0, 0)),
                  pl.BlockSpec((1, HEAD_DIM), lambda t, bb: (0, 0))],
        out_specs=pl.BlockSpec((None, None, nsb, LANES), lambda t, bb: (t, bb, 0, 0)),
        out_shape=jax.ShapeDtypeStruct((4, b, nsb, LANES), BF16),
        compiler_params=_params("parallel", "parallel"),
        name="compress_blocks",
    )(t4, phi_w1, phi_w2, cmp_pos, gain)


def _qg_kernel(h_ref, g_ref, sh_ref, sc_ref, w_ref, gain_ref, cos_ref, sin_ref, bd_ref,
               qn_ref, qr_ref, gate_ref):
    u = _normmod(h_ref[0], g_ref[...], sh_ref[...], sc_ref[...]).astype(BF16)
    bd = bd_ref[...]
    width = bd.shape[0]
    cos = jnp.concatenate([cos_ref[...]] * (width // LANES), axis=1)
    sin = jnp.concatenate([sin_ref[...]] * (width // LANES), axis=1)
    hdim = qn_ref.shape[2]

    def project(c):
        return _dot(u, w_ref[:, c * width:(c + 1) * width])

    y = project(0)
    for c in range(hdim // width):
        y_next = project(c + 1)
        sl = slice(c * width, (c + 1) * width)
        z = _head_rms(y, bd) * gain_ref[:, sl]
        qn_ref[0, :, sl] = z.astype(BF16)
        qr_ref[0, :, sl] = _rope(z, cos, sin).astype(BF16)
        y = y_next
    gate_ref[0] = _sigmoid(y)


def _qg_proj(h, norm_g3, mod5, w, gain, cos, sin, bd, l):
    b, s, d = h.shape
    n = w.shape[1]
    hdim = N_HEADS * HEAD_DIM
    tm = min(512, s)
    row = lambda width, dt: (pl.BlockSpec((1, tm, width), lambda bb, i: (bb, i, 0)),
                             jax.ShapeDtypeStruct((b, s, width), dt))
    outs = [row(hdim, BF16), row(hdim, BF16), row(n - hdim, F32)]
    return pl.pallas_call(
        _qg_kernel,
        grid=(b, s // tm),
        in_specs=[pl.BlockSpec((1, tm, d), lambda bb, i: (bb, i, 0)),
                  pl.BlockSpec((None, 1, d), lambda bb, i: (l * 3 + 1, 0, 0)),
                  _mod_block(d, l, 3),
                  _mod_block(d, l, 4),
                  pl.BlockSpec((d, n), lambda bb, i: (0, 0)),
                  pl.BlockSpec((1, hdim), lambda bb, i: (0, 0)),
                  pl.BlockSpec((tm, LANES), lambda bb, i: (i, 0)),
                  pl.BlockSpec((tm, LANES), lambda bb, i: (i, 0)),
                  pl.BlockSpec(bd.shape, lambda bb, i: (0, 0))],
        out_specs=[o[0] for o in outs],
        out_shape=[o[1] for o in outs],
        compiler_params=_params("parallel", "parallel"),
        name="nsa_query_gate_proj",
    )(h, norm_g3, mod5, mod5, w, gain, cos, sin, bd)


def _nsa_body(qn_ref, qr_ref, gate_ref, kc_ref, vc_ref, ks_ref, vs_ref, kw_ref, vw_ref, wimpt_ref, gsel_ref,
              o_ref, qa_scr, m_scr, acc_scr, mw_scr, accw_scr, *, tq, shifted, ncp):
    i = pl.program_id(2)
    start = i * tq
    rows = GROUP * tq
    low = lax.broadcasted_iota(jnp.int32, (tq, LANES), 1) < HEAD_DIM
    low_r = lax.broadcasted_iota(jnp.int32, (rows, LANES), 1) < HEAD_DIM

    def accumulate(s, v, mask, first, acc_ref=acc_scr, m_ref=m_scr):
        if not shifted:
            p = jnp.exp2(s)
            if mask is not None:
                p = jnp.where(mask, p, 0.0)
            pv = _dot(p.astype(BF16), v)
            acc_ref[...] = pv if first else acc_ref[...] + pv
            return
        if mask is not None:
            s = jnp.where(mask, s, NEG)
        if first:
            m_new = jnp.max(s, axis=-1, keepdims=True)
            acc_ref[...] = _dot(jnp.exp2(s - m_new).astype(BF16), v)
        else:
            m_old = m_ref[...]
            m_new = jnp.maximum(m_old, jnp.max(s, axis=-1, keepdims=True))
            acc_ref[...] = jnp.exp2(m_old - m_new) * acc_ref[...] + _dot(jnp.exp2(s - m_new).astype(BF16), v)
        m_ref[...] = m_new

    def stack(q):
        parts = []
        for g in range(GROUP):
            blk = q[:, (g // 2) * LANES:(g // 2 + 1) * LANES]
            parts.append(jnp.where(low if g % 2 == 0 else ~low, blk, jnp.zeros_like(blk)))
        return parts

    qn = jnp.concatenate(stack(qn_ref[0]), axis=0)
    sc = _dot_t(qn, kc_ref[0:ncp, :])
    c_end = lax.broadcasted_iota(jnp.int32, (tq, ncp), 1) * CMP_STRIDE + (CMP_LEN - 1)
    valid_t = c_end <= start + lax.broadcasted_iota(jnp.int32, (tq, ncp), 0)
    if shifted:
        valid = jnp.concatenate([valid_t] * GROUP, axis=0)
        sc = jnp.where(valid, sc, NEG)
        sc = sc - jnp.max(sc, axis=-1, keepdims=True)
        p = jnp.where(valid, jnp.exp2(sc), 0.0)
    else:
        p = jnp.exp2(sc + jnp.concatenate([jnp.where(valid_t, 0.0, NEG)] * GROUP, axis=0))
    oc = _dot(p.astype(BF16), vc_ref[0:ncp, :])
    inv = 1.0 / jnp.maximum(jnp.where(low_r, pltpu.roll(oc, HEAD_DIM, 1), oc), 1e-30)
    o_cmp = oc * inv
    psum_blocks = []
    for cb in range(ncp // LANES):
        cs = slice(cb * LANES, (cb + 1) * LANES)
        tot = p[0:tq, cs] * inv[0:tq]
        for g in range(1, GROUP):
            tot = tot + p[g * tq:(g + 1) * tq, cs] * inv[g * tq:(g + 1) * tq]
        psum_blocks.append(tot)
    psum = jnp.concatenate(psum_blocks, axis=1)

    qr_parts = stack(qr_ref[0])
    for g in range(GROUP):
        qa_scr[g * tq:(g + 1) * tq, 0:LANES] = qr_parts[g]
    qr = jnp.concatenate(qr_parts, axis=0)

    tri = (lax.broadcasted_iota(jnp.int32, (rows, tq), 1)
           <= (lax.broadcasted_iota(jnp.int32, (rows, tq), 0) & (tq - 1)))

    def key_block(kb):
        return pl.ds(pl.multiple_of(kb * tq, tq), tq)

    diag = key_block(i)
    accumulate(_dot_t(qr, kw_ref[diag, :]), vw_ref[diag, :], tri, True, accw_scr, mw_scr)

    p_hi = psum.astype(BF16)
    p_lo = (psum - p_hi.astype(F32)).astype(BF16)
    wimpt = wimpt_ref[:, 0:ncp]
    imp_t = _dot_t(wimpt, p_hi) + _dot_t(wimpt, p_lo)

    blk = key_block(jnp.maximum(i - 1, 0))
    accumulate(_dot_t(qr, kw_ref[blk, :]), vw_ref[blk, :], jnp.broadcast_to(i >= 1, tri.shape), False,
               accw_scr, mw_scr)
    blk = key_block(jnp.maximum(i - 2, 0))
    accumulate(_dot_t(qr, kw_ref[blk, :]), vw_ref[blk, :], (~tri) & (i >= 2), False, accw_scr, mw_scr)

    blk_id = lax.broadcasted_iota(jnp.int32, (LANES, tq), 0)
    jt = (start + lax.broadcasted_iota(jnp.int32, (LANES, tq), 1)) >> SEL_SHIFT
    forced = (blk_id == 0) | (blk_id == jt) | (blk_id == jt - 1)
    work = jnp.where(blk_id > jt, -1.0, jnp.where(forced, -2.0, imp_t))
    sel = forced
    for _ in range(N_SELECT - 3):
        mx = jnp.max(work, axis=0, keepdims=True)
        idx = jnp.min(jnp.where(work == mx, blk_id, LANES), axis=0, keepdims=True)
        pick = blk_id == idx
        sel = sel | pick
        work = jnp.where(pick, -2.0, work)
    sel_bias = jnp.transpose(jnp.where(sel, 0.0, NEG)).astype(BF16)

    for g in range(GROUP):
        qa_scr[g * tq:(g + 1) * tq, LANES:2 * LANES] = sel_bias

    accumulate(_dot_t(qa_scr[...], ks_ref[diag, :]), vs_ref[diag, :], tri, True)

    def key_span(first_block, n_blocks):
        return pl.ds(pl.multiple_of(first_block * tq, tq), n_blocks * tq)

    def slc_quad(kq, carry):
        span = key_span(4 * kq, 4)
        accumulate(_dot_t(qa_scr[...], ks_ref[span, :]), vs_ref[span, :], None, False)
        return carry

    lax.fori_loop(0, i // 4, slc_quad, 0)

    @pl.when(i % 4 >= 2)
    def _():
        span = key_span((i // 4) * 4, 2)
        accumulate(_dot_t(qa_scr[...], ks_ref[span, :]), vs_ref[span, :], None, False)

    @pl.when(i % 2 == 1)
    def _():
        blk = key_block(i - 1)
        accumulate(_dot_t(qa_scr[...], ks_ref[blk, :]), vs_ref[blk, :], None, False)

    gates = gate_ref[0]
    g_hi = gates.astype(BF16)
    g_lo = (gates - g_hi.astype(F32)).astype(BF16)
    gate_wide = _dot(g_hi, gsel_ref[...]) + _dot(g_lo, gsel_ref[...])
    for pr in range(GROUP // 2):
        ev = slice(2 * pr * tq, (2 * pr + 1) * tq)
        od = slice((2 * pr + 1) * tq, (2 * pr + 2) * tq)

        def pair(acc):
            a_e, a_o = acc[ev], acc[od]
            r_e = pltpu.roll(a_e, HEAD_DIM, 1)
            r_o = pltpu.roll(a_o, HEAD_DIM, 1)
            return jnp.where(low, a_e, r_o) / jnp.where(low, r_e, a_o)

        def gate(br):
            c = (br * (GROUP // 2) + pr) * LANES
            return gate_wide[:, c:c + LANES]

        o_c = jnp.where(low, o_cmp[ev], pltpu.roll(o_cmp[od], HEAD_DIM, 1))
        out = gate(0) * o_c + gate(1) * pair(acc_scr) + gate(2) * pair(accw_scr)
        o_ref[0, :, pr * LANES:(pr + 1) * LANES] = out.astype(BF16)


def _nsa_kernel(*refs, tq, shifted):
    ncp = refs[3].shape[0]
    half = ncp // 2
    if shifted or half % LANES:
        _nsa_body(*refs, tq=tq, shifted=shifted, ncp=ncp)
        return
    early = pl.program_id(2) < pl.num_programs(2) // 2

    @pl.when(early)
    def _():
        _nsa_body(*refs, tq=tq, shifted=False, ncp=half)

    @pl.when(jnp.logical_not(early))
    def _():
        _nsa_body(*refs, tq=tq, shifted=False, ncp=ncp)


def _nsa_attention(small, qn, qr, gates, cmp4, ks, vs, kw, vw, wimpt):
    b, s, hdim = qn.shape
    tq = min(256, s)
    assert s // SEL_BLOCK <= LANES and s % tq == 0 and SLIDE_WINDOW == 2 * tq
    ncp = cmp4.shape[2]
    gw = GROUP * HEAD_DIM
    rows = GROUP * tq
    qspec = pl.BlockSpec((1, tq, gw), lambda bb, kv, i: (bb, i, kv))
    full = lambda width: pl.BlockSpec((None, s, width), lambda bb, kv, i: (bb, 0, kv))
    col = np.arange(3 * (GROUP // 2) * LANES)[None, :]
    src = (col // LANES // (GROUP // 2)) * GROUP + 2 * ((col // LANES) % (GROUP // 2)) + (col % LANES) // HEAD_DIM
    gsel = (np.arange(LANES)[:, None] == src).astype(BF16)

    def run(shifted):
        return lambda: pl.pallas_call(
            functools.partial(_nsa_kernel, tq=tq, shifted=shifted),
            **common)(qn, qr, gates, cmp4, cmp4, ks, vs, kw, vw, wimpt, gsel)

    common = dict(
        grid=(b, N_KV_HEADS, s // tq),
        in_specs=[qspec, qspec,
                  pl.BlockSpec((1, tq, LANES), lambda bb, kv, i: (bb, i, kv)),
                  pl.BlockSpec((None, None, ncp, LANES), lambda bb, kv, i: (kv, bb, 0, 0)),
                  pl.BlockSpec((None, None, ncp, LANES), lambda bb, kv, i: (N_KV_HEADS + kv, bb, 0, 0)),
                  full(2 * LANES), full(LANES), full(LANES), full(LANES),
                  pl.BlockSpec((LANES, ncp), lambda bb, kv, i: (0, 0)),
                  pl.BlockSpec(gsel.shape, lambda bb, kv, i: (0, 0))],
        out_specs=qspec,
        out_shape=jax.ShapeDtypeStruct((b, s, hdim), BF16),
        scratch_shapes=[pltpu.VMEM((rows, 2 * LANES), BF16),
                        pltpu.VMEM((rows, 1), F32), pltpu.VMEM((rows, LANES), F32),
                        pltpu.VMEM((rows, 1), F32), pltpu.VMEM((rows, LANES), F32)],
        compiler_params=_params("parallel", "parallel", "arbitrary"),
        name="nsa_attention",
    )
    return lax.cond(small[0] != 0, run(False), run(True))


def _rope_tables(s, tm=None, rate=1):
    half = HEAD_DIM // 2
    row = lax.broadcasted_iota(jnp.int32, (s, LANES), 0)
    lane = lax.broadcasted_iota(jnp.int32, (s, LANES), 1)
    if rate > 1:
        seg = tm // rate
        local = row % tm
        row = row - local + (local % seg) * rate + local // seg
    inv = jnp.tile(ROPE_THETA ** (-jnp.arange(half, dtype=F32) / half), LANES // half)
    ang = row.astype(F32) * inv[None, :]
    cos, sin = jnp.cos(ang), jnp.sin(ang)
    return cos, jnp.where(lane % HEAD_DIM < half, -sin, sin), sin


def _split_half(t, axis):
    hpg = MXU_WIDTH // HEAD_DIM
    half = HEAD_DIM // 2
    shape = t.shape
    t = t.reshape(shape[:axis] + (shape[axis] // MXU_WIDTH, hpg, 2, half) + shape[axis + 1:])
    return t.swapaxes(axis + 1, axis + 2).reshape(shape)


def _importance_weights(ncp):
    r = SEL_BLOCK // CMP_STRIDE
    nper = CMP_LEN // CMP_STRIDE
    o = np.arange(ncp)[:, None] - r * np.arange(LANES)[None, :]
    w = np.zeros((ncp, LANES), np.float32)
    for off in range(-(nper - 1), r):
        cnt = sum(1 for m in range(r) for n in range(nper) if m - n == off)
        w += np.where(o == off, float(cnt), 0.0)
    return w.astype(BF16)


def kernel(x, c, norm_g, w_ada, b_ada, ffn_w_in, ffn_w_out, a_w_qkv, a_q_gain, a_k_gain, a_w_o, kv_norm_g,
           w_ada_kv, b_ada_kv, w_kv, kv_k_gain, cmp_pos, phi_w1, phi_w2, b_w_qg, b_q_gain, b_w_o):
    b, s, d = x.shape
    depth = norm_g.shape[0]
    n_a = depth // 2
    hdim = N_HEADS * HEAD_DIM
    assert depth == 2 and b <= 8

    c8 = jnp.zeros((8, d), F32).at[:b].set(c)
    mod = _ada(c8, w_ada, b_ada.reshape(depth, 1, 9 * d))
    mod5 = mod[:, :b].reshape(depth, b, 9, 1, d)
    kvmod = _ada(c8, w_ada_kv[None], b_ada_kv.reshape(1, 1, 2 * d))
    kvmod5 = kvmod[:, :b].reshape(1, b, 2, 1, d)

    norm_g3 = norm_g.reshape(depth * 3, 1, d)
    w_in = ffn_w_in.astype(BF16)
    w_out = ffn_w_out.astype(BF16)
    cos, sin, _ = _rope_tables(s)
    blk = np.arange(MXU_WIDTH) // HEAD_DIM
    bd_wide = (blk[:, None] == blk[None, :]).astype(BF16)
    blk = (np.arange(MXU_WIDTH) % LANES) // (HEAD_DIM // 2)
    bd_split = (blk[:, None] == blk[None, :]).astype(BF16)
    q_scale = HEAD_DIM ** -0.5 * LOG2E

    h = x
    for l in range(depth):
        h = _ffn(h, norm_g3, mod5, w_in, w_out, l, 0, 0)
        if l < n_a:
            gain = jnp.stack([_split_half(jnp.tile(a_q_gain[l] * q_scale, (1, N_HEADS)), 1),
                              _split_half(jnp.tile(a_k_gain[l], (1, N_HEADS)), 1),
                              jnp.ones((N_DIL, hdim), F32)], axis=1).reshape(3 * N_DIL, 1, hdim)
            cols = np.arange(N_DIL * 3 * hdim).reshape(N_DIL, 3, hdim)
            cols = np.concatenate([_split_half(cols[:, :2], 2), cols[:, 2:]], axis=1).reshape(-1)
            w_qkv = a_w_qkv[l][:, cols].astype(BF16)
            bound = HEAD_DIM * q_scale * jnp.max(jnp.abs(a_q_gain[l])) * jnp.max(jnp.abs(a_k_gain[l]))
            small = (bound <= SCORE_LIMIT).astype(jnp.int32).reshape(1)
            parts = [_dilated_attention(small, _qkv_proj(h, norm_g3, mod5, w_qkv, gain, bd_split, l, g), g)
                     for g in range(N_DIL)]
            h = _merge_out([p[0] for p in parts], [p[1] for p in parts], h, mod5, a_w_o[l].astype(BF16), l)
        else:
            j = l - n_a
            w_q = b_w_qg[j][:, :hdim]
            w_g = b_w_qg[j][:, hdim:].reshape(d, 3, N_KV_HEADS, GROUP).transpose(0, 2, 1, 3)
            w_g = jnp.pad(w_g.reshape(d, N_KV_HEADS, 3 * GROUP), ((0, 0), (0, 0), (0, LANES - 3 * GROUP)))
            w_qg = jnp.concatenate([w_q, w_g.reshape(d, N_KV_HEADS * LANES)], axis=1).astype(BF16)
            gain = jnp.tile(b_q_gain[j] * q_scale, N_HEADS)[None, :]
            qn, qr, gates = _qg_proj(h, norm_g3, mod5, w_qg, gain, cos, sin, bd_wide, l)
            bound = HEAD_DIM * q_scale * jnp.max(jnp.abs(b_q_gain[j])) * jnp.max(jnp.abs(kv_k_gain))
            small = (bound <= SCORE_LIMIT).astype(jnp.int32).reshape(1)
            pending = (_nsa_attention(small, qn, qr, gates, *shared), b_w_o[j].astype(BF16))
        h = _ffn(h, norm_g3, mod5, w_in, w_out, l, 2, 1, *(pending if l >= n_a else ()))
        if l == n_a - 1:
            col = lambda br, kv, kvh: br * 4 * HEAD_DIM + kv * 2 * HEAD_DIM + kvh * HEAD_DIM + jnp.arange(HEAD_DIM)
            dup = lambda br, kv, kvh: jnp.concatenate([col(br, kv, kvh)] * 2)
            cols = jnp.concatenate(
                [dup(br, 0, kvh) for br in (1, 2) for kvh in range(N_KV_HEADS)]
                + [dup(br, 1, kvh) for br in (1, 2) for kvh in range(N_KV_HEADS)]
                + [col(0, kv, kvh) for kv in range(2) for kvh in range(N_KV_HEADS)])
            w_kv_ext = w_kv[:, cols].astype(BF16)
            reps = MXU_WIDTH // HEAD_DIM
            kgain = jnp.concatenate([jnp.tile(kv_k_gain[1], (1, reps)), jnp.tile(kv_k_gain[2], (1, reps)),
                                     jnp.ones((6, MXU_WIDTH), F32)], axis=0)
            ks, kw, vs, vw, kvc = _kv_proj(h, kv_norm_g[None, :], kvmod5, w_kv_ext, kgain, cos, sin, bd_wide)
            nsb = s // CMP_STRIDE
            t4 = kvc.reshape(b, nsb, CMP_STRIDE, 4, HEAD_DIM).transpose(3, 0, 1, 2, 4)
            t4 = t4.reshape(4, b, nsb, CMP_STRIDE * HEAD_DIM)
            cmp4 = _compress(t4, phi_w1, phi_w2, cmp_pos.reshape(2, 1, CMP_LEN * HEAD_DIM), kv_k_gain[0][None, :])
            shared = (cmp4, ks, vs, kw, vw, _importance_weights(nsb).T)
    return h
```
